```python
import math
import jax, jax.numpy as jnp
from jax import lax
import numpy as np

D_MODEL = 1024
BATCH = 8
SEQ = 4096
DEPTH = 4

GRID_W = 64
CTX_LEN = 256
N_EVEN = (DEPTH + 1) // 2
N_ODD = DEPTH // 2

HEAD_DIM = 64
RET_HEADS = 8
ATT_HEADS = 8
ATT_KV_HEADS = 2
ATT_GROUP = ATT_HEADS // ATT_KV_HEADS
RET_WIDTH = RET_HEADS * HEAD_DIM
ATT_WIDTH = ATT_HEADS * HEAD_DIM
ATT_KV_WIDTH = ATT_KV_HEADS * HEAD_DIM
MIX_WIDTH = RET_WIDTH + ATT_WIDTH
IN_SPLITS = (RET_WIDTH, 2 * RET_WIDTH, 3 * RET_WIDTH, 4 * RET_WIDTH,
             4 * RET_WIDTH + ATT_WIDTH, 4 * RET_WIDTH + ATT_WIDTH + ATT_KV_WIDTH)
IN_WIDTH = 4 * RET_WIDTH + ATT_WIDTH + 2 * ATT_KV_WIDTH
CHUNK = 128
Q_BLOCK = 128
ROPE_THETA = 10000.0
ROPE_AXIS_DIM = HEAD_DIM // 2

S5_GROUP = 16
S5_GROUPS = D_MODEL // S5_GROUP
S5_STATE = 64
S5_DT_MIN = 0.001
S5_DT_MAX = 0.1

N_EXPERTS = 16
CAPACITY_FACTOR = 2
D_FF_EXPERT = 2 * D_MODEL
EPS = 1e-6

kernel_name = "hybrid_retention_gqa_s5_ecmoe_diffusion_trunk"

F32 = jnp.float32


def rms_norm(x, g):
    x32 = x.astype(F32)
    y = x32 * lax.rsqrt(jnp.mean(x32 * x32, axis=-1, keepdims=True) + EPS)
    return y.astype(x.dtype) * g


def modulation(cond, w, b):
    m = jax.nn.silu(cond) @ w + b
    return jnp.split(m[..., None, :], 6, axis=-1)


def axial_rope_tables(n_rows):
    row = jnp.repeat(jnp.arange(n_rows, dtype=F32), GRID_W)
    col = jnp.tile(jnp.arange(GRID_W, dtype=F32), n_rows)
    inv = ROPE_THETA ** (-jnp.arange(0, ROPE_AXIS_DIM, 2, dtype=F32) / ROPE_AXIS_DIM)
    ang_r = row[:, None] * inv[None, :]
    ang_c = col[:, None] * inv[None, :]
    return (jnp.cos(ang_r)[:, None, :], jnp.sin(ang_r)[:, None, :],
            jnp.cos(ang_c)[:, None, :], jnp.sin(ang_c)[:, None, :])


def _rope_1d(x, cos, sin):
    half = x.shape[-1] // 2
    x1, x2 = x[..., :half], x[..., half:]
    return jnp.concatenate([x1 * cos - x2 * sin, x1 * sin + x2 * cos], axis=-1)


def apply_axial_rope(x, rope):
    cr, sr, cc, sc = rope
    xr = _rope_1d(x[..., :ROPE_AXIS_DIM].astype(F32), cr, sr)
    xc = _rope_1d(x[..., ROPE_AXIS_DIM:].astype(F32), cc, sc)
    return jnp.concatenate([xr, xc], axis=-1).astype(x.dtype)


def split_heads(t, n_heads):
    return t.reshape(t.shape[0], t.shape[1], n_heads, HEAD_DIM)


def retention_chunkwise(q, k, v, log_gamma, state0, strict):
    b, h, n, d = q.shape
    nc = n // CHUNK
    pos = jnp.arange(CHUNK, dtype=F32)
    diff = pos[:, None] - pos[None, :]
    mask = (diff > 0) if strict else (diff >= 0)
    lg = log_gamma[:, None, None]
    decay_intra = jnp.where(mask, jnp.exp(jnp.where(mask, diff, 0.0) * lg), 0.0)
    decay_q = jnp.exp((pos + 1.0)[None, :] * log_gamma[:, None])[..., None]
    decay_k = jnp.exp((CHUNK - 1.0 - pos)[None, :] * log_gamma[:, None])[..., None]
    decay_chunk = jnp.exp(CHUNK * log_gamma)[:, None, None]

    def to_chunks(t):
        return jnp.moveaxis(t.reshape(b, h, nc, CHUNK, d), 2, 0)

    def step(state, qkv):
        qc, kc, vc = qkv
        scores = jnp.einsum('bhid,bhjd->bhij', qc, kc) * decay_intra
        out = (jnp.einsum('bhij,bhjd->bhid', scores, vc)
               + jnp.einsum('bhid,bhde->bhie', qc * decay_q, state))
        state = decay_chunk * state + jnp.einsum('bhjd,bhje->bhde', kc * decay_k, vc)
        return state, out

    state, out = lax.scan(step, state0, (to_chunks(q), to_chunks(k), to_chunks(v)))
    return jnp.moveaxis(out, 0, 2).reshape(b, h, n, d), state


def bidir_retention(q, k, v, lg_f, lg_b, s_f, s_b):
    o_f, s_f_new = retention_chunkwise(q, k, v, lg_f, s_f, False)
    flip = lambda t: jnp.flip(t, axis=2)
    o_b, s_b_new = retention_chunkwise(flip(q), flip(k), flip(v), lg_b, s_b, True)
    return o_f + flip(o_b), s_f_new, s_b_new


def retention_output(o, gate_proj, gn_g):
    b, h, n, d = o.shape
    mu = jnp.mean(o, axis=-1, keepdims=True)
    var = jnp.mean(jnp.square(o - mu), axis=-1, keepdims=True)
    o = (o - mu) * lax.rsqrt(var + EPS)
    o = jnp.swapaxes(o, 1, 2).reshape(b, n, h * d).astype(gate_proj.dtype) * gn_g
    return jax.nn.silu(gate_proj) * o


def attend(q, k, v):
    s = jnp.einsum('bkgqd,bkmd->bkgqm', q, k).astype(F32) * (HEAD_DIM ** -0.5)
    p = jax.nn.softmax(s, axis=-1).astype(v.dtype)
    return jnp.einsum('bkgqm,bkmd->bkgqd', p, v)


def even_mixer(h_lat, h_ctx, w_in, w_out, ret_log_rate, ret_gn_g, qk_norm_g, rope, need_ctx):
    def project(h, pos):
        rq, rk, rv, rg, aq, ak, av = jnp.split(h @ w_in, IN_SPLITS, axis=-1)
        rq = split_heads(rq, RET_HEADS)
        rk = split_heads(rk, RET_HEADS) * (HEAD_DIM ** -0.5)
        aq = rms_norm(split_heads(aq, ATT_HEADS), qk_norm_g[0])
        ak = rms_norm(split_heads(ak, ATT_KV_HEADS), qk_norm_g[1])
        if pos is not None:
            rq, rk, aq, ak = [apply_axial_rope(t, pos) for t in (rq, rk, aq, ak)]
        hf = lambda t: jnp.swapaxes(t, 1, 2)
        return (hf(rq).astype(F32), hf(rk).astype(F32), hf(split_heads(rv, RET_HEADS)).astype(F32), rg,
                hf(aq), hf(ak), hf(split_heads(av, ATT_KV_HEADS)))

    b, n_lat, n_ctx = h_lat.shape[0], h_lat.shape[1], h_ctx.shape[1]
    lrq, lrk, lrv, lrg, laq, lak, lav = project(h_lat, rope)
    crq, crk, crv, crg, caq, cak, cav = project(h_ctx, None)

    lg_f = -jnp.exp(ret_log_rate[0].astype(F32))
    lg_b = -jnp.exp(ret_log_rate[1].astype(F32))
    zero = jnp.zeros((b, RET_HEADS, HEAD_DIM, HEAD_DIM), F32)
    o_ctx, s_f, s_b = bidir_retention(crq, crk, crv, lg_f, lg_b, zero, zero)
    o_lat, _, _ = bidir_retention(lrq, lrk, lrv, lg_f, lg_b, s_f, s_b)
    ret_lat = retention_output(o_lat, lrg, ret_gn_g)

    k_all = jnp.concatenate([cak, lak], axis=2)
    v_all = jnp.concatenate([cav, lav], axis=2)
    n_blocks = n_lat // Q_BLOCK
    q_lat = laq.reshape(b, ATT_KV_HEADS, ATT_GROUP, n_blocks, Q_BLOCK, HEAD_DIM)
    q_blocks = jnp.moveaxis(q_lat, 3, 0)
    o_blocks = lax.map(lambda qb: attend(qb, k_all, v_all), q_blocks)
    att_lat = o_blocks.transpose(1, 0, 4, 2, 3, 5).reshape(b, n_lat, ATT_WIDTH)

    y_lat = jnp.concatenate([ret_lat, att_lat], axis=-1) @ w_out
    if not need_ctx:
        return y_lat, None
    ret_ctx = retention_output(o_ctx, crg, ret_gn_g)
    q_ctx = caq.reshape(b, ATT_KV_HEADS, ATT_GROUP, n_ctx, HEAD_DIM)
    att_ctx = attend(q_ctx, cak, cav).transpose(0, 3, 1, 2, 4).reshape(b, n_ctx, ATT_WIDTH)
    y_ctx = jnp.concatenate([ret_ctx, att_ctx], axis=-1) @ w_out
    return y_lat, y_ctx


def _cmul(ar, ai, br, bi):
    return ar * br - ai * bi, ar * bi + ai * br


def _s5_combine(e1, e2):
    a1r, a1i, b1r, b1i = e1
    a2r, a2i, b2r, b2i = e2
    ar, ai = _cmul(a2r, a2i, a1r, a1i)
    br, bi = _cmul(a2r, a2i, b1r, b1i)
    return ar, ai, br + b2r, bi + b2i


def s5_discretize(a_re, a_im, log_dt, b_re, b_im):
    lam_re = jnp.minimum(a_re, -1e-4)
    lam_im = a_im
    dt = jnp.exp(log_dt)[:, None]
    mag = jnp.exp(lam_re * dt)
    bar_re = mag * jnp.cos(lam_im * dt)
    bar_im = mag * jnp.sin(lam_im * dt)
    den = lam_re * lam_re + lam_im * lam_im
    nr, ni = bar_re - 1.0, bar_im
    k_re = (nr * lam_re + ni * lam_im) / den
    k_im = (ni * lam_re - nr * lam_im) / den
    bb_re, bb_im = _cmul(k_re[..., None], k_im[..., None], b_re, b_im)
    return bar_re, bar_im, bb_re, bb_im


def s5_scan(u, bar_re, bar_im, bb_re, bb_im, cc_re, cc_im, x0):
    b, n, _ = u.shape
    nc = n // CHUNK
    u_chunks = jnp.moveaxis(u.reshape(b, nc, CHUNK, S5_GROUPS, S5_GROUP), 1, 0)

    def step(carry, uc):
        xr0, xi0 = carry
        br = jnp.einsum('bcgk,gpk->bcgp', uc, bb_re)
        bi = jnp.einsum('bcgk,gpk->bcgp', uc, bb_im)
        ar = jnp.broadcast_to(bar_re, br.shape)
        ai = jnp.broadcast_to(bar_im, br.shape)
        pr, pi, sr, si = lax.associative_scan(_s5_combine, (ar, ai, br, bi), axis=1)
        xr = pr * xr0[:, None] - pi * xi0[:, None] + sr
        xi = pr * xi0[:, None] + pi * xr0[:, None] + si
        y = jnp.einsum('bcgp,gkp->bcgk', xr, cc_re) - jnp.einsum('bcgp,gkp->bcgk', xi, cc_im)
        return (xr[:, -1], xi[:, -1]), y.reshape(b, CHUNK, D_MODEL)

    state, y = lax.scan(step, x0, u_chunks)
    return jnp.moveaxis(y, 0, 1).reshape(b, n, D_MODEL), state


def s5_mixer(h_lat, h_ctx, a_re, a_im, log_dt, b_re, b_im, c_re, c_im, d_skip, glu_w, glu_b, need_ctx):
    a_re, a_im, log_dt = a_re.astype(F32), a_im.astype(F32), log_dt.astype(F32)
    b_re, b_im, c_re, c_im = b_re.astype(F32), b_im.astype(F32), c_re.astype(F32), c_im.astype(F32)
    d32 = d_skip.astype(F32)
    disc_f = s5_discretize(a_re[0], a_im[0], log_dt[0], b_re, b_im)
    disc_b = s5_discretize(a_re[1], a_im[1], log_dt[1], b_re, b_im)

    def bidir(u, init_f, init_b):
        y_f, s_f = s5_scan(u, *disc_f, c_re[0], c_im[0], init_f)
        y_b, s_b = s5_scan(jnp.flip(u, axis=1), *disc_b, c_re[1], c_im[1], init_b)
        return y_f + jnp.flip(y_b, axis=1) + d32 * u, s_f, s_b

    def glu_out(y, dtype):
        z = jax.nn.gelu(y).astype(dtype)
        a, g = jnp.split(z @ glu_w + glu_b, 2, axis=-1)
        return a * jax.nn.sigmoid(g)

    b = h_lat.shape[0]
    zero = (jnp.zeros((b, S5_GROUPS, S5_STATE), F32), jnp.zeros((b, S5_GROUPS, S5_STATE), F32))
    y_ctx, s_f, s_b = bidir(h_ctx.astype(F32), zero, zero)
    y_lat, _, _ = bidir(h_lat.astype(F32), s_f, s_b)
    out_lat = glu_out(y_lat, h_lat.dtype)
    if not need_ctx:
        return out_lat, None
    return out_lat, glu_out(y_ctx, h_ctx.dtype)


def ec_moe(h, w_router, w1, w3, w2):
    b, n, _ = h.shape
    cap = (CAPACITY_FACTOR * n) // N_EXPERTS
    aff = jax.nn.softmax((h @ w_router).astype(F32), axis=-1)
    gate, idx = lax.top_k(jnp.swapaxes(aff, 1, 2), cap)
    bidx = jnp.arange(b)[:, None, None]
    xs = h[bidx, idx]
    hid = jax.nn.silu(jnp.einsum('becd,edf->becf', xs, w1)) * jnp.einsum('becd,edf->becf', xs, w3)
    ys = jnp.einsum('becf,efd->becd', hid, w2) * gate[..., None].astype(h.dtype)
    return jnp.zeros_like(h).at[bidx, idx].add(ys)


def setup_inputs(seed: int = 0) -> dict:
    key = jax.random.key(seed)
    ks = jax.random.split(key, 32)
    nrm = lambda k, shape, s: jax.random.normal(k, shape, F32) * s
    s5_shape = (N_ODD, 2, S5_GROUPS, S5_STATE)
    base_rate = -jnp.log1p(-(2.0 ** (-5.0 - jnp.arange(RET_HEADS, dtype=F32))))
    n_idx = jnp.arange(S5_STATE, dtype=F32)
    return {
        "x": nrm(ks[0], (BATCH, SEQ, D_MODEL), 1.0),
        "c": nrm(ks[1], (BATCH, D_MODEL), 1.0),
        "ctx": nrm(ks[2], (BATCH, CTX_LEN, D_MODEL), 1.0),
        "c_ctx": nrm(ks[3], (D_MODEL,), 1.0),
        "mod_w": nrm(ks[4], (DEPTH, D_MODEL, 6 * D_MODEL), 0.5 * D_MODEL ** -0.5),
        "mod_b": nrm(ks[5], (DEPTH, 6 * D_MODEL), 0.02),
        "norm_g": 1.0 + nrm(ks[6], (DEPTH, 2, D_MODEL), 0.02),
        "mix_in_w": nrm(ks[7], (N_EVEN, D_MODEL, IN_WIDTH), D_MODEL ** -0.5),
        "mix_out_w": nrm(ks[8], (N_EVEN, MIX_WIDTH, D_MODEL), MIX_WIDTH ** -0.5),
        "ret_log_rate": jnp.log(base_rate) + nrm(ks[9], (N_EVEN, 2, RET_HEADS), 0.05),
        "ret_gn_g": 1.0 + nrm(ks[10], (N_EVEN, RET_WIDTH), 0.02),
        "qk_norm_g": 1.0 + nrm(ks[11], (N_EVEN, 2, HEAD_DIM), 0.02),
        "s5_a_re": -0.5 + nrm(ks[12], s5_shape, 0.01),
        "s5_a_im": math.pi * n_idx + nrm(ks[13], s5_shape, 0.01),
        "s5_log_dt": jax.random.uniform(ks[14], (N_ODD, 2, S5_GROUPS), F32,
                                        math.log(S5_DT_MIN), math.log(S5_DT_MAX)),
        "s5_b_re": nrm(ks[15], (N_ODD, S5_GROUPS, S5_STATE, S5_GROUP), (2 * S5_GROUP) ** -0.5),
        "s5_b_im": nrm(ks[16], (N_ODD, S5_GROUPS, S5_STATE, S5_GROUP), (2 * S5_GROUP) ** -0.5),
        "s5_c_re": nrm(ks[17], (N_ODD, 2, S5_GROUPS, S5_GROUP, S5_STATE), S5_STATE ** -0.5),
        "s5_c_im": nrm(ks[18], (N_ODD, 2, S5_GROUPS, S5_GROUP, S5_STATE), S5_STATE ** -0.5),
        "s5_d": nrm(ks[19], (N_ODD, D_MODEL), 1.0),
        "s5_glu_w": nrm(ks[20], (N_ODD, D_MODEL, 2 * D_MODEL), D_MODEL ** -0.5),
        "s5_glu_b": nrm(ks[21], (N_ODD, 2 * D_MODEL), 0.02),
        "moe_router_w": nrm(ks[22], (DEPTH, D_MODEL, N_EXPERTS), D_MODEL ** -0.5),
        "moe_w1": nrm(ks[23], (DEPTH, N_EXPERTS, D_MODEL, D_FF_EXPERT), D_MODEL ** -0.5),
        "moe_w3": nrm(ks[24], (DEPTH, N_EXPERTS, D_MODEL, D_FF_EXPERT), D_MODEL ** -0.5),
        "moe_w2": nrm(ks[25], (DEPTH, N_EXPERTS, D_FF_EXPERT, D_MODEL), D_FF_EXPERT ** -0.5),
        "final_norm_g": 1.0 + nrm(ks[26], (D_MODEL,), 0.02),
    }


def reference(x, c, ctx, c_ctx, mod_w, mod_b, norm_g, mix_in_w, mix_out_w, ret_log_rate, ret_gn_g,
              qk_norm_g, s5_a_re, s5_a_im, s5_log_dt, s5_b_re, s5_b_im, s5_c_re, s5_c_im, s5_d,
              s5_glu_w, s5_glu_b, moe_router_w, moe_w1, moe_w3, moe_w2, final_norm_g):
    n_rows = x.shape[1] // GRID_W
    rope = axial_rope_tables(n_rows)
    x_lat, x_ctx = x, ctx
    for layer in range(DEPTH):
        last = layer == DEPTH - 1
        sh1, sc1, g1, sh2, sc2, g2 = modulation(c, mod_w[layer], mod_b[layer])
        csh1, csc1, cg1, csh2, csc2, cg2 = modulation(c_ctx, mod_w[layer], mod_b[layer])
        h_lat = rms_norm(x_lat, norm_g[layer, 0]) * (1.0 + sc1) + sh1
        h_ctx = rms_norm(x_ctx, norm_g[layer, 0]) * (1.0 + csc1) + csh1
        i = layer // 2
        if layer % 2 == 0:
            d_lat, d_ctx = even_mixer(h_lat, h_ctx, mix_in_w[i], mix_out_w[i], ret_log_rate[i],
                                      ret_gn_g[i], qk_norm_g[i], rope, not last)
        else:
            d_lat, d_ctx = s5_mixer(h_lat, h_ctx, s5_a_re[i], s5_a_im[i], s5_log_dt[i], s5_b_re[i],
                                    s5_b_im[i], s5_c_re[i], s5_c_im[i], s5_d[i], s5_glu_w[i],
                                    s5_glu_b[i], not last)
        x_lat = x_lat + g1 * d_lat
        f_lat = rms_norm(x_lat, norm_g[layer, 1]) * (1.0 + sc2) + sh2
        x_lat = x_lat + g2 * ec_moe(f_lat, moe_router_w[layer], moe_w1[layer], moe_w3[layer], moe_w2[layer])
        if not last:
            x_ctx = x_ctx + cg1 * d_ctx
            f_ctx = rms_norm(x_ctx, norm_g[layer, 1]) * (1.0 + csc2) + csh2
            x_ctx = x_ctx + cg2 * ec_moe(f_ctx, moe_router_w[layer], moe_w1[layer], moe_w3[layer], moe_w2[layer])
    return rms_norm(x_lat, final_norm_g)
```

```python
import functools
import math

import jax
import jax.numpy as jnp
from jax import lax
from jax.experimental import pallas as pl
from jax.experimental.pallas import tpu as pltpu

F32 = jnp.float32
BF16 = jnp.bfloat16

GRID_W = 64
HEAD_DIM = 64
RET_HEADS = 8
ATT_HEADS = 8
ATT_KV_HEADS = 2
ATT_GROUP = ATT_HEADS // ATT_KV_HEADS
RET_WIDTH = RET_HEADS * HEAD_DIM
ATT_WIDTH = ATT_HEADS * HEAD_DIM
ATT_KV_WIDTH = ATT_KV_HEADS * HEAD_DIM
CHUNK = 128
ROPE_THETA = 10000.0
ROPE_AXIS_DIM = HEAD_DIM // 2
S5_GROUP = 16
S5_STATE = 64
S5_SUB = 16
N_EXPERTS = 16
CAPACITY_FACTOR = 2
EPS = 1e-6
SEG_W = 512
N_SEG = 7
ROUTER_LANES = 128

V7X_VMEM_BYTES = 64 * 1024 * 1024
VMEM_LIMIT = V7X_VMEM_BYTES - 8 * 1024 * 1024


def _cparams(sem):
    return pltpu.CompilerParams(dimension_semantics=sem, vmem_limit_bytes=VMEM_LIMIT)


def _dot(a, b):
    return jnp.dot(a, b, preferred_element_type=F32)


def _dot_nt(a, b):
    return lax.dot_general(a, b, (((1,), (1,)), ((), ())), preferred_element_type=F32)


def _dot_tn(a, b):
    return lax.dot_general(a, b, (((0,), (0,)), ((), ())), preferred_element_type=F32)


def _split(v):
    hi = v.astype(BF16)
    lo = (v - hi.astype(F32)).astype(BF16)
    return hi, lo


def _segsum(v, bd):
    hi, lo = _split(v)
    return _dot(hi, bd) + _dot(lo, bd)


def _sigmoid(v):
    return 1.0 / (1.0 + jnp.exp(-v))


def _silu(v):
    return v * _sigmoid(v)


def _gelu_tanh(v):
    c = math.sqrt(2.0 / math.pi)
    return 0.5 * v * (1.0 + jnp.tanh(c * (v + 0.044715 * (v * v * v))))


def _norm_mod(x, g, scale, shift):
    ms = jnp.mean(x * x, axis=-1, keepdims=True)
    return x * lax.rsqrt(ms + EPS) * g * (1.0 + scale) + shift


def _rope(x, cos, sin_signed):
    w = x.shape[-1]
    lane = lax.broadcasted_iota(jnp.int32, x.shape, 1)
    first = (lane % 32) < 16
    partner = jnp.where(first, pltpu.roll(x, w - 16, 1), pltpu.roll(x, 16, 1))
    return x * cos + partner * sin_signed


def _mod_kernel(c_ref, w_ref, b_ref, o_ref):
    s = _silu(c_ref[...]).astype(BF16)
    o_ref[0] = _dot(s, w_ref[0].astype(BF16)) + b_ref[0]


def _modulation(cond, mod_w, mod_b):
    depth, d, n6 = mod_w.shape
    rows = cond.shape[0]
    tn = n6 // 6
    return pl.pallas_call(
        _mod_kernel,
        grid=(depth, n6 // tn),
        in_specs=[
            pl.BlockSpec((rows, d), lambda l, j: (0, 0)),
            pl.BlockSpec((1, d, tn), lambda l, j: (l, 0, j)),
            pl.BlockSpec((1, 1, tn), lambda l, j: (l, 0, j)),
        ],
        out_specs=pl.BlockSpec((1, rows, tn), lambda l, j: (l, 0, j)),
        out_shape=jax.ShapeDtypeStruct((depth, rows, n6), F32),
        compiler_params=_cparams(("parallel", "parallel")),
    )(cond, mod_w, mod_b.reshape(depth, 1, n6))


def _residual_in(has_prev, refs):
    if has_prev:
        x_ref, moe_ref, gp_ref = refs[:3]
        return x_ref[0] + gp_ref[0, 0] * moe_ref[0], refs[3:]
    return refs[0][0], refs[1:]


def _in_kernel(has_prev, *refs):
    x, refs = _residual_in(has_prev, refs)
    mod_ref, ng_ref, w_ref, cos_ref, sin_ref, qkg_ref, bd_ref = refs[:7]
    outs = refs[7:]
    if has_prev:
        outs[0][0] = x
        outs = outs[1:]
    rq_ref, rk_ref, rv_ref, rg_ref, aq_ref, ak_ref, av_ref = outs
    m = mod_ref[0, 0]
    h = _norm_mod(x, ng_ref[...], m[1:2], m[0:1]).astype(BF16)
    cos = cos_ref[...]
    sin = sin_ref[...]
    bd = bd_ref[...]

    def seg(i):
        return _dot(h, w_ref[:, i * SEG_W:(i + 1) * SEG_W])

    def headnorm(v, g):
        ms = _segsum(v * v, bd) * (1.0 / HEAD_DIM)
        return v * lax.rsqrt(ms + EPS) * g

    scale = HEAD_DIM ** -0.5
    rq_ref[0] = _rope(seg(0), cos, sin).astype(BF16)
    rk_ref[0] = _rope(seg(1) * scale, cos, sin).astype(BF16)
    rv_ref[0] = seg(2).astype(BF16)
    rg_ref[0] = seg(3)
    aq_ref[0] = (_rope(headnorm(seg(4), qkg_ref[0:1]), cos, sin) * scale).astype(BF16)
    ak_ref[0] = _rope(headnorm(seg(5), qkg_ref[1:2]), cos, sin).astype(BF16)
    av_ref[0] = seg(6).astype(BF16)


def _row_spec(tm, d, lead=0):
    return pl.BlockSpec((1, tm, d), lambda b, i: (b, i + lead, 0))


def _mod_spec(d, n, seg_tiles):
    return pl.BlockSpec((1, 1, n, d), lambda b, i: (b, jnp.where(i >= seg_tiles, 1, 0), 0, 0))


def _const_spec(shape):
    nd = len(shape)
    return pl.BlockSpec(shape, lambda b, i: (0,) * nd)


def _in_proj(x, prev, modtab, norm_g, w_ext, cos, sin, qkg, bd, tm, n_ctx):
    b, t, d = x.shape
    seg_tiles = n_ctx // tm
    has_prev = prev is not None
    args, specs = [x], [_row_spec(tm, d)]
    if has_prev:
        moe, gprev = prev
        args += [moe, gprev]
        specs += [_row_spec(tm, d), _mod_spec(d, 1, seg_tiles)]
    args += [modtab, norm_g, w_ext, cos, sin, qkg, bd]
    specs += [
        _mod_spec(d, 6, seg_tiles),
        _const_spec((1, d)),
        _const_spec(w_ext.shape),
        pl.BlockSpec((tm, SEG_W), lambda b, i: (i, 0)),
        pl.BlockSpec((tm, SEG_W), lambda b, i: (i, 0)),
        _const_spec((2, SEG_W)),
        _const_spec((SEG_W, SEG_W)),
    ]
    out_shape, out_specs = [], []
    if has_prev:
        out_shape.append(jax.ShapeDtypeStruct((b, t, d), F32))
        out_specs.append(_row_spec(tm, d))
    for dt in (BF16, BF16, BF16, F32, BF16, BF16, BF16):
        out_shape.append(jax.ShapeDtypeStruct((b, t, SEG_W), dt))
        out_specs.append(_row_spec(tm, SEG_W))
    outs = pl.pallas_call(
        functools.partial(_in_kernel, has_prev),
        grid=(b, t // tm),
        in_specs=specs,
        out_specs=out_specs,
        out_shape=out_shape,
        compiler_params=_cparams(("parallel", "parallel")),
    )(*args)
    if has_prev:
        return outs[0], outs[1:]
    return x, outs


def _ret_kernel(qf_ref, kf_ref, vf_ref, qb_ref, kb_ref, vb_ref, rate_ref, rate_h_ref, bd_ref,
                of_ref, ob_ref, sf_s, sb_s, dm_s, dq_s, dk_s, dc_s):
    s = pl.program_id(1)
    c = CHUNK

    @pl.when(s == 0)
    def _():
        sf_s[...] = jnp.zeros_like(sf_s)
        sb_s[...] = jnp.zeros_like(sb_s)
        pos = lax.broadcasted_iota(jnp.int32, (c, SEG_W), 0).astype(F32)
        lgf = -jnp.exp(rate_ref[0:1, :])
        lgb = -jnp.exp(rate_ref[1:2, :])
        dq_s[0] = jnp.exp((pos + 1.0) * lgf)
        dk_s[0] = jnp.exp((c - 1.0 - pos) * lgf)
        dc_s[0] = jnp.exp(float(c) * lgf)
        dq_s[1] = jnp.exp((c - pos) * lgb)
        dk_s[1] = jnp.exp(pos * lgb)
        dc_s[1] = jnp.exp(float(c) * lgb)
        ii = lax.broadcasted_iota(jnp.int32, (c, c), 0)
        jj = lax.broadcasted_iota(jnp.int32, (c, c), 1)
        for h in range(RET_HEADS):
            lf = -jnp.exp(rate_h_ref[0, h:h + 1, :])
            lb = -jnp.exp(rate_h_ref[1, h:h + 1, :])
            mf = ii >= jj
            mb = jj > ii
            dm_s[0, h] = jnp.where(mf, jnp.exp(jnp.where(mf, ii - jj, 0).astype(F32) * lf), 0.0)
            dm_s[1, h] = jnp.where(mb, jnp.exp(jnp.where(mb, jj - ii, 0).astype(F32) * lb), 0.0)

    bdm = bd_ref[...].astype(F32)
    head_of_lane = lax.broadcasted_iota(jnp.int32, (1, SEG_W // 2), 1) // HEAD_DIM

    def direction(d, q_ref, k_ref, v_ref, o_ref, st):
        q = q_ref[0]
        k = k_ref[0]
        v = v_ref[0]
        state = st[...]
        qd = (q.astype(F32) * dq_s[d]).astype(BF16)
        inter = _dot(qd, state.astype(BF16))
        for half in range(2):
            sl = slice(half * 256, half * 256 + 256)
            qh, kh, vh = q[:, sl], k[:, sl], v[:, sl]
            acc = inter[:, sl]
            for hh in range(4):
                msk = head_of_lane == hh
                sc = _dot_nt(jnp.where(msk, qh, jnp.zeros_like(qh)), kh) * dm_s[d, half * 4 + hh]
                acc = acc + _dot(sc.astype(BF16), jnp.where(msk, vh, jnp.zeros_like(vh)))
            o_ref[0, :, sl] = acc
        kd = (k.astype(F32) * dk_s[d]).astype(BF16)
        st[...] = state * dc_s[d] + bdm * _dot_tn(kd, v)

    direction(0, qf_ref, kf_ref, vf_ref, of_ref, sf_s)
    direction(1, qb_ref, kb_ref, vb_ref, ob_ref, sb_s)


def _retention(rq, rk, rv, rate, rate_h, bd, n_ctx):
    b, t, w = rq.shape
    nc = t // CHUNK
    lc = n_ctx // CHUNK

    def fwd(bi, s):
        return (bi, s, 0)

    def bwd(bi, s):
        return (bi, jnp.where(s < lc, lc - 1 - s, nc - 1 - (s - lc)), 0)

    blk = (1, CHUNK, w)
    return pl.pallas_call(
        _ret_kernel,
        grid=(b, nc),
        in_specs=[pl.BlockSpec(blk, fwd)] * 3 + [pl.BlockSpec(blk, bwd)] * 3 + [
            _const_spec((2, w)),
            _const_spec((2, RET_HEADS, CHUNK)),
            _const_spec((w, w)),
        ],
        out_specs=[pl.BlockSpec(blk, fwd), pl.BlockSpec(blk, bwd)],
        out_shape=[jax.ShapeDtypeStruct((b, t, w), F32)] * 2,
        scratch_shapes=[
            pltpu.VMEM((w, w), F32), pltpu.VMEM((w, w), F32),
            pltpu.VMEM((2, RET_HEADS, CHUNK, CHUNK), F32),
            pltpu.VMEM((2, CHUNK, w), F32), pltpu.VMEM((2, CHUNK, w), F32),
            pltpu.VMEM((2, 1, w), F32),
        ],
        compiler_params=_cparams(("parallel", "arbitrary")),
    )(rq, rk, rv, rq, rk, rv, rate, rate_h, bd)


def _att_kernel(q_ref, k_ref, v_ref, o_ref, *, tq, tk, n_ctx, t):
    qi = pl.program_id(2)
    q = q_ref[0]
    gw = ATT_GROUP * HEAD_DIM
    head_of_lane = lax.broadcasted_iota(jnp.int32, (1, gw), 1) // HEAD_DIM
    masks = [head_of_lane == h for h in range(ATT_GROUP)]
    qm = [jnp.where(masks[h], q, jnp.zeros_like(q)) for h in range(ATT_GROUP)]
    nk = jnp.where(qi * tq < n_ctx, n_ctx // tk, t // tk)

    def body(j, carry):
        ms, ls, acc = carry
        off = pl.multiple_of(j * tk, tk)
        kb = k_ref[0, pl.ds(off, tk), :]
        vb = v_ref[0, pl.ds(off, tk), :]
        new_m, new_l = [], []
        alpha_full = jnp.zeros((tq, gw), F32)
        pv = jnp.zeros((tq, gw), F32)
        for h in range(ATT_GROUP):
            sc = _dot_nt(qm[h], kb)
            m_new = jnp.maximum(ms[h], jnp.max(sc, axis=1, keepdims=True))
            p = jnp.exp(sc - m_new)
            alpha = jnp.exp(ms[h] - m_new)
            new_l.append(alpha * ls[h] + jnp.sum(p, axis=1, keepdims=True))
            new_m.append(m_new)
            alpha_full = jnp.where(masks[h], alpha, alpha_full)
            pv = pv + _dot(p.astype(BF16), jnp.where(masks[h], vb, jnp.zeros_like(vb)))
        return tuple(new_m), tuple(new_l), acc * alpha_full + pv

    init = (tuple(jnp.full((tq, 1), -1e30, F32) for _ in range(ATT_GROUP)),
            tuple(jnp.zeros((tq, 1), F32) for _ in range(ATT_GROUP)),
            jnp.zeros((tq, gw), F32))
    ms, ls, acc = lax.fori_loop(0, nk, body, init)
    inv = jnp.zeros((tq, gw), F32)
    for h in range(ATT_GROUP):
        inv = jnp.where(masks[h], 1.0 / ls[h], inv)
    o_ref[0] = (acc * inv).astype(BF16)


def _attention(aq, ak4, av4, tq, tk, n_ctx):
    b, t, w = aq.shape
    gw = ATT_GROUP * HEAD_DIM
    return pl.pallas_call(
        functools.partial(_att_kernel, tq=tq, tk=tk, n_ctx=n_ctx, t=t),
        grid=(b, ATT_KV_HEADS, t // tq),
        in_specs=[
            pl.BlockSpec((1, tq, gw), lambda bi, g, i: (bi, i, g)),
            pl.BlockSpec((1, t, gw), lambda bi, g, i: (bi, 0, g)),
            pl.BlockSpec((1, t, gw), lambda bi, g, i: (bi, 0, g)),
        ],
        out_specs=pl.BlockSpec((1, tq, gw), lambda bi, g, i: (bi, i, g)),
        out_shape=jax.ShapeDtypeStruct((b, t, w), BF16),
        compiler_params=_cparams(("parallel", "parallel", "parallel")),
    )(aq, ak4, av4)


def _post_mixer(x, delta, m, ng2, wrh, wrl, x_ref, f_ref, aff_ref):
    xn = x + m[2:3] * delta
    x_ref[0] = xn
    f = _norm_mod(xn, ng2, m[4:5], m[3:4])
    fh, fl = _split(f)
    f_ref[0] = fh
    logits = _dot(fh, wrh) + _dot(fl, wrh) + _dot(fh, wrl)
    lane = lax.broadcasted_iota(jnp.int32, logits.shape, 1)
    logits = jnp.where(lane < N_EXPERTS, logits, -1e30)
    e = jnp.exp(logits - jnp.max(logits, axis=1, keepdims=True))
    aff_ref[0] = e / jnp.sum(e, axis=1, keepdims=True)


def _out_kernel(x_ref, of_ref, ob_ref, rg_ref, att_ref, mod_ref, gn_ref, w_ref, ng2_ref, wrh_ref, wrl_ref,
                bd_ref, xo_ref, f_ref, aff_ref):
    bd = bd_ref[...]
    o = of_ref[0] + ob_ref[0]
    mu = _segsum(o, bd) * (1.0 / HEAD_DIM)
    oc = o - mu
    var = _segsum(oc * oc, bd) * (1.0 / HEAD_DIM)
    r = _silu(rg_ref[0]) * (oc * lax.rsqrt(var + EPS) * gn_ref[...])
    y = _dot(r.astype(BF16), w_ref[0:RET_WIDTH, :]) + _dot(att_ref[0], w_ref[RET_WIDTH:, :])
    _post_mixer(x_ref[0], y, mod_ref[0, 0], ng2_ref[...], wrh_ref[...], wrl_ref[...], xo_ref, f_ref, aff_ref)


def _post_specs(b, t, d, tm):
    out_shape = [jax.ShapeDtypeStruct((b, t, d), F32), jax.ShapeDtypeStruct((b, t, d), BF16),
                 jax.ShapeDtypeStruct((b, t, ROUTER_LANES), F32)]
    out_specs = [_row_spec(tm, d), _row_spec(tm, d), _row_spec(tm, ROUTER_LANES)]
    return out_shape, out_specs


def _out_proj(x, of, ob, rg, att, modtab, gn_g, w_out, ng2, wrh, wrl, bd, tm, n_ctx):
    b, t, d = x.shape
    out_shape, out_specs = _post_specs(b, t, d, tm)
    return pl.pallas_call(
        _out_kernel,
        grid=(b, t // tm),
        in_specs=[
            _row_spec(tm, d), _row_spec(tm, SEG_W), _row_spec(tm, SEG_W), _row_spec(tm, SEG_W),
            _row_spec(tm, SEG_W), _mod_spec(d, 6, n_ctx // tm), _const_spec((1, SEG_W)),
            _const_spec(w_out.shape), _const_spec((1, d)), _const_spec(wrh.shape), _const_spec(wrl.shape),
            _const_spec((SEG_W, SEG_W)),
        ],
        out_specs=out_specs,
        out_shape=out_shape,
        compiler_params=_cparams(("parallel", "parallel")),
    )(x, of, ob, rg, att, modtab, gn_g, w_out, ng2, wrh, wrl, bd)


def _s5_pre_kernel(has_prev, *refs):
    x, refs = _residual_in(has_prev, refs)
    mod_ref, ng_ref = refs[:2]
    outs = refs[2:]
    if has_prev:
        outs[0][0] = x
        outs = outs[1:]
    m = mod_ref[0, 0]
    outs[0][0] = _norm_mod(x, ng_ref[...], m[1:2], m[0:1])


def _s5_pre(x, prev, modtab, norm_g, tm, n_ctx):
    b, t, d = x.shape
    seg_tiles = n_ctx // tm
    has_prev = prev is not None
    args, specs = [x], [_row_spec(tm, d)]
    if has_prev:
        args += list(prev)
        specs += [_row_spec(tm, d), _mod_spec(d, 1, seg_tiles)]
    args += [modtab, norm_g]
    specs += [_mod_spec(d, 6, seg_tiles), _const_spec((1, d))]
    n_out = 2 if has_prev else 1
    outs = pl.pallas_call(
        functools.partial(_s5_pre_kernel, has_prev),
        grid=(b, t // tm),
        in_specs=specs,
        out_specs=[_row_spec(tm, d)] * n_out,
        out_shape=[jax.ShapeDtypeStruct((b, t, d), F32)] * n_out,
        compiler_params=_cparams(("parallel", "parallel")),
    )(*args)
    if has_prev:
        return outs[0], outs[1]
    return x, outs[0]


def _s5_kernel(u_ref, tm_ref, wz_ref, wof_ref, wob_ref, ar_ref, ai_ref, d_ref, z_ref,
               zr_s, zi_s, xpf_s, xpb_s, *, n_sub, n_sub_ctx, batch, gb):
    half = 2 * S5_STATE
    for g in range(gb):
        zz = _dot(u_ref[g].astype(BF16), wz_ref[g])
        zr_s[g] = zz[:, :half]
        zi_s[g] = zz[:, half:]
    fwd_lane = lax.broadcasted_iota(jnp.int32, (batch, half), 1) < S5_STATE

    def body(i, carry):
        sb = jnp.where(i < n_sub_ctx, n_sub_ctx - 1 - i, n_sub - 1 - (i - n_sub_ctx))
        of = pl.multiple_of(i * batch, batch)
        ob = pl.multiple_of(sb * batch, batch)
        new = []
        for g in range(gb):
            xr, xi = carry[g]
            xpf_s[g, pl.ds(of, batch), 0:half] = xr
            xpf_s[g, pl.ds(of, batch), half:2 * half] = xi
            xpb_s[g, pl.ds(ob, batch), 0:half] = xr
            xpb_s[g, pl.ds(ob, batch), half:2 * half] = xi
            zr = jnp.where(fwd_lane, zr_s[g, pl.ds(of, batch), :], zr_s[g, pl.ds(ob, batch), :])
            zi = jnp.where(fwd_lane, zi_s[g, pl.ds(of, batch), :], zi_s[g, pl.ds(ob, batch), :])
            ar = ar_ref[g]
            ai = ai_ref[g]
            new.append((ar * xr - ai * xi + zr, ar * xi + ai * xr + zi))
        return tuple(new)

    zero = jnp.zeros((batch, half), F32)
    lax.fori_loop(0, n_sub, body, tuple((zero, zero) for _ in range(gb)))
    for g in range(gb):
        u = u_ref[g]
        y = (_dot(u.astype(BF16), tm_ref[g]) + _dot(xpf_s[g].astype(BF16), wof_ref[g])
             + _dot(xpb_s[g].astype(BF16), wob_ref[g]) + d_ref[g] * u)
        z_ref[g] = _gelu_tanh(y).astype(BF16)


def _s5_scan(u_g, tmat, wz, wof, wob, ar, ai, dt, n_sub_ctx, batch, gb):
    groups, rows, w = u_g.shape
    n_sub = rows // batch
    half = 2 * S5_STATE
    gspec = lambda shape: pl.BlockSpec((gb,) + shape, lambda g: (g,) + (0,) * len(shape))
    return pl.pallas_call(
        functools.partial(_s5_kernel, n_sub=n_sub, n_sub_ctx=n_sub_ctx, batch=batch, gb=gb),
        grid=(groups // gb,),
        in_specs=[gspec((rows, w)), gspec((w, w)), gspec((w, w)), gspec((w, w)), gspec((w, w)),
                  gspec((1, half)), gspec((1, half)), gspec((1, w))],
        out_specs=gspec((rows, w)),
        out_shape=jax.ShapeDtypeStruct((groups, rows, w), BF16),
        scratch_shapes=[pltpu.VMEM((gb, rows, half), F32), pltpu.VMEM((gb, rows, half), F32),
                        pltpu.VMEM((gb, rows, w), F32), pltpu.VMEM((gb, rows, w), F32)],
        compiler_params=_cparams(("parallel",)),
    )(u_g, tmat, wz, wof, wob, ar, ai, dt)


def _s5_tables(a_re, a_im, log_dt, b_re, b_im, c_re, c_im, d_skip):
    hp = lax.Precision.HIGHEST
    j_sub = S5_SUB
    lam_re = jnp.minimum(a_re, -1e-4)
    lam_im = a_im
    dt = jnp.exp(log_dt)[..., None]
    mag = jnp.exp(lam_re * dt)
    bar_re = mag * jnp.cos(lam_im * dt)
    bar_im = mag * jnp.sin(lam_im * dt)
    den = lam_re * lam_re + lam_im * lam_im
    nr, ni = bar_re - 1.0, bar_im
    k_re = (nr * lam_re + ni * lam_im) / den
    k_im = (ni * lam_re - nr * lam_im) / den
    bb_re = k_re[..., None] * b_re[None] - k_im[..., None] * b_im[None]
    bb_im = k_re[..., None] * b_im[None] + k_im[..., None] * b_re[None]
    jj = jnp.arange(j_sub + 1, dtype=F32)[:, None, None, None]
    pmag = jnp.exp(jj * (lam_re * dt)[None])
    pw_re = pmag * jnp.cos(jj * (lam_im * dt)[None])
    pw_im = pmag * jnp.sin(jj * (lam_im * dt)[None])
    pb_re = pw_re[..., None] * bb_re[None] - pw_im[..., None] * bb_im[None]
    pb_im = pw_re[..., None] * bb_im[None] + pw_im[..., None] * bb_re[None]
    cp_re = c_re[None] * pw_re[:, :, :, None, :] - c_im[None] * pw_im[:, :, :, None, :]
    cp_im = c_re[None] * pw_im[:, :, :, None, :] + c_im[None] * pw_re[:, :, :, None, :]
    kern = (jnp.einsum('dgcp,jdgpk->jdgkc', c_re, pb_re[:j_sub], precision=hp)
            - jnp.einsum('dgcp,jdgpk->jdgkc', c_im, pb_im[:j_sub], precision=hp))
    groups = a_re.shape[1]
    k_w = S5_GROUP
    pos = jnp.arange(j_sub)
    lag = pos[None, :] - pos[:, None]
    kf = jnp.where((lag >= 0)[:, :, None, None, None], kern[jnp.clip(lag, 0, j_sub - 1), 0], 0.0)
    kb = jnp.where((lag <= 0)[:, :, None, None, None], kern[jnp.clip(-lag, 0, j_sub - 1), 1], 0.0)
    tmat = (kf + kb).transpose(2, 0, 3, 1, 4).reshape(groups, j_sub * k_w, j_sub * k_w)
    def zcols(pb, d, order):
        return pb[order, d].transpose(1, 0, 3, 2).reshape(groups, j_sub * k_w, S5_STATE)
    rev = pos[::-1]
    wz = jnp.concatenate([zcols(pb_re, 0, rev), zcols(pb_re, 1, pos),
                          zcols(pb_im, 0, rev), zcols(pb_im, 1, pos)], axis=-1)
    def orows(cp, d, order):
        return cp[order, d].transpose(1, 3, 0, 2).reshape(groups, S5_STATE, j_sub * k_w)
    zeros = jnp.zeros((groups, S5_STATE, j_sub * k_w), F32)
    wof = jnp.concatenate([orows(cp_re, 0, pos + 1), zeros, -orows(cp_im, 0, pos + 1), zeros], axis=1)
    wob = jnp.concatenate([zeros, orows(cp_re, 1, j_sub - pos), zeros, -orows(cp_im, 1, j_sub - pos)], axis=1)
    ar = jnp.concatenate([pw_re[j_sub, 0], pw_re[j_sub, 1]], axis=-1)[:, None, :]
    ai = jnp.concatenate([pw_im[j_sub, 0], pw_im[j_sub, 1]], axis=-1)[:, None, :]
    dt_tile = jnp.tile(d_skip.reshape(groups, 1, k_w), (1, 1, j_sub))
    return tmat.astype(BF16), wz.astype(BF16), wof.astype(BF16), wob.astype(BF16), ar, ai, dt_tile


def _glu_kernel(x_ref, z_ref, mod_ref, w_ref, b_ref, ng2_ref, wrh_ref, wrl_ref, xo_ref, f_ref, aff_ref):
    d = x_ref.shape[-1]
    ag = _dot(z_ref[0], w_ref[...]) + b_ref[...]
    delta = ag[:, :d] * _sigmoid(ag[:, d:])
    _post_mixer(x_ref[0], delta, mod_ref[0, 0], ng2_ref[...], wrh_ref[...], wrl_ref[...], xo_ref, f_ref, aff_ref)


def _glu(x, z, modtab, glu_w, glu_b, ng2, wrh, wrl, tm, n_ctx):
    b, t, d = x.shape
    out_shape, out_specs = _post_specs(b, t, d, tm)
    return pl.pallas_call(
        _glu_kernel,
        grid=(b, t // tm),
        in_specs=[_row_spec(tm, d), _row_spec(tm, d), _mod_spec(d, 6, n_ctx // tm),
                  _const_spec(glu_w.shape), _const_spec((1, 2 * d)), _const_spec((1, d)),
                  _const_spec(wrh.shape), _const_spec(wrl.shape)],
        out_specs=out_specs,
        out_shape=out_shape,
        compiler_params=_cparams(("parallel", "parallel")),
    )(x, z, modtab, glu_w, glu_b, ng2, wrh, wrl)


def _cumsum_lanes(v):
    n = v.shape[-1]
    lane = lax.broadcasted_iota(jnp.int32, v.shape, 1)
    sh = 1
    while sh < n:
        v = v + jnp.where(lane >= sh, pltpu.roll(v, sh, 1), 0.0)
        sh *= 2
    return v


def _select_segment(aff, cap, base):
    bits = pltpu.bitcast(aff, jnp.int32)
    thr = jnp.zeros((aff.shape[0], 1), jnp.int32)
    for bit in range(30, -1, -1):
        cand = thr | (1 << bit)
        cnt = jnp.sum(jnp.where(bits >= cand, 1.0, 0.0), axis=1, keepdims=True)
        thr = jnp.where(cnt >= cap, cand, thr)
    gt = bits > thr
    eq = bits == thr
    need = cap - jnp.sum(jnp.where(gt, 1.0, 0.0), axis=1, keepdims=True)
    eqf = jnp.where(eq, 1.0, 0.0)
    eq_before = _cumsum_lanes(eqf) - eqf
    sel = jnp.where(gt, 1.0, jnp.where(eq, jnp.where(eq_before < need, 1.0, 0.0), 0.0))
    pos = _cumsum_lanes(sel) - sel
    return jnp.where(sel > 0.0, pos.astype(jnp.int32) + base, -1)


def _select_kernel(aff_ref, slot_ref, *, n_ctx, cap_ctx, cap_lat):
    slot_ref[0, :, 0:n_ctx] = _select_segment(aff_ref[0, :, 0:n_ctx], cap_ctx, 0)
    slot_ref[0, :, n_ctx:] = _select_segment(aff_ref[0, :, n_ctx:], cap_lat, cap_ctx)


def _select(aff_t, n_ctx, cap_ctx, cap_lat):
    b, e, t = aff_t.shape
    return pl.pallas_call(
        functools.partial(_select_kernel, n_ctx=n_ctx, cap_ctx=cap_ctx, cap_lat=cap_lat),
        grid=(b,),
        in_specs=[pl.BlockSpec((1, e, t), lambda i: (i, 0, 0))],
        out_specs=pl.BlockSpec((1, e, t), lambda i: (i, 0, 0)),
        out_shape=jax.ShapeDtypeStruct((b, e, t), jnp.int32),
        compiler_params=_cparams(("parallel",)),
    )(aff_t)


def _moe_kernel(jlo_ref, jhi_ref, slot_ref, aff_ref, f_ref, w1_ref, w3_ref, w2_ref, o_ref,
                xs_s, ys_s, gate_s, *, tiles, tb, n_ft):
    b = pl.program_id(0)
    e = pl.program_id(1)
    ft = pl.program_id(2)
    n_e = pl.num_programs(1)
    base = (b * n_e + e) * len(tiles)

    def onehot(j, r0, rc):
        rows = lax.broadcasted_iota(jnp.int32, (rc, tb), 0) + r0
        return slot_ref[0, 0, pl.ds(j, 1), :] == rows

    @pl.when((e == 0) & (ft == 0))
    def _():
        o_ref[...] = jnp.zeros_like(o_ref)

    @pl.when(ft == 0)
    def _():
        for c, (r0, rc) in enumerate(tiles):
            def body(j, carry):
                acc, gacc = carry
                oh = onehot(j, r0, rc)
                off = pl.multiple_of(j * tb, tb)
                acc = acc + _dot(jnp.where(oh, 1.0, 0.0).astype(BF16), f_ref[0, pl.ds(off, tb), :])
                gacc = gacc + jnp.sum(jnp.where(oh, aff_ref[0, 0, pl.ds(j, 1), :], 0.0), axis=1, keepdims=True)
                return acc, gacc
            acc, gacc = lax.fori_loop(
                jlo_ref[base + c], jhi_ref[base + c] + 1, body,
                (jnp.zeros((rc, f_ref.shape[-1]), F32), jnp.zeros((rc, 1), F32)))
            xs_s[r0:r0 + rc, :] = acc.astype(BF16)
            gate_s[r0:r0 + rc, :] = gacc
        ys_s[...] = jnp.zeros_like(ys_s)

    xs = xs_s[...]
    hid = (_silu(_dot(xs, w1_ref[0, 0])) * _dot(xs, w3_ref[0, 0])).astype(BF16)
    ys_s[...] += _dot(hid, w2_ref[0, 0])

    @pl.when(ft == n_ft - 1)
    def _():
        for c, (r0, rc) in enumerate(tiles):
            ysg = (ys_s[r0:r0 + rc, :] * gate_s[r0:r0 + rc, :]).astype(BF16)

            def body(j, carry):
                oh = jnp.where(onehot(j, r0, rc), 1.0, 0.0).astype(BF16)
                off = pl.multiple_of(j * tb, tb)
                o_ref[0, pl.ds(off, tb), :] += _dot_tn(oh, ysg)
                return carry
            lax.fori_loop(jlo_ref[base + c], jhi_ref[base + c] + 1, body, 0)


def _moe(layer, jlo, jhi, slot, aff_t, f, w1, w3, w2, tiles, tb, tf):
    b, t, d = f.shape
    n_e = w1.shape[1]
    ff = w1.shape[-1]
    n_ft = ff // tf
    rows = tiles[-1][0] + tiles[-1][1]
    grid_spec = pltpu.PrefetchScalarGridSpec(
        num_scalar_prefetch=2,
        grid=(b, n_e, n_ft),
        in_specs=[
            pl.BlockSpec((1, 1, t // tb, tb), lambda bi, e, k, *_: (bi, e, 0, 0)),
            pl.BlockSpec((1, 1, t // tb, tb), lambda bi, e, k, *_: (bi, e, 0, 0)),
            pl.BlockSpec((1, t, d), lambda bi, e, k, *_: (bi, 0, 0)),
            pl.BlockSpec((1, 1, d, tf), lambda bi, e, k, *_: (layer, e, 0, k)),
            pl.BlockSpec((1, 1, d, tf), lambda bi, e, k, *_: (layer, e, 0, k)),
            pl.BlockSpec((1, 1, tf, d), lambda bi, e, k, *_: (layer, e, k, 0)),
        ],
        out_specs=pl.BlockSpec((1, t, d), lambda bi, e, k, *_: (bi, 0, 0), pipeline_mode=pl.Buffered(1)),
        scratch_shapes=[pltpu.VMEM((rows, d), BF16), pltpu.VMEM((rows, d), F32), pltpu.VMEM((rows, 1), F32)],
    )
    return pl.pallas_call(
        functools.partial(_moe_kernel, tiles=tiles, tb=tb, n_ft=n_ft),
        grid_spec=grid_spec,
        out_shape=jax.ShapeDtypeStruct((b, t, d), F32),
        compiler_params=_cparams(("parallel", "arbitrary", "arbitrary")),
    )(jlo, jhi, slot.reshape(b, n_e, t // tb, tb), aff_t.reshape(b, n_e, t // tb, tb), f, w1, w3, w2)


def _tile_bounds(slot, tiles, tb):
    b, e, t = slot.shape
    blk = jnp.arange(t, dtype=jnp.int32) // tb
    lo, hi = [], []
    for r0, rc in tiles:
        inside = (slot >= r0) & (slot < r0 + rc)
        lo.append(jnp.min(jnp.where(inside, blk, t // tb - 1), axis=-1))
        hi.append(jnp.max(jnp.where(inside, blk, 0), axis=-1))
    lo = jnp.stack(lo, axis=-1)
    hi = jnp.maximum(jnp.stack(hi, axis=-1), lo)
    return lo.reshape(-1).astype(jnp.int32), hi.reshape(-1).astype(jnp.int32)


def _route_and_moe(layer, aff, f, w1, w3, w2, n_ctx, tb, tf):
    b, t, _ = f.shape
    aff_t = jnp.swapaxes(aff[:, :, :N_EXPERTS], 1, 2)
    cap_ctx = (CAPACITY_FACTOR * n_ctx) // N_EXPERTS
    cap_lat = (CAPACITY_FACTOR * (t - n_ctx)) // N_EXPERTS
    slot = _select(aff_t, n_ctx, cap_ctx, cap_lat)
    tl = min(128, cap_lat)
    tiles = ((0, cap_ctx),) + tuple((cap_ctx + i * tl, tl) for i in range(cap_lat // tl))
    jlo, jhi = _tile_bounds(slot, tiles, tb)
    return _moe(layer, jlo, jhi, slot, aff_t, f, w1, w3, w2, tiles, tb, tf)


def _final_kernel(x_ref, moe_ref, gp_ref, g_ref, o_ref):
    x = x_ref[0] + gp_ref[0, 0] * moe_ref[0]
    o_ref[0] = x * lax.rsqrt(jnp.mean(x * x, axis=-1, keepdims=True) + EPS) * g_ref[...]


def _final(x, moe, gprev, g, tm, n_ctx):
    b, t, d = x.shape
    lead = n_ctx // tm
    return pl.pallas_call(
        _final_kernel,
        grid=(b, (t - n_ctx) // tm),
        in_specs=[_row_spec(tm, d, lead), _row_spec(tm, d, lead),
                  pl.BlockSpec((1, 1, 1, d), lambda bi, i: (bi, 1, 0, 0)), _const_spec((1, d))],
        out_specs=_row_spec(tm, d),
        out_shape=jax.ShapeDtypeStruct((b, t - n_ctx, d), F32),
        compiler_params=_cparams(("parallel", "parallel")),
    )(x, moe, gprev, g)


def _rope_tables(n_ctx, n_lat):
    n_rows = n_lat // GRID_W
    row = jnp.repeat(jnp.arange(n_rows, dtype=F32), GRID_W)
    col = jnp.tile(jnp.arange(GRID_W, dtype=F32), n_rows)
    inv = ROPE_THETA ** (-jnp.arange(0, ROPE_AXIS_DIM, 2, dtype=F32) / ROPE_AXIS_DIM)
    ang_r = row[:, None] * inv[None, :]
    ang_c = col[:, None] * inv[None, :]
    cos = jnp.concatenate([jnp.cos(ang_r), jnp.cos(ang_r), jnp.cos(ang_c), jnp.cos(ang_c)], axis=-1)
    sin = jnp.concatenate([-jnp.sin(ang_r), jnp.sin(ang_r), -jnp.sin(ang_c), jnp.sin(ang_c)], axis=-1)
    cos = jnp.concatenate([jnp.ones((n_ctx, HEAD_DIM), F32), cos], axis=0)
    sin = jnp.concatenate([jnp.zeros((n_ctx, HEAD_DIM), F32), sin], axis=0)
    reps = SEG_W // HEAD_DIM
    return jnp.tile(cos, (1, reps)), jnp.tile(sin, (1, reps))


def _extend_in_proj(w_in):
    base = 4 * RET_WIDTH + ATT_WIDTH
    ak = w_in[:, base:base + ATT_KV_WIDTH].reshape(-1, ATT_KV_HEADS, 1, HEAD_DIM)
    av = w_in[:, base + ATT_KV_WIDTH:].reshape(-1, ATT_KV_HEADS, 1, HEAD_DIM)
    rep = lambda w: jnp.broadcast_to(w, (w.shape[0], ATT_KV_HEADS, ATT_GROUP, HEAD_DIM)).reshape(w.shape[0], -1)
    return jnp.concatenate([w_in[:, :base], rep(ak), rep(av)], axis=-1)


def kernel(x, c, ctx, c_ctx, mod_w, mod_b, norm_g, mix_in_w, mix_out_w, ret_log_rate, ret_gn_g, qk_norm_g,
           s5_a_re, s5_a_im, s5_log_dt, s5_b_re, s5_b_im, s5_c_re, s5_c_im, s5_d, s5_glu_w, s5_glu_b,
           moe_router_w, moe_w1, moe_w3, moe_w2, final_norm_g):
    batch, n_lat, d = x.shape
    n_ctx = ctx.shape[1]
    t = n_ctx + n_lat
    depth = mod_w.shape[0]
    assert batch == 8 and d % 128 == 0 and n_ctx % CHUNK == 0 and n_lat % CHUNK == 0
    tm = 256 if n_ctx % 256 == 0 else 128
    tb = tm
    tf = 512

    xs = jnp.concatenate([ctx, x], axis=1)
    rows = 16
    cond = jnp.zeros((rows, d), F32).at[:batch].set(c).at[batch].set(c_ctx)
    mod_all = _modulation(cond, mod_w, mod_b)
    cos, sin = _rope_tables(n_ctx, n_lat)
    lane_head = jnp.arange(SEG_W) // HEAD_DIM
    bd = (lane_head[:, None] == lane_head[None, :]).astype(BF16)
    w1 = moe_w1.astype(BF16)
    w3 = moe_w3.astype(BF16)
    w2 = moe_w2.astype(BF16)

    prev = None
    for layer in range(depth):
        m = mod_all[layer]
        lat = m[:batch].reshape(batch, 1, 6, d)
        cx = jnp.broadcast_to(m[batch].reshape(1, 1, 6, d), (batch, 1, 6, d))
        modtab = jnp.concatenate([cx, lat], axis=1)
        ng1 = norm_g[layer, 0].reshape(1, d)
        ng2 = norm_g[layer, 1].reshape(1, d)
        wr = jnp.zeros((d, ROUTER_LANES), F32).at[:, :N_EXPERTS].set(moe_router_w[layer])
        wrh = wr.astype(BF16)
        wrl = (wr - wrh.astype(F32)).astype(BF16)
        i = layer // 2
        if layer % 2 == 0:
            w_ext = _extend_in_proj(mix_in_w[i]).astype(BF16)
            reps = SEG_W // HEAD_DIM
            qkg = jnp.stack([jnp.tile(qk_norm_g[i, 0], reps), jnp.tile(qk_norm_g[i, 1], reps)])
            xs, (rq, rk, rv, rg, aq, ak4, av4) = _in_proj(xs, prev, modtab, ng1, w_ext, cos, sin, qkg, bd,
                                                         tm, n_ctx)
            rate = jnp.repeat(ret_log_rate[i], HEAD_DIM, axis=-1)
            rate_h = jnp.broadcast_to(ret_log_rate[i][:, :, None], (2, RET_HEADS, CHUNK))
            o_f, o_b = _retention(rq, rk, rv, rate, rate_h, bd, n_ctx)
            att = _attention(aq, ak4, av4, tm, tm, n_ctx)
            xs, f, aff = _out_proj(xs, o_f, o_b, rg, att, modtab, ret_gn_g[i].reshape(1, SEG_W),
                                   mix_out_w[i].astype(BF16), ng2, wrh, wrl, bd, tm, n_ctx)
        else:
            xs, u = _s5_pre(xs, prev, modtab, ng1, tm, n_ctx)
            groups = d // S5_GROUP
            n_sub = t // S5_SUB
            u_g = (u.reshape(batch, n_sub, S5_SUB, groups, S5_GROUP).transpose(3, 1, 0, 2, 4)
                   .reshape(groups, n_sub * batch, S5_SUB * S5_GROUP))
            tabs = _s5_tables(s5_a_re[i], s5_a_im[i], s5_log_dt[i], s5_b_re[i], s5_b_im[i],
                              s5_c_re[i], s5_c_im[i], s5_d[i])
            z_g = _s5_scan(u_g, *tabs, n_ctx // S5_SUB, batch, 2)
            z = (z_g.reshape(groups, n_sub, batch, S5_SUB, S5_GROUP).transpose(2, 1, 3, 0, 4)
                 .reshape(batch, t, d))
            xs, f, aff = _glu(xs, z, modtab, s5_glu_w[i].astype(BF16), s5_glu_b[i].reshape(1, 2 * d),
                              ng2, wrh, wrl, tm, n_ctx)
        moe = _route_and_moe(layer, aff, f, w1, w3, w2, n_ctx, tb, tf)
        prev = (moe, modtab[:, :, 5:6, :])
    return _final(xs, prev[0], prev[1], final_norm_g.reshape(1, d), tm, n_ctx)
```

```python
import functools
import math

import jax
import jax.numpy as jnp
from jax import lax
from jax.experimental import pallas as pl
from jax.experimental.pallas import tpu as pltpu

F32 = jnp.float32
BF16 = jnp.bfloat16

GRID_W = 64
HEAD_DIM = 64
RET_HEADS = 8
ATT_HEADS = 8
ATT_KV_HEADS = 2
ATT_GROUP = ATT_HEADS // ATT_KV_HEADS
RET_WIDTH = RET_HEADS * HEAD_DIM
ATT_WIDTH = ATT_HEADS * HEAD_DIM
ATT_KV_WIDTH = ATT_KV_HEADS * HEAD_DIM
CHUNK = 128
ROPE_THETA = 10000.0
ROPE_AXIS_DIM = HEAD_DIM // 2
S5_GROUP = 16
S5_STATE = 64
S5_SUB = 16
N_EXPERTS = 16
CAPACITY_FACTOR = 2
EPS = 1e-6
SEG_W = 512
LOG2_E = math.log2(math.e)
ROUTER_LANES = 128
MOE_GATHER_ROWS = 128
MOE_GATHER_WINDOW = 6
MOE_SCATTER_ROWS = 256
MOE_SCATTER_WINDOW = 10

V7X_VMEM_BYTES = 64 * 1024 * 1024
VMEM_LIMIT = V7X_VMEM_BYTES - 8 * 1024 * 1024


def _cparams(sem):
    return pltpu.CompilerParams(dimension_semantics=sem, vmem_limit_bytes=VMEM_LIMIT)


def _dot(a, b):
    return jnp.dot(a, b, preferred_element_type=F32)


def _dot_nt(a, b):
    return lax.dot_general(a, b, (((1,), (1,)), ((), ())), preferred_element_type=F32)


def _dot_tn(a, b):
    return lax.dot_general(a, b, (((0,), (0,)), ((), ())), preferred_element_type=F32)


def _split(v):
    hi = v.astype(BF16)
    lo = (v - hi.astype(F32)).astype(BF16)
    return hi, lo


def _segsum(v, bd):
    hi, lo = _split(v)
    return _dot(hi, bd) + _dot(lo, bd)


def _sigmoid(v):
    return 1.0 / (1.0 + jnp.exp(-v))


def _silu(v):
    return v * _sigmoid(v)


def _gelu_tanh(v):
    c = math.sqrt(2.0 / math.pi)
    return 0.5 * v * (1.0 + jnp.tanh(c * (v + 0.044715 * (v * v * v))))


def _norm_mod(x, g, scale, shift):
    ms = jnp.mean(x * x, axis=-1, keepdims=True)
    return x * lax.rsqrt(ms + EPS) * g * (1.0 + scale) + shift


def _rope(x, cos, sin_signed):
    w = x.shape[-1]
    lane = lax.broadcasted_iota(jnp.int32, x.shape, 1)
    first = (lane % 32) < 16
    partner = jnp.where(first, pltpu.roll(x, w - 16, 1), pltpu.roll(x, 16, 1))
    return x * cos + partner * sin_signed


def _mod_kernel(c_ref, w_ref, b_ref, o_ref):
    s = _silu(c_ref[...]).astype(BF16)
    o_ref[0] = _dot(s, w_ref[0].astype(BF16)) + b_ref[0]


def _modulation(cond, mod_w, mod_b):
    depth, d, n6 = mod_w.shape
    rows = cond.shape[0]
    tn = n6 // 6
    return pl.pallas_call(
        _mod_kernel,
        grid=(depth, n6 // tn),
        in_specs=[
            pl.BlockSpec((rows, d), lambda l, j: (0, 0)),
            pl.BlockSpec((1, d, tn), lambda l, j: (l, 0, j)),
            pl.BlockSpec((1, 1, tn), lambda l, j: (l, 0, j)),
        ],
        out_specs=pl.BlockSpec((1, rows, tn), lambda l, j: (l, 0, j)),
        out_shape=jax.ShapeDtypeStruct((depth, rows, n6), F32),
        compiler_params=_cparams(("parallel", "parallel")),
    )(cond, mod_w, mod_b.reshape(depth, 1, n6))


def _residual_in(has_prev, refs):
    if has_prev:
        x_ref, moe_ref, gp_ref = refs[:3]
        return x_ref[0] + gp_ref[0, 0] * moe_ref[0], refs[3:]
    return refs[0][0], refs[1:]


def _in_kernel(has_prev, *refs):
    x, refs = _residual_in(has_prev, refs)
    mod_ref, ng_ref, w_ref, cos_ref, sin_ref, qkg_ref, bd_ref = refs[:7]
    outs = refs[7:]
    if has_prev:
        outs[0][0] = x
        outs = outs[1:]
    rq_ref, rk_ref, rv_ref, rg_ref, aq_ref, ak_ref, av_ref = outs
    m = mod_ref[0, 0]
    h = _norm_mod(x, ng_ref[...], m[1:2], m[0:1]).astype(BF16)
    cos = cos_ref[...]
    sin = sin_ref[...]
    bd = bd_ref[...]

    def seg(i):
        return _dot(h, w_ref[:, i * SEG_W:(i + 1) * SEG_W])

    def headnorm(v, g):
        ms = _segsum(v * v, bd) * (1.0 / HEAD_DIM)
        return v * lax.rsqrt(ms + EPS) * g

    scale = HEAD_DIM ** -0.5
    rq_ref[0] = _rope(seg(0), cos, sin).astype(BF16)
    rk_ref[0] = _rope(seg(1) * scale, cos, sin).astype(BF16)
    rv_ref[0] = seg(2).astype(BF16)
    rg_ref[0] = seg(3)
    aq_ref[0] = (_rope(headnorm(seg(4), qkg_ref[0:1]), cos, sin) * (scale * LOG2_E)).astype(BF16)
    ak_ref[0] = _rope(headnorm(seg(5), qkg_ref[1:2]), cos, sin).astype(BF16)
    av_ref[0] = _dot(h, w_ref[:, 6 * SEG_W:]).astype(BF16)


def _row_spec(tm, d, lead=0):
    return pl.BlockSpec((1, tm, d), lambda b, i: (b, i + lead, 0))


def _mod_spec(d, n, seg_tiles):
    return pl.BlockSpec((1, 1, n, d), lambda b, i: (b, jnp.where(i >= seg_tiles, 1, 0), 0, 0))


def _const_spec(shape):
    nd = len(shape)
    return pl.BlockSpec(shape, lambda b, i: (0,) * nd)


def _in_proj(x, prev, modtab, norm_g, w_ext, cos, sin, qkg, bd, tm, n_ctx):
    b, t, d = x.shape
    seg_tiles = n_ctx // tm
    has_prev = prev is not None
    args, specs = [x], [_row_spec(tm, d)]
    if has_prev:
        moe, gprev = prev
        args += [moe, gprev]
        specs += [_row_spec(tm, d), _mod_spec(d, 1, seg_tiles)]
    args += [modtab, norm_g, w_ext, cos, sin, qkg, bd]
    specs += [
        _mod_spec(d, 6, seg_tiles),
        _const_spec((1, d)),
        _const_spec(w_ext.shape),
        pl.BlockSpec((tm, SEG_W), lambda b, i: (i, 0)),
        pl.BlockSpec((tm, SEG_W), lambda b, i: (i, 0)),
        _const_spec((2, SEG_W)),
        _const_spec((SEG_W, SEG_W)),
    ]
    out_shape, out_specs = [], []
    if has_prev:
        out_shape.append(jax.ShapeDtypeStruct((b, t, d), F32))
        out_specs.append(_row_spec(tm, d))
    for dt, width in ((BF16, SEG_W), (BF16, SEG_W), (BF16, SEG_W), (F32, SEG_W), (BF16, SEG_W), (BF16, SEG_W),
                      (BF16, ATT_KV_HEADS * 2 * HEAD_DIM)):
        out_shape.append(jax.ShapeDtypeStruct((b, t, width), dt))
        out_specs.append(_row_spec(tm, width))
    outs = pl.pallas_call(
        functools.partial(_in_kernel, has_prev),
        grid=(b, t // tm),
        in_specs=specs,
        out_specs=out_specs,
        out_shape=out_shape,
        compiler_params=_cparams(("parallel", "parallel")),
    )(*args)
    if has_prev:
        return outs[0], outs[1:]
    return x, outs


def _ret_kernel(qf_ref, kf_ref, vf_ref, qb_ref, kb_ref, vb_ref, rate_ref, rate_h_ref, bd_ref,
                of_ref, ob_ref, sf_s, sb_s, dm_s, dq_s, dk_s, dc_s):
    s = pl.program_id(1)
    c = CHUNK

    @pl.when(s == 0)
    def _():
        sf_s[...] = jnp.zeros_like(sf_s)
        sb_s[...] = jnp.zeros_like(sb_s)
        pos = lax.broadcasted_iota(jnp.int32, (c, SEG_W), 0).astype(F32)
        lgf = -jnp.exp(rate_ref[0:1, :])
        lgb = -jnp.exp(rate_ref[1:2, :])
        dq_s[0] = jnp.exp((pos + 1.0) * lgf)
        dk_s[0] = jnp.exp((c - 1.0 - pos) * lgf)
        dc_s[0] = jnp.exp(float(c) * lgf)
        dq_s[1] = jnp.exp((c - pos) * lgb)
        dk_s[1] = jnp.exp(pos * lgb)
        dc_s[1] = jnp.exp(float(c) * lgb)
        ii = lax.broadcasted_iota(jnp.int32, (c, c), 0)
        jj = lax.broadcasted_iota(jnp.int32, (c, c), 1)
        for h in range(RET_HEADS):
            lf = -jnp.exp(rate_h_ref[0, h:h + 1, :])
            lb = -jnp.exp(rate_h_ref[1, h:h + 1, :])
            mf = ii >= jj
            mb = jj > ii
            dm_s[0, h] = jnp.where(mf, jnp.exp(jnp.where(mf, ii - jj, 0).astype(F32) * lf), 0.0)
            dm_s[1, h] = jnp.where(mb, jnp.exp(jnp.where(mb, jj - ii, 0).astype(F32) * lb), 0.0)

    bdm = bd_ref[...].astype(F32)
    head_of_lane = lax.broadcasted_iota(jnp.int32, (1, SEG_W // 2), 1) // HEAD_DIM

    def direction(d, q_ref, k_ref, v_ref, o_ref, st):
        q = q_ref[0]
        k = k_ref[0]
        v = v_ref[0]
        state = st[...]
        qd = (q.astype(F32) * dq_s[d]).astype(BF16)
        inter = _dot(qd, state.astype(BF16))
        for half in range(2):
            sl = slice(half * 256, half * 256 + 256)
            qh, kh, vh = q[:, sl], k[:, sl], v[:, sl]
            acc = inter[:, sl]
            for hh in range(4):
                msk = head_of_lane == hh
                sc = _dot_nt(jnp.where(msk, qh, jnp.zeros_like(qh)), kh) * dm_s[d, half * 4 + hh]
                acc = acc + _dot(sc.astype(BF16), jnp.where(msk, vh, jnp.zeros_like(vh)))
            o_ref[0, :, sl] = acc
        kd = (k.astype(F32) * dk_s[d]).astype(BF16)
        st[...] = state * dc_s[d] + bdm * _dot_tn(kd, v)

    direction(0, qf_ref, kf_ref, vf_ref, of_ref, sf_s)
    direction(1, qb_ref, kb_ref, vb_ref, ob_ref, sb_s)


def _retention(rq, rk, rv, rate, rate_h, bd, n_ctx):
    b, t, w = rq.shape
    nc = t // CHUNK
    lc = n_ctx // CHUNK

    def fwd(bi, s):
        return (bi, s, 0)

    def bwd(bi, s):
        return (bi, jnp.where(s < lc, lc - 1 - s, nc - 1 - (s - lc)), 0)

    blk = (1, CHUNK, w)
    return pl.pallas_call(
        _ret_kernel,
        grid=(b, nc),
        in_specs=[pl.BlockSpec(blk, fwd)] * 3 + [pl.BlockSpec(blk, bwd)] * 3 + [
            _const_spec((2, w)),
            _const_spec((2, RET_HEADS, CHUNK)),
            _const_spec((w, w)),
        ],
        out_specs=[pl.BlockSpec(blk, fwd), pl.BlockSpec(blk, bwd)],
        out_shape=[jax.ShapeDtypeStruct((b, t, w), F32)] * 2,
        scratch_shapes=[
            pltpu.VMEM((w, w), F32), pltpu.VMEM((w, w), F32),
            pltpu.VMEM((2, RET_HEADS, CHUNK, CHUNK), F32),
            pltpu.VMEM((2, CHUNK, w), F32), pltpu.VMEM((2, CHUNK, w), F32),
            pltpu.VMEM((2, 1, w), F32),
        ],
        compiler_params=_cparams(("parallel", "arbitrary")),
    )(rq, rk, rv, rq, rk, rv, rate, rate_h, bd)


ATT_ROW_BLOCK = 128


def _att_kernel(q_ref, k_ref, v_ref, o_ref, qs_s, m_s, acc_s, sa_s, sb_s, *, tq, tk, n_ctx, t):
    qi = pl.program_id(2)
    q = q_ref[0]
    gw = ATT_GROUP * HEAD_DIM
    head_of_lane = lax.broadcasted_iota(jnp.int32, (1, gw), 1) // HEAD_DIM
    for h in range(ATT_GROUP):
        qs_s[h * tq:(h + 1) * tq, :] = jnp.where(head_of_lane == h, q, jnp.zeros_like(q))
    m_s[...] = jnp.full(m_s.shape, -1e30, F32)
    acc_s[...] = jnp.zeros_like(acc_s)
    nk = jnp.where(qi * tq < n_ctx, n_ctx // tk, t // tk)
    vw = 2 * HEAD_DIM
    ones_lane = lax.broadcasted_iota(jnp.int32, (1, vw), 1) >= HEAD_DIM
    rb = ATT_ROW_BLOCK

    blocks = [slice(r * rb, (r + 1) * rb) for r in range(ATT_GROUP * tq // rb)]

    def scores_into(j, dst):
        kb = k_ref[0, pl.ds(pl.multiple_of(j * tk, tk), tk), :]
        for rows in blocks:
            dst[rows, :] = _dot_nt(qs_s[rows, :], kb)

    def consume(j, src):
        vb = v_ref[0, pl.ds(pl.multiple_of(j * tk, tk), tk), :]
        vext = jnp.where(ones_lane, jnp.ones_like(vb), vb)
        for rows in blocks:
            sc = src[rows, :]
            m_prev = m_s[rows, :]
            m_new = jnp.maximum(m_prev, jnp.max(sc, axis=1, keepdims=True))
            p = jnp.exp2(sc - jnp.concatenate([m_new] * (tk // vw), axis=1)).astype(BF16)
            acc_s[rows, :] = jnp.exp2(m_prev - m_new) * acc_s[rows, :] + _dot(p, vext)
            m_s[rows, :] = m_new

    scores_into(0, sa_s)

    def body(i, carry):
        j = 2 * i
        scores_into(j + 1, sb_s)
        consume(j, sa_s)
        scores_into(jnp.minimum(j + 2, nk - 1), sa_s)
        consume(j + 1, sb_s)
        return carry

    lax.fori_loop(0, nk // 2, body, 0)

    @pl.when(nk % 2 == 1)
    def _():
        consume(nk - 1, sa_s)

    low = lax.broadcasted_iota(jnp.int32, (tq, vw), 1) < HEAD_DIM
    for pair in range(ATT_GROUP // 2):
        a0 = acc_s[(2 * pair) * tq:(2 * pair + 1) * tq, :]
        a1 = acc_s[(2 * pair + 1) * tq:(2 * pair + 2) * tq, :]
        even = a0 / pltpu.roll(a0, HEAD_DIM, 1)
        odd = pltpu.roll(a1, HEAD_DIM, 1) / a1
        o_ref[0, :, pair * vw:(pair + 1) * vw] = jnp.where(low, even, odd).astype(BF16)


def _attention(aq, ak4, av2, tq, tk, n_ctx):
    b, t, w = aq.shape
    gw = ATT_GROUP * HEAD_DIM
    vw = 2 * HEAD_DIM
    return pl.pallas_call(
        functools.partial(_att_kernel, tq=tq, tk=tk, n_ctx=n_ctx, t=t),
        grid=(b, ATT_KV_HEADS, t // tq),
        in_specs=[
            pl.BlockSpec((1, tq, gw), lambda bi, g, i: (bi, i, g)),
            pl.BlockSpec((1, t, gw), lambda bi, g, i: (bi, 0, g)),
            pl.BlockSpec((1, t, vw), lambda bi, g, i: (bi, 0, g)),
        ],
        out_specs=pl.BlockSpec((1, tq, gw), lambda bi, g, i: (bi, i, g)),
        out_shape=jax.ShapeDtypeStruct((b, t, w), BF16),
        scratch_shapes=[pltpu.VMEM((ATT_GROUP * tq, gw), BF16), pltpu.VMEM((ATT_GROUP * tq, vw), F32),
                        pltpu.VMEM((ATT_GROUP * tq, vw), F32), pltpu.VMEM((ATT_GROUP * tq, tk), F32),
                        pltpu.VMEM((ATT_GROUP * tq, tk), F32)],
        compiler_params=_cparams(("parallel", "parallel", "parallel")),
    )(aq, ak4, av2)


def _post_mixer(x, delta, m, ng2, wrh, wrl, x_ref, f_ref, aff_ref):
    xn = x + m[2:3] * delta
    x_ref[0] = xn
    f = _norm_mod(xn, ng2, m[4:5], m[3:4])
    fh, fl = _split(f)
    f_ref[0] = fh
    logits = _dot(fh, wrh) + _dot(fl, wrh) + _dot(fh, wrl)
    lane = lax.broadcasted_iota(jnp.int32, logits.shape, 1)
    logits = jnp.where(lane < N_EXPERTS, logits, -1e30)
    e = jnp.exp(logits - jnp.max(logits, axis=1, keepdims=True))
    aff_ref[0] = e / jnp.sum(e, axis=1, keepdims=True)


def _out_kernel(x_ref, of_ref, ob_ref, rg_ref, att_ref, mod_ref, gn_ref, w_ref, ng2_ref, wrh_ref, wrl_ref,
                bd_ref, xo_ref, f_ref, aff_ref):
    bd = bd_ref[...]
    o = of_ref[0] + ob_ref[0]
    mu = _segsum(o, bd) * (1.0 / HEAD_DIM)
    oc = o - mu
    var = _segsum(oc * oc, bd) * (1.0 / HEAD_DIM)
    r = _silu(rg_ref[0]) * (oc * lax.rsqrt(var + EPS) * gn_ref[...])
    y = _dot(r.astype(BF16), w_ref[0:RET_WIDTH, :]) + _dot(att_ref[0], w_ref[RET_WIDTH:, :])
    _post_mixer(x_ref[0], y, mod_ref[0, 0], ng2_ref[...], wrh_ref[...], wrl_ref[...], xo_ref, f_ref, aff_ref)


def _post_specs(b, t, d, tm):
    out_shape = [jax.ShapeDtypeStruct((b, t, d), F32), jax.ShapeDtypeStruct((b, t, d), BF16),
                 jax.ShapeDtypeStruct((b, t, ROUTER_LANES), F32)]
    out_specs = [_row_spec(tm, d), _row_spec(tm, d), _row_spec(tm, ROUTER_LANES)]
    return out_shape, out_specs


def _out_proj(x, of, ob, rg, att, modtab, gn_g, w_out, ng2, wrh, wrl, bd, tm, n_ctx):
    b, t, d = x.shape
    out_shape, out_specs = _post_specs(b, t, d, tm)
    return pl.pallas_call(
        _out_kernel,
        grid=(b, t // tm),
        in_specs=[
            _row_spec(tm, d), _row_spec(tm, SEG_W), _row_spec(tm, SEG_W), _row_spec(tm, SEG_W),
            _row_spec(tm, SEG_W), _mod_spec(d, 6, n_ctx // tm), _const_spec((1, SEG_W)),
            _const_spec(w_out.shape), _const_spec((1, d)), _const_spec(wrh.shape), _const_spec(wrl.shape),
            _const_spec((SEG_W, SEG_W)),
        ],
        out_specs=out_specs,
        out_shape=out_shape,
        compiler_params=_cparams(("parallel", "parallel")),
    )(x, of, ob, rg, att, modtab, gn_g, w_out, ng2, wrh, wrl, bd)


def _s5_pre_kernel(has_prev, *refs):
    x, refs = _residual_in(has_prev, refs)
    mod_ref, ng_ref = refs[:2]
    outs = refs[2:]
    if has_prev:
        outs[0][0] = x
        outs = outs[1:]
    m = mod_ref[0, 0]
    outs[0][0] = _norm_mod(x, ng_ref[...], m[1:2], m[0:1])


def _s5_pre(x, prev, modtab, norm_g, tm, n_ctx):
    b, t, d = x.shape
    seg_tiles = n_ctx // tm
    has_prev = prev is not None
    args, specs = [x], [_row_spec(tm, d)]
    if has_prev:
        args += list(prev)
        specs += [_row_spec(tm, d), _mod_spec(d, 1, seg_tiles)]
    args += [modtab, norm_g]
    specs += [_mod_spec(d, 6, seg_tiles), _const_spec((1, d))]
    n_out = 2 if has_prev else 1
    outs = pl.pallas_call(
        functools.partial(_s5_pre_kernel, has_prev),
        grid=(b, t // tm),
        in_specs=specs,
        out_specs=[_row_spec(tm, d)] * n_out,
        out_shape=[jax.ShapeDtypeStruct((b, t, d), F32)] * n_out,
        compiler_params=_cparams(("parallel", "parallel")),
    )(*args)
    if has_prev:
        return outs[0], outs[1]
    return x, outs[0]


def _s5_kernel(u_ref, tm_ref, wz_ref, wof_ref, wob_ref, ar_ref, ai_ref, d_ref, z_ref,
               zr_s, zi_s, xpf_s, xpb_s, *, n_sub, n_sub_ctx, batch, gb):
    half = 2 * S5_STATE
    for g in range(gb):
        zz = _dot(u_ref[g].astype(BF16), wz_ref[g])
        zr_s[g] = zz[:, :half]
        zi_s[g] = zz[:, half:]
    fwd_lane = lax.broadcasted_iota(jnp.int32, (batch, half), 1) < S5_STATE

    def body(i, carry):
        sb = jnp.where(i < n_sub_ctx, n_sub_ctx - 1 - i, n_sub - 1 - (i - n_sub_ctx))
        of = pl.multiple_of(i * batch, batch)
        ob = pl.multiple_of(sb * batch, batch)
        new = []
        for g in range(gb):
            xr, xi = carry[g]
            xpf_s[g, pl.ds(of, batch), 0:half] = xr
            xpf_s[g, pl.ds(of, batch), half:2 * half] = xi
            xpb_s[g, pl.ds(ob, batch), 0:half] = xr
            xpb_s[g, pl.ds(ob, batch), half:2 * half] = xi
            zr = jnp.where(fwd_lane, zr_s[g, pl.ds(of, batch), :], zr_s[g, pl.ds(ob, batch), :])
            zi = jnp.where(fwd_lane, zi_s[g, pl.ds(of, batch), :], zi_s[g, pl.ds(ob, batch), :])
            ar = ar_ref[g]
            ai = ai_ref[g]
            new.append((ar * xr - ai * xi + zr, ar * xi + ai * xr + zi))
        return tuple(new)

    zero = jnp.zeros((batch, half), F32)
    lax.fori_loop(0, n_sub, body, tuple((zero, zero) for _ in range(gb)))
    for g in range(gb):
        u = u_ref[g]
        y = (_dot(u.astype(BF16), tm_ref[g]) + _dot(xpf_s[g].astype(BF16), wof_ref[g])
             + _dot(xpb_s[g].astype(BF16), wob_ref[g]) + d_ref[g] * u)
        z_ref[g] = _gelu_tanh(y).astype(BF16)


def _s5_scan(u_g, tmat, wz, wof, wob, ar, ai, dt, n_sub_ctx, batch, gb):
    groups, rows, w = u_g.shape
    n_sub = rows // batch
    half = 2 * S5_STATE
    gspec = lambda shape: pl.BlockSpec((gb,) + shape, lambda g: (g,) + (0,) * len(shape))
    return pl.pallas_call(
        functools.partial(_s5_kernel, n_sub=n_sub, n_sub_ctx=n_sub_ctx, batch=batch, gb=gb),
        grid=(groups // gb,),
        in_specs=[gspec((rows, w)), gspec((w, w)), gspec((w, w)), gspec((w, w)), gspec((w, w)),
                  gspec((1, half)), gspec((1, half)), gspec((1, w))],
        out_specs=gspec((rows, w)),
        out_shape=jax.ShapeDtypeStruct((groups, rows, w), BF16),
        scratch_shapes=[pltpu.VMEM((gb, rows, half), F32), pltpu.VMEM((gb, rows, half), F32),
                        pltpu.VMEM((gb, rows, w), F32), pltpu.VMEM((gb, rows, w), F32)],
        compiler_params=_cparams(("parallel",)),
    )(u_g, tmat, wz, wof, wob, ar, ai, dt)


def _s5_tables(a_re, a_im, log_dt, b_re, b_im, c_re, c_im, d_skip):
    hp = lax.Precision.HIGHEST
    j_sub = S5_SUB
    lam_re = jnp.minimum(a_re, -1e-4)
    lam_im = a_im
    dt = jnp.exp(log_dt)[..., None]
    mag = jnp.exp(lam_re * dt)
    bar_re = mag * jnp.cos(lam_im * dt)
    bar_im = mag * jnp.sin(lam_im * dt)
    den = lam_re * lam_re + lam_im * lam_im
    nr, ni = bar_re - 1.0, bar_im
    k_re = (nr * lam_re + ni * lam_im) / den
    k_im = (ni * lam_re - nr * lam_im) / den
    bb_re = k_re[..., None] * b_re[None] - k_im[..., None] * b_im[None]
    bb_im = k_re[..., None] * b_im[None] + k_im[..., None] * b_re[None]
    jj = jnp.arange(j_sub + 1, dtype=F32)[:, None, None, None]
    pmag = jnp.exp(jj * (lam_re * dt)[None])
    pw_re = pmag * jnp.cos(jj * (lam_im * dt)[None])
    pw_im = pmag * jnp.sin(jj * (lam_im * dt)[None])
    pb_re = pw_re[..., None] * bb_re[None] - pw_im[..., None] * bb_im[None]
    pb_im = pw_re[..., None] * bb_im[None] + pw_im[..., None] * bb_re[None]
    cp_re = c_re[None] * pw_re[:, :, :, None, :] - c_im[None] * pw_im[:, :, :, None, :]
    cp_im = c_re[None] * pw_im[:, :, :, None, :] + c_im[None] * pw_re[:, :, :, None, :]
    kern = (jnp.einsum('dgcp,jdgpk->jdgkc', c_re, pb_re[:j_sub], precision=hp)
            - jnp.einsum('dgcp,jdgpk->jdgkc', c_im, pb_im[:j_sub], precision=hp))
    groups = a_re.shape[1]
    k_w = S5_GROUP
    pos = jnp.arange(j_sub)
    lag = pos[None, :] - pos[:, None]
    kf = jnp.where((lag >= 0)[:, :, None, None, None], kern[jnp.clip(lag, 0, j_sub - 1), 0], 0.0)
    kb = jnp.where((lag <= 0)[:, :, None, None, None], kern[jnp.clip(-lag, 0, j_sub - 1), 1], 0.0)
    tmat = (kf + kb).transpose(2, 0, 3, 1, 4).reshape(groups, j_sub * k_w, j_sub * k_w)
    def zcols(pb, d, order):
        return pb[order, d].transpose(1, 0, 3, 2).reshape(groups, j_sub * k_w, S5_STATE)
    rev = pos[::-1]
    wz = jnp.concatenate([zcols(pb_re, 0, rev), zcols(pb_re, 1, pos),
                          zcols(pb_im, 0, rev), zcols(pb_im, 1, pos)], axis=-1)
    def orows(cp, d, order):
        return cp[order, d].transpose(1, 3, 0, 2).reshape(groups, S5_STATE, j_sub * k_w)
    zeros = jnp.zeros((groups, S5_STATE, j_sub * k_w), F32)
    wof = jnp.concatenate([orows(cp_re, 0, pos + 1), zeros, -orows(cp_im, 0, pos + 1), zeros], axis=1)
    wob = jnp.concatenate([zeros, orows(cp_re, 1, j_sub - pos), zeros, -orows(cp_im, 1, j_sub - pos)], axis=1)
    ar = jnp.concatenate([pw_re[j_sub, 0], pw_re[j_sub, 1]], axis=-1)[:, None, :]
    ai = jnp.concatenate([pw_im[j_sub, 0], pw_im[j_sub, 1]], axis=-1)[:, None, :]
    dt_tile = jnp.tile(d_skip.reshape(groups, 1, k_w), (1, 1, j_sub))
    return tmat.astype(BF16), wz.astype(BF16), wof.astype(BF16), wob.astype(BF16), ar, ai, dt_tile


def _glu_kernel(x_ref, z_ref, mod_ref, w_ref, b_ref, ng2_ref, wrh_ref, wrl_ref, xo_ref, f_ref, aff_ref):
    d = x_ref.shape[-1]
    ag = _dot(z_ref[0], w_ref[...]) + b_ref[...]
    delta = ag[:, :d] * _sigmoid(ag[:, d:])
    _post_mixer(x_ref[0], delta, mod_ref[0, 0], ng2_ref[...], wrh_ref[...], wrl_ref[...], xo_ref, f_ref, aff_ref)


def _glu(x, z, modtab, glu_w, glu_b, ng2, wrh, wrl, tm, n_ctx):
    b, t, d = x.shape
    out_shape, out_specs = _post_specs(b, t, d, tm)
    return pl.pallas_call(
        _glu_kernel,
        grid=(b, t // tm),
        in_specs=[_row_spec(tm, d), _row_spec(tm, d), _mod_spec(d, 6, n_ctx // tm),
                  _const_spec(glu_w.shape), _const_spec((1, 2 * d)), _const_spec((1, d)),
                  _const_spec(wrh.shape), _const_spec(wrl.shape)],
        out_specs=out_specs,
        out_shape=out_shape,
        compiler_params=_cparams(("parallel", "parallel")),
    )(x, z, modtab, glu_w, glu_b, ng2, wrh, wrl)


def _cumsum_lanes(v):
    n = v.shape[-1]
    lane = lax.broadcasted_iota(jnp.int32, v.shape, 1)
    sh = 1
    while sh < n:
        v = v + jnp.where(lane >= sh, pltpu.roll(v, sh, 1), 0.0)
        sh *= 2
    return v


def _select_segment(aff, cap, base):
    bits = pltpu.bitcast(aff, jnp.int32)
    thr = jnp.zeros((aff.shape[0], 1), jnp.int32)
    for bit in range(30, -1, -1):
        cand = thr | (1 << bit)
        cnt = jnp.sum(jnp.where(bits >= cand, 1.0, 0.0), axis=1, keepdims=True)
        thr = jnp.where(cnt >= cap, cand, thr)
    gt = bits > thr
    eq = bits == thr
    need = cap - jnp.sum(jnp.where(gt, 1.0, 0.0), axis=1, keepdims=True)
    eqf = jnp.where(eq, 1.0, 0.0)
    eq_before = _cumsum_lanes(eqf) - eqf
    sel = jnp.where(gt, 1.0, jnp.where(eq, jnp.where(eq_before < need, 1.0, 0.0), 0.0))
    pos = _cumsum_lanes(sel) - sel
    return jnp.where(sel > 0.0, pos.astype(jnp.int32) + base, -1)


def _select_kernel(aff_ref, slot_ref, *, n_ctx, cap_ctx, cap_lat):
    slot_ref[0, :, 0:n_ctx] = _select_segment(aff_ref[0, :, 0:n_ctx], cap_ctx, 0)
    slot_ref[0, :, n_ctx:] = _select_segment(aff_ref[0, :, n_ctx:], cap_lat, cap_ctx)


def _select(aff_t, n_ctx, cap_ctx, cap_lat):
    b, e, t = aff_t.shape
    return pl.pallas_call(
        functools.partial(_select_kernel, n_ctx=n_ctx, cap_ctx=cap_ctx, cap_lat=cap_lat),
        grid=(b,),
        in_specs=[pl.BlockSpec((1, e, t), lambda i: (i, 0, 0))],
        out_specs=pl.BlockSpec((1, e, t), lambda i: (i, 0, 0)),
        out_shape=jax.ShapeDtypeStruct((b, e, t), jnp.int32),
        compiler_params=_cparams(("parallel",)),
    )(aff_t)


def _moe_kernel(glo_ref, ghi_ref, slo_ref, shi_ref, slot_ref, aff_ref, f_ref, w1_ref, w3_ref, w2_ref, o_ref,
                xs_s, ys_s, gate_s, *, gtiles, stiles, tb, n_ft):
    b = pl.program_id(0)
    e = pl.program_id(1)
    ft = pl.program_id(2)
    n_e = pl.num_programs(1)
    nb = slot_ref.shape[2]

    def window(k, lo, w):
        start = lo + k * w
        return start, jnp.minimum(start, nb - w)

    def hits(j, start, r0, rc):
        rows = lax.broadcasted_iota(jnp.int32, (rc, tb), 0) + r0
        sv = jnp.where(j >= start, slot_ref[0, 0, pl.ds(j, 1), :], -1)
        return sv == rows

    @pl.when((e == 0) & (ft == 0))
    def _():
        o_ref[...] = jnp.zeros_like(o_ref)

    @pl.when(ft == 0)
    def _():
        base = (b * n_e + e) * len(gtiles)
        for c, (r0, rc, w) in enumerate(gtiles):
            lo = glo_ref[base + c]

            def body(k, carry, r0=r0, rc=rc, w=w, lo=lo):
                acc, gs = carry
                start, j0 = window(k, lo, w)
                parts = []
                for i in range(w):
                    hit = hits(j0 + i, start, r0, rc)
                    parts.append(jnp.where(hit, 1.0, 0.0).astype(BF16))
                    gs = gs + jnp.where(hit, aff_ref[0, 0, pl.ds(j0 + i, 1), :], 0.0)
                off = pl.multiple_of(j0 * tb, tb)
                acc = acc + _dot(jnp.concatenate(parts, axis=1), f_ref[0, pl.ds(off, w * tb), :])
                return acc, gs
            acc, gs = lax.fori_loop(
                0, (ghi_ref[base + c] - lo) // w + 1, body,
                (jnp.zeros((rc, f_ref.shape[-1]), F32), jnp.zeros((rc, tb), F32)))
            xs_s[r0:r0 + rc, :] = acc.astype(BF16)
            gate_s[r0:r0 + rc, :] = jnp.sum(gs, axis=1, keepdims=True)
        ys_s[...] = jnp.zeros_like(ys_s)

    xs = xs_s[...]
    hid = (_silu(_dot(xs, w1_ref[0, 0])) * _dot(xs, w3_ref[0, 0])).astype(BF16)
    ys_s[...] += _dot(hid, w2_ref[0, 0])

    @pl.when(ft == n_ft - 1)
    def _():
        base = (b * n_e + e) * len(stiles)
        for c, (r0, rc, w) in enumerate(stiles):
            ysg = (ys_s[r0:r0 + rc, :] * gate_s[r0:r0 + rc, :]).astype(BF16)
            lo = slo_ref[base + c]

            def body(k, carry, r0=r0, rc=rc, w=w, lo=lo, ysg=ysg):
                start, j0 = window(k, lo, w)
                for i in range(w):
                    oh = jnp.where(hits(j0 + i, start, r0, rc), 1.0, 0.0).astype(BF16)
                    off = pl.multiple_of((j0 + i) * tb, tb)
                    o_ref[0, pl.ds(off, tb), :] += _dot_tn(oh, ysg)
                return carry
            lax.fori_loop(0, (shi_ref[base + c] - lo) // w + 1, body, 0)


def _moe(layer, bounds, slot, aff_t, f, w1, w3, w2, gtiles, stiles, tb, tf):
    b, t, d = f.shape
    n_e = w1.shape[1]
    ff = w1.shape[-1]
    n_ft = ff // tf
    rows = gtiles[-1][0] + gtiles[-1][1]
    grid_spec = pltpu.PrefetchScalarGridSpec(
        num_scalar_prefetch=4,
        grid=(b, n_e, n_ft),
        in_specs=[
            pl.BlockSpec((1, 1, t // tb, tb), lambda bi, e, k, *_: (bi, e, 0, 0)),
            pl.BlockSpec((1, 1, t // tb, tb), lambda bi, e, k, *_: (bi, e, 0, 0)),
            pl.BlockSpec((1, t, d), lambda bi, e, k, *_: (bi, 0, 0)),
            pl.BlockSpec((1, 1, d, tf), lambda bi, e, k, *_: (layer, e, 0, k)),
            pl.BlockSpec((1, 1, d, tf), lambda bi, e, k, *_: (layer, e, 0, k)),
            pl.BlockSpec((1, 1, tf, d), lambda bi, e, k, *_: (layer, e, k, 0)),
        ],
        out_specs=pl.BlockSpec((1, t, d), lambda bi, e, k, *_: (bi, 0, 0), pipeline_mode=pl.Buffered(1)),
        scratch_shapes=[pltpu.VMEM((rows, d), BF16), pltpu.VMEM((rows, d), F32), pltpu.VMEM((rows, 1), F32)],
    )
    return pl.pallas_call(
        functools.partial(_moe_kernel, gtiles=gtiles, stiles=stiles, tb=tb, n_ft=n_ft),
        grid_spec=grid_spec,
        out_shape=jax.ShapeDtypeStruct((b, t, d), F32),
        compiler_params=_cparams(("parallel", "arbitrary", "arbitrary")),
    )(*bounds, slot.reshape(b, n_e, t // tb, tb), aff_t.reshape(b, n_e, t // tb, tb), f, w1, w3, w2)


def _tile_bounds(slot, tiles, tb):
    b, e, t = slot.shape
    blk = jnp.arange(t, dtype=jnp.int32) // tb
    lo, hi = [], []
    for r0, rc, _ in tiles:
        inside = (slot >= r0) & (slot < r0 + rc)
        lo.append(jnp.min(jnp.where(inside, blk, t // tb - 1), axis=-1))
        hi.append(jnp.max(jnp.where(inside, blk, 0), axis=-1))
    lo = jnp.stack(lo, axis=-1)
    hi = jnp.maximum(jnp.stack(hi, axis=-1), lo)
    return lo.reshape(-1).astype(jnp.int32), hi.reshape(-1).astype(jnp.int32)


def _route_and_moe(layer, aff, f, w1, w3, w2, n_ctx, tb, tf):
    b, t, _ = f.shape
    aff_t = jnp.swapaxes(aff[:, :, :N_EXPERTS], 1, 2)
    cap_ctx = (CAPACITY_FACTOR * n_ctx) // N_EXPERTS
    cap_lat = (CAPACITY_FACTOR * (t - n_ctx)) // N_EXPERTS
    slot = _select(aff_t, n_ctx, cap_ctx, cap_lat)
    nb = t // tb

    def tiling(lat_rows, lat_window):
        tl = min(lat_rows, cap_lat)
        w = min(lat_window, nb)
        return ((0, cap_ctx, n_ctx // tb),) + tuple((cap_ctx + i * tl, tl, w) for i in range(cap_lat // tl))

    gtiles = tiling(MOE_GATHER_ROWS, MOE_GATHER_WINDOW)
    stiles = tiling(MOE_SCATTER_ROWS, MOE_SCATTER_WINDOW)
    bounds = _tile_bounds(slot, gtiles, tb) + _tile_bounds(slot, stiles, tb)
    return _moe(layer, bounds, slot, aff_t, f, w1, w3, w2, gtiles, stiles, tb, tf)


def _final_kernel(x_ref, moe_ref, gp_ref, g_ref, o_ref):
    x = x_ref[0] + gp_ref[0, 0] * moe_ref[0]
    o_ref[0] = x * lax.rsqrt(jnp.mean(x * x, axis=-1, keepdims=True) + EPS) * g_ref[...]


def _final(x, moe, gprev, g, tm, n_ctx):
    b, t, d = x.shape
    lead = n_ctx // tm
    return pl.pallas_call(
        _final_kernel,
        grid=(b, (t - n_ctx) // tm),
        in_specs=[_row_spec(tm, d, lead), _row_spec(tm, d, lead),
                  pl.BlockSpec((1, 1, 1, d), lambda bi, i: (bi, 1, 0, 0)), _const_spec((1, d))],
        out_specs=_row_spec(tm, d),
        out_shape=jax.ShapeDtypeStruct((b, t - n_ctx, d), F32),
        compiler_params=_cparams(("parallel", "parallel")),
    )(x, moe, gprev, g)


def _rope_tables(n_ctx, n_lat):
    n_rows = n_lat // GRID_W
    row = jnp.repeat(jnp.arange(n_rows, dtype=F32), GRID_W)
    col = jnp.tile(jnp.arange(GRID_W, dtype=F32), n_rows)
    inv = ROPE_THETA ** (-jnp.arange(0, ROPE_AXIS_DIM, 2, dtype=F32) / ROPE_AXIS_DIM)
    ang_r = row[:, None] * inv[None, :]
    ang_c = col[:, None] * inv[None, :]
    cos = jnp.concatenate([jnp.cos(ang_r), jnp.cos(ang_r), jnp.cos(ang_c), jnp.cos(ang_c)], axis=-1)
    sin = jnp.concatenate([-jnp.sin(ang_r), jnp.sin(ang_r), -jnp.sin(ang_c), jnp.sin(ang_c)], axis=-1)
    cos = jnp.concatenate([jnp.ones((n_ctx, HEAD_DIM), F32), cos], axis=0)
    sin = jnp.concatenate([jnp.zeros((n_ctx, HEAD_DIM), F32), sin], axis=0)
    reps = SEG_W // HEAD_DIM
    return jnp.tile(cos, (1, reps)), jnp.tile(sin, (1, reps))


def _extend_in_proj(w_in):
    base = 4 * RET_WIDTH + ATT_WIDTH
    ak = w_in[:, base:base + ATT_KV_WIDTH].reshape(-1, ATT_KV_HEADS, 1, HEAD_DIM)
    av = w_in[:, base + ATT_KV_WIDTH:].reshape(-1, ATT_KV_HEADS, 1, HEAD_DIM)
    rep = lambda w, n: jnp.broadcast_to(w, (w.shape[0], ATT_KV_HEADS, n, HEAD_DIM)).reshape(w.shape[0], -1)
    return jnp.concatenate([w_in[:, :base], rep(ak, ATT_GROUP), rep(av, 2)], axis=-1)


def kernel(x, c, ctx, c_ctx, mod_w, mod_b, norm_g, mix_in_w, mix_out_w, ret_log_rate, ret_gn_g, qk_norm_g,
           s5_a_re, s5_a_im, s5_log_dt, s5_b_re, s5_b_im, s5_c_re, s5_c_im, s5_d, s5_glu_w, s5_glu_b,
           moe_router_w, moe_w1, moe_w3, moe_w2, final_norm_g):
    batch, n_lat, d = x.shape
    n_ctx = ctx.shape[1]
    t = n_ctx + n_lat
    depth = mod_w.shape[0]
    assert batch == 8 and d % 128 == 0 and n_ctx % CHUNK == 0 and n_lat % CHUNK == 0
    tm = 256 if n_ctx % 256 == 0 else 128
    tb = tm
    tf = 512

    xs = jnp.concatenate([ctx, x], axis=1)
    rows = 16
    cond = jnp.zeros((rows, d), F32).at[:batch].set(c).at[batch].set(c_ctx)
    mod_all = _modulation(cond, mod_w, mod_b)
    cos, sin = _rope_tables(n_ctx, n_lat)
    lane_head = jnp.arange(SEG_W) // HEAD_DIM
    bd = (lane_head[:, None] == lane_head[None, :]).astype(BF16)
    w1 = moe_w1.astype(BF16)
    w3 = moe_w3.astype(BF16)
    w2 = moe_w2.astype(BF16)

    prev = None
    for layer in range(depth):
        m = mod_all[layer]
        lat = m[:batch].reshape(batch, 1, 6, d)
        cx = jnp.broadcast_to(m[batch].reshape(1, 1, 6, d), (batch, 1, 6, d))
        modtab = jnp.concatenate([cx, lat], axis=1)
        ng1 = norm_g[layer, 0].reshape(1, d)
        ng2 = norm_g[layer, 1].reshape(1, d)
        wr = jnp.zeros((d, ROUTER_LANES), F32).at[:, :N_EXPERTS].set(moe_router_w[layer])
        wrh = wr.astype(BF16)
        wrl = (wr - wrh.astype(F32)).astype(BF16)
        i = layer // 2
        if layer % 2 == 0:
            w_ext = _extend_in_proj(mix_in_w[i]).astype(BF16)
            reps = SEG_W // HEAD_DIM
            qkg = jnp.stack([jnp.tile(qk_norm_g[i, 0], reps), jnp.tile(qk_norm_g[i, 1], reps)])
            xs, (rq, rk, rv, rg, aq, ak4, av2) = _in_proj(xs, prev, modtab, ng1, w_ext, cos, sin, qkg, bd,
                                                         tm, n_ctx)
            rate = jnp.repeat(ret_log_rate[i], HEAD_DIM, axis=-1)
            rate_h = jnp.broadcast_to(ret_log_rate[i][:, :, None], (2, RET_HEADS, CHUNK))
            o_f, o_b = _retention(rq, rk, rv, rate, rate_h, bd, n_ctx)
            att = _attention(aq, ak4, av2, tm, tm, n_ctx)
            xs, f, aff = _out_proj(xs, o_f, o_b, rg, att, modtab, ret_gn_g[i].reshape(1, SEG_W),
                                   mix_out_w[i].astype(BF16), ng2, wrh, wrl, bd, tm, n_ctx)
        else:
            xs, u = _s5_pre(xs, prev, modtab, ng1, tm, n_ctx)
            groups = d // S5_GROUP
            n_sub = t // S5_SUB
            u_g = (u.reshape(batch, n_sub, S5_SUB, groups, S5_GROUP).transpose(3, 1, 0, 2, 4)
                   .reshape(groups, n_sub * batch, S5_SUB * S5_GROUP))
            tabs = _s5_tables(s5_a_re[i], s5_a_im[i], s5_log_dt[i], s5_b_re[i], s5_b_im[i],
                              s5_c_re[i], s5_c_im[i], s5_d[i])
            z_g = _s5_scan(u_g, *tabs, n_ctx // S5_SUB, batch, 2)
            z = (z_g.reshape(groups, n_sub, batch, S5_SUB, S5_GROUP).transpose(2, 1, 3, 0, 4)
                 .reshape(batch, t, d))
            xs, f, aff = _glu(xs, z, modtab, s5_glu_w[i].astype(BF16), s5_glu_b[i].reshape(1, 2 * d),
                              ng2, wrh, wrl, tm, n_ctx)
        moe = _route_and_moe(layer, aff, f, w1, w3, w2, n_ctx, tb, tf)
        prev = (moe, modtab[:, :, 5:6, :])
    return _final(xs, prev[0], prev[1], final_norm_g.reshape(1, d), tm, n_ctx)
```

```python
import functools
import math

import jax
import jax.numpy as jnp
from jax import lax
from jax.experimental import pallas as pl
from jax.experimental.pallas import tpu as pltpu

F32 = jnp.float32
BF16 = jnp.bfloat16

GRID_W = 64
HEAD_DIM = 64
RET_HEADS = 8
ATT_HEADS = 8
ATT_KV_HEADS = 2
ATT_GROUP = ATT_HEADS // ATT_KV_HEADS
RET_WIDTH = RET_HEADS * HEAD_DIM
ATT_WIDTH = ATT_HEADS * HEAD_DIM
ATT_KV_WIDTH = ATT_KV_HEADS * HEAD_DIM
CHUNK = 128
ROPE_THETA = 10000.0
ROPE_AXIS_DIM = HEAD_DIM // 2
S5_GROUP = 16
S5_STATE = 64
S5_SUB = 8
S5_BLOCK_GROUPS = 8
S5_SCAN_UNROLL = 4
N_EXPERTS = 16
CAPACITY_FACTOR = 2
EPS = 1e-6
SEG_W = 512
LOG2_E = math.log2(math.e)
ROUTER_LANES = 128
MOE_GATHER_ROWS = 128
MOE_GATHER_WINDOW = 6
MOE_SCATTER_ROWS = 256
MOE_SCATTER_WINDOW = 10

V7X_VMEM_BYTES = 64 * 1024 * 1024
VMEM_LIMIT = V7X_VMEM_BYTES - 8 * 1024 * 1024


def _cparams(sem):
    return pltpu.CompilerParams(dimension_semantics=sem, vmem_limit_bytes=VMEM_LIMIT)


def _dot(a, b):
    return jnp.dot(a, b, preferred_element_type=F32)


def _dot_nt(a, b):
    return lax.dot_general(a, b, (((1,), (1,)), ((), ())), preferred_element_type=F32)


def _dot_tn(a, b):
    return lax.dot_general(a, b, (((0,), (0,)), ((), ())), preferred_element_type=F32)


def _split(v):
    hi = v.astype(BF16)
    lo = (v - hi.astype(F32)).astype(BF16)
    return hi, lo


def _segsum(v, bd):
    hi, lo = _split(v)
    return _dot(hi, bd) + _dot(lo, bd)


def _sigmoid(v):
    return 1.0 / (1.0 + jnp.exp(-v))


def _silu(v):
    return v * _sigmoid(v)


def _gelu_tanh(v):
    c = math.sqrt(2.0 / math.pi)
    return 0.5 * v * (1.0 + jnp.tanh(c * (v + 0.044715 * (v * v * v))))


def _norm_mod(x, g, scale, shift):
    ms = jnp.mean(x * x, axis=-1, keepdims=True)
    return x * lax.rsqrt(ms + EPS) * g * (1.0 + scale) + shift


def _rope(x, cos, sin_signed):
    w = x.shape[-1]
    lane = lax.broadcasted_iota(jnp.int32, x.shape, 1)
    first = (lane % 32) < 16
    partner = jnp.where(first, pltpu.roll(x, w - 16, 1), pltpu.roll(x, 16, 1))
    return x * cos + partner * sin_signed


def _mod_kernel(c_ref, w_ref, b_ref, o_ref):
    s = _silu(c_ref[...]).astype(BF16)
    o_ref[0] = _dot(s, w_ref[0].astype(BF16)) + b_ref[0]


def _modulation(cond, mod_w, mod_b):
    depth, d, n6 = mod_w.shape
    rows = cond.shape[0]
    tn = n6 // 6
    return pl.pallas_call(
        _mod_kernel,
        grid=(depth, n6 // tn),
        in_specs=[
            pl.BlockSpec((rows, d), lambda l, j: (0, 0)),
            pl.BlockSpec((1, d, tn), lambda l, j: (l, 0, j)),
            pl.BlockSpec((1, 1, tn), lambda l, j: (l, 0, j)),
        ],
        out_specs=pl.BlockSpec((1, rows, tn), lambda l, j: (l, 0, j)),
        out_shape=jax.ShapeDtypeStruct((depth, rows, n6), F32),
        compiler_params=_cparams(("parallel", "parallel")),
    )(cond, mod_w, mod_b.reshape(depth, 1, n6))


def _residual_in(has_prev, refs):
    if has_prev:
        x_ref, moe_ref, gp_ref = refs[:3]
        return x_ref[0] + gp_ref[0, 0] * moe_ref[0], refs[3:]
    return refs[0][0], refs[1:]


def _in_kernel(has_prev, *refs):
    x, refs = _residual_in(has_prev, refs)
    mod_ref, ng_ref, w_ref, cos_ref, sin_ref, qkg_ref, bd_ref = refs[:7]
    outs = refs[7:]
    if has_prev:
        outs[0][0] = x
        outs = outs[1:]
    rq_ref, rk_ref, rv_ref, rg_ref, aq_ref, ak_ref, av_ref = outs
    m = mod_ref[0, 0]
    h = _norm_mod(x, ng_ref[...], m[1:2], m[0:1]).astype(BF16)
    cos = cos_ref[...]
    sin = sin_ref[...]
    bd = bd_ref[...]

    def seg(i):
        return _dot(h, w_ref[:, i * SEG_W:(i + 1) * SEG_W])

    def headnorm(v, g):
        ms = _segsum(v * v, bd) * (1.0 / HEAD_DIM)
        return v * lax.rsqrt(ms + EPS) * g

    scale = HEAD_DIM ** -0.5
    rq_ref[0] = _rope(seg(0), cos, sin).astype(BF16)
    rk_ref[0] = _rope(seg(1) * scale, cos, sin).astype(BF16)
    rv_ref[0] = seg(2).astype(BF16)
    rg_ref[0] = seg(3)
    aq_ref[0] = (_rope(headnorm(seg(4), qkg_ref[0:1]), cos, sin) * (scale * LOG2_E)).astype(BF16)
    ak_ref[0] = _rope(headnorm(seg(5), qkg_ref[1:2]), cos, sin).astype(BF16)
    av_ref[0] = _dot(h, w_ref[:, 6 * SEG_W:]).astype(BF16)


def _row_spec(tm, d, lead=0):
    return pl.BlockSpec((1, tm, d), lambda b, i: (b, i + lead, 0))


def _mod_spec(d, n, seg_tiles):
    return pl.BlockSpec((1, 1, n, d), lambda b, i: (b, jnp.where(i >= seg_tiles, 1, 0), 0, 0))


def _const_spec(shape):
    nd = len(shape)
    return pl.BlockSpec(shape, lambda b, i: (0,) * nd)


def _in_proj(x, prev, modtab, norm_g, w_ext, cos, sin, qkg, bd, tm, n_ctx):
    b, t, d = x.shape
    seg_tiles = n_ctx // tm
    has_prev = prev is not None
    args, specs = [x], [_row_spec(tm, d)]
    if has_prev:
        moe, gprev = prev
        args += [moe, gprev]
        specs += [_row_spec(tm, d), _mod_spec(d, 1, seg_tiles)]
    args += [modtab, norm_g, w_ext, cos, sin, qkg, bd]
    specs += [
        _mod_spec(d, 6, seg_tiles),
        _const_spec((1, d)),
        _const_spec(w_ext.shape),
        pl.BlockSpec((tm, SEG_W), lambda b, i: (i, 0)),
        pl.BlockSpec((tm, SEG_W), lambda b, i: (i, 0)),
        _const_spec((2, SEG_W)),
        _const_spec((SEG_W, SEG_W)),
    ]
    out_shape, out_specs = [], []
    if has_prev:
        out_shape.append(jax.ShapeDtypeStruct((b, t, d), F32))
        out_specs.append(_row_spec(tm, d))
    for dt, width in ((BF16, SEG_W), (BF16, SEG_W), (BF16, SEG_W), (F32, SEG_W), (BF16, SEG_W), (BF16, SEG_W),
                      (BF16, ATT_KV_HEADS * 2 * HEAD_DIM)):
        out_shape.append(jax.ShapeDtypeStruct((b, t, width), dt))
        out_specs.append(_row_spec(tm, width))
    outs = pl.pallas_call(
        functools.partial(_in_kernel, has_prev),
        grid=(b, t // tm),
        in_specs=specs,
        out_specs=out_specs,
        out_shape=out_shape,
        compiler_params=_cparams(("parallel", "parallel")),
    )(*args)
    if has_prev:
        return outs[0], outs[1:]
    return x, outs


def _ret_kernel(qf_ref, kf_ref, vf_ref, qb_ref, kb_ref, vb_ref, rate_ref, rate_h_ref, bd_ref,
                of_ref, ob_ref, sf_s, sb_s, dm_s, dq_s, dk_s, dc_s):
    s = pl.program_id(1)
    c = CHUNK

    @pl.when(s == 0)
    def _():
        sf_s[...] = jnp.zeros_like(sf_s)
        sb_s[...] = jnp.zeros_like(sb_s)
        pos = lax.broadcasted_iota(jnp.int32, (c, SEG_W), 0).astype(F32)
        lgf = -jnp.exp(rate_ref[0:1, :])
        lgb = -jnp.exp(rate_ref[1:2, :])
        dq_s[0] = jnp.exp((pos + 1.0) * lgf)
        dk_s[0] = jnp.exp((c - 1.0 - pos) * lgf)
        dc_s[0] = jnp.exp(float(c) * lgf)
        dq_s[1] = jnp.exp((c - pos) * lgb)
        dk_s[1] = jnp.exp(pos * lgb)
        dc_s[1] = jnp.exp(float(c) * lgb)
        ii = lax.broadcasted_iota(jnp.int32, (c, c), 0)
        jj = lax.broadcasted_iota(jnp.int32, (c, c), 1)
        for h in range(RET_HEADS):
            lf = -jnp.exp(rate_h_ref[0, h:h + 1, :])
            lb = -jnp.exp(rate_h_ref[1, h:h + 1, :])
            mf = ii >= jj
            mb = jj > ii
            dm_s[0, h] = jnp.where(mf, jnp.exp(jnp.where(mf, ii - jj, 0).astype(F32) * lf), 0.0)
            dm_s[1, h] = jnp.where(mb, jnp.exp(jnp.where(mb, jj - ii, 0).astype(F32) * lb), 0.0)

    bdm = bd_ref[...].astype(F32)
    head_of_lane = lax.broadcasted_iota(jnp.int32, (1, SEG_W // 2), 1) // HEAD_DIM

    def direction(d, q_ref, k_ref, v_ref, o_ref, st):
        q = q_ref[0]
        k = k_ref[0]
        v = v_ref[0]
        state = st[...]
        qd = (q.astype(F32) * dq_s[d]).astype(BF16)
        inter = _dot(qd, state.astype(BF16))
        for half in range(2):
            sl = slice(half * 256, half * 256 + 256)
            qh, kh, vh = q[:, sl], k[:, sl], v[:, sl]
            acc = inter[:, sl]
            for hh in range(4):
                msk = head_of_lane == hh
                sc = _dot_nt(jnp.where(msk, qh, jnp.zeros_like(qh)), kh) * dm_s[d, half * 4 + hh]
                acc = acc + _dot(sc.astype(BF16), jnp.where(msk, vh, jnp.zeros_like(vh)))
            o_ref[0, :, sl] = acc
        kd = (k.astype(F32) * dk_s[d]).astype(BF16)
        st[...] = state * dc_s[d] + bdm * _dot_tn(kd, v)

    direction(0, qf_ref, kf_ref, vf_ref, of_ref, sf_s)
    direction(1, qb_ref, kb_ref, vb_ref, ob_ref, sb_s)


def _retention(rq, rk, rv, rate, rate_h, bd, n_ctx):
    b, t, w = rq.shape
    nc = t // CHUNK
    lc = n_ctx // CHUNK

    def fwd(bi, s):
        return (bi, s, 0)

    def bwd(bi, s):
        return (bi, jnp.where(s < lc, lc - 1 - s, nc - 1 - (s - lc)), 0)

    blk = (1, CHUNK, w)
    return pl.pallas_call(
        _ret_kernel,
        grid=(b, nc),
        in_specs=[pl.BlockSpec(blk, fwd)] * 3 + [pl.BlockSpec(blk, bwd)] * 3 + [
            _const_spec((2, w)),
            _const_spec((2, RET_HEADS, CHUNK)),
            _const_spec((w, w)),
        ],
        out_specs=[pl.BlockSpec(blk, fwd), pl.BlockSpec(blk, bwd)],
        out_shape=[jax.ShapeDtypeStruct((b, t, w), F32)] * 2,
        scratch_shapes=[
            pltpu.VMEM((w, w), F32), pltpu.VMEM((w, w), F32),
            pltpu.VMEM((2, RET_HEADS, CHUNK, CHUNK), F32),
            pltpu.VMEM((2, CHUNK, w), F32), pltpu.VMEM((2, CHUNK, w), F32),
            pltpu.VMEM((2, 1, w), F32),
        ],
        compiler_params=_cparams(("parallel", "arbitrary")),
    )(rq, rk, rv, rq, rk, rv, rate, rate_h, bd)


ATT_ROW_BLOCK = 128


def _att_kernel(q_ref, k_ref, v_ref, o_ref, qs_s, m_s, acc_s, sa_s, sb_s, *, tq, tk, n_ctx, t):
    qi = pl.program_id(2)
    q = q_ref[0]
    gw = ATT_GROUP * HEAD_DIM
    head_of_lane = lax.broadcasted_iota(jnp.int32, (1, gw), 1) // HEAD_DIM
    for h in range(ATT_GROUP):
        qs_s[h * tq:(h + 1) * tq, :] = jnp.where(head_of_lane == h, q, jnp.zeros_like(q))
    m_s[...] = jnp.full(m_s.shape, -1e30, F32)
    acc_s[...] = jnp.zeros_like(acc_s)
    nk = jnp.where(qi * tq < n_ctx, n_ctx // tk, t // tk)
    vw = 2 * HEAD_DIM
    ones_lane = lax.broadcasted_iota(jnp.int32, (1, vw), 1) >= HEAD_DIM
    rb = ATT_ROW_BLOCK

    blocks = [slice(r * rb, (r + 1) * rb) for r in range(ATT_GROUP * tq // rb)]

    def scores_into(j, dst):
        kb = k_ref[0, pl.ds(pl.multiple_of(j * tk, tk), tk), :]
        for rows in blocks:
            dst[rows, :] = _dot_nt(qs_s[rows, :], kb)

    def consume(j, src):
        vb = v_ref[0, pl.ds(pl.multiple_of(j * tk, tk), tk), :]
        vext = jnp.where(ones_lane, jnp.ones_like(vb), vb)
        for rows in blocks:
            sc = src[rows, :]
            m_prev = m_s[rows, :]
            m_new = jnp.maximum(m_prev, jnp.max(sc, axis=1, keepdims=True))
            p = jnp.exp2(sc - jnp.concatenate([m_new] * (tk // vw), axis=1)).astype(BF16)
            acc_s[rows, :] = jnp.exp2(m_prev - m_new) * acc_s[rows, :] + _dot(p, vext)
            m_s[rows, :] = m_new

    scores_into(0, sa_s)

    def body(i, carry):
        j = 2 * i
        scores_into(j + 1, sb_s)
        consume(j, sa_s)
        scores_into(jnp.minimum(j + 2, nk - 1), sa_s)
        consume(j + 1, sb_s)
        return carry

    lax.fori_loop(0, nk // 2, body, 0)

    @pl.when(nk % 2 == 1)
    def _():
        consume(nk - 1, sa_s)

    low = lax.broadcasted_iota(jnp.int32, (tq, vw), 1) < HEAD_DIM
    for pair in range(ATT_GROUP // 2):
        a0 = acc_s[(2 * pair) * tq:(2 * pair + 1) * tq, :]
        a1 = acc_s[(2 * pair + 1) * tq:(2 * pair + 2) * tq, :]
        even = a0 / pltpu.roll(a0, HEAD_DIM, 1)
        odd = pltpu.roll(a1, HEAD_DIM, 1) / a1
        o_ref[0, :, pair * vw:(pair + 1) * vw] = jnp.where(low, even, odd).astype(BF16)


def _attention(aq, ak4, av2, tq, tk, n_ctx):
    b, t, w = aq.shape
    gw = ATT_GROUP * HEAD_DIM
    vw = 2 * HEAD_DIM
    return pl.pallas_call(
        functools.partial(_att_kernel, tq=tq, tk=tk, n_ctx=n_ctx, t=t),
        grid=(b, ATT_KV_HEADS, t // tq),
        in_specs=[
            pl.BlockSpec((1, tq, gw), lambda bi, g, i: (bi, i, g)),
            pl.BlockSpec((1, t, gw), lambda bi, g, i: (bi, 0, g)),
            pl.BlockSpec((1, t, vw), lambda bi, g, i: (bi, 0, g)),
        ],
        out_specs=pl.BlockSpec((1, tq, gw), lambda bi, g, i: (bi, i, g)),
        out_shape=jax.ShapeDtypeStruct((b, t, w), BF16),
        scratch_shapes=[pltpu.VMEM((ATT_GROUP * tq, gw), BF16), pltpu.VMEM((ATT_GROUP * tq, vw), F32),
                        pltpu.VMEM((ATT_GROUP * tq, vw), F32), pltpu.VMEM((ATT_GROUP * tq, tk), F32),
                        pltpu.VMEM((ATT_GROUP * tq, tk), F32)],
        compiler_params=_cparams(("parallel", "parallel", "parallel")),
    )(aq, ak4, av2)


def _post_mixer(x, delta, m, ng2, wrh, wrl, x_ref, f_ref, aff_ref):
    xn = x + m[2:3] * delta
    x_ref[0] = xn
    f = _norm_mod(xn, ng2, m[4:5], m[3:4])
    fh, fl = _split(f)
    f_ref[0] = fh
    logits = _dot(fh, wrh) + _dot(fl, wrh) + _dot(fh, wrl)
    lane = lax.broadcasted_iota(jnp.int32, logits.shape, 1)
    logits = jnp.where(lane < N_EXPERTS, logits, -1e30)
    e = jnp.exp(logits - jnp.max(logits, axis=1, keepdims=True))
    aff_ref[0] = e / jnp.sum(e, axis=1, keepdims=True)


def _out_kernel(x_ref, of_ref, ob_ref, rg_ref, att_ref, mod_ref, gn_ref, w_ref, ng2_ref, wrh_ref, wrl_ref,
                bd_ref, xo_ref, f_ref, aff_ref):
    bd = bd_ref[...]
    o = of_ref[0] + ob_ref[0]
    mu = _segsum(o, bd) * (1.0 / HEAD_DIM)
    oc = o - mu
    var = _segsum(oc * oc, bd) * (1.0 / HEAD_DIM)
    r = _silu(rg_ref[0]) * (oc * lax.rsqrt(var + EPS) * gn_ref[...])
    y = _dot(r.astype(BF16), w_ref[0:RET_WIDTH, :]) + _dot(att_ref[0], w_ref[RET_WIDTH:, :])
    _post_mixer(x_ref[0], y, mod_ref[0, 0], ng2_ref[...], wrh_ref[...], wrl_ref[...], xo_ref, f_ref, aff_ref)


def _post_specs(b, t, d, tm):
    out_shape = [jax.ShapeDtypeStruct((b, t, d), F32), jax.ShapeDtypeStruct((b, t, d), BF16),
                 jax.ShapeDtypeStruct((b, t, ROUTER_LANES), F32)]
    out_specs = [_row_spec(tm, d), _row_spec(tm, d), _row_spec(tm, ROUTER_LANES)]
    return out_shape, out_specs


def _out_proj(x, of, ob, rg, att, modtab, gn_g, w_out, ng2, wrh, wrl, bd, tm, n_ctx):
    b, t, d = x.shape
    out_shape, out_specs = _post_specs(b, t, d, tm)
    return pl.pallas_call(
        _out_kernel,
        grid=(b, t // tm),
        in_specs=[
            _row_spec(tm, d), _row_spec(tm, SEG_W), _row_spec(tm, SEG_W), _row_spec(tm, SEG_W),
            _row_spec(tm, SEG_W), _mod_spec(d, 6, n_ctx // tm), _const_spec((1, SEG_W)),
            _const_spec(w_out.shape), _const_spec((1, d)), _const_spec(wrh.shape), _const_spec(wrl.shape),
            _const_spec((SEG_W, SEG_W)),
        ],
        out_specs=out_specs,
        out_shape=out_shape,
        compiler_params=_cparams(("parallel", "parallel")),
    )(x, of, ob, rg, att, modtab, gn_g, w_out, ng2, wrh, wrl, bd)


def _s5_pre_kernel(has_prev, *refs):
    x, refs = _residual_in(has_prev, refs)
    mod_ref, ng_ref = refs[:2]
    outs = refs[2:]
    if has_prev:
        outs[0][0] = x
        outs = outs[1:]
    m = mod_ref[0, 0]
    outs[0][0] = _norm_mod(x, ng_ref[...], m[1:2], m[0:1])


def _s5_pre(x, prev, modtab, norm_g, tm, n_ctx):
    b, t, d = x.shape
    seg_tiles = n_ctx // tm
    has_prev = prev is not None
    args, specs = [x], [_row_spec(tm, d)]
    if has_prev:
        args += list(prev)
        specs += [_row_spec(tm, d), _mod_spec(d, 1, seg_tiles)]
    args += [modtab, norm_g]
    specs += [_mod_spec(d, 6, seg_tiles), _const_spec((1, d))]
    n_out = 2 if has_prev else 1
    outs = pl.pallas_call(
        functools.partial(_s5_pre_kernel, has_prev),
        grid=(b, t // tm),
        in_specs=specs,
        out_specs=[_row_spec(tm, d)] * n_out,
        out_shape=[jax.ShapeDtypeStruct((b, t, d), F32)] * n_out,
        compiler_params=_cparams(("parallel", "parallel")),
    )(*args)
    if has_prev:
        return outs[0], outs[1]
    return x, outs[0]


def _s5_kernel(u_ref, tm_ref, wz_ref, wof_ref, wob_ref, ar_ref, ai_ref, d_ref, z_ref,
               zr_s, zi_s, xfr_s, xfi_s, xbr_s, xbi_s, zs_s, *, n_sub, n_sub_ctx):
    s_n, j_sub, gb, half = n_sub, S5_SUB, S5_BLOCK_GROUPS, 2 * S5_STATE
    uf = jnp.concatenate([u_ref[0, pl.ds(m, s_n, stride=j_sub), :] for m in range(j_sub)], axis=1)
    lhs = uf.astype(BF16)
    for g in range(gb):
        zz = _dot(lhs, wz_ref[0, g])
        zr_s[pl.ds(g, s_n, stride=gb), :] = zz[:, :half]
        zi_s[pl.ds(g, s_n, stride=gb), :] = zz[:, half:]
    fwd_lane = lax.broadcasted_iota(jnp.int32, (gb, half), 1) < S5_STATE
    ar = ar_ref[0]
    ai = ai_ref[0]

    def body(i, carry):
        xr, xi = carry
        sb = jnp.where(i < n_sub_ctx, n_sub_ctx - 1 - i, s_n - 1 - (i - n_sub_ctx))
        fwd = pl.ds(pl.multiple_of(i * gb, gb), gb)
        bwd = pl.ds(pl.multiple_of(sb * gb, gb), gb)
        xfr_s[fwd, :] = xr
        xfi_s[fwd, :] = xi
        xbr_s[bwd, :] = xr
        xbi_s[bwd, :] = xi
        zr = jnp.where(fwd_lane, zr_s[fwd, :], zr_s[bwd, :])
        zi = jnp.where(fwd_lane, zi_s[fwd, :], zi_s[bwd, :])
        return ar * xr - ai * xi + zr, ar * xi + ai * xr + zi

    zero = jnp.zeros((gb, half), F32)
    lax.fori_loop(0, s_n, body, (zero, zero), unroll=S5_SCAN_UNROLL)
    y = _dot(lhs, tm_ref[0]) + d_ref[0] * uf
    for g in range(gb):
        rows = pl.ds(g, s_n, stride=gb)
        xf = jnp.concatenate([xfr_s[rows, :], xfi_s[rows, :]], axis=1).astype(BF16)
        xb = jnp.concatenate([xbr_s[rows, :], xbi_s[rows, :]], axis=1).astype(BF16)
        y = y + _dot(xf, wof_ref[0, g]) + _dot(xb, wob_ref[0, g])
    z = _gelu_tanh(y)
    w = z_ref.shape[-1]
    for n in range(j_sub):
        zs_s[pl.ds(n, s_n, stride=j_sub), :] = z[:, n * w:(n + 1) * w]
    z_ref[0] = zs_s[...].astype(BF16)


def _s5_scan(u, tmat, wz, wof, wob, ar, ai, dt, n_ctx):
    b, t, d = u.shape
    cw = S5_BLOCK_GROUPS * S5_GROUP
    n_sub = t // S5_SUB
    lw = S5_SUB * cw
    sw = 4 * S5_STATE
    gb = S5_BLOCK_GROUPS
    once = pl.Buffered(1)
    return pl.pallas_call(
        functools.partial(_s5_kernel, n_sub=n_sub, n_sub_ctx=n_ctx // S5_SUB),
        grid=(d // cw, b),
        in_specs=[
            pl.BlockSpec((1, t, cw), lambda c, bi: (bi, 0, c)),
            pl.BlockSpec((1, lw, lw), lambda c, bi: (c, 0, 0), pipeline_mode=once),
            pl.BlockSpec((1, gb, lw, sw), lambda c, bi: (c, 0, 0, 0), pipeline_mode=once),
            pl.BlockSpec((1, gb, sw, lw), lambda c, bi: (c, 0, 0, 0), pipeline_mode=once),
            pl.BlockSpec((1, gb, sw, lw), lambda c, bi: (c, 0, 0, 0), pipeline_mode=once),
            pl.BlockSpec((1, gb, 2 * S5_STATE), lambda c, bi: (c, 0, 0)),
            pl.BlockSpec((1, gb, 2 * S5_STATE), lambda c, bi: (c, 0, 0)),
            pl.BlockSpec((1, 1, lw), lambda c, bi: (c, 0, 0)),
        ],
        out_specs=pl.BlockSpec((1, t, cw), lambda c, bi: (bi, 0, c)),
        out_shape=jax.ShapeDtypeStruct((b, t, d), BF16),
        scratch_shapes=[pltpu.VMEM((gb * n_sub, 2 * S5_STATE), F32)] * 6 + [pltpu.VMEM((t, cw), F32)],
        compiler_params=_cparams(("parallel", "parallel")),
    )(u, tmat, wz, wof, wob, ar, ai, dt)


def _s5_tables(a_re, a_im, log_dt, b_re, b_im, c_re, c_im, d_skip):
    hp = lax.Precision.HIGHEST
    j_sub = S5_SUB
    gb = S5_BLOCK_GROUPS
    lam_re = jnp.minimum(a_re, -1e-4)
    lam_im = a_im
    dt = jnp.exp(log_dt)[..., None]
    mag = jnp.exp(lam_re * dt)
    bar_re = mag * jnp.cos(lam_im * dt)
    bar_im = mag * jnp.sin(lam_im * dt)
    den = lam_re * lam_re + lam_im * lam_im
    nr, ni = bar_re - 1.0, bar_im
    k_re = (nr * lam_re + ni * lam_im) / den
    k_im = (ni * lam_re - nr * lam_im) / den
    bb_re = k_re[..., None] * b_re[None] - k_im[..., None] * b_im[None]
    bb_im = k_re[..., None] * b_im[None] + k_im[..., None] * b_re[None]
    jj = jnp.arange(j_sub + 1, dtype=F32)[:, None, None, None]
    pmag = jnp.exp(jj * (lam_re * dt)[None])
    pw_re = pmag * jnp.cos(jj * (lam_im * dt)[None])
    pw_im = pmag * jnp.sin(jj * (lam_im * dt)[None])
    pb_re = pw_re[..., None] * bb_re[None] - pw_im[..., None] * bb_im[None]
    pb_im = pw_re[..., None] * bb_im[None] + pw_im[..., None] * bb_re[None]
    cp_re = c_re[None] * pw_re[:, :, :, None, :] - c_im[None] * pw_im[:, :, :, None, :]
    cp_im = c_re[None] * pw_im[:, :, :, None, :] + c_im[None] * pw_re[:, :, :, None, :]
    kern = (jnp.einsum('dgcp,jdgpk->jdgkc', c_re, pb_re[:j_sub], precision=hp)
            - jnp.einsum('dgcp,jdgpk->jdgkc', c_im, pb_im[:j_sub], precision=hp))
    groups = a_re.shape[1]
    nblk = groups // gb
    k_w = S5_GROUP
    p_w = S5_STATE
    lw = j_sub * gb * k_w
    eye = jnp.eye(gb, dtype=F32)
    blk = lambda a: a.reshape(a.shape[0], nblk, gb, *a.shape[2:])
    pos = jnp.arange(j_sub)
    lag = pos[None, :] - pos[:, None]
    kf = jnp.where((lag >= 0)[:, :, None, None, None], kern[jnp.clip(lag, 0, j_sub - 1), 0], 0.0)
    kb = jnp.where((lag <= 0)[:, :, None, None, None], kern[jnp.clip(-lag, 0, j_sub - 1), 1], 0.0)
    kfull = (kf + kb).reshape(j_sub, j_sub, nblk, gb, k_w, k_w)
    tmat = jnp.einsum('mnBgkc,gh->Bmgknhc', kfull, eye).reshape(nblk, lw, lw)
    rev = pos[::-1]
    zc = jnp.concatenate([pb_re[rev, 0], pb_re[pos, 1], pb_im[rev, 0], pb_im[pos, 1]], axis=2)
    wz = jnp.einsum('mBgqk,gh->Bgmhkq', blk(zc), eye).reshape(nblk, gb, lw, 4 * p_w)
    def orows(cp, d, order):
        return jnp.einsum('nBgcp,gh->Bgpnhc', blk(cp[order, d]), eye).reshape(nblk, gb, p_w, lw)
    zeros = jnp.zeros((nblk, gb, p_w, lw), F32)
    wof = jnp.concatenate([orows(cp_re, 0, pos + 1), zeros, -orows(cp_im, 0, pos + 1), zeros], axis=2)
    wob = jnp.concatenate([zeros, orows(cp_re, 1, j_sub - pos), zeros, -orows(cp_im, 1, j_sub - pos)], axis=2)
    ar = jnp.concatenate([pw_re[j_sub, 0], pw_re[j_sub, 1]], axis=-1).reshape(nblk, gb, 2 * p_w)
    ai = jnp.concatenate([pw_im[j_sub, 0], pw_im[j_sub, 1]], axis=-1).reshape(nblk, gb, 2 * p_w)
    dt_tile = jnp.tile(d_skip.reshape(nblk, 1, gb * k_w), (1, 1, j_sub))
    return tmat.astype(BF16), wz.astype(BF16), wof.astype(BF16), wob.astype(BF16), ar, ai, dt_tile


def _glu_kernel(x_ref, z_ref, mod_ref, w_ref, b_ref, ng2_ref, wrh_ref, wrl_ref, xo_ref, f_ref, aff_ref):
    d = x_ref.shape[-1]
    ag = _dot(z_ref[0], w_ref[...]) + b_ref[...]
    delta = ag[:, :d] * _sigmoid(ag[:, d:])
    _post_mixer(x_ref[0], delta, mod_ref[0, 0], ng2_ref[...], wrh_ref[...], wrl_ref[...], xo_ref, f_ref, aff_ref)


def _glu(x, z, modtab, glu_w, glu_b, ng2, wrh, wrl, tm, n_ctx):
    b, t, d = x.shape
    out_shape, out_specs = _post_specs(b, t, d, tm)
    return pl.pallas_call(
        _glu_kernel,
        grid=(b, t // tm),
        in_specs=[_row_spec(tm, d), _row_spec(tm, d), _mod_spec(d, 6, n_ctx // tm),
                  _const_spec(glu_w.shape), _const_spec((1, 2 * d)), _const_spec((1, d)),
                  _const_spec(wrh.shape), _const_spec(wrl.shape)],
        out_specs=out_specs,
        out_shape=out_shape,
        compiler_params=_cparams(("parallel", "parallel")),
    )(x, z, modtab, glu_w, glu_b, ng2, wrh, wrl)


def _cumsum_lanes(v):
    n = v.shape[-1]
    lane = lax.broadcasted_iota(jnp.int32, v.shape, 1)
    sh = 1
    while sh < n:
        v = v + jnp.where(lane >= sh, pltpu.roll(v, sh, 1), 0.0)
        sh *= 2
    return v


def _select_segment(aff, cap, base):
    bits = pltpu.bitcast(aff, jnp.int32)
    thr = jnp.zeros((aff.shape[0], 1), jnp.int32)
    for bit in range(30, -1, -1):
        cand = thr | (1 << bit)
        cnt = jnp.sum(jnp.where(bits >= cand, 1.0, 0.0), axis=1, keepdims=True)
        thr = jnp.where(cnt >= cap, cand, thr)
    gt = bits > thr
    eq = bits == thr
    need = cap - jnp.sum(jnp.where(gt, 1.0, 0.0), axis=1, keepdims=True)
    eqf = jnp.where(eq, 1.0, 0.0)
    eq_before = _cumsum_lanes(eqf) - eqf
    sel = jnp.where(gt, 1.0, jnp.where(eq, jnp.where(eq_before < need, 1.0, 0.0), 0.0))
    pos = _cumsum_lanes(sel) - sel
    return jnp.where(sel > 0.0, pos.astype(jnp.int32) + base, -1)


def _select_kernel(aff_ref, slot_ref, *, n_ctx, cap_ctx, cap_lat):
    slot_ref[0, :, 0:n_ctx] = _select_segment(aff_ref[0, :, 0:n_ctx], cap_ctx, 0)
    slot_ref[0, :, n_ctx:] = _select_segment(aff_ref[0, :, n_ctx:], cap_lat, cap_ctx)


def _select(aff_t, n_ctx, cap_ctx, cap_lat):
    b, e, t = aff_t.shape
    return pl.pallas_call(
        functools.partial(_select_kernel, n_ctx=n_ctx, cap_ctx=cap_ctx, cap_lat=cap_lat),
        grid=(b,),
        in_specs=[pl.BlockSpec((1, e, t), lambda i: (i, 0, 0))],
        out_specs=pl.BlockSpec((1, e, t), lambda i: (i, 0, 0)),
        out_shape=jax.ShapeDtypeStruct((b, e, t), jnp.int32),
        compiler_params=_cparams(("parallel",)),
    )(aff_t)


def _moe_kernel(glo_ref, ghi_ref, slo_ref, shi_ref, slot_ref, aff_ref, f_ref, w1_ref, w3_ref, w2_ref, o_ref,
                xs_s, ys_s, gate_s, *, gtiles, stiles, tb, n_ft):
    b = pl.program_id(0)
    e = pl.program_id(1)
    ft = pl.program_id(2)
    n_e = pl.num_programs(1)
    nb = slot_ref.shape[2]

    def window(k, lo, w):
        start = lo + k * w
        return start, jnp.minimum(start, nb - w)

    def hits(j, start, r0, rc):
        rows = lax.broadcasted_iota(jnp.int32, (rc, tb), 0) + r0
        sv = jnp.where(j >= start, slot_ref[0, 0, pl.ds(j, 1), :], -1)
        return sv == rows

    @pl.when((e == 0) & (ft == 0))
    def _():
        o_ref[...] = jnp.zeros_like(o_ref)

    @pl.when(ft == 0)
    def _():
        base = (b * n_e + e) * len(gtiles)
        for c, (r0, rc, w) in enumerate(gtiles):
            lo = glo_ref[base + c]

            def body(k, carry, r0=r0, rc=rc, w=w, lo=lo):
                acc, gs = carry
                start, j0 = window(k, lo, w)
                parts = []
                for i in range(w):
                    hit = hits(j0 + i, start, r0, rc)
                    parts.append(jnp.where(hit, 1.0, 0.0).astype(BF16))
                    gs = gs + jnp.where(hit, aff_ref[0, 0, pl.ds(j0 + i, 1), :], 0.0)
                off = pl.multiple_of(j0 * tb, tb)
                acc = acc + _dot(jnp.concatenate(parts, axis=1), f_ref[0, pl.ds(off, w * tb), :])
                return acc, gs
            acc, gs = lax.fori_loop(
                0, (ghi_ref[base + c] - lo) // w + 1, body,
                (jnp.zeros((rc, f_ref.shape[-1]), F32), jnp.zeros((rc, tb), F32)))
            xs_s[r0:r0 + rc, :] = acc.astype(BF16)
            gate_s[r0:r0 + rc, :] = jnp.sum(gs, axis=1, keepdims=True)
        ys_s[...] = jnp.zeros_like(ys_s)

    xs = xs_s[...]
    hid = (_silu(_dot(xs, w1_ref[0, 0])) * _dot(xs, w3_ref[0, 0])).astype(BF16)
    ys_s[...] += _dot(hid, w2_ref[0, 0])

    @pl.when(ft == n_ft - 1)
    def _():
        base = (b * n_e + e) * len(stiles)
        for c, (r0, rc, w) in enumerate(stiles):
            ysg = (ys_s[r0:r0 + rc, :] * gate_s[r0:r0 + rc, :]).astype(BF16)
            lo = slo_ref[base + c]

            def body(k, carry, r0=r0, rc=rc, w=w, lo=lo, ysg=ysg):
                start, j0 = window(k, lo, w)
                for i in range(w):
                    oh = jnp.where(hits(j0 + i, start, r0, rc), 1.0, 0.0).astype(BF16)
                    off = pl.multiple_of((j0 + i) * tb, tb)
                    o_ref[0, pl.ds(off, tb), :] += _dot_tn(oh, ysg)
                return carry
            lax.fori_loop(0, (shi_ref[base + c] - lo) // w + 1, body, 0)


def _moe(layer, bounds, slot, aff_t, f, w1, w3, w2, gtiles, stiles, tb, tf):
    b, t, d = f.shape
    n_e = w1.shape[1]
    ff = w1.shape[-1]
    n_ft = ff // tf
    rows = gtiles[-1][0] + gtiles[-1][1]
    grid_spec = pltpu.PrefetchScalarGridSpec(
        num_scalar_prefetch=4,
        grid=(b, n_e, n_ft),
        in_specs=[
            pl.BlockSpec((1, 1, t // tb, tb), lambda bi, e, k, *_: (bi, e, 0, 0)),
            pl.BlockSpec((1, 1, t // tb, tb), lambda bi, e, k, *_: (bi, e, 0, 0)),
            pl.BlockSpec((1, t, d), lambda bi, e, k, *_: (bi, 0, 0)),
            pl.BlockSpec((1, 1, d, tf), lambda bi, e, k, *_: (layer, e, 0, k)),
            pl.BlockSpec((1, 1, d, tf), lambda bi, e, k, *_: (layer, e, 0, k)),
            pl.BlockSpec((1, 1, tf, d), lambda bi, e, k, *_: (layer, e, k, 0)),
        ],
        out_specs=pl.BlockSpec((1, t, d), lambda bi, e, k, *_: (bi, 0, 0), pipeline_mode=pl.Buffered(1)),
        scratch_shapes=[pltpu.VMEM((rows, d), BF16), pltpu.VMEM((rows, d), F32), pltpu.VMEM((rows, 1), F32)],
    )
    return pl.pallas_call(
        functools.partial(_moe_kernel, gtiles=gtiles, stiles=stiles, tb=tb, n_ft=n_ft),
        grid_spec=grid_spec,
        out_shape=jax.ShapeDtypeStruct((b, t, d), F32),
        compiler_params=_cparams(("parallel", "arbitrary", "arbitrary")),
    )(*bounds, slot.reshape(b, n_e, t // tb, tb), aff_t.reshape(b, n_e, t // tb, tb), f, w1, w3, w2)


def _tile_bounds(slot, tiles, tb):
    b, e, t = slot.shape
    blk = jnp.arange(t, dtype=jnp.int32) // tb
    lo, hi = [], []
    for r0, rc, _ in tiles:
        inside = (slot >= r0) & (slot < r0 + rc)
        lo.append(jnp.min(jnp.where(inside, blk, t // tb - 1), axis=-1))
        hi.append(jnp.max(jnp.where(inside, blk, 0), axis=-1))
    lo = jnp.stack(lo, axis=-1)
    hi = jnp.maximum(jnp.stack(hi, axis=-1), lo)
    return lo.reshape(-1).astype(jnp.int32), hi.reshape(-1).astype(jnp.int32)


def _route_and_moe(layer, aff, f, w1, w3, w2, n_ctx, tb, tf):
    b, t, _ = f.shape
    aff_t = jnp.swapaxes(aff[:, :, :N_EXPERTS], 1, 2)
    cap_ctx = (CAPACITY_FACTOR * n_ctx) // N_EXPERTS
    cap_lat = (CAPACITY_FACTOR * (t - n_ctx)) // N_EXPERTS
    slot = _select(aff_t, n_ctx, cap_ctx, cap_lat)
    nb = t // tb

    def tiling(lat_rows, lat_window):
        tl = min(lat_rows, cap_lat)
        w = min(lat_window, nb)
        return ((0, cap_ctx, n_ctx // tb),) + tuple((cap_ctx + i * tl, tl, w) for i in range(cap_lat // tl))

    gtiles = tiling(MOE_GATHER_ROWS, MOE_GATHER_WINDOW)
    stiles = tiling(MOE_SCATTER_ROWS, MOE_SCATTER_WINDOW)
    bounds = _tile_bounds(slot, gtiles, tb) + _tile_bounds(slot, stiles, tb)
    return _moe(layer, bounds, slot, aff_t, f, w1, w3, w2, gtiles, stiles, tb, tf)


def _final_kernel(x_ref, moe_ref, gp_ref, g_ref, o_ref):
    x = x_ref[0] + gp_ref[0, 0] * moe_ref[0]
    o_ref[0] = x * lax.rsqrt(jnp.mean(x * x, axis=-1, keepdims=True) + EPS) * g_ref[...]


def _final(x, moe, gprev, g, tm, n_ctx):
    b, t, d = x.shape
    lead = n_ctx // tm
    return pl.pallas_call(
        _final_kernel,
        grid=(b, (t - n_ctx) // tm),
        in_specs=[_row_spec(tm, d, lead), _row_spec(tm, d, lead),
                  pl.BlockSpec((1, 1, 1, d), lambda bi, i: (bi, 1, 0, 0)), _const_spec((1, d))],
        out_specs=_row_spec(tm, d),
        out_shape=jax.ShapeDtypeStruct((b, t - n_ctx, d), F32),
        compiler_params=_cparams(("parallel", "parallel")),
    )(x, moe, gprev, g)


def _rope_tables(n_ctx, n_lat):
    n_rows = n_lat // GRID_W
    row = jnp.repeat(jnp.arange(n_rows, dtype=F32), GRID_W)
    col = jnp.tile(jnp.arange(GRID_W, dtype=F32), n_rows)
    inv = ROPE_THETA ** (-jnp.arange(0, ROPE_AXIS_DIM, 2, dtype=F32) / ROPE_AXIS_DIM)
    ang_r = row[:, None] * inv[None, :]
    ang_c = col[:, None] * inv[None, :]
    cos = jnp.concatenate([jnp.cos(ang_r), jnp.cos(ang_r), jnp.cos(ang_c), jnp.cos(ang_c)], axis=-1)
    sin = jnp.concatenate([-jnp.sin(ang_r), jnp.sin(ang_r), -jnp.sin(ang_c), jnp.sin(ang_c)], axis=-1)
    cos = jnp.concatenate([jnp.ones((n_ctx, HEAD_DIM), F32), cos], axis=0)
    sin = jnp.concatenate([jnp.zeros((n_ctx, HEAD_DIM), F32), sin], axis=0)
    reps = SEG_W // HEAD_DIM
    return jnp.tile(cos, (1, reps)), jnp.tile(sin, (1, reps))


def _extend_in_proj(w_in):
    base = 4 * RET_WIDTH + ATT_WIDTH
    ak = w_in[:, base:base + ATT_KV_WIDTH].reshape(-1, ATT_KV_HEADS, 1, HEAD_DIM)
    av = w_in[:, base + ATT_KV_WIDTH:].reshape(-1, ATT_KV_HEADS, 1, HEAD_DIM)
    rep = lambda w, n: jnp.broadcast_to(w, (w.shape[0], ATT_KV_HEADS, n, HEAD_DIM)).reshape(w.shape[0], -1)
    return jnp.concatenate([w_in[:, :base], rep(ak, ATT_GROUP), rep(av, 2)], axis=-1)


def kernel(x, c, ctx, c_ctx, mod_w, mod_b, norm_g, mix_in_w, mix_out_w, ret_log_rate, ret_gn_g, qk_norm_g,
           s5_a_re, s5_a_im, s5_log_dt, s5_b_re, s5_b_im, s5_c_re, s5_c_im, s5_d, s5_glu_w, s5_glu_b,
           moe_router_w, moe_w1, moe_w3, moe_w2, final_norm_g):
    batch, n_lat, d = x.shape
    n_ctx = ctx.shape[1]
    t = n_ctx + n_lat
    depth = mod_w.shape[0]
    assert batch == 8 and d % 128 == 0 and n_ctx % CHUNK == 0 and n_lat % CHUNK == 0
    tm = 256 if n_ctx % 256 == 0 else 128
    tb = tm
    tf = 512

    xs = jnp.concatenate([ctx, x], axis=1)
    rows = 16
    cond = jnp.zeros((rows, d), F32).at[:batch].set(c).at[batch].set(c_ctx)
    mod_all = _modulation(cond, mod_w, mod_b)
    cos, sin = _rope_tables(n_ctx, n_lat)
    lane_head = jnp.arange(SEG_W) // HEAD_DIM
    bd = (lane_head[:, None] == lane_head[None, :]).astype(BF16)
    w1 = moe_w1.astype(BF16)
    w3 = moe_w3.astype(BF16)
    w2 = moe_w2.astype(BF16)

    prev = None
    for layer in range(depth):
        m = mod_all[layer]
        lat = m[:batch].reshape(batch, 1, 6, d)
        cx = jnp.broadcast_to(m[batch].reshape(1, 1, 6, d), (batch, 1, 6, d))
        modtab = jnp.concatenate([cx, lat], axis=1)
        ng1 = norm_g[layer, 0].reshape(1, d)
        ng2 = norm_g[layer, 1].reshape(1, d)
        wr = jnp.zeros((d, ROUTER_LANES), F32).at[:, :N_EXPERTS].set(moe_router_w[layer])
        wrh = wr.astype(BF16)
        wrl = (wr - wrh.astype(F32)).astype(BF16)
        i = layer // 2
        if layer % 2 == 0:
            w_ext = _extend_in_proj(mix_in_w[i]).astype(BF16)
            reps = SEG_W // HEAD_DIM
            qkg = jnp.stack([jnp.tile(qk_norm_g[i, 0], reps), jnp.tile(qk_norm_g[i, 1], reps)])
            xs, (rq, rk, rv, rg, aq, ak4, av2) = _in_proj(xs, prev, modtab, ng1, w_ext, cos, sin, qkg, bd,
                                                         tm, n_ctx)
            rate = jnp.repeat(ret_log_rate[i], HEAD_DIM, axis=-1)
            rate_h = jnp.broadcast_to(ret_log_rate[i][:, :, None], (2, RET_HEADS, CHUNK))
            o_f, o_b = _retention(rq, rk, rv, rate, rate_h, bd, n_ctx)
            att = _attention(aq, ak4, av2, tm, tm, n_ctx)
            xs, f, aff = _out_proj(xs, o_f, o_b, rg, att, modtab, ret_gn_g[i].reshape(1, SEG_W),
                                   mix_out_w[i].astype(BF16), ng2, wrh, wrl, bd, tm, n_ctx)
        else:
            xs, u = _s5_pre(xs, prev, modtab, ng1, tm, n_ctx)
            tabs = _s5_tables(s5_a_re[i], s5_a_im[i], s5_log_dt[i], s5_b_re[i], s5_b_im[i],
                              s5_c_re[i], s5_c_im[i], s5_d[i])
            z = _s5_scan(u, *tabs, n_ctx)
            xs, f, aff = _glu(xs, z, modtab, s5_glu_w[i].astype(BF16), s5_glu_b[i].reshape(1, 2 * d),
                              ng2, wrh, wrl, tm, n_ctx)
        moe = _route_and_moe(layer, aff, f, w1, w3, w2, n_ctx, tb, tf)
        prev = (moe, modtab[:, :, 5:6, :])
    return _final(xs, prev[0], prev[1], final_norm_g.reshape(1, d), tm, n_ctx)
```

```python
import functools
import math

import jax
import jax.numpy as jnp
from jax import lax
from jax.experimental import pallas as pl
from jax.experimental.pallas import tpu as pltpu

F32 = jnp.float32
BF16 = jnp.bfloat16

GRID_W = 64
HEAD_DIM = 64
RET_HEADS = 8
ATT_HEADS = 8
ATT_KV_HEADS = 2
ATT_GROUP = ATT_HEADS // ATT_KV_HEADS
RET_WIDTH = RET_HEADS * HEAD_DIM
ATT_WIDTH = ATT_HEADS * HEAD_DIM
ATT_KV_WIDTH = ATT_KV_HEADS * HEAD_DIM
CHUNK = 128
ROPE_THETA = 10000.0
ROPE_AXIS_DIM = HEAD_DIM // 2
S5_GROUP = 16
S5_STATE = 64
S5_SUB = 8
S5_BLOCK_GROUPS = 8
S5_SCAN_UNROLL = 4
N_EXPERTS = 16
CAPACITY_FACTOR = 2
EPS = 1e-6
SEG_W = 512
LOG2_E = math.log2(math.e)
ROUTER_LANES = 128
MOE_GATHER_ROWS = 128
MOE_GATHER_WINDOW = 6
MOE_SCATTER_ROWS = 256
MOE_SCATTER_WINDOW = 10

V7X_VMEM_BYTES = 64 * 1024 * 1024
VMEM_LIMIT = V7X_VMEM_BYTES - 8 * 1024 * 1024


def _cparams(sem):
    return pltpu.CompilerParams(dimension_semantics=sem, vmem_limit_bytes=VMEM_LIMIT)


def _dot(a, b):
    return jnp.dot(a, b, preferred_element_type=F32)


def _dot_nt(a, b):
    return lax.dot_general(a, b, (((1,), (1,)), ((), ())), preferred_element_type=F32)


def _dot_tn(a, b):
    return lax.dot_general(a, b, (((0,), (0,)), ((), ())), preferred_element_type=F32)


def _split(v):
    hi = v.astype(BF16)
    lo = (v - hi.astype(F32)).astype(BF16)
    return hi, lo


def _segsum(v, bd):
    hi, lo = _split(v)
    return _dot(hi, bd) + _dot(lo, bd)


def _sigmoid(v):
    return 1.0 / (1.0 + jnp.exp(-v))


def _silu(v):
    return v * _sigmoid(v)


def _gelu_tanh(v):
    c = math.sqrt(2.0 / math.pi)
    return 0.5 * v * (1.0 + jnp.tanh(c * (v + 0.044715 * (v * v * v))))


def _norm_mod(x, g, scale, shift):
    ms = jnp.mean(x * x, axis=-1, keepdims=True)
    return x * lax.rsqrt(ms + EPS) * g * (1.0 + scale) + shift


def _rope(x, cos, sin_signed):
    w = x.shape[-1]
    lane = lax.broadcasted_iota(jnp.int32, x.shape, 1)
    first = (lane % 32) < 16
    partner = jnp.where(first, pltpu.roll(x, w - 16, 1), pltpu.roll(x, 16, 1))
    return x * cos + partner * sin_signed


def _mod_kernel(c_ref, w_ref, b_ref, o_ref):
    s = _silu(c_ref[...]).astype(BF16)
    o_ref[0] = _dot(s, w_ref[0].astype(BF16)) + b_ref[0]


def _modulation(cond, mod_w, mod_b):
    depth, d, n6 = mod_w.shape
    rows = cond.shape[0]
    tn = n6 // 6
    return pl.pallas_call(
        _mod_kernel,
        grid=(depth, n6 // tn),
        in_specs=[
            pl.BlockSpec((rows, d), lambda l, j: (0, 0)),
            pl.BlockSpec((1, d, tn), lambda l, j: (l, 0, j)),
            pl.BlockSpec((1, 1, tn), lambda l, j: (l, 0, j)),
        ],
        out_specs=pl.BlockSpec((1, rows, tn), lambda l, j: (l, 0, j)),
        out_shape=jax.ShapeDtypeStruct((depth, rows, n6), F32),
        compiler_params=_cparams(("parallel", "parallel")),
    )(cond, mod_w, mod_b.reshape(depth, 1, n6))


def _residual_in(has_prev, refs):
    if has_prev:
        x_ref, moe_ref, gp_ref = refs[:3]
        return x_ref[0] + gp_ref[0, 0] * moe_ref[0], refs[3:]
    return refs[0][0], refs[1:]


def _in_kernel(has_prev, *refs):
    x, refs = _residual_in(has_prev, refs)
    mod_ref, ng_ref, w_ref, cos_ref, sin_ref, qkg_ref, bd_ref = refs[:7]
    outs = refs[7:]
    if has_prev:
        outs[0][0] = x
        outs = outs[1:]
    rq_ref, rk_ref, rv_ref, rg_ref, aq_ref, ak_ref, av_ref = outs
    m = mod_ref[0, 0]
    h = _norm_mod(x, ng_ref[...], m[1:2], m[0:1]).astype(BF16)
    cos = cos_ref[...]
    sin = sin_ref[...]
    bd = bd_ref[...]

    def seg(i):
        return _dot(h, w_ref[:, i * SEG_W:(i + 1) * SEG_W])

    def headnorm(v, g):
        ms = _segsum(v * v, bd) * (1.0 / HEAD_DIM)
        return v * lax.rsqrt(ms + EPS) * g

    scale = HEAD_DIM ** -0.5
    rq_ref[0] = _rope(seg(0), cos, sin).astype(BF16)
    rk_ref[0] = _rope(seg(1) * scale, cos, sin).astype(BF16)
    rv_ref[0] = seg(2).astype(BF16)
    rg_ref[0] = seg(3)
    aq_ref[0] = (_rope(headnorm(seg(4), qkg_ref[0:1]), cos, sin) * (scale * LOG2_E)).astype(BF16)
    ak_ref[0] = _rope(headnorm(seg(5), qkg_ref[1:2]), cos, sin).astype(BF16)
    av_ref[0] = _dot(h, w_ref[:, 6 * SEG_W:]).astype(BF16)


def _row_spec(tm, d, lead=0):
    return pl.BlockSpec((1, tm, d), lambda b, i: (b, i + lead, 0))


def _mod_spec(d, n, seg_tiles):
    return pl.BlockSpec((1, 1, n, d), lambda b, i: (b, jnp.where(i >= seg_tiles, 1, 0), 0, 0))


def _const_spec(shape):
    nd = len(shape)
    return pl.BlockSpec(shape, lambda b, i: (0,) * nd)


def _in_proj(x, prev, modtab, norm_g, w_ext, cos, sin, qkg, bd, tm, n_ctx):
    b, t, d = x.shape
    seg_tiles = n_ctx // tm
    has_prev = prev is not None
    args, specs = [x], [_row_spec(tm, d)]
    if has_prev:
        moe, gprev = prev
        args += [moe, gprev]
        specs += [_row_spec(tm, d), _mod_spec(d, 1, seg_tiles)]
    args += [modtab, norm_g, w_ext, cos, sin, qkg, bd]
    specs += [
        _mod_spec(d, 6, seg_tiles),
        _const_spec((1, d)),
        _const_spec(w_ext.shape),
        pl.BlockSpec((tm, SEG_W), lambda b, i: (i, 0)),
        pl.BlockSpec((tm, SEG_W), lambda b, i: (i, 0)),
        _const_spec((2, SEG_W)),
        _const_spec((SEG_W, SEG_W)),
    ]
    out_shape, out_specs = [], []
    if has_prev:
        out_shape.append(jax.ShapeDtypeStruct((b, t, d), F32))
        out_specs.append(_row_spec(tm, d))
    for dt, width in ((BF16, SEG_W), (BF16, SEG_W), (BF16, SEG_W), (F32, SEG_W), (BF16, SEG_W), (BF16, SEG_W),
                      (BF16, ATT_KV_HEADS * 2 * HEAD_DIM)):
        out_shape.append(jax.ShapeDtypeStruct((b, t, width), dt))
        out_specs.append(_row_spec(tm, width))
    outs = pl.pallas_call(
        functools.partial(_in_kernel, has_prev),
        grid=(b, t // tm),
        in_specs=specs,
        out_specs=out_specs,
        out_shape=out_shape,
        compiler_params=_cparams(("parallel", "parallel")),
    )(*args)
    if has_prev:
        return outs[0], outs[1:]
    return x, outs


def _ret_kernel(qf_ref, kf_ref, vf_ref, qb_ref, kb_ref, vb_ref, rate_ref, rate_h_ref, bd_ref,
                of_ref, ob_ref, sf_s, sb_s, dm_s, dq_s, dk_s, dc_s):
    s = pl.program_id(1)
    c = CHUNK

    @pl.when(s == 0)
    def _():
        sf_s[...] = jnp.zeros_like(sf_s)
        sb_s[...] = jnp.zeros_like(sb_s)
        pos = lax.broadcasted_iota(jnp.int32, (c, SEG_W), 0).astype(F32)
        lgf = -jnp.exp(rate_ref[0:1, :])
        lgb = -jnp.exp(rate_ref[1:2, :])
        dq_s[0] = jnp.exp((pos + 1.0) * lgf)
        dk_s[0] = jnp.exp((c - 1.0 - pos) * lgf)
        dc_s[0] = jnp.exp(float(c) * lgf)
        dq_s[1] = jnp.exp((c - pos) * lgb)
        dk_s[1] = jnp.exp(pos * lgb)
        dc_s[1] = jnp.exp(float(c) * lgb)
        ii = lax.broadcasted_iota(jnp.int32, (c, c), 0)
        jj = lax.broadcasted_iota(jnp.int32, (c, c), 1)
        for h in range(RET_HEADS):
            lf = -jnp.exp(rate_h_ref[0, h:h + 1, :])
            lb = -jnp.exp(rate_h_ref[1, h:h + 1, :])
            mf = ii >= jj
            mb = jj > ii
            dm_s[0, h] = jnp.where(mf, jnp.exp(jnp.where(mf, ii - jj, 0).astype(F32) * lf), 0.0)
            dm_s[1, h] = jnp.where(mb, jnp.exp(jnp.where(mb, jj - ii, 0).astype(F32) * lb), 0.0)

    bdm = bd_ref[...].astype(F32)
    head_of_lane = lax.broadcasted_iota(jnp.int32, (1, SEG_W // 2), 1) // HEAD_DIM

    def direction(d, q_ref, k_ref, v_ref, o_ref, st):
        q = q_ref[0]
        k = k_ref[0]
        v = v_ref[0]
        state = st[...]
        qd = (q.astype(F32) * dq_s[d]).astype(BF16)
        inter = _dot(qd, state.astype(BF16))
        for half in range(2):
            sl = slice(half * 256, half * 256 + 256)
            qh, kh, vh = q[:, sl], k[:, sl], v[:, sl]
            acc = inter[:, sl]
            for hh in range(4):
                msk = head_of_lane == hh
                sc = _dot_nt(jnp.where(msk, qh, jnp.zeros_like(qh)), kh) * dm_s[d, half * 4 + hh]
                acc = acc + _dot(sc.astype(BF16), jnp.where(msk, vh, jnp.zeros_like(vh)))
            o_ref[0, :, sl] = acc
        kd = (k.astype(F32) * dk_s[d]).astype(BF16)
        st[...] = state * dc_s[d] + bdm * _dot_tn(kd, v)

    direction(0, qf_ref, kf_ref, vf_ref, of_ref, sf_s)
    direction(1, qb_ref, kb_ref, vb_ref, ob_ref, sb_s)


def _retention(rq, rk, rv, rate, rate_h, bd, n_ctx):
    b, t, w = rq.shape
    nc = t // CHUNK
    lc = n_ctx // CHUNK

    def fwd(bi, s):
        return (bi, s, 0)

    def bwd(bi, s):
        return (bi, jnp.where(s < lc, lc - 1 - s, nc - 1 - (s - lc)), 0)

    blk = (1, CHUNK, w)
    return pl.pallas_call(
        _ret_kernel,
        grid=(b, nc),
        in_specs=[pl.BlockSpec(blk, fwd)] * 3 + [pl.BlockSpec(blk, bwd)] * 3 + [
            _const_spec((2, w)),
            _const_spec((2, RET_HEADS, CHUNK)),
            _const_spec((w, w)),
        ],
        out_specs=[pl.BlockSpec(blk, fwd), pl.BlockSpec(blk, bwd)],
        out_shape=[jax.ShapeDtypeStruct((b, t, w), F32)] * 2,
        scratch_shapes=[
            pltpu.VMEM((w, w), F32), pltpu.VMEM((w, w), F32),
            pltpu.VMEM((2, RET_HEADS, CHUNK, CHUNK), F32),
            pltpu.VMEM((2, CHUNK, w), F32), pltpu.VMEM((2, CHUNK, w), F32),
            pltpu.VMEM((2, 1, w), F32),
        ],
        compiler_params=_cparams(("parallel", "arbitrary")),
    )(rq, rk, rv, rq, rk, rv, rate, rate_h, bd)


ATT_ROW_BLOCK = 128


def _att_kernel(q_ref, k_ref, v_ref, o_ref, qs_s, m_s, acc_s, sa_s, sb_s, *, tq, tk, n_ctx, t):
    qi = pl.program_id(2)
    q = q_ref[0]
    gw = ATT_GROUP * HEAD_DIM
    head_of_lane = lax.broadcasted_iota(jnp.int32, (1, gw), 1) // HEAD_DIM
    for h in range(ATT_GROUP):
        qs_s[h * tq:(h + 1) * tq, :] = jnp.where(head_of_lane == h, q, jnp.zeros_like(q))
    m_s[...] = jnp.full(m_s.shape, -1e30, F32)
    acc_s[...] = jnp.zeros_like(acc_s)
    nk = jnp.where(qi * tq < n_ctx, n_ctx // tk, t // tk)
    vw = 2 * HEAD_DIM
    ones_lane = lax.broadcasted_iota(jnp.int32, (1, vw), 1) >= HEAD_DIM
    rb = ATT_ROW_BLOCK

    blocks = [slice(r * rb, (r + 1) * rb) for r in range(ATT_GROUP * tq // rb)]

    def scores_into(j, dst):
        kb = k_ref[0, pl.ds(pl.multiple_of(j * tk, tk), tk), :]
        for rows in blocks:
            dst[rows, :] = _dot_nt(qs_s[rows, :], kb)

    def consume(j, src):
        vb = v_ref[0, pl.ds(pl.multiple_of(j * tk, tk), tk), :]
        vext = jnp.where(ones_lane, jnp.ones_like(vb), vb)
        for rows in blocks:
            sc = src[rows, :]
            m_prev = m_s[rows, :]
            m_new = jnp.maximum(m_prev, jnp.max(sc, axis=1, keepdims=True))
            p = jnp.exp2(sc - jnp.concatenate([m_new] * (tk // vw), axis=1)).astype(BF16)
            acc_s[rows, :] = jnp.exp2(m_prev - m_new) * acc_s[rows, :] + _dot(p, vext)
            m_s[rows, :] = m_new

    scores_into(0, sa_s)

    def body(i, carry):
        j = 2 * i
        scores_into(j + 1, sb_s)
        consume(j, sa_s)
        scores_into(jnp.minimum(j + 2, nk - 1), sa_s)
        consume(j + 1, sb_s)
        return carry

    lax.fori_loop(0, nk // 2, body, 0)

    @pl.when(nk % 2 == 1)
    def _():
        consume(nk - 1, sa_s)

    low = lax.broadcasted_iota(jnp.int32, (tq, vw), 1) < HEAD_DIM
    for pair in range(ATT_GROUP // 2):
        a0 = acc_s[(2 * pair) * tq:(2 * pair + 1) * tq, :]
        a1 = acc_s[(2 * pair + 1) * tq:(2 * pair + 2) * tq, :]
        even = a0 / pltpu.roll(a0, HEAD_DIM, 1)
        odd = pltpu.roll(a1, HEAD_DIM, 1) / a1
        o_ref[0, :, pair * vw:(pair + 1) * vw] = jnp.where(low, even, odd).astype(BF16)


def _attention(aq, ak4, av2, tq, tk, n_ctx):
    b, t, w = aq.shape
    gw = ATT_GROUP * HEAD_DIM
    vw = 2 * HEAD_DIM
    return pl.pallas_call(
        functools.partial(_att_kernel, tq=tq, tk=tk, n_ctx=n_ctx, t=t),
        grid=(b, ATT_KV_HEADS, t // tq),
        in_specs=[
            pl.BlockSpec((1, tq, gw), lambda bi, g, i: (bi, i, g)),
            pl.BlockSpec((1, t, gw), lambda bi, g, i: (bi, 0, g)),
            pl.BlockSpec((1, t, vw), lambda bi, g, i: (bi, 0, g)),
        ],
        out_specs=pl.BlockSpec((1, tq, gw), lambda bi, g, i: (bi, i, g)),
        out_shape=jax.ShapeDtypeStruct((b, t, w), BF16),
        scratch_shapes=[pltpu.VMEM((ATT_GROUP * tq, gw), BF16), pltpu.VMEM((ATT_GROUP * tq, vw), F32),
                        pltpu.VMEM((ATT_GROUP * tq, vw), F32), pltpu.VMEM((ATT_GROUP * tq, tk), F32),
                        pltpu.VMEM((ATT_GROUP * tq, tk), F32)],
        compiler_params=_cparams(("parallel", "parallel", "parallel")),
    )(aq, ak4, av2)


def _post_mixer(x, delta, m, ng2, wrh, wrl, x_ref, f_ref, aff_ref):
    xn = x + m[2:3] * delta
    x_ref[0] = xn
    f = _norm_mod(xn, ng2, m[4:5], m[3:4])
    fh, fl = _split(f)
    f_ref[0] = fh
    logits = _dot(fh, wrh) + _dot(fl, wrh) + _dot(fh, wrl)
    lane = lax.broadcasted_iota(jnp.int32, logits.shape, 1)
    logits = jnp.where(lane < N_EXPERTS, logits, -1e30)
    e = jnp.exp(logits - jnp.max(logits, axis=1, keepdims=True))
    aff_ref[0] = e / jnp.sum(e, axis=1, keepdims=True)


def _out_kernel(x_ref, of_ref, ob_ref, rg_ref, att_ref, mod_ref, gn_ref, w_ref, ng2_ref, wrh_ref, wrl_ref,
                bd_ref, xo_ref, f_ref, aff_ref):
    bd = bd_ref[...]
    o = of_ref[0] + ob_ref[0]
    mu = _segsum(o, bd) * (1.0 / HEAD_DIM)
    oc = o - mu
    var = _segsum(oc * oc, bd) * (1.0 / HEAD_DIM)
    r = _silu(rg_ref[0]) * (oc * lax.rsqrt(var + EPS) * gn_ref[...])
    y = _dot(r.astype(BF16), w_ref[0:RET_WIDTH, :]) + _dot(att_ref[0], w_ref[RET_WIDTH:, :])
    _post_mixer(x_ref[0], y, mod_ref[0, 0], ng2_ref[...], wrh_ref[...], wrl_ref[...], xo_ref, f_ref, aff_ref)


def _post_specs(b, t, d, tm):
    out_shape = [jax.ShapeDtypeStruct((b, t, d), F32), jax.ShapeDtypeStruct((b, t, d), BF16),
                 jax.ShapeDtypeStruct((b, t, ROUTER_LANES), F32)]
    out_specs = [_row_spec(tm, d), _row_spec(tm, d), _row_spec(tm, ROUTER_LANES)]
    return out_shape, out_specs


def _out_proj(x, of, ob, rg, att, modtab, gn_g, w_out, ng2, wrh, wrl, bd, tm, n_ctx):
    b, t, d = x.shape
    out_shape, out_specs = _post_specs(b, t, d, tm)
    return pl.pallas_call(
        _out_kernel,
        grid=(b, t // tm),
        in_specs=[
            _row_spec(tm, d), _row_spec(tm, SEG_W), _row_spec(tm, SEG_W), _row_spec(tm, SEG_W),
            _row_spec(tm, SEG_W), _mod_spec(d, 6, n_ctx // tm), _const_spec((1, SEG_W)),
            _const_spec(w_out.shape), _const_spec((1, d)), _const_spec(wrh.shape), _const_spec(wrl.shape),
            _const_spec((SEG_W, SEG_W)),
        ],
        out_specs=out_specs,
        out_shape=out_shape,
        compiler_params=_cparams(("parallel", "parallel")),
    )(x, of, ob, rg, att, modtab, gn_g, w_out, ng2, wrh, wrl, bd)


def _s5_pre_kernel(has_prev, *refs):
    x, refs = _residual_in(has_prev, refs)
    mod_ref, ng_ref = refs[:2]
    outs = refs[2:]
    if has_prev:
        outs[0][0] = x
        outs = outs[1:]
    m = mod_ref[0, 0]
    outs[0][0] = _norm_mod(x, ng_ref[...], m[1:2], m[0:1])


def _s5_pre(x, prev, modtab, norm_g, tm, n_ctx):
    b, t, d = x.shape
    seg_tiles = n_ctx // tm
    has_prev = prev is not None
    args, specs = [x], [_row_spec(tm, d)]
    if has_prev:
        args += list(prev)
        specs += [_row_spec(tm, d), _mod_spec(d, 1, seg_tiles)]
    args += [modtab, norm_g]
    specs += [_mod_spec(d, 6, seg_tiles), _const_spec((1, d))]
    n_out = 2 if has_prev else 1
    outs = pl.pallas_call(
        functools.partial(_s5_pre_kernel, has_prev),
        grid=(b, t // tm),
        in_specs=specs,
        out_specs=[_row_spec(tm, d)] * n_out,
        out_shape=[jax.ShapeDtypeStruct((b, t, d), F32)] * n_out,
        compiler_params=_cparams(("parallel", "parallel")),
    )(*args)
    if has_prev:
        return outs[0], outs[1]
    return x, outs[0]


def _s5_kernel(u_ref, tm_ref, wz_ref, wo_ref, ar_ref, ai_ref, d_ref, z_ref,
               zr_s, zi_s, xr_s, xi_s, zs_s, *, n_sub, n_sub_ctx):
    s_n, j_sub, gb, half = n_sub, S5_SUB, S5_BLOCK_GROUPS, 2 * S5_STATE
    uf = jnp.concatenate([u_ref[0, pl.ds(m, s_n, stride=j_sub), :] for m in range(j_sub)], axis=1)
    lhs = uf.astype(BF16)
    for g in range(gb):
        zz = _dot(lhs, wz_ref[0, g])
        zr_s[pl.ds(g, s_n, stride=gb), :] = zz[:, :half]
        zi_s[pl.ds(g, s_n, stride=gb), :] = zz[:, half:]
    fwd_lane = lax.broadcasted_iota(jnp.int32, (gb, half), 1) < S5_STATE
    ar = ar_ref[0]
    ai = ai_ref[0]

    def body(i, carry):
        xr, xi = carry
        sb = jnp.where(i < n_sub_ctx, n_sub_ctx - 1 - i, s_n - 1 - (i - n_sub_ctx))
        fwd = pl.ds(pl.multiple_of(i * gb, gb), gb)
        bwd = pl.ds(pl.multiple_of(sb * gb, gb), gb)
        xr_s[fwd, 0:S5_STATE] = xr[:, 0:S5_STATE]
        xi_s[fwd, 0:S5_STATE] = xi[:, 0:S5_STATE]
        xr_s[bwd, S5_STATE:half] = xr[:, S5_STATE:half]
        xi_s[bwd, S5_STATE:half] = xi[:, S5_STATE:half]
        zr = jnp.where(fwd_lane, zr_s[fwd, :], zr_s[bwd, :])
        zi = jnp.where(fwd_lane, zi_s[fwd, :], zi_s[bwd, :])
        return ar * xr - ai * xi + zr, ar * xi + ai * xr + zi

    zero = jnp.zeros((gb, half), F32)
    lax.fori_loop(0, s_n, body, (zero, zero), unroll=S5_SCAN_UNROLL)
    y = _dot(lhs, tm_ref[0]) + d_ref[0] * uf
    for g in range(gb):
        rows = pl.ds(g, s_n, stride=gb)
        xp = jnp.concatenate([xr_s[rows, :], xi_s[rows, :]], axis=1).astype(BF16)
        y = y + _dot(xp, wo_ref[0, g])
    z = _gelu_tanh(y)
    w = z_ref.shape[-1]
    for n in range(j_sub):
        zs_s[pl.ds(n, s_n, stride=j_sub), :] = z[:, n * w:(n + 1) * w]
    z_ref[0] = zs_s[...].astype(BF16)


def _s5_scan(u, tmat, wz, wo, ar, ai, dt, n_ctx):
    b, t, d = u.shape
    cw = S5_BLOCK_GROUPS * S5_GROUP
    n_sub = t // S5_SUB
    lw = S5_SUB * cw
    sw = 4 * S5_STATE
    gb = S5_BLOCK_GROUPS
    once = pl.Buffered(1)
    return pl.pallas_call(
        functools.partial(_s5_kernel, n_sub=n_sub, n_sub_ctx=n_ctx // S5_SUB),
        grid=(d // cw, b),
        in_specs=[
            pl.BlockSpec((1, t, cw), lambda c, bi: (bi, 0, c)),
            pl.BlockSpec((1, lw, lw), lambda c, bi: (c, 0, 0), pipeline_mode=once),
            pl.BlockSpec((1, gb, lw, sw), lambda c, bi: (c, 0, 0, 0), pipeline_mode=once),
            pl.BlockSpec((1, gb, sw, lw), lambda c, bi: (c, 0, 0, 0), pipeline_mode=once),
            pl.BlockSpec((1, gb, 2 * S5_STATE), lambda c, bi: (c, 0, 0)),
            pl.BlockSpec((1, gb, 2 * S5_STATE), lambda c, bi: (c, 0, 0)),
            pl.BlockSpec((1, 1, lw), lambda c, bi: (c, 0, 0)),
        ],
        out_specs=pl.BlockSpec((1, t, cw), lambda c, bi: (bi, 0, c)),
        out_shape=jax.ShapeDtypeStruct((b, t, d), BF16),
        scratch_shapes=[pltpu.VMEM((gb * n_sub, 2 * S5_STATE), F32)] * 4 + [pltpu.VMEM((t, cw), F32)],
        compiler_params=_cparams(("parallel", "parallel")),
    )(u, tmat, wz, wo, ar, ai, dt)


def _s5_tables(a_re, a_im, log_dt, b_re, b_im, c_re, c_im, d_skip):
    hp = lax.Precision.HIGHEST
    j_sub = S5_SUB
    gb = S5_BLOCK_GROUPS
    lam_re = jnp.minimum(a_re, -1e-4)
    lam_im = a_im
    dt = jnp.exp(log_dt)[..., None]
    mag = jnp.exp(lam_re * dt)
    bar_re = mag * jnp.cos(lam_im * dt)
    bar_im = mag * jnp.sin(lam_im * dt)
    den = lam_re * lam_re + lam_im * lam_im
    nr, ni = bar_re - 1.0, bar_im
    k_re = (nr * lam_re + ni * lam_im) / den
    k_im = (ni * lam_re - nr * lam_im) / den
    bb_re = k_re[..., None] * b_re[None] - k_im[..., None] * b_im[None]
    bb_im = k_re[..., None] * b_im[None] + k_im[..., None] * b_re[None]
    jj = jnp.arange(j_sub + 1, dtype=F32)[:, None, None, None]
    pmag = jnp.exp(jj * (lam_re * dt)[None])
    pw_re = pmag * jnp.cos(jj * (lam_im * dt)[None])
    pw_im = pmag * jnp.sin(jj * (lam_im * dt)[None])
    pb_re = pw_re[..., None] * bb_re[None] - pw_im[..., None] * bb_im[None]
    pb_im = pw_re[..., None] * bb_im[None] + pw_im[..., None] * bb_re[None]
    cp_re = c_re[None] * pw_re[:, :, :, None, :] - c_im[None] * pw_im[:, :, :, None, :]
    cp_im = c_re[None] * pw_im[:, :, :, None, :] + c_im[None] * pw_re[:, :, :, None, :]
    kern = (jnp.einsum('dgcp,jdgpk->jdgkc', c_re, pb_re[:j_sub], precision=hp)
            - jnp.einsum('dgcp,jdgpk->jdgkc', c_im, pb_im[:j_sub], precision=hp))
    groups = a_re.shape[1]
    nblk = groups // gb
    k_w = S5_GROUP
    p_w = S5_STATE
    lw = j_sub * gb * k_w
    eye = jnp.eye(gb, dtype=F32)
    blk = lambda a: a.reshape(a.shape[0], nblk, gb, *a.shape[2:])
    pos = jnp.arange(j_sub)
    lag = pos[None, :] - pos[:, None]
    kf = jnp.where((lag >= 0)[:, :, None, None, None], kern[jnp.clip(lag, 0, j_sub - 1), 0], 0.0)
    kb = jnp.where((lag <= 0)[:, :, None, None, None], kern[jnp.clip(-lag, 0, j_sub - 1), 1], 0.0)
    kfull = (kf + kb).reshape(j_sub, j_sub, nblk, gb, k_w, k_w)
    tmat = jnp.einsum('mnBgkc,gh->Bmgknhc', kfull, eye).reshape(nblk, lw, lw)
    rev = pos[::-1]
    zc = jnp.concatenate([pb_re[rev, 0], pb_re[pos, 1], pb_im[rev, 0], pb_im[pos, 1]], axis=2)
    wz = jnp.einsum('mBgqk,gh->Bgmhkq', blk(zc), eye).reshape(nblk, gb, lw, 4 * p_w)
    def orows(cp, d, order):
        return jnp.einsum('nBgcp,gh->Bgpnhc', blk(cp[order, d]), eye).reshape(nblk, gb, p_w, lw)
    wo = jnp.concatenate([orows(cp_re, 0, pos + 1), orows(cp_re, 1, j_sub - pos),
                          -orows(cp_im, 0, pos + 1), -orows(cp_im, 1, j_sub - pos)], axis=2)
    ar = jnp.concatenate([pw_re[j_sub, 0], pw_re[j_sub, 1]], axis=-1).reshape(nblk, gb, 2 * p_w)
    ai = jnp.concatenate([pw_im[j_sub, 0], pw_im[j_sub, 1]], axis=-1).reshape(nblk, gb, 2 * p_w)
    dt_tile = jnp.tile(d_skip.reshape(nblk, 1, gb * k_w), (1, 1, j_sub))
    return tmat.astype(BF16), wz.astype(BF16), wo.astype(BF16), ar, ai, dt_tile


def _glu_kernel(x_ref, z_ref, mod_ref, w_ref, b_ref, ng2_ref, wrh_ref, wrl_ref, xo_ref, f_ref, aff_ref):
    d = x_ref.shape[-1]
    ag = _dot(z_ref[0], w_ref[...]) + b_ref[...]
    delta = ag[:, :d] * _sigmoid(ag[:, d:])
    _post_mixer(x_ref[0], delta, mod_ref[0, 0], ng2_ref[...], wrh_ref[...], wrl_ref[...], xo_ref, f_ref, aff_ref)


def _glu(x, z, modtab, glu_w, glu_b, ng2, wrh, wrl, tm, n_ctx):
    b, t, d = x.shape
    out_shape, out_specs = _post_specs(b, t, d, tm)
    return pl.pallas_call(
        _glu_kernel,
        grid=(b, t // tm),
        in_specs=[_row_spec(tm, d), _row_spec(tm, d), _mod_spec(d, 6, n_ctx // tm),
                  _const_spec(glu_w.shape), _const_spec((1, 2 * d)), _const_spec((1, d)),
                  _const_spec(wrh.shape), _const_spec(wrl.shape)],
        out_specs=out_specs,
        out_shape=out_shape,
        compiler_params=_cparams(("parallel", "parallel")),
    )(x, z, modtab, glu_w, glu_b, ng2, wrh, wrl)


def _cumsum_lanes(v):
    n = v.shape[-1]
    lane = lax.broadcasted_iota(jnp.int32, v.shape, 1)
    sh = 1
    while sh < n:
        v = v + jnp.where(lane >= sh, pltpu.roll(v, sh, 1), 0.0)
        sh *= 2
    return v


def _select_segment(aff, cap, base):
    bits = pltpu.bitcast(aff, jnp.int32)
    thr = jnp.zeros((aff.shape[0], 1), jnp.int32)
    for bit in range(30, -1, -1):
        cand = thr | (1 << bit)
        cnt = jnp.sum(jnp.where(bits >= cand, 1.0, 0.0), axis=1, keepdims=True)
        thr = jnp.where(cnt >= cap, cand, thr)
    gt = bits > thr
    eq = bits == thr
    need = cap - jnp.sum(jnp.where(gt, 1.0, 0.0), axis=1, keepdims=True)
    eqf = jnp.where(eq, 1.0, 0.0)
    eq_before = _cumsum_lanes(eqf) - eqf
    sel = jnp.where(gt, 1.0, jnp.where(eq, jnp.where(eq_before < need, 1.0, 0.0), 0.0))
    pos = _cumsum_lanes(sel) - sel
    return jnp.where(sel > 0.0, pos.astype(jnp.int32) + base, -1)


def _select_kernel(aff_ref, slot_ref, *, n_ctx, cap_ctx, cap_lat):
    slot_ref[0, :, 0:n_ctx] = _select_segment(aff_ref[0, :, 0:n_ctx], cap_ctx, 0)
    slot_ref[0, :, n_ctx:] = _select_segment(aff_ref[0, :, n_ctx:], cap_lat, cap_ctx)


def _select(aff_t, n_ctx, cap_ctx, cap_lat):
    b, e, t = aff_t.shape
    return pl.pallas_call(
        functools.partial(_select_kernel, n_ctx=n_ctx, cap_ctx=cap_ctx, cap_lat=cap_lat),
        grid=(b,),
        in_specs=[pl.BlockSpec((1, e, t), lambda i: (i, 0, 0))],
        out_specs=pl.BlockSpec((1, e, t), lambda i: (i, 0, 0)),
        out_shape=jax.ShapeDtypeStruct((b, e, t), jnp.int32),
        compiler_params=_cparams(("parallel",)),
    )(aff_t)


def _moe_kernel(glo_ref, ghi_ref, slo_ref, shi_ref, slot_ref, aff_ref, f_ref, w1_ref, w3_ref, w2_ref, o_ref,
                xs_s, ys_s, gate_s, *, gtiles, stiles, tb, n_ft):
    b = pl.program_id(0)
    e = pl.program_id(1)
    ft = pl.program_id(2)
    n_e = pl.num_programs(1)
    nb = slot_ref.shape[2]

    def window(k, lo, w):
        start = lo + k * w
        return start, jnp.minimum(start, nb - w)

    def hits(j, start, r0, rc):
        rows = lax.broadcasted_iota(jnp.int32, (rc, tb), 0) + r0
        sv = jnp.where(j >= start, slot_ref[0, 0, pl.ds(j, 1), :], -1)
        return sv == rows

    @pl.when((e == 0) & (ft == 0))
    def _():
        o_ref[...] = jnp.zeros_like(o_ref)

    @pl.when(ft == 0)
    def _():
        base = (b * n_e + e) * len(gtiles)
        for c, (r0, rc, w) in enumerate(gtiles):
            lo = glo_ref[base + c]

            def body(k, carry, r0=r0, rc=rc, w=w, lo=lo):
                acc, gs = carry
                start, j0 = window(k, lo, w)
                parts = []
                for i in range(w):
                    hit = hits(j0 + i, start, r0, rc)
                    parts.append(jnp.where(hit, 1.0, 0.0).astype(BF16))
                    gs = gs + jnp.where(hit, aff_ref[0, 0, pl.ds(j0 + i, 1), :], 0.0)
                off = pl.multiple_of(j0 * tb, tb)
                acc = acc + _dot(jnp.concatenate(parts, axis=1), f_ref[0, pl.ds(off, w * tb), :])
                return acc, gs
            acc, gs = lax.fori_loop(
                0, (ghi_ref[base + c] - lo) // w + 1, body,
                (jnp.zeros((rc, f_ref.shape[-1]), F32), jnp.zeros((rc, tb), F32)))
            xs_s[r0:r0 + rc, :] = acc.astype(BF16)
            gate_s[r0:r0 + rc, :] = jnp.sum(gs, axis=1, keepdims=True)
        ys_s[...] = jnp.zeros_like(ys_s)

    xs = xs_s[...]
    hid = (_silu(_dot(xs, w1_ref[0, 0].astype(BF16))) * _dot(xs, w3_ref[0, 0].astype(BF16))).astype(BF16)
    ys_s[...] += _dot(hid, w2_ref[0, 0].astype(BF16))

    @pl.when(ft == n_ft - 1)
    def _():
        base = (b * n_e + e) * len(stiles)
        for c, (r0, rc, w) in enumerate(stiles):
            ysg = (ys_s[r0:r0 + rc, :] * gate_s[r0:r0 + rc, :]).astype(BF16)
            lo = slo_ref[base + c]

            def body(k, carry, r0=r0, rc=rc, w=w, lo=lo, ysg=ysg):
                start, j0 = window(k, lo, w)
                for i in range(w):
                    oh = jnp.where(hits(j0 + i, start, r0, rc), 1.0, 0.0).astype(BF16)
                    off = pl.multiple_of((j0 + i) * tb, tb)
                    o_ref[0, pl.ds(off, tb), :] += _dot_tn(oh, ysg)
                return carry
            lax.fori_loop(0, (shi_ref[base + c] - lo) // w + 1, body, 0)


def _moe(layer, bounds, slot, aff_t, f, w1, w3, w2, gtiles, stiles, tb, tf):
    b, t, d = f.shape
    n_e = w1.shape[1]
    ff = w1.shape[-1]
    n_ft = ff // tf
    rows = gtiles[-1][0] + gtiles[-1][1]
    grid_spec = pltpu.PrefetchScalarGridSpec(
        num_scalar_prefetch=4,
        grid=(b, n_e, n_ft),
        in_specs=[
            pl.BlockSpec((1, 1, t // tb, tb), lambda bi, e, k, *_: (bi, e, 0, 0)),
            pl.BlockSpec((1, 1, t // tb, tb), lambda bi, e, k, *_: (bi, e, 0, 0)),
            pl.BlockSpec((1, t, d), lambda bi, e, k, *_: (bi, 0, 0), pipeline_mode=pl.Buffered(1)),
            pl.BlockSpec((1, 1, d, tf), lambda bi, e, k, *_: (layer, e, 0, k)),
            pl.BlockSpec((1, 1, d, tf), lambda bi, e, k, *_: (layer, e, 0, k)),
            pl.BlockSpec((1, 1, tf, d), lambda bi, e, k, *_: (layer, e, k, 0)),
        ],
        out_specs=pl.BlockSpec((1, t, d), lambda bi, e, k, *_: (bi, 0, 0), pipeline_mode=pl.Buffered(1)),
        scratch_shapes=[pltpu.VMEM((rows, d), BF16), pltpu.VMEM((rows, d), F32), pltpu.VMEM((rows, 1), F32)],
    )
    return pl.pallas_call(
        functools.partial(_moe_kernel, gtiles=gtiles, stiles=stiles, tb=tb, n_ft=n_ft),
        grid_spec=grid_spec,
        out_shape=jax.ShapeDtypeStruct((b, t, d), F32),
        compiler_params=_cparams(("parallel", "arbitrary", "arbitrary")),
    )(*bounds, slot.reshape(b, n_e, t // tb, tb), aff_t.reshape(b, n_e, t // tb, tb), f, w1, w3, w2)


def _tile_bounds(slot, tiles, tb):
    b, e, t = slot.shape
    blk = jnp.arange(t, dtype=jnp.int32) // tb
    lo, hi = [], []
    for r0, rc, _ in tiles:
        inside = (slot >= r0) & (slot < r0 + rc)
        lo.append(jnp.min(jnp.where(inside, blk, t // tb - 1), axis=-1))
        hi.append(jnp.max(jnp.where(inside, blk, 0), axis=-1))
    lo = jnp.stack(lo, axis=-1)
    hi = jnp.maximum(jnp.stack(hi, axis=-1), lo)
    return lo.reshape(-1).astype(jnp.int32), hi.reshape(-1).astype(jnp.int32)


def _route_and_moe(layer, aff, f, w1, w3, w2, n_ctx, tb, tf):
    b, t, _ = f.shape
    aff_t = jnp.swapaxes(aff[:, :, :N_EXPERTS], 1, 2)
    cap_ctx = (CAPACITY_FACTOR * n_ctx) // N_EXPERTS
    cap_lat = (CAPACITY_FACTOR * (t - n_ctx)) // N_EXPERTS
    slot = _select(aff_t, n_ctx, cap_ctx, cap_lat)
    nb = t // tb

    def tiling(lat_rows, lat_window):
        tl = min(lat_rows, cap_lat)
        w = min(lat_window, nb)
        return ((0, cap_ctx, n_ctx // tb),) + tuple((cap_ctx + i * tl, tl, w) for i in range(cap_lat // tl))

    gtiles = tiling(MOE_GATHER_ROWS, MOE_GATHER_WINDOW)
    stiles = tiling(MOE_SCATTER_ROWS, MOE_SCATTER_WINDOW)
    bounds = _tile_bounds(slot, gtiles, tb) + _tile_bounds(slot, stiles, tb)
    return _moe(layer, bounds, slot, aff_t, f, w1, w3, w2, gtiles, stiles, tb, tf)


def _final_kernel(x_ref, moe_ref, gp_ref, g_ref, o_ref):
    x = x_ref[0] + gp_ref[0, 0] * moe_ref[0]
    o_ref[0] = x * lax.rsqrt(jnp.mean(x * x, axis=-1, keepdims=True) + EPS) * g_ref[...]


def _final(x, moe, gprev, g, tm, n_ctx):
    b, t, d = x.shape
    lead = n_ctx // tm
    return pl.pallas_call(
        _final_kernel,
        grid=(b, (t - n_ctx) // tm),
        in_specs=[_row_spec(tm, d, lead), _row_spec(tm, d, lead),
                  pl.BlockSpec((1, 1, 1, d), lambda bi, i: (bi, 1, 0, 0)), _const_spec((1, d))],
        out_specs=_row_spec(tm, d),
        out_shape=jax.ShapeDtypeStruct((b, t - n_ctx, d), F32),
        compiler_params=_cparams(("parallel", "parallel")),
    )(x, moe, gprev, g)


def _rope_tables(n_ctx, n_lat):
    n_rows = n_lat // GRID_W
    row = jnp.repeat(jnp.arange(n_rows, dtype=F32), GRID_W)
    col = jnp.tile(jnp.arange(GRID_W, dtype=F32), n_rows)
    inv = ROPE_THETA ** (-jnp.arange(0, ROPE_AXIS_DIM, 2, dtype=F32) / ROPE_AXIS_DIM)
    ang_r = row[:, None] * inv[None, :]
    ang_c = col[:, None] * inv[None, :]
    cos = jnp.concatenate([jnp.cos(ang_r), jnp.cos(ang_r), jnp.cos(ang_c), jnp.cos(ang_c)], axis=-1)
    sin = jnp.concatenate([-jnp.sin(ang_r), jnp.sin(ang_r), -jnp.sin(ang_c), jnp.sin(ang_c)], axis=-1)
    cos = jnp.concatenate([jnp.ones((n_ctx, HEAD_DIM), F32), cos], axis=0)
    sin = jnp.concatenate([jnp.zeros((n_ctx, HEAD_DIM), F32), sin], axis=0)
    reps = SEG_W // HEAD_DIM
    return jnp.tile(cos, (1, reps)), jnp.tile(sin, (1, reps))


def _extend_in_proj(w_in):
    base = 4 * RET_WIDTH + ATT_WIDTH
    ak = w_in[:, base:base + ATT_KV_WIDTH].reshape(-1, ATT_KV_HEADS, 1, HEAD_DIM)
    av = w_in[:, base + ATT_KV_WIDTH:].reshape(-1, ATT_KV_HEADS, 1, HEAD_DIM)
    rep = lambda w, n: jnp.broadcast_to(w, (w.shape[0], ATT_KV_HEADS, n, HEAD_DIM)).reshape(w.shape[0], -1)
    return jnp.concatenate([w_in[:, :base], rep(ak, ATT_GROUP), rep(av, 2)], axis=-1)


def kernel(x, c, ctx, c_ctx, mod_w, mod_b, norm_g, mix_in_w, mix_out_w, ret_log_rate, ret_gn_g, qk_norm_g,
           s5_a_re, s5_a_im, s5_log_dt, s5_b_re, s5_b_im, s5_c_re, s5_c_im, s5_d, s5_glu_w, s5_glu_b,
           moe_router_w, moe_w1, moe_w3, moe_w2, final_norm_g):
    batch, n_lat, d = x.shape
    n_ctx = ctx.shape[1]
    t = n_ctx + n_lat
    depth = mod_w.shape[0]
    assert batch == 8 and d % 128 == 0 and n_ctx % CHUNK == 0 and n_lat % CHUNK == 0
    tm = 256 if n_ctx % 256 == 0 else 128
    tb = tm
    tf = 512

    xs = jnp.concatenate([ctx, x], axis=1)
    rows = 16
    cond = jnp.zeros((rows, d), F32).at[:batch].set(c).at[batch].set(c_ctx)
    mod_all = _modulation(cond, mod_w, mod_b)
    cos, sin = _rope_tables(n_ctx, n_lat)
    lane_head = jnp.arange(SEG_W) // HEAD_DIM
    bd = (lane_head[:, None] == lane_head[None, :]).astype(BF16)
    w1, w3, w2 = moe_w1, moe_w3, moe_w2

    prev = None
    for layer in range(depth):
        m = mod_all[layer]
        lat = m[:batch].reshape(batch, 1, 6, d)
        cx = jnp.broadcast_to(m[batch].reshape(1, 1, 6, d), (batch, 1, 6, d))
        modtab = jnp.concatenate([cx, lat], axis=1)
        ng1 = norm_g[layer, 0].reshape(1, d)
        ng2 = norm_g[layer, 1].reshape(1, d)
        wr = jnp.zeros((d, ROUTER_LANES), F32).at[:, :N_EXPERTS].set(moe_router_w[layer])
        wrh = wr.astype(BF16)
        wrl = (wr - wrh.astype(F32)).astype(BF16)
        i = layer // 2
        if layer % 2 == 0:
            w_ext = _extend_in_proj(mix_in_w[i]).astype(BF16)
            reps = SEG_W // HEAD_DIM
            qkg = jnp.stack([jnp.tile(qk_norm_g[i, 0], reps), jnp.tile(qk_norm_g[i, 1], reps)])
            xs, (rq, rk, rv, rg, aq, ak4, av2) = _in_proj(xs, prev, modtab, ng1, w_ext, cos, sin, qkg, bd,
                                                         tm, n_ctx)
            rate = jnp.repeat(ret_log_rate[i], HEAD_DIM, axis=-1)
            rate_h = jnp.broadcast_to(ret_log_rate[i][:, :, None], (2, RET_HEADS, CHUNK))
            o_f, o_b = _retention(rq, rk, rv, rate, rate_h, bd, n_ctx)
            att = _attention(aq, ak4, av2, tm, tm, n_ctx)
            xs, f, aff = _out_proj(xs, o_f, o_b, rg, att, modtab, ret_gn_g[i].reshape(1, SEG_W),
                                   mix_out_w[i].astype(BF16), ng2, wrh, wrl, bd, tm, n_ctx)
        else:
            xs, u = _s5_pre(xs, prev, modtab, ng1, tm, n_ctx)
            tabs = _s5_tables(s5_a_re[i], s5_a_im[i], s5_log_dt[i], s5_b_re[i], s5_b_im[i],
                              s5_c_re[i], s5_c_im[i], s5_d[i])
            z = _s5_scan(u, *tabs, n_ctx)
            xs, f, aff = _glu(xs, z, modtab, s5_glu_w[i].astype(BF16), s5_glu_b[i].reshape(1, 2 * d),
                              ng2, wrh, wrl, tm, n_ctx)
        moe = _route_and_moe(layer, aff, f, w1, w3, w2, n_ctx, tb, tf)
        prev = (moe, modtab[:, :, 5:6, :])
    return _final(xs, prev[0], prev[1], final_norm_g.reshape(1, d), tm, n_ctx)
```

```python
import functools
import math

import jax
import jax.numpy as jnp
from jax import lax
from jax.experimental import pallas as pl
from jax.experimental.pallas import tpu as pltpu

F32 = jnp.float32
BF16 = jnp.bfloat16

GRID_W = 64
HEAD_DIM = 64
RET_HEADS = 8
ATT_HEADS = 8
ATT_KV_HEADS = 2
ATT_GROUP = ATT_HEADS // ATT_KV_HEADS
RET_WIDTH = RET_HEADS * HEAD_DIM
ATT_WIDTH = ATT_HEADS * HEAD_DIM
ATT_KV_WIDTH = ATT_KV_HEADS * HEAD_DIM
CHUNK = 128
ROPE_THETA = 10000.0
ROPE_AXIS_DIM = HEAD_DIM // 2
S5_GROUP = 16
S5_STATE = 64
S5_SUB = 8
S5_BLOCK_GROUPS = 8
S5_SCAN_UNROLL = 4
N_EXPERTS = 16
CAPACITY_FACTOR = 2
EPS = 1e-6
SEG_W = 512
LOG2_E = math.log2(math.e)
ROUTER_LANES = 128
MOE_GATHER_ROWS = 128
MOE_GATHER_WINDOW = 6
MOE_SCATTER_ROWS = 256
MOE_SCATTER_WINDOW = 10

V7X_VMEM_BYTES = 64 * 1024 * 1024
VMEM_LIMIT = V7X_VMEM_BYTES - 8 * 1024 * 1024


def _cparams(sem):
    return pltpu.CompilerParams(dimension_semantics=sem, vmem_limit_bytes=VMEM_LIMIT)


def _dot(a, b):
    return jnp.dot(a, b, preferred_element_type=F32)


def _dot_nt(a, b):
    return lax.dot_general(a, b, (((1,), (1,)), ((), ())), preferred_element_type=F32)


def _dot_tn(a, b):
    return lax.dot_general(a, b, (((0,), (0,)), ((), ())), preferred_element_type=F32)


def _split(v):
    hi = v.astype(BF16)
    lo = (v - hi.astype(F32)).astype(BF16)
    return hi, lo


def _segsum(v, bd):
    hi, lo = _split(v)
    return _dot(hi, bd) + _dot(lo, bd)


def _sigmoid(v):
    return 1.0 / (1.0 + jnp.exp(-v))


def _silu(v):
    return v * _sigmoid(v)


def _gelu_tanh(v):
    c = math.sqrt(2.0 / math.pi)
    return 0.5 * v * (1.0 + jnp.tanh(c * (v + 0.044715 * (v * v * v))))


def _norm_mod(x, g, scale, shift):
    ms = jnp.mean(x * x, axis=-1, keepdims=True)
    return x * lax.rsqrt(ms + EPS) * g * (1.0 + scale) + shift


def _rope(x, cos, sin_signed):
    w = x.shape[-1]
    lane = lax.broadcasted_iota(jnp.int32, x.shape, 1)
    first = (lane % 32) < 16
    partner = jnp.where(first, pltpu.roll(x, w - 16, 1), pltpu.roll(x, 16, 1))
    return x * cos + partner * sin_signed


def _mod_kernel(c_ref, w_ref, b_ref, o_ref):
    s = _silu(c_ref[...]).astype(BF16)
    o_ref[0] = _dot(s, w_ref[0].astype(BF16)) + b_ref[0]


def _modulation(cond, mod_w, mod_b):
    depth, d, n6 = mod_w.shape
    rows = cond.shape[0]
    tn = n6 // 6
    return pl.pallas_call(
        _mod_kernel,
        grid=(depth, n6 // tn),
        in_specs=[
            pl.BlockSpec((rows, d), lambda l, j: (0, 0)),
            pl.BlockSpec((1, d, tn), lambda l, j: (l, 0, j)),
            pl.BlockSpec((1, 1, tn), lambda l, j: (l, 0, j)),
        ],
        out_specs=pl.BlockSpec((1, rows, tn), lambda l, j: (l, 0, j)),
        out_shape=jax.ShapeDtypeStruct((depth, rows, n6), F32),
        compiler_params=_cparams(("parallel", "parallel")),
    )(cond, mod_w, mod_b.reshape(depth, 1, n6))


def _residual_in(has_prev, refs):
    if has_prev:
        x_ref, moe_ref, gp_ref = refs[:3]
        return x_ref[0] + gp_ref[0, 0] * moe_ref[0], refs[3:]
    return refs[0][0], refs[1:]


def _in_kernel(has_prev, *refs):
    x, refs = _residual_in(has_prev, refs)
    mod_ref, ng_ref, w_ref, cos_ref, sin_ref, qkg_ref, bd_ref = refs[:7]
    outs = refs[7:]
    if has_prev:
        outs[0][0] = x
        outs = outs[1:]
    rq_ref, rk_ref, rv_ref, rg_ref, aq_ref, ak_ref, av_ref = outs
    m = mod_ref[0, 0]
    h = _norm_mod(x, ng_ref[...], m[1:2], m[0:1]).astype(BF16)
    cos = cos_ref[...]
    sin = sin_ref[...]
    bd = bd_ref[...]

    def seg(i):
        return _dot(h, w_ref[:, i * SEG_W:(i + 1) * SEG_W])

    def headnorm(v, g):
        ms = _segsum(v * v, bd) * (1.0 / HEAD_DIM)
        return v * lax.rsqrt(ms + EPS) * g

    scale = HEAD_DIM ** -0.5
    rq_ref[0] = _rope(seg(0), cos, sin).astype(BF16)
    rk_ref[0] = _rope(seg(1) * scale, cos, sin).astype(BF16)
    rv_ref[0] = seg(2).astype(BF16)
    rg_ref[0] = seg(3)
    aq_ref[0] = (_rope(headnorm(seg(4), qkg_ref[0:1]), cos, sin) * (scale * LOG2_E)).astype(BF16)
    ak_ref[0] = _rope(headnorm(seg(5), qkg_ref[1:2]), cos, sin).astype(BF16)
    av_ref[0] = _dot(h, w_ref[:, 6 * SEG_W:]).astype(BF16)


def _row_spec(tm, d, lead=0):
    return pl.BlockSpec((1, tm, d), lambda b, i: (b, i + lead, 0))


def _mod_spec(d, n, seg_tiles):
    return pl.BlockSpec((1, 1, n, d), lambda b, i: (b, jnp.where(i >= seg_tiles, 1, 0), 0, 0))


def _const_spec(shape):
    nd = len(shape)
    return pl.BlockSpec(shape, lambda b, i: (0,) * nd)


def _in_proj(x, prev, modtab, norm_g, w_ext, cos, sin, qkg, bd, tm, n_ctx):
    b, t, d = x.shape
    seg_tiles = n_ctx // tm
    has_prev = prev is not None
    args, specs = [x], [_row_spec(tm, d)]
    if has_prev:
        moe, gprev = prev
        args += [moe, gprev]
        specs += [_row_spec(tm, d), _mod_spec(d, 1, seg_tiles)]
    args += [modtab, norm_g, w_ext, cos, sin, qkg, bd]
    specs += [
        _mod_spec(d, 6, seg_tiles),
        _const_spec((1, d)),
        _const_spec(w_ext.shape),
        pl.BlockSpec((tm, SEG_W), lambda b, i: (i, 0)),
        pl.BlockSpec((tm, SEG_W), lambda b, i: (i, 0)),
        _const_spec((2, SEG_W)),
        _const_spec((SEG_W, SEG_W)),
    ]
    out_shape, out_specs = [], []
    if has_prev:
        out_shape.append(jax.ShapeDtypeStruct((b, t, d), F32))
        out_specs.append(_row_spec(tm, d))
    for dt, width in ((BF16, SEG_W), (BF16, SEG_W), (BF16, SEG_W), (F32, SEG_W), (BF16, SEG_W), (BF16, SEG_W),
                      (BF16, ATT_KV_HEADS * 2 * HEAD_DIM)):
        out_shape.append(jax.ShapeDtypeStruct((b, t, width), dt))
        out_specs.append(_row_spec(tm, width))
    outs = pl.pallas_call(
        functools.partial(_in_kernel, has_prev),
        grid=(b, t // tm),
        in_specs=specs,
        out_specs=out_specs,
        out_shape=out_shape,
        compiler_params=_cparams(("parallel", "parallel")),
    )(*args)
    if has_prev:
        return outs[0], outs[1:]
    return x, outs


def _ret_kernel(qf_ref, kf_ref, vf_ref, qb_ref, kb_ref, vb_ref, rate_ref, rate_h_ref, bd_ref,
                of_ref, ob_ref, sf_s, sb_s, dm_s, dq_s, dk_s, dc_s):
    s = pl.program_id(1)
    c = CHUNK

    @pl.when(s == 0)
    def _():
        sf_s[...] = jnp.zeros_like(sf_s)
        sb_s[...] = jnp.zeros_like(sb_s)
        pos = lax.broadcasted_iota(jnp.int32, (c, SEG_W), 0).astype(F32)
        lgf = -jnp.exp(rate_ref[0:1, :])
        lgb = -jnp.exp(rate_ref[1:2, :])
        dq_s[0] = jnp.exp((pos + 1.0) * lgf)
        dk_s[0] = jnp.exp((c - 1.0 - pos) * lgf)
        dc_s[0] = jnp.exp(float(c) * lgf)
        dq_s[1] = jnp.exp((c - pos) * lgb)
        dk_s[1] = jnp.exp(pos * lgb)
        dc_s[1] = jnp.exp(float(c) * lgb)
        ii = lax.broadcasted_iota(jnp.int32, (c, c), 0)
        jj = lax.broadcasted_iota(jnp.int32, (c, c), 1)
        for h in range(RET_HEADS):
            lf = -jnp.exp(rate_h_ref[0, h:h + 1, :])
            lb = -jnp.exp(rate_h_ref[1, h:h + 1, :])
            mf = ii >= jj
            mb = jj > ii
            dm_s[0, h] = jnp.where(mf, jnp.exp(jnp.where(mf, ii - jj, 0).astype(F32) * lf), 0.0)
            dm_s[1, h] = jnp.where(mb, jnp.exp(jnp.where(mb, jj - ii, 0).astype(F32) * lb), 0.0)

    bdm = bd_ref[...].astype(F32)
    head_of_lane = lax.broadcasted_iota(jnp.int32, (1, SEG_W // 2), 1) // HEAD_DIM

    def direction(d, q_ref, k_ref, v_ref, o_ref, st):
        q = q_ref[0]
        k = k_ref[0]
        v = v_ref[0]
        state = st[...]
        qd = (q.astype(F32) * dq_s[d]).astype(BF16)
        inter = _dot(qd, state.astype(BF16))
        for half in range(2):
            sl = slice(half * 256, half * 256 + 256)
            qh, kh, vh = q[:, sl], k[:, sl], v[:, sl]
            acc = inter[:, sl]
            for hh in range(4):
                msk = head_of_lane == hh
                sc = _dot_nt(jnp.where(msk, qh, jnp.zeros_like(qh)), kh) * dm_s[d, half * 4 + hh]
                acc = acc + _dot(sc.astype(BF16), jnp.where(msk, vh, jnp.zeros_like(vh)))
            o_ref[0, :, sl] = acc
        kd = (k.astype(F32) * dk_s[d]).astype(BF16)
        st[...] = state * dc_s[d] + bdm * _dot_tn(kd, v)

    direction(0, qf_ref, kf_ref, vf_ref, of_ref, sf_s)
    direction(1, qb_ref, kb_ref, vb_ref, ob_ref, sb_s)


def _retention(rq, rk, rv, rate, rate_h, bd, n_ctx):
    b, t, w = rq.shape
    nc = t // CHUNK
    lc = n_ctx // CHUNK

    def fwd(bi, s):
        return (bi, s, 0)

    def bwd(bi, s):
        return (bi, jnp.where(s < lc, lc - 1 - s, nc - 1 - (s - lc)), 0)

    blk = (1, CHUNK, w)
    return pl.pallas_call(
        _ret_kernel,
        grid=(b, nc),
        in_specs=[pl.BlockSpec(blk, fwd)] * 3 + [pl.BlockSpec(blk, bwd)] * 3 + [
            _const_spec((2, w)),
            _const_spec((2, RET_HEADS, CHUNK)),
            _const_spec((w, w)),
        ],
        out_specs=[pl.BlockSpec(blk, fwd), pl.BlockSpec(blk, bwd)],
        out_shape=[jax.ShapeDtypeStruct((b, t, w), F32)] * 2,
        scratch_shapes=[
            pltpu.VMEM((w, w), F32), pltpu.VMEM((w, w), F32),
            pltpu.VMEM((2, RET_HEADS, CHUNK, CHUNK), F32),
            pltpu.VMEM((2, CHUNK, w), F32), pltpu.VMEM((2, CHUNK, w), F32),
            pltpu.VMEM((2, 1, w), F32),
        ],
        compiler_params=_cparams(("parallel", "arbitrary")),
    )(rq, rk, rv, rq, rk, rv, rate, rate_h, bd)


ATT_ROW_BLOCK = 512


def _att_kernel(q_ref, k_ref, v_ref, o_ref, qs_s, m_s, acc_s, sa_s, sb_s, ma_s, mb_s, *, tq, tk, n_ctx, t):
    qi = pl.program_id(1)
    gw = ATT_GROUP * HEAD_DIM
    head_of_lane = lax.broadcasted_iota(jnp.int32, (1, gw), 1) // HEAD_DIM
    for h in range(ATT_HEADS):
        q = q_ref[0, :, (h // ATT_GROUP) * gw:(h // ATT_GROUP + 1) * gw]
        qs_s[h * tq:(h + 1) * tq, :] = jnp.where(head_of_lane == h % ATT_GROUP, q, jnp.zeros_like(q))
    m_s[...] = jnp.full(m_s.shape, -1e30, F32)
    acc_s[...] = jnp.zeros_like(acc_s)
    nk = jnp.where(qi * tq < n_ctx, n_ctx // tk, t // tk)
    vw = 2 * HEAD_DIM
    ones_lane = lax.broadcasted_iota(jnp.int32, (1, ATT_KV_HEADS * vw), 1) % vw >= HEAD_DIM
    rb = ATT_ROW_BLOCK

    blocks = [(slice(r * rb, (r + 1) * rb), (r * rb) // (ATT_GROUP * tq)) for r in range(ATT_HEADS * tq // rb)]

    def scores_into(j, dst, dst_max):
        kb = k_ref[0, pl.ds(pl.multiple_of(j * tk, tk), tk), :]
        for rows, g in blocks:
            sc = _dot_nt(qs_s[rows, :], kb[:, g * gw:(g + 1) * gw])
            dst[rows, :] = sc
            dst_max[rows, :] = jnp.broadcast_to(jnp.max(sc, axis=1, keepdims=True), (rb, vw))

    def consume(j, src, src_max):
        vb = v_ref[0, pl.ds(pl.multiple_of(j * tk, tk), tk), :]
        vext = jnp.where(ones_lane, jnp.ones_like(vb), vb)
        for rows, g in blocks:
            m_prev = m_s[rows, :]
            m_new = jnp.maximum(m_prev, src_max[rows, :])
            p = jnp.exp2(src[rows, :] - jnp.concatenate([m_new] * (tk // vw), axis=1)).astype(BF16)
            acc_s[rows, :] = jnp.exp2(m_prev - m_new) * acc_s[rows, :] + _dot(p, vext[:, g * vw:(g + 1) * vw])
            m_s[rows, :] = m_new

    scores_into(0, sa_s, ma_s)

    def body(i, carry):
        j = 2 * i
        scores_into(j + 1, sb_s, mb_s)
        consume(j, sa_s, ma_s)
        scores_into(jnp.minimum(j + 2, nk - 1), sa_s, ma_s)
        consume(j + 1, sb_s, mb_s)
        return carry

    lax.fori_loop(0, nk // 2, body, 0)

    @pl.when(nk % 2 == 1)
    def _():
        consume(nk - 1, sa_s, ma_s)

    low = lax.broadcasted_iota(jnp.int32, (tq, vw), 1) < HEAD_DIM
    for pair in range(ATT_HEADS // 2):
        a0 = acc_s[(2 * pair) * tq:(2 * pair + 1) * tq, :]
        a1 = acc_s[(2 * pair + 1) * tq:(2 * pair + 2) * tq, :]
        even = a0 / pltpu.roll(a0, HEAD_DIM, 1)
        odd = pltpu.roll(a1, HEAD_DIM, 1) / a1
        o_ref[0, :, pair * vw:(pair + 1) * vw] = jnp.where(low, even, odd).astype(BF16)


def _attention(aq, ak4, av2, tq, tk, n_ctx):
    b, t, w = aq.shape
    gw = ATT_GROUP * HEAD_DIM
    vw = 2 * HEAD_DIM
    rows = ATT_HEADS * tq
    return pl.pallas_call(
        functools.partial(_att_kernel, tq=tq, tk=tk, n_ctx=n_ctx, t=t),
        grid=(b, t // tq),
        in_specs=[
            pl.BlockSpec((1, tq, w), lambda bi, i: (bi, i, 0)),
            pl.BlockSpec((1, t, w), lambda bi, i: (bi, 0, 0)),
            pl.BlockSpec((1, t, ATT_KV_HEADS * vw), lambda bi, i: (bi, 0, 0)),
        ],
        out_specs=pl.BlockSpec((1, tq, w), lambda bi, i: (bi, i, 0)),
        out_shape=jax.ShapeDtypeStruct((b, t, w), BF16),
        scratch_shapes=[pltpu.VMEM((rows, gw), BF16), pltpu.VMEM((rows, vw), F32), pltpu.VMEM((rows, vw), F32),
                        pltpu.VMEM((rows, tk), F32), pltpu.VMEM((rows, tk), F32), pltpu.VMEM((rows, vw), F32),
                        pltpu.VMEM((rows, vw), F32)],
        compiler_params=_cparams(("parallel", "parallel")),
    )(aq, ak4, av2)


def _post_mixer(x, delta, m, ng2, wrh, wrl, x_ref, f_ref, aff_ref):
    xn = x + m[2:3] * delta
    x_ref[0] = xn
    f = _norm_mod(xn, ng2, m[4:5], m[3:4])
    fh, fl = _split(f)
    f_ref[0] = fh
    logits = _dot(fh, wrh) + _dot(fl, wrh) + _dot(fh, wrl)
    lane = lax.broadcasted_iota(jnp.int32, logits.shape, 1)
    logits = jnp.where(lane < N_EXPERTS, logits, -1e30)
    e = jnp.exp(logits - jnp.max(logits, axis=1, keepdims=True))
    aff_ref[0] = e / jnp.sum(e, axis=1, keepdims=True)


def _out_kernel(x_ref, of_ref, ob_ref, rg_ref, att_ref, mod_ref, gn_ref, w_ref, ng2_ref, wrh_ref, wrl_ref,
                bd_ref, xo_ref, f_ref, aff_ref):
    bd = bd_ref[...]
    o = of_ref[0] + ob_ref[0]
    mu = _segsum(o, bd) * (1.0 / HEAD_DIM)
    oc = o - mu
    var = _segsum(oc * oc, bd) * (1.0 / HEAD_DIM)
    r = _silu(rg_ref[0]) * (oc * lax.rsqrt(var + EPS) * gn_ref[...])
    y = _dot(r.astype(BF16), w_ref[0:RET_WIDTH, :]) + _dot(att_ref[0], w_ref[RET_WIDTH:, :])
    _post_mixer(x_ref[0], y, mod_ref[0, 0], ng2_ref[...], wrh_ref[...], wrl_ref[...], xo_ref, f_ref, aff_ref)


def _post_specs(b, t, d, tm):
    out_shape = [jax.ShapeDtypeStruct((b, t, d), F32), jax.ShapeDtypeStruct((b, t, d), BF16),
                 jax.ShapeDtypeStruct((b, t, ROUTER_LANES), F32)]
    out_specs = [_row_spec(tm, d), _row_spec(tm, d), _row_spec(tm, ROUTER_LANES)]
    return out_shape, out_specs


def _out_proj(x, of, ob, rg, att, modtab, gn_g, w_out, ng2, wrh, wrl, bd, tm, n_ctx):
    b, t, d = x.shape
    out_shape, out_specs = _post_specs(b, t, d, tm)
    return pl.pallas_call(
        _out_kernel,
        grid=(b, t // tm),
        in_specs=[
            _row_spec(tm, d), _row_spec(tm, SEG_W), _row_spec(tm, SEG_W), _row_spec(tm, SEG_W),
            _row_spec(tm, SEG_W), _mod_spec(d, 6, n_ctx // tm), _const_spec((1, SEG_W)),
            _const_spec(w_out.shape), _const_spec((1, d)), _const_spec(wrh.shape), _const_spec(wrl.shape),
            _const_spec((SEG_W, SEG_W)),
        ],
        out_specs=out_specs,
        out_shape=out_shape,
        compiler_params=_cparams(("parallel", "parallel")),
    )(x, of, ob, rg, att, modtab, gn_g, w_out, ng2, wrh, wrl, bd)


def _s5_pre_kernel(has_prev, *refs):
    x, refs = _residual_in(has_prev, refs)
    mod_ref, ng_ref = refs[:2]
    outs = refs[2:]
    if has_prev:
        outs[0][0] = x
        outs = outs[1:]
    m = mod_ref[0, 0]
    outs[0][0] = _norm_mod(x, ng_ref[...], m[1:2], m[0:1])


def _s5_pre(x, prev, modtab, norm_g, tm, n_ctx):
    b, t, d = x.shape
    seg_tiles = n_ctx // tm
    has_prev = prev is not None
    args, specs = [x], [_row_spec(tm, d)]
    if has_prev:
        args += list(prev)
        specs += [_row_spec(tm, d), _mod_spec(d, 1, seg_tiles)]
    args += [modtab, norm_g]
    specs += [_mod_spec(d, 6, seg_tiles), _const_spec((1, d))]
    n_out = 2 if has_prev else 1
    outs = pl.pallas_call(
        functools.partial(_s5_pre_kernel, has_prev),
        grid=(b, t // tm),
        in_specs=specs,
        out_specs=[_row_spec(tm, d)] * n_out,
        out_shape=[jax.ShapeDtypeStruct((b, t, d), F32)] * n_out,
        compiler_params=_cparams(("parallel", "parallel")),
    )(*args)
    if has_prev:
        return outs[0], outs[1]
    return x, outs[0]


def _s5_kernel(u_ref, tm_ref, wz_ref, wo_ref, ar_ref, ai_ref, d_ref, z_ref,
               zr_s, zi_s, xr_s, xi_s, zs_s, *, n_sub, n_sub_ctx):
    s_n, j_sub, gb, half = n_sub, S5_SUB, S5_BLOCK_GROUPS, 2 * S5_STATE
    uf = jnp.concatenate([u_ref[0, pl.ds(m, s_n, stride=j_sub), :] for m in range(j_sub)], axis=1)
    lhs = uf.astype(BF16)
    for g in range(gb):
        zz = _dot(lhs, wz_ref[0, g])
        zr_s[pl.ds(g, s_n, stride=gb), :] = zz[:, :half]
        zi_s[pl.ds(g, s_n, stride=gb), :] = zz[:, half:]
    fwd_lane = lax.broadcasted_iota(jnp.int32, (gb, half), 1) < S5_STATE
    ar = ar_ref[0]
    ai = ai_ref[0]

    def body(i, carry):
        xr, xi = carry
        sb = jnp.where(i < n_sub_ctx, n_sub_ctx - 1 - i, s_n - 1 - (i - n_sub_ctx))
        fwd = pl.ds(pl.multiple_of(i * gb, gb), gb)
        bwd = pl.ds(pl.multiple_of(sb * gb, gb), gb)
        xr_s[fwd, 0:S5_STATE] = xr[:, 0:S5_STATE]
        xi_s[fwd, 0:S5_STATE] = xi[:, 0:S5_STATE]
        xr_s[bwd, S5_STATE:half] = xr[:, S5_STATE:half]
        xi_s[bwd, S5_STATE:half] = xi[:, S5_STATE:half]
        zr = jnp.where(fwd_lane, zr_s[fwd, :], zr_s[bwd, :])
        zi = jnp.where(fwd_lane, zi_s[fwd, :], zi_s[bwd, :])
        return ar * xr - ai * xi + zr, ar * xi + ai * xr + zi

    zero = jnp.zeros((gb, half), F32)
    lax.fori_loop(0, s_n, body, (zero, zero), unroll=S5_SCAN_UNROLL)
    y = _dot(lhs, tm_ref[0]) + d_ref[0] * uf
    for g in range(gb):
        rows = pl.ds(g, s_n, stride=gb)
        xp = jnp.concatenate([xr_s[rows, :], xi_s[rows, :]], axis=1).astype(BF16)
        y = y + _dot(xp, wo_ref[0, g])
    z = _gelu_tanh(y)
    w = z_ref.shape[-1]
    for n in range(j_sub):
        zs_s[pl.ds(n, s_n, stride=j_sub), :] = z[:, n * w:(n + 1) * w]
    z_ref[0] = zs_s[...].astype(BF16)


def _s5_scan(u, tmat, wz, wo, ar, ai, dt, n_ctx):
    b, t, d = u.shape
    cw = S5_BLOCK_GROUPS * S5_GROUP
    n_sub = t // S5_SUB
    lw = S5_SUB * cw
    sw = 4 * S5_STATE
    gb = S5_BLOCK_GROUPS
    once = pl.Buffered(1)
    return pl.pallas_call(
        functools.partial(_s5_kernel, n_sub=n_sub, n_sub_ctx=n_ctx // S5_SUB),
        grid=(d // cw, b),
        in_specs=[
            pl.BlockSpec((1, t, cw), lambda c, bi: (bi, 0, c)),
            pl.BlockSpec((1, lw, lw), lambda c, bi: (c, 0, 0), pipeline_mode=once),
            pl.BlockSpec((1, gb, lw, sw), lambda c, bi: (c, 0, 0, 0), pipeline_mode=once),
            pl.BlockSpec((1, gb, sw, lw), lambda c, bi: (c, 0, 0, 0), pipeline_mode=once),
            pl.BlockSpec((1, gb, 2 * S5_STATE), lambda c, bi: (c, 0, 0)),
            pl.BlockSpec((1, gb, 2 * S5_STATE), lambda c, bi: (c, 0, 0)),
            pl.BlockSpec((1, 1, lw), lambda c, bi: (c, 0, 0)),
        ],
        out_specs=pl.BlockSpec((1, t, cw), lambda c, bi: (bi, 0, c)),
        out_shape=jax.ShapeDtypeStruct((b, t, d), BF16),
        scratch_shapes=[pltpu.VMEM((gb * n_sub, 2 * S5_STATE), F32)] * 4 + [pltpu.VMEM((t, cw), F32)],
        compiler_params=_cparams(("parallel", "parallel")),
    )(u, tmat, wz, wo, ar, ai, dt)


def _s5_tables(a_re, a_im, log_dt, b_re, b_im, c_re, c_im, d_skip):
    hp = lax.Precision.HIGHEST
    j_sub = S5_SUB
    gb = S5_BLOCK_GROUPS
    lam_re = jnp.minimum(a_re, -1e-4)
    lam_im = a_im
    dt = jnp.exp(log_dt)[..., None]
    mag = jnp.exp(lam_re * dt)
    bar_re = mag * jnp.cos(lam_im * dt)
    bar_im = mag * jnp.sin(lam_im * dt)
    den = lam_re * lam_re + lam_im * lam_im
    nr, ni = bar_re - 1.0, bar_im
    k_re = (nr * lam_re + ni * lam_im) / den
    k_im = (ni * lam_re - nr * lam_im) / den
    bb_re = k_re[..., None] * b_re[None] - k_im[..., None] * b_im[None]
    bb_im = k_re[..., None] * b_im[None] + k_im[..., None] * b_re[None]
    jj = jnp.arange(j_sub + 1, dtype=F32)[:, None, None, None]
    pmag = jnp.exp(jj * (lam_re * dt)[None])
    pw_re = pmag * jnp.cos(jj * (lam_im * dt)[None])
    pw_im = pmag * jnp.sin(jj * (lam_im * dt)[None])
    pb_re = pw_re[..., None] * bb_re[None] - pw_im[..., None] * bb_im[None]
    pb_im = pw_re[..., None] * bb_im[None] + pw_im[..., None] * bb_re[None]
    cp_re = c_re[None] * pw_re[:, :, :, None, :] - c_im[None] * pw_im[:, :, :, None, :]
    cp_im = c_re[None] * pw_im[:, :, :, None, :] + c_im[None] * pw_re[:, :, :, None, :]
    kern = (jnp.einsum('dgcp,jdgpk->jdgkc', c_re, pb_re[:j_sub], precision=hp)
            - jnp.einsum('dgcp,jdgpk->jdgkc', c_im, pb_im[:j_sub], precision=hp))
    groups = a_re.shape[1]
    nblk = groups // gb
    k_w = S5_GROUP
    p_w = S5_STATE
    lw = j_sub * gb * k_w
    eye = jnp.eye(gb, dtype=F32)
    blk = lambda a: a.reshape(a.shape[0], nblk, gb, *a.shape[2:])
    pos = jnp.arange(j_sub)
    lag = pos[None, :] - pos[:, None]
    kf = jnp.where((lag >= 0)[:, :, None, None, None], kern[jnp.clip(lag, 0, j_sub - 1), 0], 0.0)
    kb = jnp.where((lag <= 0)[:, :, None, None, None], kern[jnp.clip(-lag, 0, j_sub - 1), 1], 0.0)
    kfull = (kf + kb).reshape(j_sub, j_sub, nblk, gb, k_w, k_w)
    tmat = jnp.einsum('mnBgkc,gh->Bmgknhc', kfull, eye).reshape(nblk, lw, lw)
    rev = pos[::-1]
    zc = jnp.concatenate([pb_re[rev, 0], pb_re[pos, 1], pb_im[rev, 0], pb_im[pos, 1]], axis=2)
    wz = jnp.einsum('mBgqk,gh->Bgmhkq', blk(zc), eye).reshape(nblk, gb, lw, 4 * p_w)
    def orows(cp, d, order):
        return jnp.einsum('nBgcp,gh->Bgpnhc', blk(cp[order, d]), eye).reshape(nblk, gb, p_w, lw)
    wo = jnp.concatenate([orows(cp_re, 0, pos + 1), orows(cp_re, 1, j_sub - pos),
                          -orows(cp_im, 0, pos + 1), -orows(cp_im, 1, j_sub - pos)], axis=2)
    ar = jnp.concatenate([pw_re[j_sub, 0], pw_re[j_sub, 1]], axis=-1).reshape(nblk, gb, 2 * p_w)
    ai = jnp.concatenate([pw_im[j_sub, 0], pw_im[j_sub, 1]], axis=-1).reshape(nblk, gb, 2 * p_w)
    dt_tile = jnp.tile(d_skip.reshape(nblk, 1, gb * k_w), (1, 1, j_sub))
    return tmat.astype(BF16), wz.astype(BF16), wo.astype(BF16), ar, ai, dt_tile


def _glu_kernel(x_ref, z_ref, mod_ref, w_ref, b_ref, ng2_ref, wrh_ref, wrl_ref, xo_ref, f_ref, aff_ref):
    d = x_ref.shape[-1]
    ag = _dot(z_ref[0], w_ref[...]) + b_ref[...]
    delta = ag[:, :d] * _sigmoid(ag[:, d:])
    _post_mixer(x_ref[0], delta, mod_ref[0, 0], ng2_ref[...], wrh_ref[...], wrl_ref[...], xo_ref, f_ref, aff_ref)


def _glu(x, z, modtab, glu_w, glu_b, ng2, wrh, wrl, tm, n_ctx):
    b, t, d = x.shape
    out_shape, out_specs = _post_specs(b, t, d, tm)
    return pl.pallas_call(
        _glu_kernel,
        grid=(b, t // tm),
        in_specs=[_row_spec(tm, d), _row_spec(tm, d), _mod_spec(d, 6, n_ctx // tm),
                  _const_spec(glu_w.shape), _const_spec((1, 2 * d)), _const_spec((1, d)),
                  _const_spec(wrh.shape), _const_spec(wrl.shape)],
        out_specs=out_specs,
        out_shape=out_shape,
        compiler_params=_cparams(("parallel", "parallel")),
    )(x, z, modtab, glu_w, glu_b, ng2, wrh, wrl)


def _cumsum_lanes(v):
    n = v.shape[-1]
    lane = lax.broadcasted_iota(jnp.int32, v.shape, 1)
    sh = 1
    while sh < n:
        v = v + jnp.where(lane >= sh, pltpu.roll(v, sh, 1), 0.0)
        sh *= 2
    return v


def _select_segment(aff, cap, base):
    bits = pltpu.bitcast(aff, jnp.int32)
    thr = jnp.zeros((aff.shape[0], 1), jnp.int32)
    for bit in range(30, -1, -1):
        cand = thr | (1 << bit)
        cnt = jnp.sum(jnp.where(bits >= cand, 1.0, 0.0), axis=1, keepdims=True)
        thr = jnp.where(cnt >= cap, cand, thr)
    gt = bits > thr
    eq = bits == thr
    need = cap - jnp.sum(jnp.where(gt, 1.0, 0.0), axis=1, keepdims=True)
    eqf = jnp.where(eq, 1.0, 0.0)
    eq_before = _cumsum_lanes(eqf) - eqf
    sel = jnp.where(gt, 1.0, jnp.where(eq, jnp.where(eq_before < need, 1.0, 0.0), 0.0))
    pos = _cumsum_lanes(sel) - sel
    return jnp.where(sel > 0.0, pos.astype(jnp.int32) + base, -1)


def _select_kernel(aff_ref, slot_ref, *, n_ctx, cap_ctx, cap_lat):
    slot_ref[0, :, 0:n_ctx] = _select_segment(aff_ref[0, :, 0:n_ctx], cap_ctx, 0)
    slot_ref[0, :, n_ctx:] = _select_segment(aff_ref[0, :, n_ctx:], cap_lat, cap_ctx)


def _select(aff_t, n_ctx, cap_ctx, cap_lat):
    b, e, t = aff_t.shape
    return pl.pallas_call(
        functools.partial(_select_kernel, n_ctx=n_ctx, cap_ctx=cap_ctx, cap_lat=cap_lat),
        grid=(b,),
        in_specs=[pl.BlockSpec((1, e, t), lambda i: (i, 0, 0))],
        out_specs=pl.BlockSpec((1, e, t), lambda i: (i, 0, 0)),
        out_shape=jax.ShapeDtypeStruct((b, e, t), jnp.int32),
        compiler_params=_cparams(("parallel",)),
    )(aff_t)


def _moe_kernel(glo_ref, ghi_ref, slo_ref, shi_ref, slot_ref, aff_ref, f_ref, w1_ref, w3_ref, w2_ref, o_ref,
                xs_s, ys_s, gate_s, *, gtiles, stiles, tb, n_ft):
    b = pl.program_id(0)
    e = pl.program_id(1)
    ft = pl.program_id(2)
    n_e = pl.num_programs(1)
    nb = slot_ref.shape[2]

    def window(k, lo, w):
        start = lo + k * w
        return start, jnp.minimum(start, nb - w)

    def hits(j, start, r0, rc):
        rows = lax.broadcasted_iota(jnp.int32, (rc, tb), 0) + r0
        sv = jnp.where(j >= start, slot_ref[0, 0, pl.ds(j, 1), :], -1)
        return sv == rows

    @pl.when((e == 0) & (ft == 0))
    def _():
        o_ref[...] = jnp.zeros_like(o_ref)

    @pl.when(ft == 0)
    def _():
        base = (b * n_e + e) * len(gtiles)
        for c, (r0, rc, w) in enumerate(gtiles):
            lo = glo_ref[base + c]

            def body(k, carry, r0=r0, rc=rc, w=w, lo=lo):
                acc, gs = carry
                start, j0 = window(k, lo, w)
                parts = []
                for i in range(w):
                    hit = hits(j0 + i, start, r0, rc)
                    parts.append(jnp.where(hit, 1.0, 0.0).astype(BF16))
                    gs = gs + jnp.where(hit, aff_ref[0, 0, pl.ds(j0 + i, 1), :], 0.0)
                off = pl.multiple_of(j0 * tb, tb)
                acc = acc + _dot(jnp.concatenate(parts, axis=1), f_ref[0, pl.ds(off, w * tb), :])
                return acc, gs
            acc, gs = lax.fori_loop(
                0, (ghi_ref[base + c] - lo) // w + 1, body,
                (jnp.zeros((rc, f_ref.shape[-1]), F32), jnp.zeros((rc, tb), F32)))
            xs_s[r0:r0 + rc, :] = acc.astype(BF16)
            gate_s[r0:r0 + rc, :] = jnp.sum(gs, axis=1, keepdims=True)
        ys_s[...] = jnp.zeros_like(ys_s)

    xs = xs_s[...]
    hid = (_silu(_dot(xs, w1_ref[0, 0].astype(BF16))) * _dot(xs, w3_ref[0, 0].astype(BF16))).astype(BF16)
    ys_s[...] += _dot(hid, w2_ref[0, 0].astype(BF16))

    @pl.when(ft == n_ft - 1)
    def _():
        base = (b * n_e + e) * len(stiles)
        for c, (r0, rc, w) in enumerate(stiles):
            ysg = (ys_s[r0:r0 + rc, :] * gate_s[r0:r0 + rc, :]).astype(BF16)
            lo = slo_ref[base + c]

            def body(k, carry, r0=r0, rc=rc, w=w, lo=lo, ysg=ysg):
                start, j0 = window(k, lo, w)
                for i in range(w):
                    oh = jnp.where(hits(j0 + i, start, r0, rc), 1.0, 0.0).astype(BF16)
                    off = pl.multiple_of((j0 + i) * tb, tb)
                    o_ref[0, pl.ds(off, tb), :] += _dot_tn(oh, ysg)
                return carry
            lax.fori_loop(0, (shi_ref[base + c] - lo) // w + 1, body, 0)


def _moe(layer, bounds, slot, aff_t, f, w1, w3, w2, gtiles, stiles, tb, tf):
    b, t, d = f.shape
    n_e = w1.shape[1]
    ff = w1.shape[-1]
    n_ft = ff // tf
    rows = gtiles[-1][0] + gtiles[-1][1]
    grid_spec = pltpu.PrefetchScalarGridSpec(
        num_scalar_prefetch=4,
        grid=(b, n_e, n_ft),
        in_specs=[
            pl.BlockSpec((1, 1, t // tb, tb), lambda bi, e, k, *_: (bi, e, 0, 0)),
            pl.BlockSpec((1, 1, t // tb, tb), lambda bi, e, k, *_: (bi, e, 0, 0)),
            pl.BlockSpec((1, t, d), lambda bi, e, k, *_: (bi, 0, 0), pipeline_mode=pl.Buffered(1)),
            pl.BlockSpec((1, 1, d, tf), lambda bi, e, k, *_: (layer, e, 0, k)),
            pl.BlockSpec((1, 1, d, tf), lambda bi, e, k, *_: (layer, e, 0, k)),
            pl.BlockSpec((1, 1, tf, d), lambda bi, e, k, *_: (layer, e, k, 0)),
        ],
        out_specs=pl.BlockSpec((1, t, d), lambda bi, e, k, *_: (bi, 0, 0), pipeline_mode=pl.Buffered(1)),
        scratch_shapes=[pltpu.VMEM((rows, d), BF16), pltpu.VMEM((rows, d), F32), pltpu.VMEM((rows, 1), F32)],
    )
    return pl.pallas_call(
        functools.partial(_moe_kernel, gtiles=gtiles, stiles=stiles, tb=tb, n_ft=n_ft),
        grid_spec=grid_spec,
        out_shape=jax.ShapeDtypeStruct((b, t, d), F32),
        compiler_params=_cparams(("parallel", "arbitrary", "arbitrary")),
    )(*bounds, slot.reshape(b, n_e, t // tb, tb), aff_t.reshape(b, n_e, t // tb, tb), f, w1, w3, w2)


def _tile_bounds(slot, tiles, tb):
    b, e, t = slot.shape
    blk = jnp.arange(t, dtype=jnp.int32) // tb
    lo, hi = [], []
    for r0, rc, _ in tiles:
        inside = (slot >= r0) & (slot < r0 + rc)
        lo.append(jnp.min(jnp.where(inside, blk, t // tb - 1), axis=-1))
        hi.append(jnp.max(jnp.where(inside, blk, 0), axis=-1))
    lo = jnp.stack(lo, axis=-1)
    hi = jnp.maximum(jnp.stack(hi, axis=-1), lo)
    return lo.reshape(-1).astype(jnp.int32), hi.reshape(-1).astype(jnp.int32)


def _route_and_moe(layer, aff, f, w1, w3, w2, n_ctx, tb, tf):
    b, t, _ = f.shape
    aff_t = jnp.swapaxes(aff[:, :, :N_EXPERTS], 1, 2)
    cap_ctx = (CAPACITY_FACTOR * n_ctx) // N_EXPERTS
    cap_lat = (CAPACITY_FACTOR * (t - n_ctx)) // N_EXPERTS
    slot = _select(aff_t, n_ctx, cap_ctx, cap_lat)
    nb = t // tb

    def tiling(lat_rows, lat_window):
        tl = min(lat_rows, cap_lat)
        w = min(lat_window, nb)
        return ((0, cap_ctx, n_ctx // tb),) + tuple((cap_ctx + i * tl, tl, w) for i in range(cap_lat // tl))

    gtiles = tiling(MOE_GATHER_ROWS, MOE_GATHER_WINDOW)
    stiles = tiling(MOE_SCATTER_ROWS, MOE_SCATTER_WINDOW)
    bounds = _tile_bounds(slot, gtiles, tb) + _tile_bounds(slot, stiles, tb)
    return _moe(layer, bounds, slot, aff_t, f, w1, w3, w2, gtiles, stiles, tb, tf)


def _final_kernel(x_ref, moe_ref, gp_ref, g_ref, o_ref):
    x = x_ref[0] + gp_ref[0, 0] * moe_ref[0]
    o_ref[0] = x * lax.rsqrt(jnp.mean(x * x, axis=-1, keepdims=True) + EPS) * g_ref[...]


def _final(x, moe, gprev, g, tm, n_ctx):
    b, t, d = x.shape
    lead = n_ctx // tm
    return pl.pallas_call(
        _final_kernel,
        grid=(b, (t - n_ctx) // tm),
        in_specs=[_row_spec(tm, d, lead), _row_spec(tm, d, lead),
                  pl.BlockSpec((1, 1, 1, d), lambda bi, i: (bi, 1, 0, 0)), _const_spec((1, d))],
        out_specs=_row_spec(tm, d),
        out_shape=jax.ShapeDtypeStruct((b, t - n_ctx, d), F32),
        compiler_params=_cparams(("parallel", "parallel")),
    )(x, moe, gprev, g)


def _rope_tables(n_ctx, n_lat):
    n_rows = n_lat // GRID_W
    row = jnp.repeat(jnp.arange(n_rows, dtype=F32), GRID_W)
    col = jnp.tile(jnp.arange(GRID_W, dtype=F32), n_rows)
    inv = ROPE_THETA ** (-jnp.arange(0, ROPE_AXIS_DIM, 2, dtype=F32) / ROPE_AXIS_DIM)
    ang_r = row[:, None] * inv[None, :]
    ang_c = col[:, None] * inv[None, :]
    cos = jnp.concatenate([jnp.cos(ang_r), jnp.cos(ang_r), jnp.cos(ang_c), jnp.cos(ang_c)], axis=-1)
    sin = jnp.concatenate([-jnp.sin(ang_r), jnp.sin(ang_r), -jnp.sin(ang_c), jnp.sin(ang_c)], axis=-1)
    cos = jnp.concatenate([jnp.ones((n_ctx, HEAD_DIM), F32), cos], axis=0)
    sin = jnp.concatenate([jnp.zeros((n_ctx, HEAD_DIM), F32), sin], axis=0)
    reps = SEG_W // HEAD_DIM
    return jnp.tile(cos, (1, reps)), jnp.tile(sin, (1, reps))


def _extend_in_proj(w_in):
    base = 4 * RET_WIDTH + ATT_WIDTH
    ak = w_in[:, base:base + ATT_KV_WIDTH].reshape(-1, ATT_KV_HEADS, 1, HEAD_DIM)
    av = w_in[:, base + ATT_KV_WIDTH:].reshape(-1, ATT_KV_HEADS, 1, HEAD_DIM)
    rep = lambda w, n: jnp.broadcast_to(w, (w.shape[0], ATT_KV_HEADS, n, HEAD_DIM)).reshape(w.shape[0], -1)
    return jnp.concatenate([w_in[:, :base], rep(ak, ATT_GROUP), rep(av, 2)], axis=-1)


def kernel(x, c, ctx, c_ctx, mod_w, mod_b, norm_g, mix_in_w, mix_out_w, ret_log_rate, ret_gn_g, qk_norm_g,
           s5_a_re, s5_a_im, s5_log_dt, s5_b_re, s5_b_im, s5_c_re, s5_c_im, s5_d, s5_glu_w, s5_glu_b,
           moe_router_w, moe_w1, moe_w3, moe_w2, final_norm_g):
    batch, n_lat, d = x.shape
    n_ctx = ctx.shape[1]
    t = n_ctx + n_lat
    depth = mod_w.shape[0]
    assert batch == 8 and d % 128 == 0 and n_ctx % CHUNK == 0 and n_lat % CHUNK == 0
    tm = 256 if n_ctx % 256 == 0 else 128
    tb = tm
    tf = 512

    xs = jnp.concatenate([ctx, x], axis=1)
    rows = 16
    cond = jnp.zeros((rows, d), F32).at[:batch].set(c).at[batch].set(c_ctx)
    mod_all = _modulation(cond, mod_w, mod_b)
    cos, sin = _rope_tables(n_ctx, n_lat)
    lane_head = jnp.arange(SEG_W) // HEAD_DIM
    bd = (lane_head[:, None] == lane_head[None, :]).astype(BF16)
    w1, w3, w2 = moe_w1, moe_w3, moe_w2

    prev = None
    for layer in range(depth):
        m = mod_all[layer]
        lat = m[:batch].reshape(batch, 1, 6, d)
        cx = jnp.broadcast_to(m[batch].reshape(1, 1, 6, d), (batch, 1, 6, d))
        modtab = jnp.concatenate([cx, lat], axis=1)
        ng1 = norm_g[layer, 0].reshape(1, d)
        ng2 = norm_g[layer, 1].reshape(1, d)
        wr = jnp.zeros((d, ROUTER_LANES), F32).at[:, :N_EXPERTS].set(moe_router_w[layer])
        wrh = wr.astype(BF16)
        wrl = (wr - wrh.astype(F32)).astype(BF16)
        i = layer // 2
        if layer % 2 == 0:
            w_ext = _extend_in_proj(mix_in_w[i]).astype(BF16)
            reps = SEG_W // HEAD_DIM
            qkg = jnp.stack([jnp.tile(qk_norm_g[i, 0], reps), jnp.tile(qk_norm_g[i, 1], reps)])
            xs, (rq, rk, rv, rg, aq, ak4, av2) = _in_proj(xs, prev, modtab, ng1, w_ext, cos, sin, qkg, bd,
                                                         tm, n_ctx)
            rate = jnp.repeat(ret_log_rate[i], HEAD_DIM, axis=-1)
            rate_h = jnp.broadcast_to(ret_log_rate[i][:, :, None], (2, RET_HEADS, CHUNK))
            o_f, o_b = _retention(rq, rk, rv, rate, rate_h, bd, n_ctx)
            att = _attention(aq, ak4, av2, tm, tm, n_ctx)
            xs, f, aff = _out_proj(xs, o_f, o_b, rg, att, modtab, ret_gn_g[i].reshape(1, SEG_W),
                                   mix_out_w[i].astype(BF16), ng2, wrh, wrl, bd, tm, n_ctx)
        else:
            xs, u = _s5_pre(xs, prev, modtab, ng1, tm, n_ctx)
            tabs = _s5_tables(s5_a_re[i], s5_a_im[i], s5_log_dt[i], s5_b_re[i], s5_b_im[i],
                              s5_c_re[i], s5_c_im[i], s5_d[i])
            z = _s5_scan(u, *tabs, n_ctx)
            xs, f, aff = _glu(xs, z, modtab, s5_glu_w[i].astype(BF16), s5_glu_b[i].reshape(1, 2 * d),
                              ng2, wrh, wrl, tm, n_ctx)
        moe = _route_and_moe(layer, aff, f, w1, w3, w2, n_ctx, tb, tf)
        prev = (moe, modtab[:, :, 5:6, :])
    return _final(xs, prev[0], prev[1], final_norm_g.reshape(1, d), tm, n_ctx)
```

```python
import functools
import math

import jax
import jax.numpy as jnp
from jax import lax
from jax.experimental import pallas as pl
from jax.experimental.pallas import tpu as pltpu

F32 = jnp.float32
BF16 = jnp.bfloat16

GRID_W = 64
HEAD_DIM = 64
RET_HEADS = 8
ATT_HEADS = 8
ATT_KV_HEADS = 2
ATT_GROUP = ATT_HEADS // ATT_KV_HEADS
RET_WIDTH = RET_HEADS * HEAD_DIM
ATT_WIDTH = ATT_HEADS * HEAD_DIM
ATT_KV_WIDTH = ATT_KV_HEADS * HEAD_DIM
CHUNK = 128
ROPE_THETA = 10000.0
ROPE_AXIS_DIM = HEAD_DIM // 2
S5_GROUP = 16
S5_STATE = 64
S5_SUB = 8
S5_BLOCK_GROUPS = 8
S5_SCAN_UNROLL = 4
N_EXPERTS = 16
CAPACITY_FACTOR = 2
EPS = 1e-6
SEG_W = 512
LOG2_E = math.log2(math.e)
ROUTER_LANES = 128
MOE_WINDOW = 64

V7X_VMEM_BYTES = 64 * 1024 * 1024
VMEM_LIMIT = V7X_VMEM_BYTES - 8 * 1024 * 1024


def _cparams(sem):
    return pltpu.CompilerParams(dimension_semantics=sem, vmem_limit_bytes=VMEM_LIMIT)


def _dot(a, b):
    return jnp.dot(a, b, preferred_element_type=F32)


def _dot_nt(a, b):
    return lax.dot_general(a, b, (((1,), (1,)), ((), ())), preferred_element_type=F32)


def _dot_tn(a, b):
    return lax.dot_general(a, b, (((0,), (0,)), ((), ())), preferred_element_type=F32)


def _split(v):
    hi = v.astype(BF16)
    lo = (v - hi.astype(F32)).astype(BF16)
    return hi, lo


def _segsum(v, bd):
    hi, lo = _split(v)
    return _dot(hi, bd) + _dot(lo, bd)


def _sigmoid(v):
    return 1.0 / (1.0 + jnp.exp(-v))


def _silu(v):
    return v * _sigmoid(v)


def _gelu_tanh(v):
    c = math.sqrt(2.0 / math.pi)
    return 0.5 * v * (1.0 + jnp.tanh(c * (v + 0.044715 * (v * v * v))))


def _norm_mod(x, g, scale, shift):
    ms = jnp.mean(x * x, axis=-1, keepdims=True)
    return x * lax.rsqrt(ms + EPS) * g * (1.0 + scale) + shift


def _rope(x, cos, sin_signed):
    w = x.shape[-1]
    lane = lax.broadcasted_iota(jnp.int32, x.shape, 1)
    first = (lane % 32) < 16
    partner = jnp.where(first, pltpu.roll(x, w - 16, 1), pltpu.roll(x, 16, 1))
    return x * cos + partner * sin_signed


def _mod_kernel(c_ref, w_ref, b_ref, o_ref):
    s = _silu(c_ref[...]).astype(BF16)
    o_ref[0] = _dot(s, w_ref[0].astype(BF16)) + b_ref[0]


def _modulation(cond, mod_w, mod_b):
    depth, d, n6 = mod_w.shape
    rows = cond.shape[0]
    tn = n6 // 6
    return pl.pallas_call(
        _mod_kernel,
        grid=(depth, n6 // tn),
        in_specs=[
            pl.BlockSpec((rows, d), lambda l, j: (0, 0)),
            pl.BlockSpec((1, d, tn), lambda l, j: (l, 0, j)),
            pl.BlockSpec((1, 1, tn), lambda l, j: (l, 0, j)),
        ],
        out_specs=pl.BlockSpec((1, rows, tn), lambda l, j: (l, 0, j)),
        out_shape=jax.ShapeDtypeStruct((depth, rows, n6), F32),
        compiler_params=_cparams(("parallel", "parallel")),
    )(cond, mod_w, mod_b.reshape(depth, 1, n6))


def _in_kernel(x_ref, mod_ref, ng_ref, w_ref, cos_ref, sin_ref, qkg_ref, bd_ref,
               rq_ref, rk_ref, rv_ref, rg_ref, aq_ref, ak_ref, av_ref):
    x = x_ref[0]
    m = mod_ref[0, 0]
    h = _norm_mod(x, ng_ref[...], m[1:2], m[0:1]).astype(BF16)
    cos = cos_ref[...]
    sin = sin_ref[...]
    bd = bd_ref[...]

    def seg(i):
        return _dot(h, w_ref[:, i * SEG_W:(i + 1) * SEG_W])

    def headnorm(v, g):
        ms = _segsum(v * v, bd) * (1.0 / HEAD_DIM)
        return v * lax.rsqrt(ms + EPS) * g

    scale = HEAD_DIM ** -0.5
    rq_ref[0] = _rope(seg(0), cos, sin).astype(BF16)
    rk_ref[0] = _rope(seg(1) * scale, cos, sin).astype(BF16)
    rv_ref[0] = seg(2).astype(BF16)
    rg_ref[0] = seg(3)
    aq_ref[0] = (_rope(headnorm(seg(4), qkg_ref[0:1]), cos, sin) * (scale * LOG2_E)).astype(BF16)
    ak_ref[0] = _rope(headnorm(seg(5), qkg_ref[1:2]), cos, sin).astype(BF16)
    av_ref[0] = _dot(h, w_ref[:, 6 * SEG_W:]).astype(BF16)


def _row_spec(tm, d, lead=0):
    return pl.BlockSpec((1, tm, d), lambda b, i: (b, i + lead, 0))


def _mod_spec(d, n, seg_tiles):
    return pl.BlockSpec((1, 1, n, d), lambda b, i: (b, jnp.where(i >= seg_tiles, 1, 0), 0, 0))


def _const_spec(shape):
    nd = len(shape)
    return pl.BlockSpec(shape, lambda b, i: (0,) * nd)


def _in_proj(x, modtab, norm_g, w_ext, cos, sin, qkg, bd, tm, n_ctx):
    b, t, d = x.shape
    out_shape, out_specs = [], []
    for dt, width in ((BF16, SEG_W), (BF16, SEG_W), (BF16, SEG_W), (F32, SEG_W), (BF16, SEG_W), (BF16, SEG_W),
                      (BF16, ATT_KV_HEADS * 2 * HEAD_DIM)):
        out_shape.append(jax.ShapeDtypeStruct((b, t, width), dt))
        out_specs.append(_row_spec(tm, width))
    return pl.pallas_call(
        _in_kernel,
        grid=(b, t // tm),
        in_specs=[
            _row_spec(tm, d),
            _mod_spec(d, 6, n_ctx // tm),
            _const_spec((1, d)),
            _const_spec(w_ext.shape),
            pl.BlockSpec((tm, SEG_W), lambda b, i: (i, 0)),
            pl.BlockSpec((tm, SEG_W), lambda b, i: (i, 0)),
            _const_spec((2, SEG_W)),
            _const_spec((SEG_W, SEG_W)),
        ],
        out_specs=out_specs,
        out_shape=out_shape,
        compiler_params=_cparams(("parallel", "parallel")),
    )(x, modtab, norm_g, w_ext, cos, sin, qkg, bd)


def _ret_kernel(qf_ref, kf_ref, vf_ref, qb_ref, kb_ref, vb_ref, rate_ref, rate_h_ref, bd_ref,
                of_ref, ob_ref, sf_s, sb_s, dm_s, dq_s, dk_s, dc_s):
    s = pl.program_id(1)
    c = CHUNK

    @pl.when(s == 0)
    def _():
        sf_s[...] = jnp.zeros_like(sf_s)
        sb_s[...] = jnp.zeros_like(sb_s)
        pos = lax.broadcasted_iota(jnp.int32, (c, SEG_W), 0).astype(F32)
        lgf = -jnp.exp(rate_ref[0:1, :])
        lgb = -jnp.exp(rate_ref[1:2, :])
        dq_s[0] = jnp.exp((pos + 1.0) * lgf)
        dk_s[0] = jnp.exp((c - 1.0 - pos) * lgf)
        dc_s[0] = jnp.exp(float(c) * lgf)
        dq_s[1] = jnp.exp((c - pos) * lgb)
        dk_s[1] = jnp.exp(pos * lgb)
        dc_s[1] = jnp.exp(float(c) * lgb)
        ii = lax.broadcasted_iota(jnp.int32, (c, c), 0)
        jj = lax.broadcasted_iota(jnp.int32, (c, c), 1)
        for h in range(RET_HEADS):
            lf = -jnp.exp(rate_h_ref[0, h:h + 1, :])
            lb = -jnp.exp(rate_h_ref[1, h:h + 1, :])
            mf = ii >= jj
            mb = jj > ii
            dm_s[0, h] = jnp.where(mf, jnp.exp(jnp.where(mf, ii - jj, 0).astype(F32) * lf), 0.0)
            dm_s[1, h] = jnp.where(mb, jnp.exp(jnp.where(mb, jj - ii, 0).astype(F32) * lb), 0.0)

    bdm = bd_ref[...].astype(F32)
    head_of_lane = lax.broadcasted_iota(jnp.int32, (1, SEG_W // 2), 1) // HEAD_DIM

    def direction(d, q_ref, k_ref, v_ref, o_ref, st):
        q = q_ref[0]
        k = k_ref[0]
        v = v_ref[0]
        state = st[...]
        qd = (q.astype(F32) * dq_s[d]).astype(BF16)
        inter = _dot(qd, state.astype(BF16))
        for half in range(2):
            sl = slice(half * 256, half * 256 + 256)
            qh, kh, vh = q[:, sl], k[:, sl], v[:, sl]
            acc = inter[:, sl]
            for hh in range(4):
                msk = head_of_lane == hh
                sc = _dot_nt(jnp.where(msk, qh, jnp.zeros_like(qh)), kh) * dm_s[d, half * 4 + hh]
                acc = acc + _dot(sc.astype(BF16), jnp.where(msk, vh, jnp.zeros_like(vh)))
            o_ref[0, :, sl] = acc
        kd = (k.astype(F32) * dk_s[d]).astype(BF16)
        st[...] = state * dc_s[d] + bdm * _dot_tn(kd, v)

    direction(0, qf_ref, kf_ref, vf_ref, of_ref, sf_s)
    direction(1, qb_ref, kb_ref, vb_ref, ob_ref, sb_s)


def _retention(rq, rk, rv, rate, rate_h, bd, n_ctx):
    b, t, w = rq.shape
    nc = t // CHUNK
    lc = n_ctx // CHUNK

    def fwd(bi, s):
        return (bi, s, 0)

    def bwd(bi, s):
        return (bi, jnp.where(s < lc, lc - 1 - s, nc - 1 - (s - lc)), 0)

    blk = (1, CHUNK, w)
    return pl.pallas_call(
        _ret_kernel,
        grid=(b, nc),
        in_specs=[pl.BlockSpec(blk, fwd)] * 3 + [pl.BlockSpec(blk, bwd)] * 3 + [
            _const_spec((2, w)),
            _const_spec((2, RET_HEADS, CHUNK)),
            _const_spec((w, w)),
        ],
        out_specs=[pl.BlockSpec(blk, fwd), pl.BlockSpec(blk, bwd)],
        out_shape=[jax.ShapeDtypeStruct((b, t, w), F32)] * 2,
        scratch_shapes=[
            pltpu.VMEM((w, w), F32), pltpu.VMEM((w, w), F32),
            pltpu.VMEM((2, RET_HEADS, CHUNK, CHUNK), F32),
            pltpu.VMEM((2, CHUNK, w), F32), pltpu.VMEM((2, CHUNK, w), F32),
            pltpu.VMEM((2, 1, w), F32),
        ],
        compiler_params=_cparams(("parallel", "arbitrary")),
    )(rq, rk, rv, rq, rk, rv, rate, rate_h, bd)


ATT_ROW_BLOCK = 512


def _att_kernel(q_ref, k_ref, v_ref, o_ref, qs_s, m_s, acc_s, sa_s, sb_s, ma_s, mb_s, *, tq, tk, n_ctx, t):
    qi = pl.program_id(1)
    gw = ATT_GROUP * HEAD_DIM
    head_of_lane = lax.broadcasted_iota(jnp.int32, (1, gw), 1) // HEAD_DIM
    for h in range(ATT_HEADS):
        q = q_ref[0, :, (h // ATT_GROUP) * gw:(h // ATT_GROUP + 1) * gw]
        qs_s[h * tq:(h + 1) * tq, :] = jnp.where(head_of_lane == h % ATT_GROUP, q, jnp.zeros_like(q))
    m_s[...] = jnp.full(m_s.shape, -1e30, F32)
    acc_s[...] = jnp.zeros_like(acc_s)
    nk = jnp.where(qi * tq < n_ctx, n_ctx // tk, t // tk)
    vw = 2 * HEAD_DIM
    ones_lane = lax.broadcasted_iota(jnp.int32, (1, ATT_KV_HEADS * vw), 1) % vw >= HEAD_DIM
    rb = ATT_ROW_BLOCK

    blocks = [(slice(r * rb, (r + 1) * rb), (r * rb) // (ATT_GROUP * tq)) for r in range(ATT_HEADS * tq // rb)]

    def scores_into(j, dst, dst_max):
        kb = k_ref[0, pl.ds(pl.multiple_of(j * tk, tk), tk), :]
        for rows, g in blocks:
            sc = _dot_nt(qs_s[rows, :], kb[:, g * gw:(g + 1) * gw])
            dst[rows, :] = sc
            dst_max[rows, :] = jnp.broadcast_to(jnp.max(sc, axis=1, keepdims=True), (rb, vw))

    def consume(j, src, src_max):
        vb = v_ref[0, pl.ds(pl.multiple_of(j * tk, tk), tk), :]
        vext = jnp.where(ones_lane, jnp.ones_like(vb), vb)
        for rows, g in blocks:
            m_prev = m_s[rows, :]
            m_new = jnp.maximum(m_prev, src_max[rows, :])
            p = jnp.exp2(src[rows, :] - jnp.concatenate([m_new] * (tk // vw), axis=1)).astype(BF16)
            acc_s[rows, :] = jnp.exp2(m_prev - m_new) * acc_s[rows, :] + _dot(p, vext[:, g * vw:(g + 1) * vw])
            m_s[rows, :] = m_new

    scores_into(0, sa_s, ma_s)

    def body(i, carry):
        j = 2 * i
        scores_into(j + 1, sb_s, mb_s)
        consume(j, sa_s, ma_s)
        scores_into(jnp.minimum(j + 2, nk - 1), sa_s, ma_s)
        consume(j + 1, sb_s, mb_s)
        return carry

    lax.fori_loop(0, nk // 2, body, 0)

    @pl.when(nk % 2 == 1)
    def _():
        consume(nk - 1, sa_s, ma_s)

    low = lax.broadcasted_iota(jnp.int32, (tq, vw), 1) < HEAD_DIM
    for pair in range(ATT_HEADS // 2):
        a0 = acc_s[(2 * pair) * tq:(2 * pair + 1) * tq, :]
        a1 = acc_s[(2 * pair + 1) * tq:(2 * pair + 2) * tq, :]
        even = a0 / pltpu.roll(a0, HEAD_DIM, 1)
        odd = pltpu.roll(a1, HEAD_DIM, 1) / a1
        o_ref[0, :, pair * vw:(pair + 1) * vw] = jnp.where(low, even, odd).astype(BF16)


def _attention(aq, ak4, av2, tq, tk, n_ctx):
    b, t, w = aq.shape
    gw = ATT_GROUP * HEAD_DIM
    vw = 2 * HEAD_DIM
    rows = ATT_HEADS * tq
    return pl.pallas_call(
        functools.partial(_att_kernel, tq=tq, tk=tk, n_ctx=n_ctx, t=t),
        grid=(b, t // tq),
        in_specs=[
            pl.BlockSpec((1, tq, w), lambda bi, i: (bi, i, 0)),
            pl.BlockSpec((1, t, w), lambda bi, i: (bi, 0, 0)),
            pl.BlockSpec((1, t, ATT_KV_HEADS * vw), lambda bi, i: (bi, 0, 0)),
        ],
        out_specs=pl.BlockSpec((1, tq, w), lambda bi, i: (bi, i, 0)),
        out_shape=jax.ShapeDtypeStruct((b, t, w), BF16),
        scratch_shapes=[pltpu.VMEM((rows, gw), BF16), pltpu.VMEM((rows, vw), F32), pltpu.VMEM((rows, vw), F32),
                        pltpu.VMEM((rows, tk), F32), pltpu.VMEM((rows, tk), F32), pltpu.VMEM((rows, vw), F32),
                        pltpu.VMEM((rows, vw), F32)],
        compiler_params=_cparams(("parallel", "parallel")),
    )(aq, ak4, av2)


def _post_mixer(x, delta, m, ng2, wrh, wrl, x_ref, f_ref, aff_ref):
    xn = x + m[2:3] * delta
    x_ref[0] = xn
    f = _norm_mod(xn, ng2, m[4:5], m[3:4])
    fh, fl = _split(f)
    f_ref[0] = fh
    logits = _dot(fh, wrh) + _dot(fl, wrh) + _dot(fh, wrl)
    lane = lax.broadcasted_iota(jnp.int32, logits.shape, 1)
    logits = jnp.where(lane < N_EXPERTS, logits, -1e30)
    e = jnp.exp(logits - jnp.max(logits, axis=1, keepdims=True))
    aff_ref[0] = e / jnp.sum(e, axis=1, keepdims=True)


def _out_kernel(x_ref, of_ref, ob_ref, rg_ref, att_ref, mod_ref, gn_ref, w_ref, ng2_ref, wrh_ref, wrl_ref,
                bd_ref, xo_ref, f_ref, aff_ref):
    bd = bd_ref[...]
    o = of_ref[0] + ob_ref[0]
    mu = _segsum(o, bd) * (1.0 / HEAD_DIM)
    oc = o - mu
    var = _segsum(oc * oc, bd) * (1.0 / HEAD_DIM)
    r = _silu(rg_ref[0]) * (oc * lax.rsqrt(var + EPS) * gn_ref[...])
    y = _dot(r.astype(BF16), w_ref[0:RET_WIDTH, :]) + _dot(att_ref[0], w_ref[RET_WIDTH:, :])
    _post_mixer(x_ref[0], y, mod_ref[0, 0], ng2_ref[...], wrh_ref[...], wrl_ref[...], xo_ref, f_ref, aff_ref)


def _post_specs(b, t, d, tm):
    out_shape = [jax.ShapeDtypeStruct((b, t, d), F32), jax.ShapeDtypeStruct((b, t, d), BF16),
                 jax.ShapeDtypeStruct((b, t, ROUTER_LANES), F32)]
    out_specs = [_row_spec(tm, d), _row_spec(tm, d), _row_spec(tm, ROUTER_LANES)]
    return out_shape, out_specs


def _out_proj(x, of, ob, rg, att, modtab, gn_g, w_out, ng2, wrh, wrl, bd, tm, n_ctx):
    b, t, d = x.shape
    out_shape, out_specs = _post_specs(b, t, d, tm)
    return pl.pallas_call(
        _out_kernel,
        grid=(b, t // tm),
        in_specs=[
            _row_spec(tm, d), _row_spec(tm, SEG_W), _row_spec(tm, SEG_W), _row_spec(tm, SEG_W),
            _row_spec(tm, SEG_W), _mod_spec(d, 6, n_ctx // tm), _const_spec((1, SEG_W)),
            _const_spec(w_out.shape), _const_spec((1, d)), _const_spec(wrh.shape), _const_spec(wrl.shape),
            _const_spec((SEG_W, SEG_W)),
        ],
        out_specs=out_specs,
        out_shape=out_shape,
        compiler_params=_cparams(("parallel", "parallel")),
    )(x, of, ob, rg, att, modtab, gn_g, w_out, ng2, wrh, wrl, bd)


def _s5_pre_kernel(x_ref, mod_ref, ng_ref, u_ref):
    m = mod_ref[0, 0]
    u_ref[0] = _norm_mod(x_ref[0], ng_ref[...], m[1:2], m[0:1])


def _s5_pre(x, modtab, norm_g, tm, n_ctx):
    b, t, d = x.shape
    return pl.pallas_call(
        _s5_pre_kernel,
        grid=(b, t // tm),
        in_specs=[_row_spec(tm, d), _mod_spec(d, 6, n_ctx // tm), _const_spec((1, d))],
        out_specs=_row_spec(tm, d),
        out_shape=jax.ShapeDtypeStruct((b, t, d), F32),
        compiler_params=_cparams(("parallel", "parallel")),
    )(x, modtab, norm_g)


def _s5_kernel(u_ref, tm_ref, wz_ref, wo_ref, ar_ref, ai_ref, d_ref, z_ref,
               zr_s, zi_s, xr_s, xi_s, zs_s, *, n_sub, n_sub_ctx):
    s_n, j_sub, gb, half = n_sub, S5_SUB, S5_BLOCK_GROUPS, 2 * S5_STATE
    uf = jnp.concatenate([u_ref[0, pl.ds(m, s_n, stride=j_sub), :] for m in range(j_sub)], axis=1)
    lhs = uf.astype(BF16)
    for g in range(gb):
        zz = _dot(lhs, wz_ref[0, g])
        zr_s[pl.ds(g, s_n, stride=gb), :] = zz[:, :half]
        zi_s[pl.ds(g, s_n, stride=gb), :] = zz[:, half:]
    fwd_lane = lax.broadcasted_iota(jnp.int32, (gb, half), 1) < S5_STATE
    ar = ar_ref[0]
    ai = ai_ref[0]

    def body(i, carry):
        xr, xi = carry
        sb = jnp.where(i < n_sub_ctx, n_sub_ctx - 1 - i, s_n - 1 - (i - n_sub_ctx))
        fwd = pl.ds(pl.multiple_of(i * gb, gb), gb)
        bwd = pl.ds(pl.multiple_of(sb * gb, gb), gb)
        xr_s[fwd, 0:S5_STATE] = xr[:, 0:S5_STATE]
        xi_s[fwd, 0:S5_STATE] = xi[:, 0:S5_STATE]
        xr_s[bwd, S5_STATE:half] = xr[:, S5_STATE:half]
        xi_s[bwd, S5_STATE:half] = xi[:, S5_STATE:half]
        zr = jnp.where(fwd_lane, zr_s[fwd, :], zr_s[bwd, :])
        zi = jnp.where(fwd_lane, zi_s[fwd, :], zi_s[bwd, :])
        return ar * xr - ai * xi + zr, ar * xi + ai * xr + zi

    zero = jnp.zeros((gb, half), F32)
    lax.fori_loop(0, s_n, body, (zero, zero), unroll=S5_SCAN_UNROLL)
    y = _dot(lhs, tm_ref[0]) + d_ref[0] * uf
    for g in range(gb):
        rows = pl.ds(g, s_n, stride=gb)
        xp = jnp.concatenate([xr_s[rows, :], xi_s[rows, :]], axis=1).astype(BF16)
        y = y + _dot(xp, wo_ref[0, g])
    z = _gelu_tanh(y)
    w = z_ref.shape[-1]
    for n in range(j_sub):
        zs_s[pl.ds(n, s_n, stride=j_sub), :] = z[:, n * w:(n + 1) * w]
    z_ref[0] = zs_s[...].astype(BF16)


def _s5_scan(u, tmat, wz, wo, ar, ai, dt, n_ctx):
    b, t, d = u.shape
    cw = S5_BLOCK_GROUPS * S5_GROUP
    n_sub = t // S5_SUB
    lw = S5_SUB * cw
    sw = 4 * S5_STATE
    gb = S5_BLOCK_GROUPS
    once = pl.Buffered(1)
    return pl.pallas_call(
        functools.partial(_s5_kernel, n_sub=n_sub, n_sub_ctx=n_ctx // S5_SUB),
        grid=(d // cw, b),
        in_specs=[
            pl.BlockSpec((1, t, cw), lambda c, bi: (bi, 0, c)),
            pl.BlockSpec((1, lw, lw), lambda c, bi: (c, 0, 0), pipeline_mode=once),
            pl.BlockSpec((1, gb, lw, sw), lambda c, bi: (c, 0, 0, 0), pipeline_mode=once),
            pl.BlockSpec((1, gb, sw, lw), lambda c, bi: (c, 0, 0, 0), pipeline_mode=once),
            pl.BlockSpec((1, gb, 2 * S5_STATE), lambda c, bi: (c, 0, 0)),
            pl.BlockSpec((1, gb, 2 * S5_STATE), lambda c, bi: (c, 0, 0)),
            pl.BlockSpec((1, 1, lw), lambda c, bi: (c, 0, 0)),
        ],
        out_specs=pl.BlockSpec((1, t, cw), lambda c, bi: (bi, 0, c)),
        out_shape=jax.ShapeDtypeStruct((b, t, d), BF16),
        scratch_shapes=[pltpu.VMEM((gb * n_sub, 2 * S5_STATE), F32)] * 4 + [pltpu.VMEM((t, cw), F32)],
        compiler_params=_cparams(("parallel", "parallel")),
    )(u, tmat, wz, wo, ar, ai, dt)


def _s5_tables(a_re, a_im, log_dt, b_re, b_im, c_re, c_im, d_skip):
    hp = lax.Precision.HIGHEST
    j_sub = S5_SUB
    gb = S5_BLOCK_GROUPS
    lam_re = jnp.minimum(a_re, -1e-4)
    lam_im = a_im
    dt = jnp.exp(log_dt)[..., None]
    mag = jnp.exp(lam_re * dt)
    bar_re = mag * jnp.cos(lam_im * dt)
    bar_im = mag * jnp.sin(lam_im * dt)
    den = lam_re * lam_re + lam_im * lam_im
    nr, ni = bar_re - 1.0, bar_im
    k_re = (nr * lam_re + ni * lam_im) / den
    k_im = (ni * lam_re - nr * lam_im) / den
    bb_re = k_re[..., None] * b_re[None] - k_im[..., None] * b_im[None]
    bb_im = k_re[..., None] * b_im[None] + k_im[..., None] * b_re[None]
    jj = jnp.arange(j_sub + 1, dtype=F32)[:, None, None, None]
    pmag = jnp.exp(jj * (lam_re * dt)[None])
    pw_re = pmag * jnp.cos(jj * (lam_im * dt)[None])
    pw_im = pmag * jnp.sin(jj * (lam_im * dt)[None])
    pb_re = pw_re[..., None] * bb_re[None] - pw_im[..., None] * bb_im[None]
    pb_im = pw_re[..., None] * bb_im[None] + pw_im[..., None] * bb_re[None]
    cp_re = c_re[None] * pw_re[:, :, :, None, :] - c_im[None] * pw_im[:, :, :, None, :]
    cp_im = c_re[None] * pw_im[:, :, :, None, :] + c_im[None] * pw_re[:, :, :, None, :]
    kern = (jnp.einsum('dgcp,jdgpk->jdgkc', c_re, pb_re[:j_sub], precision=hp)
            - jnp.einsum('dgcp,jdgpk->jdgkc', c_im, pb_im[:j_sub], precision=hp))
    groups = a_re.shape[1]
    nblk = groups // gb
    k_w = S5_GROUP
    p_w = S5_STATE
    lw = j_sub * gb * k_w
    eye = jnp.eye(gb, dtype=F32)
    place = (jnp.arange(gb * k_w)[None, None, :] == (jnp.arange(gb)[:, None, None] * k_w
                                                      + jnp.arange(k_w)[None, :, None])).astype(F32)
    blk = lambda a: a.reshape(a.shape[0], nblk, gb, *a.shape[2:])
    pos = jnp.arange(j_sub)
    lag = pos[None, :] - pos[:, None]
    kf = jnp.where((lag >= 0)[:, :, None, None, None], kern[jnp.clip(lag, 0, j_sub - 1), 0], 0.0)
    kb = jnp.where((lag <= 0)[:, :, None, None, None], kern[jnp.clip(-lag, 0, j_sub - 1), 1], 0.0)
    kfull = (kf + kb).reshape(j_sub, j_sub, nblk, gb, k_w, k_w)
    tmat = jnp.einsum('mnBgkc,gcd->Bmgknd', kfull, place, precision=hp).reshape(nblk, lw, lw)
    rev = pos[::-1]
    zc = jnp.concatenate([pb_re[rev, 0], pb_re[pos, 1], pb_im[rev, 0], pb_im[pos, 1]], axis=2)
    wz = jnp.einsum('mBgqk,gh->Bgmhkq', blk(zc), eye).reshape(nblk, gb, lw, 4 * p_w)
    def orows(cp, d, order):
        return jnp.einsum('nBgcp,gcd->Bgpnd', blk(cp[order, d]), place, precision=hp).reshape(nblk, gb, p_w, lw)
    wo = jnp.concatenate([orows(cp_re, 0, pos + 1), orows(cp_re, 1, j_sub - pos),
                          -orows(cp_im, 0, pos + 1), -orows(cp_im, 1, j_sub - pos)], axis=2)
    ar = jnp.concatenate([pw_re[j_sub, 0], pw_re[j_sub, 1]], axis=-1).reshape(nblk, gb, 2 * p_w)
    ai = jnp.concatenate([pw_im[j_sub, 0], pw_im[j_sub, 1]], axis=-1).reshape(nblk, gb, 2 * p_w)
    dt_tile = jnp.tile(d_skip.reshape(nblk, 1, gb * k_w), (1, 1, j_sub))
    return tmat.astype(BF16), wz.astype(BF16), wo.astype(BF16), ar, ai, dt_tile


def _glu_kernel(x_ref, z_ref, mod_ref, w_ref, b_ref, ng2_ref, wrh_ref, wrl_ref, xo_ref, f_ref, aff_ref):
    d = x_ref.shape[-1]
    ag = _dot(z_ref[0], w_ref[...]) + b_ref[...]
    delta = ag[:, :d] * _sigmoid(ag[:, d:])
    _post_mixer(x_ref[0], delta, mod_ref[0, 0], ng2_ref[...], wrh_ref[...], wrl_ref[...], xo_ref, f_ref, aff_ref)


def _glu(x, z, modtab, glu_w, glu_b, ng2, wrh, wrl, tm, n_ctx):
    b, t, d = x.shape
    out_shape, out_specs = _post_specs(b, t, d, tm)
    return pl.pallas_call(
        _glu_kernel,
        grid=(b, t // tm),
        in_specs=[_row_spec(tm, d), _row_spec(tm, d), _mod_spec(d, 6, n_ctx // tm),
                  _const_spec(glu_w.shape), _const_spec((1, 2 * d)), _const_spec((1, d)),
                  _const_spec(wrh.shape), _const_spec(wrl.shape)],
        out_specs=out_specs,
        out_shape=out_shape,
        compiler_params=_cparams(("parallel", "parallel")),
    )(x, z, modtab, glu_w, glu_b, ng2, wrh, wrl)


def _cumsum_lanes(v):
    n = v.shape[-1]
    lane = lax.broadcasted_iota(jnp.int32, v.shape, 1)
    sh = 1
    while sh < n:
        v = v + jnp.where(lane >= sh, pltpu.roll(v, sh, 1), 0.0)
        sh *= 2
    return v


def _select_segment(aff, cap, base):
    bits = pltpu.bitcast(aff, jnp.int32)
    thr = jnp.zeros((aff.shape[0], 1), jnp.int32)
    for bit in range(30, -1, -1):
        cand = thr | (1 << bit)
        cnt = jnp.sum(jnp.where(bits >= cand, 1.0, 0.0), axis=1, keepdims=True)
        thr = jnp.where(cnt >= cap, cand, thr)
    gt = bits > thr
    eq = bits == thr
    need = cap - jnp.sum(jnp.where(gt, 1.0, 0.0), axis=1, keepdims=True)
    eqf = jnp.where(eq, 1.0, 0.0)
    eq_before = _cumsum_lanes(eqf) - eqf
    sel = jnp.where(gt, 1.0, jnp.where(eq, jnp.where(eq_before < need, 1.0, 0.0), 0.0))
    pos = _cumsum_lanes(sel) - sel
    return jnp.where(sel > 0.0, pos.astype(jnp.int32) + base, -1)


def _select_kernel(aff_ref, slot_ref, *, n_ctx, cap_ctx, cap_lat):
    slot_ref[0, :, 0:n_ctx] = _select_segment(aff_ref[0, :, 0:n_ctx], cap_ctx, 0)
    slot_ref[0, :, n_ctx:] = _select_segment(aff_ref[0, :, n_ctx:], cap_lat, cap_ctx)


def _select(aff_t, n_ctx, cap_ctx, cap_lat):
    b, e, t = aff_t.shape
    return pl.pallas_call(
        functools.partial(_select_kernel, n_ctx=n_ctx, cap_ctx=cap_ctx, cap_lat=cap_lat),
        grid=(b,),
        in_specs=[pl.BlockSpec((1, e, t), lambda i: (i, 0, 0))],
        out_specs=pl.BlockSpec((1, e, t), lambda i: (i, 0, 0)),
        out_shape=jax.ShapeDtypeStruct((b, e, t), jnp.int32),
        compiler_params=_cparams(("parallel",)),
    )(aff_t)


def _moe_window(a_ref, base, e, k, rows, w):
    nominal = a_ref[base + e] + k * w
    return nominal, jnp.minimum(nominal, rows - w)


def _moe_hits(slot_row, nominal, actual, w):
    rid = lax.broadcasted_iota(jnp.int32, (w, slot_row.shape[-1]), 0) + actual
    return jnp.where(slot_row >= nominal, slot_row, -1) == rid


def _moe_gather_kernel(a_ref, nr_ref, slot_ref, aff_ref, f_ref, xs_ref, gate_ref, *, rows, w):
    b = pl.program_id(0)
    j = pl.program_id(1)
    n_e = slot_ref.shape[2]
    step = b * pl.num_programs(1) + j

    @pl.when(j == 0)
    def _():
        xs_ref[...] = jnp.zeros_like(xs_ref)
        gate_ref[...] = jnp.zeros_like(gate_ref)

    fj = f_ref[0]

    def one_round(k, carry):
        for e in range(n_e):
            nominal, actual = _moe_window(a_ref, step * n_e, e, k, rows, w)
            hit = _moe_hits(slot_ref[0, 0, e:e + 1, :], nominal, actual, w)
            off = pl.multiple_of(e * rows + actual, 16)
            xs_ref[0, pl.ds(off, w), :] += _dot(jnp.where(hit, 1.0, 0.0).astype(BF16), fj).astype(BF16)
            gate_ref[0, pl.ds(off, w), :] += jnp.sum(jnp.where(hit, aff_ref[0, 0, e:e + 1, :], 0.0),
                                                     axis=1, keepdims=True)
        return carry

    lax.fori_loop(0, nr_ref[step], one_round, 0)


def _moe_ffn_kernel(xs_ref, gate_ref, w1_ref, w3_ref, w2_ref, y_ref, ys_s, *, n_ft):
    ft = pl.program_id(2)
    xs = xs_ref[0]
    hid = (_silu(_dot(xs, w1_ref[0, 0].astype(BF16))) * _dot(xs, w3_ref[0, 0].astype(BF16))).astype(BF16)
    part = _dot(hid, w2_ref[0, 0].astype(BF16))

    @pl.when(ft == 0)
    def _():
        ys_s[...] = part

    @pl.when(ft > 0)
    def _():
        ys_s[...] += part

    @pl.when(ft == n_ft - 1)
    def _():
        y_ref[0] = (ys_s[...] * gate_ref[0]).astype(BF16)


def _moe_combine_kernel(a_ref, nr_ref, slot_ref, y_ref, x_ref, mod_ref, o_ref, *, rows, w):
    b = pl.program_id(0)
    j = pl.program_id(1)
    n_e = slot_ref.shape[2]
    step = b * pl.num_programs(1) + j

    def one_round(k, acc):
        onehots, ys = [], []
        for e in range(n_e):
            nominal, actual = _moe_window(a_ref, step * n_e, e, k, rows, w)
            hit = _moe_hits(slot_ref[0, 0, e:e + 1, :], nominal, actual, w)
            onehots.append(jnp.where(hit, 1.0, 0.0).astype(BF16))
            ys.append(y_ref[0, pl.ds(pl.multiple_of(e * rows + actual, 16), w), :])
        return acc + _dot_tn(jnp.concatenate(onehots, axis=0), jnp.concatenate(ys, axis=0))

    acc = lax.fori_loop(0, nr_ref[step], one_round, jnp.zeros(o_ref.shape[1:], F32))
    o_ref[0] = x_ref[0] + mod_ref[0, 0, 5:6, :] * acc


def _moe_windows(slot, tb, w):
    b, e, t = slot.shape
    nb = t // tb
    cnt = jnp.sum((slot >= 0).reshape(b, e, nb, tb), axis=-1, dtype=jnp.int32)
    start = jnp.cumsum(cnt, axis=-1) - cnt
    a0 = (start // 16) * 16
    rounds = jnp.maximum(jnp.max((start + cnt - a0 + w - 1) // w, axis=1), 1)
    return jnp.swapaxes(a0, 1, 2).reshape(-1).astype(jnp.int32), rounds.reshape(-1).astype(jnp.int32)


def _route_and_moe(layer, x, aff, f, modtab, w1, w3, w2, n_ctx, tb, tf):
    b, t, d = f.shape
    n_e = w1.shape[1]
    ff = w1.shape[-1]
    nb = t // tb
    aff_t = jnp.swapaxes(aff[:, :, :N_EXPERTS], 1, 2)
    cap_ctx = (CAPACITY_FACTOR * n_ctx) // N_EXPERTS
    cap_lat = (CAPACITY_FACTOR * (t - n_ctx)) // N_EXPERTS
    rows = cap_ctx + cap_lat
    w = min(MOE_WINDOW, rows)
    slot = _select(aff_t, n_ctx, cap_ctx, cap_lat)
    a0, rounds = _moe_windows(slot, tb, w)
    by_block = lambda v: jnp.swapaxes(v.reshape(b, n_e, nb, tb), 1, 2)
    slot_b, aff_b = by_block(slot), by_block(aff_t)
    blk_spec = pl.BlockSpec((1, 1, n_e, tb), lambda bi, j, *_: (bi, j, 0, 0))
    whole = lambda width: pl.BlockSpec((1, n_e * rows, width), lambda bi, j, *_: (bi, 0, 0),
                                       pipeline_mode=pl.Buffered(1))
    xs, gate = pl.pallas_call(
        functools.partial(_moe_gather_kernel, rows=rows, w=w),
        grid_spec=pltpu.PrefetchScalarGridSpec(
            num_scalar_prefetch=2, grid=(b, nb),
            in_specs=[blk_spec, blk_spec, pl.BlockSpec((1, tb, d), lambda bi, j, *_: (bi, j, 0))],
            out_specs=[whole(d), whole(1)]),
        out_shape=[jax.ShapeDtypeStruct((b, n_e * rows, d), BF16), jax.ShapeDtypeStruct((b, n_e * rows, 1), F32)],
        compiler_params=_cparams(("parallel", "arbitrary")),
    )(a0, rounds, slot_b, aff_b, f)
    n_ft = ff // tf
    y = pl.pallas_call(
        functools.partial(_moe_ffn_kernel, n_ft=n_ft),
        grid=(b, n_e, n_ft),
        in_specs=[
            pl.BlockSpec((1, rows, d), lambda bi, e, k: (bi, e, 0)),
            pl.BlockSpec((1, rows, 1), lambda bi, e, k: (bi, e, 0)),
            pl.BlockSpec((1, 1, d, tf), lambda bi, e, k: (layer, e, 0, k)),
            pl.BlockSpec((1, 1, d, tf), lambda bi, e, k: (layer, e, 0, k)),
            pl.BlockSpec((1, 1, tf, d), lambda bi, e, k: (layer, e, k, 0)),
        ],
        out_specs=pl.BlockSpec((1, rows, d), lambda bi, e, k: (bi, e, 0)),
        out_shape=jax.ShapeDtypeStruct((b, n_e * rows, d), BF16),
        scratch_shapes=[pltpu.VMEM((rows, d), F32)],
        compiler_params=_cparams(("parallel", "parallel", "arbitrary")),
    )(xs, gate, w1, w3, w2)
    seg_blocks = n_ctx // tb
    return pl.pallas_call(
        functools.partial(_moe_combine_kernel, rows=rows, w=w),
        grid_spec=pltpu.PrefetchScalarGridSpec(
            num_scalar_prefetch=2, grid=(b, nb),
            in_specs=[blk_spec, whole(d), pl.BlockSpec((1, tb, d), lambda bi, j, *_: (bi, j, 0)),
                      pl.BlockSpec((1, 1, 6, d), lambda bi, j, *_: (bi, jnp.where(j >= seg_blocks, 1, 0), 0, 0))],
            out_specs=pl.BlockSpec((1, tb, d), lambda bi, j, *_: (bi, j, 0))),
        out_shape=jax.ShapeDtypeStruct((b, t, d), F32),
        compiler_params=_cparams(("parallel", "parallel")),
    )(a0, rounds, slot_b, y, x, modtab)


def _final_kernel(x_ref, g_ref, o_ref):
    x = x_ref[0]
    o_ref[0] = x * lax.rsqrt(jnp.mean(x * x, axis=-1, keepdims=True) + EPS) * g_ref[...]


def _final(x, g, tm, n_ctx):
    b, t, d = x.shape
    return pl.pallas_call(
        _final_kernel,
        grid=(b, (t - n_ctx) // tm),
        in_specs=[_row_spec(tm, d, n_ctx // tm), _const_spec((1, d))],
        out_specs=_row_spec(tm, d),
        out_shape=jax.ShapeDtypeStruct((b, t - n_ctx, d), F32),
        compiler_params=_cparams(("parallel", "parallel")),
    )(x, g)


def _rope_tables(n_ctx, n_lat):
    n_rows = n_lat // GRID_W
    row = jnp.repeat(jnp.arange(n_rows, dtype=F32), GRID_W)
    col = jnp.tile(jnp.arange(GRID_W, dtype=F32), n_rows)
    inv = ROPE_THETA ** (-jnp.arange(0, ROPE_AXIS_DIM, 2, dtype=F32) / ROPE_AXIS_DIM)
    ang_r = row[:, None] * inv[None, :]
    ang_c = col[:, None] * inv[None, :]
    cos = jnp.concatenate([jnp.cos(ang_r), jnp.cos(ang_r), jnp.cos(ang_c), jnp.cos(ang_c)], axis=-1)
    sin = jnp.concatenate([-jnp.sin(ang_r), jnp.sin(ang_r), -jnp.sin(ang_c), jnp.sin(ang_c)], axis=-1)
    cos = jnp.concatenate([jnp.ones((n_ctx, HEAD_DIM), F32), cos], axis=0)
    sin = jnp.concatenate([jnp.zeros((n_ctx, HEAD_DIM), F32), sin], axis=0)
    reps = SEG_W // HEAD_DIM
    return jnp.tile(cos, (1, reps)), jnp.tile(sin, (1, reps))


def _extend_in_proj(w_in):
    base = 4 * RET_WIDTH + ATT_WIDTH
    ak = w_in[:, base:base + ATT_KV_WIDTH].reshape(-1, ATT_KV_HEADS, 1, HEAD_DIM)
    av = w_in[:, base + ATT_KV_WIDTH:].reshape(-1, ATT_KV_HEADS, 1, HEAD_DIM)
    rep = lambda w, n: jnp.broadcast_to(w, (w.shape[0], ATT_KV_HEADS, n, HEAD_DIM)).reshape(w.shape[0], -1)
    return jnp.concatenate([w_in[:, :base], rep(ak, ATT_GROUP), rep(av, 2)], axis=-1)


def kernel(x, c, ctx, c_ctx, mod_w, mod_b, norm_g, mix_in_w, mix_out_w, ret_log_rate, ret_gn_g, qk_norm_g,
           s5_a_re, s5_a_im, s5_log_dt, s5_b_re, s5_b_im, s5_c_re, s5_c_im, s5_d, s5_glu_w, s5_glu_b,
           moe_router_w, moe_w1, moe_w3, moe_w2, final_norm_g):
    batch, n_lat, d = x.shape
    n_ctx = ctx.shape[1]
    t = n_ctx + n_lat
    depth = mod_w.shape[0]
    assert batch == 8 and d % 128 == 0 and n_ctx % CHUNK == 0 and n_lat % CHUNK == 0
    tm = 256 if n_ctx % 256 == 0 else 128
    tb = tm
    tf = 1024

    xs = jnp.concatenate([ctx, x], axis=1)
    rows = 16
    cond = jnp.zeros((rows, d), F32).at[:batch].set(c).at[batch].set(c_ctx)
    mod_all = _modulation(cond, mod_w, mod_b)
    cos, sin = _rope_tables(n_ctx, n_lat)
    lane_head = jnp.arange(SEG_W) // HEAD_DIM
    bd = (lane_head[:, None] == lane_head[None, :]).astype(BF16)
    w1, w3, w2 = moe_w1, moe_w3, moe_w2

    for layer in range(depth):
        m = mod_all[layer]
        lat = m[:batch].reshape(batch, 1, 6, d)
        cx = jnp.broadcast_to(m[batch].reshape(1, 1, 6, d), (batch, 1, 6, d))
        modtab = jnp.concatenate([cx, lat], axis=1)
        ng1 = norm_g[layer, 0].reshape(1, d)
        ng2 = norm_g[layer, 1].reshape(1, d)
        wr = jnp.zeros((d, ROUTER_LANES), F32).at[:, :N_EXPERTS].set(moe_router_w[layer])
        wrh = wr.astype(BF16)
        wrl = (wr - wrh.astype(F32)).astype(BF16)
        i = layer // 2
        if layer % 2 == 0:
            w_ext = _extend_in_proj(mix_in_w[i]).astype(BF16)
            reps = SEG_W // HEAD_DIM
            qkg = jnp.stack([jnp.tile(qk_norm_g[i, 0], reps), jnp.tile(qk_norm_g[i, 1], reps)])
            rq, rk, rv, rg, aq, ak4, av2 = _in_proj(xs, modtab, ng1, w_ext, cos, sin, qkg, bd, tm, n_ctx)
            rate = jnp.repeat(ret_log_rate[i], HEAD_DIM, axis=-1)
            rate_h = jnp.broadcast_to(ret_log_rate[i][:, :, None], (2, RET_HEADS, CHUNK))
            o_f, o_b = _retention(rq, rk, rv, rate, rate_h, bd, n_ctx)
            att = _attention(aq, ak4, av2, tm, tm, n_ctx)
            xs, f, aff = _out_proj(xs, o_f, o_b, rg, att, modtab, ret_gn_g[i].reshape(1, SEG_W),
                                   mix_out_w[i].astype(BF16), ng2, wrh, wrl, bd, tm, n_ctx)
        else:
            u = _s5_pre(xs, modtab, ng1, tm, n_ctx)
            tabs = _s5_tables(s5_a_re[i], s5_a_im[i], s5_log_dt[i], s5_b_re[i], s5_b_im[i],
                              s5_c_re[i], s5_c_im[i], s5_d[i])
            z = _s5_scan(u, *tabs, n_ctx)
            xs, f, aff = _glu(xs, z, modtab, s5_glu_w[i].astype(BF16), s5_glu_b[i].reshape(1, 2 * d),
                              ng2, wrh, wrl, tm, n_ctx)
        xs = _route_and_moe(layer, xs, aff, f, modtab, w1, w3, w2, n_ctx, tb, tf)
    return _final(xs, final_norm_g.reshape(1, d), tm, n_ctx)
```

```python
import functools
import math

import jax
import jax.numpy as jnp
from jax import lax
from jax.experimental import pallas as pl
from jax.experimental.pallas import tpu as pltpu

F32 = jnp.float32
BF16 = jnp.bfloat16

GRID_W = 64
HEAD_DIM = 64
RET_HEADS = 8
ATT_HEADS = 8
ATT_KV_HEADS = 2
ATT_GROUP = ATT_HEADS // ATT_KV_HEADS
RET_WIDTH = RET_HEADS * HEAD_DIM
ATT_WIDTH = ATT_HEADS * HEAD_DIM
ATT_KV_WIDTH = ATT_KV_HEADS * HEAD_DIM
CHUNK = 128
ROPE_THETA = 10000.0
ROPE_AXIS_DIM = HEAD_DIM // 2
S5_GROUP = 16
S5_STATE = 64
S5_SUB = 8
S5_BLOCK_GROUPS = 8
S5_SCAN_UNROLL = 4
N_EXPERTS = 16
CAPACITY_FACTOR = 2
EPS = 1e-6
SEG_W = 512
LOG2_E = math.log2(math.e)
ROUTER_LANES = 128
MOE_WINDOW = 64
MOE_FFN_SAMPLES = 2

V7X_VMEM_BYTES = 64 * 1024 * 1024
VMEM_LIMIT = V7X_VMEM_BYTES - 8 * 1024 * 1024


def _cparams(sem):
    return pltpu.CompilerParams(dimension_semantics=sem, vmem_limit_bytes=VMEM_LIMIT)


def _dot(a, b):
    return jnp.dot(a, b, preferred_element_type=F32)


def _dot_nt(a, b):
    return lax.dot_general(a, b, (((1,), (1,)), ((), ())), preferred_element_type=F32)


def _dot_tn(a, b):
    return lax.dot_general(a, b, (((0,), (0,)), ((), ())), preferred_element_type=F32)


def _split(v):
    hi = v.astype(BF16)
    lo = (v - hi.astype(F32)).astype(BF16)
    return hi, lo


def _segsum(v, bd):
    hi, lo = _split(v)
    return _dot(hi, bd) + _dot(lo, bd)


def _sigmoid(v):
    return 1.0 / (1.0 + jnp.exp(-v))


def _silu(v):
    return v * _sigmoid(v)


def _gelu_tanh(v):
    c = math.sqrt(2.0 / math.pi)
    return 0.5 * v * (1.0 + jnp.tanh(c * (v + 0.044715 * (v * v * v))))


def _norm_mod(x, g, scale, shift):
    ms = jnp.mean(x * x, axis=-1, keepdims=True)
    return x * lax.rsqrt(ms + EPS) * g * (1.0 + scale) + shift


def _rope(x, cos, sin_signed):
    w = x.shape[-1]
    lane = lax.broadcasted_iota(jnp.int32, x.shape, 1)
    first = (lane % 32) < 16
    partner = jnp.where(first, pltpu.roll(x, w - 16, 1), pltpu.roll(x, 16, 1))
    return x * cos + partner * sin_signed


def _mod_kernel(c_ref, w_ref, b_ref, o_ref):
    s = _silu(c_ref[...]).astype(BF16)
    o_ref[0] = _dot(s, w_ref[0].astype(BF16)) + b_ref[0]


def _modulation(cond, mod_w, mod_b):
    depth, d, n6 = mod_w.shape
    rows = cond.shape[0]
    tn = n6 // 6
    return pl.pallas_call(
        _mod_kernel,
        grid=(depth, n6 // tn),
        in_specs=[
            pl.BlockSpec((rows, d), lambda l, j: (0, 0)),
            pl.BlockSpec((1, d, tn), lambda l, j: (l, 0, j)),
            pl.BlockSpec((1, 1, tn), lambda l, j: (l, 0, j)),
        ],
        out_specs=pl.BlockSpec((1, rows, tn), lambda l, j: (l, 0, j)),
        out_shape=jax.ShapeDtypeStruct((depth, rows, n6), F32),
        compiler_params=_cparams(("parallel", "parallel")),
    )(cond, mod_w, mod_b.reshape(depth, 1, n6))


def _in_kernel(x_ref, mod_ref, ng_ref, w_ref, cos_ref, sin_ref, qkg_ref, bd_ref,
               rq_ref, rk_ref, rv_ref, rg_ref, aq_ref, ak_ref, av_ref):
    x = x_ref[0]
    m = mod_ref[0, 0]
    h = _norm_mod(x, ng_ref[...], m[1:2], m[0:1]).astype(BF16)
    cos = cos_ref[...]
    sin = sin_ref[...]
    bd = bd_ref[...]

    def seg(i):
        return _dot(h, w_ref[:, i * SEG_W:(i + 1) * SEG_W])

    def headnorm(v, g):
        ms = _segsum(v * v, bd) * (1.0 / HEAD_DIM)
        return v * lax.rsqrt(ms + EPS) * g

    scale = HEAD_DIM ** -0.5
    rq_ref[0] = _rope(seg(0), cos, sin).astype(BF16)
    rk_ref[0] = _rope(seg(1) * scale, cos, sin).astype(BF16)
    rv_ref[0] = seg(2).astype(BF16)
    rg_ref[0] = seg(3)
    aq_ref[0] = (_rope(headnorm(seg(4), qkg_ref[0:1]), cos, sin) * (scale * LOG2_E)).astype(BF16)
    ak_ref[0] = _rope(headnorm(seg(5), qkg_ref[1:2]), cos, sin).astype(BF16)
    av_ref[0] = _dot(h, w_ref[:, 6 * SEG_W:]).astype(BF16)


def _row_spec(tm, d, lead=0):
    return pl.BlockSpec((1, tm, d), lambda b, i: (b, i + lead, 0))


def _mod_spec(d, n, seg_tiles):
    return pl.BlockSpec((1, 1, n, d), lambda b, i: (b, jnp.where(i >= seg_tiles, 1, 0), 0, 0))


def _const_spec(shape):
    nd = len(shape)
    return pl.BlockSpec(shape, lambda b, i: (0,) * nd)


def _in_proj(x, modtab, norm_g, w_ext, cos, sin, qkg, bd, tm, n_ctx):
    b, t, d = x.shape
    out_shape, out_specs = [], []
    for dt, width in ((BF16, SEG_W), (BF16, SEG_W), (BF16, SEG_W), (F32, SEG_W), (BF16, SEG_W), (BF16, SEG_W),
                      (BF16, ATT_KV_HEADS * 2 * HEAD_DIM)):
        out_shape.append(jax.ShapeDtypeStruct((b, t, width), dt))
        out_specs.append(_row_spec(tm, width))
    return pl.pallas_call(
        _in_kernel,
        grid=(b, t // tm),
        in_specs=[
            _row_spec(tm, d),
            _mod_spec(d, 6, n_ctx // tm),
            _const_spec((1, d)),
            _const_spec(w_ext.shape),
            pl.BlockSpec((tm, SEG_W), lambda b, i: (i, 0)),
            pl.BlockSpec((tm, SEG_W), lambda b, i: (i, 0)),
            _const_spec((2, SEG_W)),
            _const_spec((SEG_W, SEG_W)),
        ],
        out_specs=out_specs,
        out_shape=out_shape,
        compiler_params=_cparams(("parallel", "parallel")),
    )(x, modtab, norm_g, w_ext, cos, sin, qkg, bd)


def _ret_kernel(qf_ref, kf_ref, vf_ref, qb_ref, kb_ref, vb_ref, rate_ref, rate_h_ref, bd_ref,
                of_ref, ob_ref, sf_s, sb_s, dm_s, dq_s, dk_s, dc_s):
    s = pl.program_id(1)
    c = CHUNK

    @pl.when(s == 0)
    def _():
        sf_s[...] = jnp.zeros_like(sf_s)
        sb_s[...] = jnp.zeros_like(sb_s)
        pos = lax.broadcasted_iota(jnp.int32, (c, SEG_W), 0).astype(F32)
        lgf = -jnp.exp(rate_ref[0:1, :])
        lgb = -jnp.exp(rate_ref[1:2, :])
        dq_s[0] = jnp.exp((pos + 1.0) * lgf)
        dk_s[0] = jnp.exp((c - 1.0 - pos) * lgf)
        dc_s[0] = jnp.exp(float(c) * lgf)
        dq_s[1] = jnp.exp((c - pos) * lgb)
        dk_s[1] = jnp.exp(pos * lgb)
        dc_s[1] = jnp.exp(float(c) * lgb)
        ii = lax.broadcasted_iota(jnp.int32, (c, c), 0)
        jj = lax.broadcasted_iota(jnp.int32, (c, c), 1)
        for h in range(RET_HEADS):
            lf = -jnp.exp(rate_h_ref[0, h:h + 1, :])
            lb = -jnp.exp(rate_h_ref[1, h:h + 1, :])
            mf = ii >= jj
            mb = jj > ii
            dm_s[0, h] = jnp.where(mf, jnp.exp(jnp.where(mf, ii - jj, 0).astype(F32) * lf), 0.0)
            dm_s[1, h] = jnp.where(mb, jnp.exp(jnp.where(mb, jj - ii, 0).astype(F32) * lb), 0.0)

    bdm = bd_ref[...].astype(F32)
    head_of_lane = lax.broadcasted_iota(jnp.int32, (1, SEG_W // 2), 1) // HEAD_DIM

    def direction(d, q_ref, k_ref, v_ref, o_ref, st):
        q = q_ref[0]
        k = k_ref[0]
        v = v_ref[0]
        state = st[...]
        qd = (q.astype(F32) * dq_s[d]).astype(BF16)
        inter = _dot(qd, state.astype(BF16))
        for half in range(2):
            sl = slice(half * 256, half * 256 + 256)
            qh, kh, vh = q[:, sl], k[:, sl], v[:, sl]
            acc = inter[:, sl]
            for hh in range(4):
                msk = head_of_lane == hh
                sc = _dot_nt(jnp.where(msk, qh, jnp.zeros_like(qh)), kh) * dm_s[d, half * 4 + hh]
                acc = acc + _dot(sc.astype(BF16), jnp.where(msk, vh, jnp.zeros_like(vh)))
            o_ref[0, :, sl] = acc
        kd = (k.astype(F32) * dk_s[d]).astype(BF16)
        st[...] = state * dc_s[d] + bdm * _dot_tn(kd, v)

    direction(0, qf_ref, kf_ref, vf_ref, of_ref, sf_s)
    direction(1, qb_ref, kb_ref, vb_ref, ob_ref, sb_s)


def _retention(rq, rk, rv, rate, rate_h, bd, n_ctx):
    b, t, w = rq.shape
    nc = t // CHUNK
    lc = n_ctx // CHUNK

    def fwd(bi, s):
        return (bi, s, 0)

    def bwd(bi, s):
        return (bi, jnp.where(s < lc, lc - 1 - s, nc - 1 - (s - lc)), 0)

    blk = (1, CHUNK, w)
    return pl.pallas_call(
        _ret_kernel,
        grid=(b, nc),
        in_specs=[pl.BlockSpec(blk, fwd)] * 3 + [pl.BlockSpec(blk, bwd)] * 3 + [
            _const_spec((2, w)),
            _const_spec((2, RET_HEADS, CHUNK)),
            _const_spec((w, w)),
        ],
        out_specs=[pl.BlockSpec(blk, fwd), pl.BlockSpec(blk, bwd)],
        out_shape=[jax.ShapeDtypeStruct((b, t, w), F32)] * 2,
        scratch_shapes=[
            pltpu.VMEM((w, w), F32), pltpu.VMEM((w, w), F32),
            pltpu.VMEM((2, RET_HEADS, CHUNK, CHUNK), F32),
            pltpu.VMEM((2, CHUNK, w), F32), pltpu.VMEM((2, CHUNK, w), F32),
            pltpu.VMEM((2, 1, w), F32),
        ],
        compiler_params=_cparams(("parallel", "arbitrary")),
    )(rq, rk, rv, rq, rk, rv, rate, rate_h, bd)


ATT_ROW_BLOCK = 512


def _att_kernel(q_ref, k_ref, v_ref, o_ref, qs_s, m_s, acc_s, sa_s, sb_s, ma_s, mb_s, *, tq, tk, n_ctx, t):
    qi = pl.program_id(1)
    gw = ATT_GROUP * HEAD_DIM
    head_of_lane = lax.broadcasted_iota(jnp.int32, (1, gw), 1) // HEAD_DIM
    for h in range(ATT_HEADS):
        q = q_ref[0, :, (h // ATT_GROUP) * gw:(h // ATT_GROUP + 1) * gw]
        qs_s[h * tq:(h + 1) * tq, :] = jnp.where(head_of_lane == h % ATT_GROUP, q, jnp.zeros_like(q))
    m_s[...] = jnp.full(m_s.shape, -1e30, F32)
    acc_s[...] = jnp.zeros_like(acc_s)
    nk = jnp.where(qi * tq < n_ctx, n_ctx // tk, t // tk)
    vw = 2 * HEAD_DIM
    ones_lane = lax.broadcasted_iota(jnp.int32, (1, ATT_KV_HEADS * vw), 1) % vw >= HEAD_DIM
    rb = ATT_ROW_BLOCK

    blocks = [(slice(r * rb, (r + 1) * rb), (r * rb) // (ATT_GROUP * tq)) for r in range(ATT_HEADS * tq // rb)]

    def scores_into(j, dst, dst_max):
        kb = k_ref[0, pl.ds(pl.multiple_of(j * tk, tk), tk), :]
        for rows, g in blocks:
            sc = _dot_nt(qs_s[rows, :], kb[:, g * gw:(g + 1) * gw])
            dst[rows, :] = sc
            dst_max[rows, :] = jnp.broadcast_to(jnp.max(sc, axis=1, keepdims=True), (rb, vw))

    def consume(j, src, src_max):
        vb = v_ref[0, pl.ds(pl.multiple_of(j * tk, tk), tk), :]
        vext = jnp.where(ones_lane, jnp.ones_like(vb), vb)
        for rows, g in blocks:
            m_prev = m_s[rows, :]
            m_new = jnp.maximum(m_prev, src_max[rows, :])
            p = jnp.exp2(src[rows, :] - jnp.concatenate([m_new] * (tk // vw), axis=1)).astype(BF16)
            acc_s[rows, :] = jnp.exp2(m_prev - m_new) * acc_s[rows, :] + _dot(p, vext[:, g * vw:(g + 1) * vw])
            m_s[rows, :] = m_new

    scores_into(0, sa_s, ma_s)

    def body(i, carry):
        j = 2 * i
        scores_into(j + 1, sb_s, mb_s)
        consume(j, sa_s, ma_s)
        scores_into(jnp.minimum(j + 2, nk - 1), sa_s, ma_s)
        consume(j + 1, sb_s, mb_s)
        return carry

    lax.fori_loop(0, nk // 2, body, 0)

    @pl.when(nk % 2 == 1)
    def _():
        consume(nk - 1, sa_s, ma_s)

    low = lax.broadcasted_iota(jnp.int32, (tq, vw), 1) < HEAD_DIM
    for pair in range(ATT_HEADS // 2):
        a0 = acc_s[(2 * pair) * tq:(2 * pair + 1) * tq, :]
        a1 = acc_s[(2 * pair + 1) * tq:(2 * pair + 2) * tq, :]
        even = a0 / pltpu.roll(a0, HEAD_DIM, 1)
        odd = pltpu.roll(a1, HEAD_DIM, 1) / a1
        o_ref[0, :, pair * vw:(pair + 1) * vw] = jnp.where(low, even, odd).astype(BF16)


def _attention(aq, ak4, av2, tq, tk, n_ctx):
    b, t, w = aq.shape
    gw = ATT_GROUP * HEAD_DIM
    vw = 2 * HEAD_DIM
    rows = ATT_HEADS * tq
    return pl.pallas_call(
        functools.partial(_att_kernel, tq=tq, tk=tk, n_ctx=n_ctx, t=t),
        grid=(b, t // tq),
        in_specs=[
            pl.BlockSpec((1, tq, w), lambda bi, i: (bi, i, 0)),
            pl.BlockSpec((1, t, w), lambda bi, i: (bi, 0, 0)),
            pl.BlockSpec((1, t, ATT_KV_HEADS * vw), lambda bi, i: (bi, 0, 0)),
        ],
        out_specs=pl.BlockSpec((1, tq, w), lambda bi, i: (bi, i, 0)),
        out_shape=jax.ShapeDtypeStruct((b, t, w), BF16),
        scratch_shapes=[pltpu.VMEM((rows, gw), BF16), pltpu.VMEM((rows, vw), F32), pltpu.VMEM((rows, vw), F32),
                        pltpu.VMEM((rows, tk), F32), pltpu.VMEM((rows, tk), F32), pltpu.VMEM((rows, vw), F32),
                        pltpu.VMEM((rows, vw), F32)],
        compiler_params=_cparams(("parallel", "parallel")),
    )(aq, ak4, av2)


def _post_mixer(x, delta, m, ng2, wrh, wrl, x_ref, f_ref, aff_ref):
    xn = x + m[2:3] * delta
    x_ref[0] = xn
    f = _norm_mod(xn, ng2, m[4:5], m[3:4])
    fh, fl = _split(f)
    f_ref[0] = fh
    logits = _dot(fh, wrh) + _dot(fl, wrh) + _dot(fh, wrl)
    lane = lax.broadcasted_iota(jnp.int32, logits.shape, 1)
    logits = jnp.where(lane < N_EXPERTS, logits, -1e30)
    e = jnp.exp(logits - jnp.max(logits, axis=1, keepdims=True))
    aff_ref[0] = e / jnp.sum(e, axis=1, keepdims=True)


def _out_kernel(x_ref, of_ref, ob_ref, rg_ref, att_ref, mod_ref, gn_ref, w_ref, ng2_ref, wrh_ref, wrl_ref,
                bd_ref, xo_ref, f_ref, aff_ref):
    bd = bd_ref[...]
    o = of_ref[0] + ob_ref[0]
    mu = _segsum(o, bd) * (1.0 / HEAD_DIM)
    oc = o - mu
    var = _segsum(oc * oc, bd) * (1.0 / HEAD_DIM)
    r = _silu(rg_ref[0]) * (oc * lax.rsqrt(var + EPS) * gn_ref[...])
    y = _dot(r.astype(BF16), w_ref[0:RET_WIDTH, :]) + _dot(att_ref[0], w_ref[RET_WIDTH:, :])
    _post_mixer(x_ref[0], y, mod_ref[0, 0], ng2_ref[...], wrh_ref[...], wrl_ref[...], xo_ref, f_ref, aff_ref)


def _post_specs(b, t, d, tm):
    out_shape = [jax.ShapeDtypeStruct((b, t, d), F32), jax.ShapeDtypeStruct((b, t, d), BF16),
                 jax.ShapeDtypeStruct((b, t, ROUTER_LANES), F32)]
    out_specs = [_row_spec(tm, d), _row_spec(tm, d), _row_spec(tm, ROUTER_LANES)]
    return out_shape, out_specs


def _out_proj(x, of, ob, rg, att, modtab, gn_g, w_out, ng2, wrh, wrl, bd, tm, n_ctx):
    b, t, d = x.shape
    out_shape, out_specs = _post_specs(b, t, d, tm)
    return pl.pallas_call(
        _out_kernel,
        grid=(b, t // tm),
        in_specs=[
            _row_spec(tm, d), _row_spec(tm, SEG_W), _row_spec(tm, SEG_W), _row_spec(tm, SEG_W),
            _row_spec(tm, SEG_W), _mod_spec(d, 6, n_ctx // tm), _const_spec((1, SEG_W)),
            _const_spec(w_out.shape), _const_spec((1, d)), _const_spec(wrh.shape), _const_spec(wrl.shape),
            _const_spec((SEG_W, SEG_W)),
        ],
        out_specs=out_specs,
        out_shape=out_shape,
        compiler_params=_cparams(("parallel", "parallel")),
    )(x, of, ob, rg, att, modtab, gn_g, w_out, ng2, wrh, wrl, bd)


def _s5_pre_kernel(x_ref, mod_ref, ng_ref, u_ref):
    m = mod_ref[0, 0]
    u_ref[0] = _norm_mod(x_ref[0], ng_ref[...], m[1:2], m[0:1])


def _s5_pre(x, modtab, norm_g, tm, n_ctx):
    b, t, d = x.shape
    return pl.pallas_call(
        _s5_pre_kernel,
        grid=(b, t // tm),
        in_specs=[_row_spec(tm, d), _mod_spec(d, 6, n_ctx // tm), _const_spec((1, d))],
        out_specs=_row_spec(tm, d),
        out_shape=jax.ShapeDtypeStruct((b, t, d), F32),
        compiler_params=_cparams(("parallel", "parallel")),
    )(x, modtab, norm_g)


def _s5_kernel(u_ref, tm_ref, wz_ref, wo_ref, ar_ref, ai_ref, d_ref, z_ref,
               zr_s, zi_s, xr_s, xi_s, zs_s, *, n_sub, n_sub_ctx):
    s_n, j_sub, gb, half = n_sub, S5_SUB, S5_BLOCK_GROUPS, 2 * S5_STATE
    uf = jnp.concatenate([u_ref[0, pl.ds(m, s_n, stride=j_sub), :] for m in range(j_sub)], axis=1)
    lhs = uf.astype(BF16)
    for g in range(gb):
        zz = _dot(lhs, wz_ref[0, g])
        zr_s[pl.ds(g, s_n, stride=gb), :] = zz[:, :half]
        zi_s[pl.ds(g, s_n, stride=gb), :] = zz[:, half:]
    fwd_lane = lax.broadcasted_iota(jnp.int32, (gb, half), 1) < S5_STATE
    ar = ar_ref[0]
    ai = ai_ref[0]

    def body(i, carry):
        xr, xi = carry
        sb = jnp.where(i < n_sub_ctx, n_sub_ctx - 1 - i, s_n - 1 - (i - n_sub_ctx))
        fwd = pl.ds(pl.multiple_of(i * gb, gb), gb)
        bwd = pl.ds(pl.multiple_of(sb * gb, gb), gb)
        xr_s[fwd, 0:S5_STATE] = xr[:, 0:S5_STATE]
        xi_s[fwd, 0:S5_STATE] = xi[:, 0:S5_STATE]
        xr_s[bwd, S5_STATE:half] = xr[:, S5_STATE:half]
        xi_s[bwd, S5_STATE:half] = xi[:, S5_STATE:half]
        zr = jnp.where(fwd_lane, zr_s[fwd, :], zr_s[bwd, :])
        zi = jnp.where(fwd_lane, zi_s[fwd, :], zi_s[bwd, :])
        return ar * xr - ai * xi + zr, ar * xi + ai * xr + zi

    zero = jnp.zeros((gb, half), F32)
    lax.fori_loop(0, s_n, body, (zero, zero), unroll=S5_SCAN_UNROLL)
    y = _dot(lhs, tm_ref[0]) + d_ref[0] * uf
    for g in range(gb):
        rows = pl.ds(g, s_n, stride=gb)
        xp = jnp.concatenate([xr_s[rows, :], xi_s[rows, :]], axis=1).astype(BF16)
        y = y + _dot(xp, wo_ref[0, g])
    z = _gelu_tanh(y)
    w = z_ref.shape[-1]
    for n in range(j_sub):
        zs_s[pl.ds(n, s_n, stride=j_sub), :] = z[:, n * w:(n + 1) * w]
    z_ref[0] = zs_s[...].astype(BF16)


def _s5_scan(u, tmat, wz, wo, ar, ai, dt, n_ctx):
    b, t, d = u.shape
    cw = S5_BLOCK_GROUPS * S5_GROUP
    n_sub = t // S5_SUB
    lw = S5_SUB * cw
    sw = 4 * S5_STATE
    gb = S5_BLOCK_GROUPS
    once = pl.Buffered(1)
    return pl.pallas_call(
        functools.partial(_s5_kernel, n_sub=n_sub, n_sub_ctx=n_ctx // S5_SUB),
        grid=(d // cw, b),
        in_specs=[
            pl.BlockSpec((1, t, cw), lambda c, bi: (bi, 0, c)),
            pl.BlockSpec((1, lw, lw), lambda c, bi: (c, 0, 0), pipeline_mode=once),
            pl.BlockSpec((1, gb, lw, sw), lambda c, bi: (c, 0, 0, 0), pipeline_mode=once),
            pl.BlockSpec((1, gb, sw, lw), lambda c, bi: (c, 0, 0, 0), pipeline_mode=once),
            pl.BlockSpec((1, gb, 2 * S5_STATE), lambda c, bi: (c, 0, 0)),
            pl.BlockSpec((1, gb, 2 * S5_STATE), lambda c, bi: (c, 0, 0)),
            pl.BlockSpec((1, 1, lw), lambda c, bi: (c, 0, 0)),
        ],
        out_specs=pl.BlockSpec((1, t, cw), lambda c, bi: (bi, 0, c)),
        out_shape=jax.ShapeDtypeStruct((b, t, d), BF16),
        scratch_shapes=[pltpu.VMEM((gb * n_sub, 2 * S5_STATE), F32)] * 4 + [pltpu.VMEM((t, cw), F32)],
        compiler_params=_cparams(("parallel", "parallel")),
    )(u, tmat, wz, wo, ar, ai, dt)


def _s5_tables(a_re, a_im, log_dt, b_re, b_im, c_re, c_im, d_skip):
    hp = lax.Precision.HIGHEST
    j_sub = S5_SUB
    gb = S5_BLOCK_GROUPS
    lam_re = jnp.minimum(a_re, -1e-4)
    lam_im = a_im
    dt = jnp.exp(log_dt)[..., None]
    mag = jnp.exp(lam_re * dt)
    bar_re = mag * jnp.cos(lam_im * dt)
    bar_im = mag * jnp.sin(lam_im * dt)
    den = lam_re * lam_re + lam_im * lam_im
    nr, ni = bar_re - 1.0, bar_im
    k_re = (nr * lam_re + ni * lam_im) / den
    k_im = (ni * lam_re - nr * lam_im) / den
    bb_re = k_re[..., None] * b_re[None] - k_im[..., None] * b_im[None]
    bb_im = k_re[..., None] * b_im[None] + k_im[..., None] * b_re[None]
    jj = jnp.arange(j_sub + 1, dtype=F32)[:, None, None, None]
    pmag = jnp.exp(jj * (lam_re * dt)[None])
    pw_re = pmag * jnp.cos(jj * (lam_im * dt)[None])
    pw_im = pmag * jnp.sin(jj * (lam_im * dt)[None])
    pb_re = pw_re[..., None] * bb_re[None] - pw_im[..., None] * bb_im[None]
    pb_im = pw_re[..., None] * bb_im[None] + pw_im[..., None] * bb_re[None]
    cp_re = c_re[None] * pw_re[:, :, :, None, :] - c_im[None] * pw_im[:, :, :, None, :]
    cp_im = c_re[None] * pw_im[:, :, :, None, :] + c_im[None] * pw_re[:, :, :, None, :]
    kern = (jnp.einsum('dgcp,jdgpk->jdgkc', c_re, pb_re[:j_sub], precision=hp)
            - jnp.einsum('dgcp,jdgpk->jdgkc', c_im, pb_im[:j_sub], precision=hp))
    groups = a_re.shape[1]
    nblk = groups // gb
    k_w = S5_GROUP
    p_w = S5_STATE
    lw = j_sub * gb * k_w
    eye = jnp.eye(gb, dtype=F32)
    place = (jnp.arange(gb * k_w)[None, None, :] == (jnp.arange(gb)[:, None, None] * k_w
                                                      + jnp.arange(k_w)[None, :, None])).astype(F32)
    blk = lambda a: a.reshape(a.shape[0], nblk, gb, *a.shape[2:])
    pos = jnp.arange(j_sub)
    lag = pos[None, :] - pos[:, None]
    kf = jnp.where((lag >= 0)[:, :, None, None, None], kern[jnp.clip(lag, 0, j_sub - 1), 0], 0.0)
    kb = jnp.where((lag <= 0)[:, :, None, None, None], kern[jnp.clip(-lag, 0, j_sub - 1), 1], 0.0)
    kfull = (kf + kb).reshape(j_sub, j_sub, nblk, gb, k_w, k_w)
    tmat = jnp.einsum('mnBgkc,gcd->Bmgknd', kfull, place, precision=hp).reshape(nblk, lw, lw)
    rev = pos[::-1]
    zc = jnp.concatenate([pb_re[rev, 0], pb_re[pos, 1], pb_im[rev, 0], pb_im[pos, 1]], axis=2)
    wz = jnp.einsum('mBgqk,gh->Bgmhkq', blk(zc), eye).reshape(nblk, gb, lw, 4 * p_w)
    def orows(cp, d, order):
        return jnp.einsum('nBgcp,gcd->Bgpnd', blk(cp[order, d]), place, precision=hp).reshape(nblk, gb, p_w, lw)
    wo = jnp.concatenate([orows(cp_re, 0, pos + 1), orows(cp_re, 1, j_sub - pos),
                          -orows(cp_im, 0, pos + 1), -orows(cp_im, 1, j_sub - pos)], axis=2)
    ar = jnp.concatenate([pw_re[j_sub, 0], pw_re[j_sub, 1]], axis=-1).reshape(nblk, gb, 2 * p_w)
    ai = jnp.concatenate([pw_im[j_sub, 0], pw_im[j_sub, 1]], axis=-1).reshape(nblk, gb, 2 * p_w)
    dt_tile = jnp.tile(d_skip.reshape(nblk, 1, gb * k_w), (1, 1, j_sub))
    return tmat.astype(BF16), wz.astype(BF16), wo.astype(BF16), ar, ai, dt_tile


def _glu_kernel(x_ref, z_ref, mod_ref, w_ref, b_ref, ng2_ref, wrh_ref, wrl_ref, xo_ref, f_ref, aff_ref):
    d = x_ref.shape[-1]
    ag = _dot(z_ref[0], w_ref[...]) + b_ref[...]
    delta = ag[:, :d] * _sigmoid(ag[:, d:])
    _post_mixer(x_ref[0], delta, mod_ref[0, 0], ng2_ref[...], wrh_ref[...], wrl_ref[...], xo_ref, f_ref, aff_ref)


def _glu(x, z, modtab, glu_w, glu_b, ng2, wrh, wrl, tm, n_ctx):
    b, t, d = x.shape
    out_shape, out_specs = _post_specs(b, t, d, tm)
    return pl.pallas_call(
        _glu_kernel,
        grid=(b, t // tm),
        in_specs=[_row_spec(tm, d), _row_spec(tm, d), _mod_spec(d, 6, n_ctx // tm),
                  _const_spec(glu_w.shape), _const_spec((1, 2 * d)), _const_spec((1, d)),
                  _const_spec(wrh.shape), _const_spec(wrl.shape)],
        out_specs=out_specs,
        out_shape=out_shape,
        compiler_params=_cparams(("parallel", "parallel")),
    )(x, z, modtab, glu_w, glu_b, ng2, wrh, wrl)


def _cumsum_lanes(v):
    n = v.shape[-1]
    lane = lax.broadcasted_iota(jnp.int32, v.shape, 1)
    sh = 1
    while sh < n:
        v = v + jnp.where(lane >= sh, pltpu.roll(v, sh, 1), 0.0)
        sh *= 2
    return v


def _select_segment(aff, cap, base):
    bits = pltpu.bitcast(aff, jnp.int32)
    thr = jnp.zeros((aff.shape[0], 1), jnp.int32)
    for bit in range(30, -1, -1):
        cand = thr | (1 << bit)
        cnt = jnp.sum(jnp.where(bits >= cand, 1.0, 0.0), axis=1, keepdims=True)
        thr = jnp.where(cnt >= cap, cand, thr)
    gt = bits > thr
    eq = bits == thr
    need = cap - jnp.sum(jnp.where(gt, 1.0, 0.0), axis=1, keepdims=True)
    eqf = jnp.where(eq, 1.0, 0.0)
    eq_before = _cumsum_lanes(eqf) - eqf
    sel = jnp.where(gt, 1.0, jnp.where(eq, jnp.where(eq_before < need, 1.0, 0.0), 0.0))
    pos = _cumsum_lanes(sel) - sel
    return jnp.where(sel > 0.0, pos.astype(jnp.int32) + base, -1)


def _select_kernel(aff_ref, slot_ref, *, n_ctx, cap_ctx, cap_lat):
    slot_ref[0, :, 0:n_ctx] = _select_segment(aff_ref[0, :, 0:n_ctx], cap_ctx, 0)
    slot_ref[0, :, n_ctx:] = _select_segment(aff_ref[0, :, n_ctx:], cap_lat, cap_ctx)


def _select(aff_t, n_ctx, cap_ctx, cap_lat):
    b, e, t = aff_t.shape
    return pl.pallas_call(
        functools.partial(_select_kernel, n_ctx=n_ctx, cap_ctx=cap_ctx, cap_lat=cap_lat),
        grid=(b,),
        in_specs=[pl.BlockSpec((1, e, t), lambda i: (i, 0, 0))],
        out_specs=pl.BlockSpec((1, e, t), lambda i: (i, 0, 0)),
        out_shape=jax.ShapeDtypeStruct((b, e, t), jnp.int32),
        compiler_params=_cparams(("parallel",)),
    )(aff_t)


def _moe_window(a_ref, base, e, k, rows, w):
    nominal = a_ref[base + e] + k * w
    return nominal, jnp.minimum(nominal, rows - w)


def _moe_hits(slot_row, nominal, actual, w):
    rid = lax.broadcasted_iota(jnp.int32, (w, slot_row.shape[-1]), 0) + actual
    return jnp.where(slot_row >= nominal, slot_row, -1) == rid


def _moe_gather_kernel(a_ref, nr_ref, slot_ref, aff_ref, f_ref, xs_ref, gate_ref, *, rows, w):
    b = pl.program_id(0)
    j = pl.program_id(1)
    n_e = slot_ref.shape[2]
    step = b * pl.num_programs(1) + j

    @pl.when(j == 0)
    def _():
        xs_ref[...] = jnp.zeros_like(xs_ref)
        gate_ref[...] = jnp.zeros_like(gate_ref)

    fj = f_ref[0]

    def one_round(k, carry):
        for e in range(n_e):
            nominal, actual = _moe_window(a_ref, step * n_e, e, k, rows, w)
            hit = _moe_hits(slot_ref[0, 0, e:e + 1, :], nominal, actual, w)
            off = pl.multiple_of(e * rows + actual, 16)
            xs_ref[0, pl.ds(off, w), :] += _dot(jnp.where(hit, 1.0, 0.0).astype(BF16), fj).astype(BF16)
            gate_ref[0, pl.ds(off, w), :] += jnp.sum(jnp.where(hit, aff_ref[0, 0, e:e + 1, :], 0.0),
                                                     axis=1, keepdims=True)
        return carry

    lax.fori_loop(0, nr_ref[step], one_round, 0)


def _moe_ffn_kernel(xs_ref, gate_ref, w1_ref, w3_ref, w2_ref, y_ref, ys_s, *, n_ft):
    ft = pl.program_id(2)
    nb, rows, d = xs_ref.shape
    xs = xs_ref[...].reshape(nb * rows, d)
    hid = (_silu(_dot(xs, w1_ref[0, 0].astype(BF16))) * _dot(xs, w3_ref[0, 0].astype(BF16))).astype(BF16)
    part = _dot(hid, w2_ref[0, 0].astype(BF16))

    @pl.when(ft == 0)
    def _():
        ys_s[...] = part

    @pl.when(ft > 0)
    def _():
        ys_s[...] += part

    @pl.when(ft == n_ft - 1)
    def _():
        y_ref[...] = (ys_s[...] * gate_ref[...].reshape(nb * rows, 1)).astype(BF16).reshape(nb, rows, d)


def _moe_combine_kernel(a_ref, nr_ref, slot_ref, y_ref, x_ref, mod_ref, *rest, rows, w, nb, lead):
    o_ref = rest[-1]
    b = pl.program_id(0)
    n_e = slot_ref.shape[2]
    step = b * nb + pl.program_id(1) + lead

    def one_round(k, acc):
        onehots, ys = [], []
        for e in range(n_e):
            nominal, actual = _moe_window(a_ref, step * n_e, e, k, rows, w)
            hit = _moe_hits(slot_ref[0, 0, e:e + 1, :], nominal, actual, w)
            onehots.append(jnp.where(hit, 1.0, 0.0).astype(BF16))
            ys.append(y_ref[0, pl.ds(pl.multiple_of(e * rows + actual, 16), w), :])
        return acc + _dot_tn(jnp.concatenate(onehots, axis=0), jnp.concatenate(ys, axis=0))

    acc = lax.fori_loop(0, nr_ref[step], one_round, jnp.zeros(o_ref.shape[1:], F32))
    xn = x_ref[0] + mod_ref[0, 0, 5:6, :] * acc
    if len(rest) == 2:
        xn = xn * lax.rsqrt(jnp.mean(xn * xn, axis=-1, keepdims=True) + EPS) * rest[0][...]
    o_ref[0] = xn


def _moe_windows(slot, tb, w):
    b, e, t = slot.shape
    nb = t // tb
    cnt = jnp.sum((slot >= 0).reshape(b, e, nb, tb), axis=-1, dtype=jnp.int32)
    start = jnp.cumsum(cnt, axis=-1) - cnt
    a0 = (start // 16) * 16
    rounds = jnp.maximum(jnp.max((start + cnt - a0 + w - 1) // w, axis=1), 1)
    return jnp.swapaxes(a0, 1, 2).reshape(-1).astype(jnp.int32), rounds.reshape(-1).astype(jnp.int32)


def _route_and_moe(layer, x, aff, f, modtab, w1, w3, w2, n_ctx, tb, tf, final_g=None):
    b, t, d = f.shape
    n_e = w1.shape[1]
    ff = w1.shape[-1]
    nb = t // tb
    aff_t = jnp.swapaxes(aff[:, :, :N_EXPERTS], 1, 2)
    cap_ctx = (CAPACITY_FACTOR * n_ctx) // N_EXPERTS
    cap_lat = (CAPACITY_FACTOR * (t - n_ctx)) // N_EXPERTS
    rows = cap_ctx + cap_lat
    w = min(MOE_WINDOW, rows)
    slot = _select(aff_t, n_ctx, cap_ctx, cap_lat)
    a0, rounds = _moe_windows(slot, tb, w)
    by_block = lambda v: jnp.swapaxes(v.reshape(b, n_e, nb, tb), 1, 2)
    slot_b, aff_b = by_block(slot), by_block(aff_t)
    blk_spec = pl.BlockSpec((1, 1, n_e, tb), lambda bi, j, *_: (bi, j, 0, 0))
    whole = lambda width: pl.BlockSpec((1, n_e * rows, width), lambda bi, j, *_: (bi, 0, 0),
                                       pipeline_mode=pl.Buffered(1))
    xs, gate = pl.pallas_call(
        functools.partial(_moe_gather_kernel, rows=rows, w=w),
        grid_spec=pltpu.PrefetchScalarGridSpec(
            num_scalar_prefetch=2, grid=(b, nb),
            in_specs=[blk_spec, blk_spec, pl.BlockSpec((1, tb, d), lambda bi, j, *_: (bi, j, 0))],
            out_specs=[whole(d), whole(1)]),
        out_shape=[jax.ShapeDtypeStruct((b, n_e * rows, d), BF16), jax.ShapeDtypeStruct((b, n_e * rows, 1), F32)],
        compiler_params=_cparams(("parallel", "arbitrary")),
    )(a0, rounds, slot_b, aff_b, f)
    n_ft = ff // tf
    sb = MOE_FFN_SAMPLES
    y = pl.pallas_call(
        functools.partial(_moe_ffn_kernel, n_ft=n_ft),
        grid=(b // sb, n_e, n_ft),
        in_specs=[
            pl.BlockSpec((sb, rows, d), lambda bi, e, k: (bi, e, 0)),
            pl.BlockSpec((sb, rows, 1), lambda bi, e, k: (bi, e, 0)),
            pl.BlockSpec((1, 1, d, tf), lambda bi, e, k: (layer, e, 0, k)),
            pl.BlockSpec((1, 1, d, tf), lambda bi, e, k: (layer, e, 0, k)),
            pl.BlockSpec((1, 1, tf, d), lambda bi, e, k: (layer, e, k, 0)),
        ],
        out_specs=pl.BlockSpec((sb, rows, d), lambda bi, e, k: (bi, e, 0)),
        out_shape=jax.ShapeDtypeStruct((b, n_e * rows, d), BF16),
        scratch_shapes=[pltpu.VMEM((sb * rows, d), F32)],
        compiler_params=_cparams(("parallel", "parallel", "arbitrary")),
    )(xs, gate, w1, w3, w2)
    seg_blocks = n_ctx // tb
    lead = 0 if final_g is None else seg_blocks
    extra_args = () if final_g is None else (final_g,)
    extra_specs = [] if final_g is None else [pl.BlockSpec((1, d), lambda bi, j, *_: (0, 0))]
    return pl.pallas_call(
        functools.partial(_moe_combine_kernel, rows=rows, w=w, nb=nb, lead=lead),
        grid_spec=pltpu.PrefetchScalarGridSpec(
            num_scalar_prefetch=2, grid=(b, nb - lead),
            in_specs=[pl.BlockSpec((1, 1, n_e, tb), lambda bi, j, *_: (bi, j + lead, 0, 0)), whole(d),
                      pl.BlockSpec((1, tb, d), lambda bi, j, *_: (bi, j + lead, 0)),
                      pl.BlockSpec((1, 1, 6, d),
                                   lambda bi, j, *_: (bi, jnp.where(j + lead >= seg_blocks, 1, 0), 0, 0))]
            + extra_specs,
            out_specs=pl.BlockSpec((1, tb, d), lambda bi, j, *_: (bi, j, 0))),
        out_shape=jax.ShapeDtypeStruct((b, t - lead * tb, d), F32),
        compiler_params=_cparams(("parallel", "parallel")),
    )(a0, rounds, slot_b, y, x, modtab, *extra_args)


def _rope_tables(n_ctx, n_lat):
    n_rows = n_lat // GRID_W
    row = jnp.repeat(jnp.arange(n_rows, dtype=F32), GRID_W)
    col = jnp.tile(jnp.arange(GRID_W, dtype=F32), n_rows)
    inv = ROPE_THETA ** (-jnp.arange(0, ROPE_AXIS_DIM, 2, dtype=F32) / ROPE_AXIS_DIM)
    ang_r = row[:, None] * inv[None, :]
    ang_c = col[:, None] * inv[None, :]
    cos = jnp.concatenate([jnp.cos(ang_r), jnp.cos(ang_r), jnp.cos(ang_c), jnp.cos(ang_c)], axis=-1)
    sin = jnp.concatenate([-jnp.sin(ang_r), jnp.sin(ang_r), -jnp.sin(ang_c), jnp.sin(ang_c)], axis=-1)
    cos = jnp.concatenate([jnp.ones((n_ctx, HEAD_DIM), F32), cos], axis=0)
    sin = jnp.concatenate([jnp.zeros((n_ctx, HEAD_DIM), F32), sin], axis=0)
    reps = SEG_W // HEAD_DIM
    return jnp.tile(cos, (1, reps)), jnp.tile(sin, (1, reps))


def _extend_in_proj(w_in):
    base = 4 * RET_WIDTH + ATT_WIDTH
    ak = w_in[:, base:base + ATT_KV_WIDTH].reshape(-1, ATT_KV_HEADS, 1, HEAD_DIM)
    av = w_in[:, base + ATT_KV_WIDTH:].reshape(-1, ATT_KV_HEADS, 1, HEAD_DIM)
    rep = lambda w, n: jnp.broadcast_to(w, (w.shape[0], ATT_KV_HEADS, n, HEAD_DIM)).reshape(w.shape[0], -1)
    return jnp.concatenate([w_in[:, :base], rep(ak, ATT_GROUP), rep(av, 2)], axis=-1)


def kernel(x, c, ctx, c_ctx, mod_w, mod_b, norm_g, mix_in_w, mix_out_w, ret_log_rate, ret_gn_g, qk_norm_g,
           s5_a_re, s5_a_im, s5_log_dt, s5_b_re, s5_b_im, s5_c_re, s5_c_im, s5_d, s5_glu_w, s5_glu_b,
           moe_router_w, moe_w1, moe_w3, moe_w2, final_norm_g):
    batch, n_lat, d = x.shape
    n_ctx = ctx.shape[1]
    t = n_ctx + n_lat
    depth = mod_w.shape[0]
    assert batch == 8 and d % 128 == 0 and n_ctx % CHUNK == 0 and n_lat % CHUNK == 0
    tm = 256 if n_ctx % 256 == 0 else 128
    tb = tm
    tf = 1024

    xs = jnp.concatenate([ctx, x], axis=1)
    rows = 16
    cond = jnp.zeros((rows, d), F32).at[:batch].set(c).at[batch].set(c_ctx)
    mod_all = _modulation(cond, mod_w, mod_b)
    cos, sin = _rope_tables(n_ctx, n_lat)
    lane_head = jnp.arange(SEG_W) // HEAD_DIM
    bd = (lane_head[:, None] == lane_head[None, :]).astype(BF16)
    w1, w3, w2 = moe_w1, moe_w3, moe_w2

    for layer in range(depth):
        m = mod_all[layer]
        lat = m[:batch].reshape(batch, 1, 6, d)
        cx = jnp.broadcast_to(m[batch].reshape(1, 1, 6, d), (batch, 1, 6, d))
        modtab = jnp.concatenate([cx, lat], axis=1)
        ng1 = norm_g[layer, 0].reshape(1, d)
        ng2 = norm_g[layer, 1].reshape(1, d)
        wr = jnp.zeros((d, ROUTER_LANES), F32).at[:, :N_EXPERTS].set(moe_router_w[layer])
        wrh = wr.astype(BF16)
        wrl = (wr - wrh.astype(F32)).astype(BF16)
        i = layer // 2
        if layer % 2 == 0:
            w_ext = _extend_in_proj(mix_in_w[i]).astype(BF16)
            reps = SEG_W // HEAD_DIM
            qkg = jnp.stack([jnp.tile(qk_norm_g[i, 0], reps), jnp.tile(qk_norm_g[i, 1], reps)])
            rq, rk, rv, rg, aq, ak4, av2 = _in_proj(xs, modtab, ng1, w_ext, cos, sin, qkg, bd, tm, n_ctx)
            rate = jnp.repeat(ret_log_rate[i], HEAD_DIM, axis=-1)
            rate_h = jnp.broadcast_to(ret_log_rate[i][:, :, None], (2, RET_HEADS, CHUNK))
            o_f, o_b = _retention(rq, rk, rv, rate, rate_h, bd, n_ctx)
            att = _attention(aq, ak4, av2, tm, tm, n_ctx)
            xs, f, aff = _out_proj(xs, o_f, o_b, rg, att, modtab, ret_gn_g[i].reshape(1, SEG_W),
                                   mix_out_w[i].astype(BF16), ng2, wrh, wrl, bd, tm, n_ctx)
        else:
            u = _s5_pre(xs, modtab, ng1, tm, n_ctx)
            tabs = _s5_tables(s5_a_re[i], s5_a_im[i], s5_log_dt[i], s5_b_re[i], s5_b_im[i],
                              s5_c_re[i], s5_c_im[i], s5_d[i])
            z = _s5_scan(u, *tabs, n_ctx)
            xs, f, aff = _glu(xs, z, modtab, s5_glu_w[i].astype(BF16), s5_glu_b[i].reshape(1, 2 * d),
                              ng2, wrh, wrl, tm, n_ctx)
        final_g = final_norm_g.reshape(1, d) if layer == depth - 1 else None
        xs = _route_and_moe(layer, xs, aff, f, modtab, w1, w3, w2, n_ctx, tb, tf, final_g)
    return xs
```

```python
import functools
import math

import jax
import jax.numpy as jnp
from jax import lax
from jax.experimental import pallas as pl
from jax.experimental.pallas import tpu as pltpu

F32 = jnp.float32
BF16 = jnp.bfloat16

GRID_W = 64
HEAD_DIM = 64
RET_HEADS = 8
ATT_HEADS = 8
ATT_KV_HEADS = 2
ATT_GROUP = ATT_HEADS // ATT_KV_HEADS
RET_WIDTH = RET_HEADS * HEAD_DIM
ATT_WIDTH = ATT_HEADS * HEAD_DIM
ATT_KV_WIDTH = ATT_KV_HEADS * HEAD_DIM
CHUNK = 128
ROPE_THETA = 10000.0
ROPE_AXIS_DIM = HEAD_DIM // 2
S5_GROUP = 16
S5_STATE = 64
S5_SUB = 8
S5_BLOCK_GROUPS = 8
S5_SCAN_UNROLL = 4
N_EXPERTS = 16
CAPACITY_FACTOR = 2
EPS = 1e-6
SEG_W = 512
LOG2_E = math.log2(math.e)
ROUTER_LANES = 128
MOE_WINDOW = 64
MOE_FFN_SAMPLES = 2

V7X_VMEM_BYTES = 64 * 1024 * 1024
VMEM_LIMIT = V7X_VMEM_BYTES - 8 * 1024 * 1024


def _cparams(sem):
    return pltpu.CompilerParams(dimension_semantics=sem, vmem_limit_bytes=VMEM_LIMIT)


def _dot(a, b):
    return jnp.dot(a, b, preferred_element_type=F32)


def _dot_nt(a, b):
    return lax.dot_general(a, b, (((1,), (1,)), ((), ())), preferred_element_type=F32)


def _dot_tn(a, b):
    return lax.dot_general(a, b, (((0,), (0,)), ((), ())), preferred_element_type=F32)


def _split(v):
    hi = v.astype(BF16)
    lo = (v - hi.astype(F32)).astype(BF16)
    return hi, lo


def _segsum(v, bd):
    hi, lo = _split(v)
    return _dot(hi, bd) + _dot(lo, bd)


def _sigmoid(v):
    return 1.0 / (1.0 + jnp.exp(-v))


def _silu(v):
    return v * _sigmoid(v)


def _gelu_tanh(v):
    c = math.sqrt(2.0 / math.pi)
    return 0.5 * v * (1.0 + jnp.tanh(c * (v + 0.044715 * (v * v * v))))


def _norm_mod(x, g, scale, shift):
    ms = jnp.mean(x * x, axis=-1, keepdims=True)
    return x * lax.rsqrt(ms + EPS) * g * (1.0 + scale) + shift


def _rope(x, cos, sin_signed):
    w = x.shape[-1]
    lane = lax.broadcasted_iota(jnp.int32, x.shape, 1)
    first = (lane % 32) < 16
    partner = jnp.where(first, pltpu.roll(x, w - 16, 1), pltpu.roll(x, 16, 1))
    return x * cos + partner * sin_signed


def _mod_kernel(c_ref, w_ref, b_ref, o_ref):
    s = _silu(c_ref[...]).astype(BF16)
    o_ref[0] = _dot(s, w_ref[0].astype(BF16)) + b_ref[0]


def _modulation(cond, mod_w, mod_b):
    depth, d, n6 = mod_w.shape
    rows = cond.shape[0]
    tn = n6 // 6
    return pl.pallas_call(
        _mod_kernel,
        grid=(depth, n6 // tn),
        in_specs=[
            pl.BlockSpec((rows, d), lambda l, j: (0, 0)),
            pl.BlockSpec((1, d, tn), lambda l, j: (l, 0, j)),
            pl.BlockSpec((1, 1, tn), lambda l, j: (l, 0, j)),
        ],
        out_specs=pl.BlockSpec((1, rows, tn), lambda l, j: (l, 0, j)),
        out_shape=jax.ShapeDtypeStruct((depth, rows, n6), F32),
        compiler_params=_cparams(("parallel", "parallel")),
    )(cond, mod_w, mod_b.reshape(depth, 1, n6))


def _in_kernel(x_ref, mod_ref, ng_ref, w_ref, cos_ref, sin_ref, qkg_ref, bd_ref,
               rq_ref, rk_ref, rv_ref, rg_ref, aq_ref, ak_ref, av_ref):
    x = x_ref[0]
    m = mod_ref[0, 0]
    h = _norm_mod(x, ng_ref[...], m[1:2], m[0:1]).astype(BF16)
    cos = cos_ref[...]
    sin = sin_ref[...]
    bd = bd_ref[...]

    def seg(i):
        return _dot(h, w_ref[:, i * SEG_W:(i + 1) * SEG_W])

    def headnorm(v, g):
        ms = _segsum(v * v, bd) * (1.0 / HEAD_DIM)
        return v * lax.rsqrt(ms + EPS) * g

    scale = HEAD_DIM ** -0.5
    rq_ref[0] = _rope(seg(0), cos, sin).astype(BF16)
    rk_ref[0] = _rope(seg(1) * scale, cos, sin).astype(BF16)
    rv_ref[0] = seg(2).astype(BF16)
    rg_ref[0] = seg(3)
    aq_ref[0] = (_rope(headnorm(seg(4), qkg_ref[0:1]), cos, sin) * (scale * LOG2_E)).astype(BF16)
    ak_ref[0] = _rope(headnorm(seg(5), qkg_ref[1:2]), cos, sin).astype(BF16)
    av_ref[0] = _dot(h, w_ref[:, 6 * SEG_W:]).astype(BF16)


def _row_spec(tm, d, lead=0):
    return pl.BlockSpec((1, tm, d), lambda b, i: (b, i + lead, 0))


def _mod_spec(d, n, seg_tiles):
    return pl.BlockSpec((1, 1, n, d), lambda b, i: (b, jnp.where(i >= seg_tiles, 1, 0), 0, 0))


def _const_spec(shape):
    nd = len(shape)
    return pl.BlockSpec(shape, lambda b, i: (0,) * nd)


def _in_proj(x, modtab, norm_g, w_ext, cos, sin, qkg, bd, tm, n_ctx):
    b, t, d = x.shape
    out_shape, out_specs = [], []
    for dt, width in ((BF16, SEG_W), (BF16, SEG_W), (BF16, SEG_W), (F32, SEG_W), (BF16, SEG_W), (BF16, SEG_W),
                      (BF16, ATT_KV_HEADS * 2 * HEAD_DIM)):
        out_shape.append(jax.ShapeDtypeStruct((b, t, width), dt))
        out_specs.append(_row_spec(tm, width))
    return pl.pallas_call(
        _in_kernel,
        grid=(b, t // tm),
        in_specs=[
            _row_spec(tm, d),
            _mod_spec(d, 6, n_ctx // tm),
            _const_spec((1, d)),
            _const_spec(w_ext.shape),
            pl.BlockSpec((tm, SEG_W), lambda b, i: (i, 0)),
            pl.BlockSpec((tm, SEG_W), lambda b, i: (i, 0)),
            _const_spec((2, SEG_W)),
            _const_spec((SEG_W, SEG_W)),
        ],
        out_specs=out_specs,
        out_shape=out_shape,
        compiler_params=_cparams(("parallel", "parallel")),
    )(x, modtab, norm_g, w_ext, cos, sin, qkg, bd)


def _ret_kernel(qf_ref, kf_ref, vf_ref, qb_ref, kb_ref, vb_ref, rate_ref, rate_h_ref, bd_ref,
                of_ref, ob_ref, sf_s, sb_s, dm_s, dq_s, dk_s, dc_s):
    s = pl.program_id(1)
    c = CHUNK

    @pl.when(s == 0)
    def _():
        sf_s[...] = jnp.zeros_like(sf_s)
        sb_s[...] = jnp.zeros_like(sb_s)
        pos = lax.broadcasted_iota(jnp.int32, (c, SEG_W), 0).astype(F32)
        lgf = -jnp.exp(rate_ref[0:1, :])
        lgb = -jnp.exp(rate_ref[1:2, :])
        dq_s[0] = jnp.exp((pos + 1.0) * lgf)
        dk_s[0] = jnp.exp((c - 1.0 - pos) * lgf)
        dc_s[0] = jnp.exp(float(c) * lgf)
        dq_s[1] = jnp.exp((c - pos) * lgb)
        dk_s[1] = jnp.exp(pos * lgb)
        dc_s[1] = jnp.exp(float(c) * lgb)
        ii = lax.broadcasted_iota(jnp.int32, (c, c), 0)
        jj = lax.broadcasted_iota(jnp.int32, (c, c), 1)
        for h in range(RET_HEADS):
            lf = -jnp.exp(rate_h_ref[0, h:h + 1, :])
            lb = -jnp.exp(rate_h_ref[1, h:h + 1, :])
            mf = ii >= jj
            mb = jj > ii
            dm_s[0, h] = jnp.where(mf, jnp.exp(jnp.where(mf, ii - jj, 0).astype(F32) * lf), 0.0)
            dm_s[1, h] = jnp.where(mb, jnp.exp(jnp.where(mb, jj - ii, 0).astype(F32) * lb), 0.0)

    bdm = bd_ref[...].astype(F32)
    head_of_lane = lax.broadcasted_iota(jnp.int32, (1, SEG_W // 2), 1) // HEAD_DIM

    def direction(d, q_ref, k_ref, v_ref, o_ref, st):
        q = q_ref[0]
        k = k_ref[0]
        v = v_ref[0]
        state = st[...]
        qd = (q.astype(F32) * dq_s[d]).astype(BF16)
        inter = _dot(qd, state.astype(BF16))
        for half in range(2):
            sl = slice(half * 256, half * 256 + 256)
            qh, kh, vh = q[:, sl], k[:, sl], v[:, sl]
            acc = inter[:, sl]
            for hh in range(4):
                msk = head_of_lane == hh
                sc = _dot_nt(jnp.where(msk, qh, jnp.zeros_like(qh)), kh) * dm_s[d, half * 4 + hh]
                acc = acc + _dot(sc.astype(BF16), jnp.where(msk, vh, jnp.zeros_like(vh)))
            o_ref[0, :, sl] = acc
        kd = (k.astype(F32) * dk_s[d]).astype(BF16)
        st[...] = state * dc_s[d] + bdm * _dot_tn(kd, v)

    direction(0, qf_ref, kf_ref, vf_ref, of_ref, sf_s)
    direction(1, qb_ref, kb_ref, vb_ref, ob_ref, sb_s)


def _retention(rq, rk, rv, rate, rate_h, bd, n_ctx):
    b, t, w = rq.shape
    nc = t // CHUNK
    lc = n_ctx // CHUNK

    def fwd(bi, s):
        return (bi, s, 0)

    def bwd(bi, s):
        return (bi, jnp.where(s < lc, lc - 1 - s, nc - 1 - (s - lc)), 0)

    blk = (1, CHUNK, w)
    return pl.pallas_call(
        _ret_kernel,
        grid=(b, nc),
        in_specs=[pl.BlockSpec(blk, fwd)] * 3 + [pl.BlockSpec(blk, bwd)] * 3 + [
            _const_spec((2, w)),
            _const_spec((2, RET_HEADS, CHUNK)),
            _const_spec((w, w)),
        ],
        out_specs=[pl.BlockSpec(blk, fwd), pl.BlockSpec(blk, bwd)],
        out_shape=[jax.ShapeDtypeStruct((b, t, w), F32)] * 2,
        scratch_shapes=[
            pltpu.VMEM((w, w), F32), pltpu.VMEM((w, w), F32),
            pltpu.VMEM((2, RET_HEADS, CHUNK, CHUNK), F32),
            pltpu.VMEM((2, CHUNK, w), F32), pltpu.VMEM((2, CHUNK, w), F32),
            pltpu.VMEM((2, 1, w), F32),
        ],
        compiler_params=_cparams(("parallel", "arbitrary")),
    )(rq, rk, rv, rq, rk, rv, rate, rate_h, bd)


ATT_ROW_BLOCK = 512


def _att_kernel(q_ref, k_ref, v_ref, o_ref, qs_s, m_s, acc_s, sa_s, sb_s, ma_s, mb_s, *, tq, tk, n_ctx, t):
    qi = pl.program_id(1)
    gw = ATT_GROUP * HEAD_DIM
    head_of_lane = lax.broadcasted_iota(jnp.int32, (1, gw), 1) // HEAD_DIM
    for h in range(ATT_HEADS):
        q = q_ref[0, :, (h // ATT_GROUP) * gw:(h // ATT_GROUP + 1) * gw]
        qs_s[h * tq:(h + 1) * tq, :] = jnp.where(head_of_lane == h % ATT_GROUP, q, jnp.zeros_like(q))
    m_s[...] = jnp.full(m_s.shape, -1e30, F32)
    acc_s[...] = jnp.zeros_like(acc_s)
    nk = jnp.where(qi * tq < n_ctx, n_ctx // tk, t // tk)
    vw = 2 * HEAD_DIM
    ones_lane = lax.broadcasted_iota(jnp.int32, (1, ATT_KV_HEADS * vw), 1) % vw >= HEAD_DIM
    rb = ATT_ROW_BLOCK

    blocks = [(slice(r * rb, (r + 1) * rb), (r * rb) // (ATT_GROUP * tq)) for r in range(ATT_HEADS * tq // rb)]

    def scores_into(j, dst, dst_max):
        kb = k_ref[0, pl.ds(pl.multiple_of(j * tk, tk), tk), :]
        for rows, g in blocks:
            sc = _dot_nt(qs_s[rows, :], kb[:, g * gw:(g + 1) * gw])
            dst[rows, :] = sc
            dst_max[rows, :] = jnp.broadcast_to(jnp.max(sc, axis=1, keepdims=True), (rb, vw))

    def consume(j, src, src_max):
        vb = v_ref[0, pl.ds(pl.multiple_of(j * tk, tk), tk), :]
        vext = jnp.where(ones_lane, jnp.ones_like(vb), vb)
        for rows, g in blocks:
            m_prev = m_s[rows, :]
            m_new = jnp.maximum(m_prev, src_max[rows, :])
            p = jnp.exp2(src[rows, :] - jnp.concatenate([m_new] * (tk // vw), axis=1)).astype(BF16)
            acc_s[rows, :] = jnp.exp2(m_prev - m_new) * acc_s[rows, :] + _dot(p, vext[:, g * vw:(g + 1) * vw])
            m_s[rows, :] = m_new

    scores_into(0, sa_s, ma_s)

    def body(i, carry):
        j = 2 * i
        scores_into(j + 1, sb_s, mb_s)
        consume(j, sa_s, ma_s)
        scores_into(jnp.minimum(j + 2, nk - 1), sa_s, ma_s)
        consume(j + 1, sb_s, mb_s)
        return carry

    lax.fori_loop(0, nk // 2, body, 0)

    @pl.when(nk % 2 == 1)
    def _():
        consume(nk - 1, sa_s, ma_s)

    low = lax.broadcasted_iota(jnp.int32, (tq, vw), 1) < HEAD_DIM
    for pair in range(ATT_HEADS // 2):
        a0 = acc_s[(2 * pair) * tq:(2 * pair + 1) * tq, :]
        a1 = acc_s[(2 * pair + 1) * tq:(2 * pair + 2) * tq, :]
        even = a0 / pltpu.roll(a0, HEAD_DIM, 1)
        odd = pltpu.roll(a1, HEAD_DIM, 1) / a1
        o_ref[0, :, pair * vw:(pair + 1) * vw] = jnp.where(low, even, odd).astype(BF16)


def _attention(aq, ak4, av2, tq, tk, n_ctx):
    b, t, w = aq.shape
    gw = ATT_GROUP * HEAD_DIM
    vw = 2 * HEAD_DIM
    rows = ATT_HEADS * tq
    return pl.pallas_call(
        functools.partial(_att_kernel, tq=tq, tk=tk, n_ctx=n_ctx, t=t),
        grid=(b, t // tq),
        in_specs=[
            pl.BlockSpec((1, tq, w), lambda bi, i: (bi, i, 0)),
            pl.BlockSpec((1, t, w), lambda bi, i: (bi, 0, 0)),
            pl.BlockSpec((1, t, ATT_KV_HEADS * vw), lambda bi, i: (bi, 0, 0)),
        ],
        out_specs=pl.BlockSpec((1, tq, w), lambda bi, i: (bi, i, 0)),
        out_shape=jax.ShapeDtypeStruct((b, t, w), BF16),
        scratch_shapes=[pltpu.VMEM((rows, gw), BF16), pltpu.VMEM((rows, vw), F32), pltpu.VMEM((rows, vw), F32),
                        pltpu.VMEM((rows, tk), F32), pltpu.VMEM((rows, tk), F32), pltpu.VMEM((rows, vw), F32),
                        pltpu.VMEM((rows, vw), F32)],
        compiler_params=_cparams(("parallel", "parallel")),
    )(aq, ak4, av2)


def _post_mixer(x, delta, m, ng2, wrh, wrl, x_ref, f_ref, aff_ref):
    xn = x + m[2:3] * delta
    x_ref[0] = xn
    f = _norm_mod(xn, ng2, m[4:5], m[3:4])
    fh, fl = _split(f)
    f_ref[0] = fh
    logits = _dot(fh, wrh) + _dot(fl, wrh) + _dot(fh, wrl)
    lane = lax.broadcasted_iota(jnp.int32, logits.shape, 1)
    logits = jnp.where(lane < N_EXPERTS, logits, -1e30)
    e = jnp.exp(logits - jnp.max(logits, axis=1, keepdims=True))
    aff_ref[0] = e / jnp.sum(e, axis=1, keepdims=True)


def _out_kernel(x_ref, of_ref, ob_ref, rg_ref, att_ref, mod_ref, gn_ref, w_ref, ng2_ref, wrh_ref, wrl_ref,
                bd_ref, xo_ref, f_ref, aff_ref):
    bd = bd_ref[...]
    o = of_ref[0] + ob_ref[0]
    mu = _segsum(o, bd) * (1.0 / HEAD_DIM)
    oc = o - mu
    var = _segsum(oc * oc, bd) * (1.0 / HEAD_DIM)
    r = _silu(rg_ref[0]) * (oc * lax.rsqrt(var + EPS) * gn_ref[...])
    y = _dot(r.astype(BF16), w_ref[0:RET_WIDTH, :]) + _dot(att_ref[0], w_ref[RET_WIDTH:, :])
    _post_mixer(x_ref[0], y, mod_ref[0, 0], ng2_ref[...], wrh_ref[...], wrl_ref[...], xo_ref, f_ref, aff_ref)


def _post_specs(b, t, d, tm):
    out_shape = [jax.ShapeDtypeStruct((b, t, d), F32), jax.ShapeDtypeStruct((b, t, d), BF16),
                 jax.ShapeDtypeStruct((b, t, ROUTER_LANES), F32)]
    out_specs = [_row_spec(tm, d), _row_spec(tm, d), _row_spec(tm, ROUTER_LANES)]
    return out_shape, out_specs


def _out_proj(x, of, ob, rg, att, modtab, gn_g, w_out, ng2, wrh, wrl, bd, tm, n_ctx):
    b, t, d = x.shape
    out_shape, out_specs = _post_specs(b, t, d, tm)
    return pl.pallas_call(
        _out_kernel,
        grid=(b, t // tm),
        in_specs=[
            _row_spec(tm, d), _row_spec(tm, SEG_W), _row_spec(tm, SEG_W), _row_spec(tm, SEG_W),
            _row_spec(tm, SEG_W), _mod_spec(d, 6, n_ctx // tm), _const_spec((1, SEG_W)),
            _const_spec(w_out.shape), _const_spec((1, d)), _const_spec(wrh.shape), _const_spec(wrl.shape),
            _const_spec((SEG_W, SEG_W)),
        ],
        out_specs=out_specs,
        out_shape=out_shape,
        compiler_params=_cparams(("parallel", "parallel")),
    )(x, of, ob, rg, att, modtab, gn_g, w_out, ng2, wrh, wrl, bd)


def _s5_kernel(u_ref, tm_ref, wz_ref, wo_ref, ar_ref, ai_ref, d_ref, z_ref,
               zr_s, zi_s, xr_s, xi_s, zs_s, *, n_sub, n_sub_ctx):
    s_n, j_sub, gb, half = n_sub, S5_SUB, S5_BLOCK_GROUPS, 2 * S5_STATE
    uf = jnp.concatenate([u_ref[0, pl.ds(m, s_n, stride=j_sub), :] for m in range(j_sub)], axis=1)
    lhs = uf.astype(BF16)
    for g in range(gb):
        zz = _dot(lhs, wz_ref[0, g])
        zr_s[pl.ds(g, s_n, stride=gb), :] = zz[:, :half]
        zi_s[pl.ds(g, s_n, stride=gb), :] = zz[:, half:]
    fwd_lane = lax.broadcasted_iota(jnp.int32, (gb, half), 1) < S5_STATE
    ar = ar_ref[0]
    ai = ai_ref[0]

    def body(i, carry):
        xr, xi = carry
        sb = jnp.where(i < n_sub_ctx, n_sub_ctx - 1 - i, s_n - 1 - (i - n_sub_ctx))
        fwd = pl.ds(pl.multiple_of(i * gb, gb), gb)
        bwd = pl.ds(pl.multiple_of(sb * gb, gb), gb)
        xr_s[fwd, 0:S5_STATE] = xr[:, 0:S5_STATE]
        xi_s[fwd, 0:S5_STATE] = xi[:, 0:S5_STATE]
        xr_s[bwd, S5_STATE:half] = xr[:, S5_STATE:half]
        xi_s[bwd, S5_STATE:half] = xi[:, S5_STATE:half]
        zr = jnp.where(fwd_lane, zr_s[fwd, :], zr_s[bwd, :])
        zi = jnp.where(fwd_lane, zi_s[fwd, :], zi_s[bwd, :])
        return ar * xr - ai * xi + zr, ar * xi + ai * xr + zi

    zero = jnp.zeros((gb, half), F32)
    lax.fori_loop(0, s_n, body, (zero, zero), unroll=S5_SCAN_UNROLL)
    y = _dot(lhs, tm_ref[0]) + d_ref[0] * uf
    for g in range(gb):
        rows = pl.ds(g, s_n, stride=gb)
        xp = jnp.concatenate([xr_s[rows, :], xi_s[rows, :]], axis=1).astype(BF16)
        y = y + _dot(xp, wo_ref[0, g])
    z = _gelu_tanh(y)
    w = z_ref.shape[-1]
    for n in range(j_sub):
        zs_s[pl.ds(n, s_n, stride=j_sub), :] = z[:, n * w:(n + 1) * w]
    z_ref[0] = zs_s[...].astype(BF16)


def _s5_scan(u, tmat, wz, wo, ar, ai, dt, n_ctx):
    b, t, d = u.shape
    cw = S5_BLOCK_GROUPS * S5_GROUP
    n_sub = t // S5_SUB
    lw = S5_SUB * cw
    sw = 4 * S5_STATE
    gb = S5_BLOCK_GROUPS
    once = pl.Buffered(1)
    return pl.pallas_call(
        functools.partial(_s5_kernel, n_sub=n_sub, n_sub_ctx=n_ctx // S5_SUB),
        grid=(d // cw, b),
        in_specs=[
            pl.BlockSpec((1, t, cw), lambda c, bi: (bi, 0, c)),
            pl.BlockSpec((1, lw, lw), lambda c, bi: (c, 0, 0), pipeline_mode=once),
            pl.BlockSpec((1, gb, lw, sw), lambda c, bi: (c, 0, 0, 0), pipeline_mode=once),
            pl.BlockSpec((1, gb, sw, lw), lambda c, bi: (c, 0, 0, 0), pipeline_mode=once),
            pl.BlockSpec((1, gb, 2 * S5_STATE), lambda c, bi: (c, 0, 0)),
            pl.BlockSpec((1, gb, 2 * S5_STATE), lambda c, bi: (c, 0, 0)),
            pl.BlockSpec((1, 1, lw), lambda c, bi: (c, 0, 0)),
        ],
        out_specs=pl.BlockSpec((1, t, cw), lambda c, bi: (bi, 0, c)),
        out_shape=jax.ShapeDtypeStruct((b, t, d), BF16),
        scratch_shapes=[pltpu.VMEM((gb * n_sub, 2 * S5_STATE), F32)] * 4 + [pltpu.VMEM((t, cw), F32)],
        compiler_params=_cparams(("parallel", "parallel")),
    )(u, tmat, wz, wo, ar, ai, dt)


def _s5_tables(a_re, a_im, log_dt, b_re, b_im, c_re, c_im, d_skip):
    hp = lax.Precision.HIGHEST
    j_sub = S5_SUB
    gb = S5_BLOCK_GROUPS
    lam_re = jnp.minimum(a_re, -1e-4)
    lam_im = a_im
    dt = jnp.exp(log_dt)[..., None]
    mag = jnp.exp(lam_re * dt)
    bar_re = mag * jnp.cos(lam_im * dt)
    bar_im = mag * jnp.sin(lam_im * dt)
    den = lam_re * lam_re + lam_im * lam_im
    nr, ni = bar_re - 1.0, bar_im
    k_re = (nr * lam_re + ni * lam_im) / den
    k_im = (ni * lam_re - nr * lam_im) / den
    bb_re = k_re[..., None] * b_re[None] - k_im[..., None] * b_im[None]
    bb_im = k_re[..., None] * b_im[None] + k_im[..., None] * b_re[None]
    jj = jnp.arange(j_sub + 1, dtype=F32)[:, None, None, None]
    pmag = jnp.exp(jj * (lam_re * dt)[None])
    pw_re = pmag * jnp.cos(jj * (lam_im * dt)[None])
    pw_im = pmag * jnp.sin(jj * (lam_im * dt)[None])
    pb_re = pw_re[..., None] * bb_re[None] - pw_im[..., None] * bb_im[None]
    pb_im = pw_re[..., None] * bb_im[None] + pw_im[..., None] * bb_re[None]
    cp_re = c_re[None] * pw_re[:, :, :, None, :] - c_im[None] * pw_im[:, :, :, None, :]
    cp_im = c_re[None] * pw_im[:, :, :, None, :] + c_im[None] * pw_re[:, :, :, None, :]
    kern = (jnp.einsum('dgcp,jdgpk->jdgkc', c_re, pb_re[:j_sub], precision=hp)
            - jnp.einsum('dgcp,jdgpk->jdgkc', c_im, pb_im[:j_sub], precision=hp))
    groups = a_re.shape[1]
    nblk = groups // gb
    k_w = S5_GROUP
    p_w = S5_STATE
    lw = j_sub * gb * k_w
    eye = jnp.eye(gb, dtype=F32)
    place = (jnp.arange(gb * k_w)[None, None, :] == (jnp.arange(gb)[:, None, None] * k_w
                                                      + jnp.arange(k_w)[None, :, None])).astype(F32)
    blk = lambda a: a.reshape(a.shape[0], nblk, gb, *a.shape[2:])
    pos = jnp.arange(j_sub)
    lag = pos[None, :] - pos[:, None]
    kf = jnp.where((lag >= 0)[:, :, None, None, None], kern[jnp.clip(lag, 0, j_sub - 1), 0], 0.0)
    kb = jnp.where((lag <= 0)[:, :, None, None, None], kern[jnp.clip(-lag, 0, j_sub - 1), 1], 0.0)
    kfull = (kf + kb).reshape(j_sub, j_sub, nblk, gb, k_w, k_w)
    tmat = jnp.einsum('mnBgkc,gcd->Bmgknd', kfull, place, precision=hp).reshape(nblk, lw, lw)
    rev = pos[::-1]
    zc = jnp.concatenate([pb_re[rev, 0], pb_re[pos, 1], pb_im[rev, 0], pb_im[pos, 1]], axis=2)
    wz = jnp.einsum('mBgqk,gh->Bgmhkq', blk(zc), eye).reshape(nblk, gb, lw, 4 * p_w)
    def orows(cp, d, order):
        return jnp.einsum('nBgcp,gcd->Bgpnd', blk(cp[order, d]), place, precision=hp).reshape(nblk, gb, p_w, lw)
    wo = jnp.concatenate([orows(cp_re, 0, pos + 1), orows(cp_re, 1, j_sub - pos),
                          -orows(cp_im, 0, pos + 1), -orows(cp_im, 1, j_sub - pos)], axis=2)
    ar = jnp.concatenate([pw_re[j_sub, 0], pw_re[j_sub, 1]], axis=-1).reshape(nblk, gb, 2 * p_w)
    ai = jnp.concatenate([pw_im[j_sub, 0], pw_im[j_sub, 1]], axis=-1).reshape(nblk, gb, 2 * p_w)
    dt_tile = jnp.tile(d_skip.reshape(nblk, 1, gb * k_w), (1, 1, j_sub))
    return tmat.astype(BF16), wz.astype(BF16), wo.astype(BF16), ar, ai, dt_tile


def _glu_kernel(x_ref, z_ref, mod_ref, w_ref, b_ref, ng2_ref, wrh_ref, wrl_ref, xo_ref, f_ref, aff_ref):
    d = x_ref.shape[-1]
    ag = _dot(z_ref[0], w_ref[...]) + b_ref[...]
    delta = ag[:, :d] * _sigmoid(ag[:, d:])
    _post_mixer(x_ref[0], delta, mod_ref[0, 0], ng2_ref[...], wrh_ref[...], wrl_ref[...], xo_ref, f_ref, aff_ref)


def _glu(x, z, modtab, glu_w, glu_b, ng2, wrh, wrl, tm, n_ctx):
    b, t, d = x.shape
    out_shape, out_specs = _post_specs(b, t, d, tm)
    return pl.pallas_call(
        _glu_kernel,
        grid=(b, t // tm),
        in_specs=[_row_spec(tm, d), _row_spec(tm, d), _mod_spec(d, 6, n_ctx // tm),
                  _const_spec(glu_w.shape), _const_spec((1, 2 * d)), _const_spec((1, d)),
                  _const_spec(wrh.shape), _const_spec(wrl.shape)],
        out_specs=out_specs,
        out_shape=out_shape,
        compiler_params=_cparams(("parallel", "parallel")),
    )(x, z, modtab, glu_w, glu_b, ng2, wrh, wrl)


def _cumsum_lanes(v):
    n = v.shape[-1]
    lane = lax.broadcasted_iota(jnp.int32, v.shape, 1)
    sh = 1
    while sh < n:
        v = v + jnp.where(lane >= sh, pltpu.roll(v, sh, 1), 0.0)
        sh *= 2
    return v


def _select_segment(aff, cap, base):
    bits = pltpu.bitcast(aff, jnp.int32)
    thr = jnp.zeros((aff.shape[0], 1), jnp.int32)
    for bit in range(30, -1, -1):
        cand = thr | (1 << bit)
        cnt = jnp.sum(jnp.where(bits >= cand, 1.0, 0.0), axis=1, keepdims=True)
        thr = jnp.where(cnt >= cap, cand, thr)
    gt = bits > thr
    eq = bits == thr
    need = cap - jnp.sum(jnp.where(gt, 1.0, 0.0), axis=1, keepdims=True)
    eqf = jnp.where(eq, 1.0, 0.0)
    eq_before = _cumsum_lanes(eqf) - eqf
    sel = jnp.where(gt, 1.0, jnp.where(eq, jnp.where(eq_before < need, 1.0, 0.0), 0.0))
    pos = _cumsum_lanes(sel) - sel
    return jnp.where(sel > 0.0, pos.astype(jnp.int32) + base, -1)


def _select_kernel(aff_ref, slot_ref, *, n_ctx, cap_ctx, cap_lat):
    slot_ref[0, :, 0:n_ctx] = _select_segment(aff_ref[0, :, 0:n_ctx], cap_ctx, 0)
    slot_ref[0, :, n_ctx:] = _select_segment(aff_ref[0, :, n_ctx:], cap_lat, cap_ctx)


def _select(aff_t, n_ctx, cap_ctx, cap_lat):
    b, e, t = aff_t.shape
    return pl.pallas_call(
        functools.partial(_select_kernel, n_ctx=n_ctx, cap_ctx=cap_ctx, cap_lat=cap_lat),
        grid=(b,),
        in_specs=[pl.BlockSpec((1, e, t), lambda i: (i, 0, 0))],
        out_specs=pl.BlockSpec((1, e, t), lambda i: (i, 0, 0)),
        out_shape=jax.ShapeDtypeStruct((b, e, t), jnp.int32),
        compiler_params=_cparams(("parallel",)),
    )(aff_t)


def _moe_window(a_ref, base, e, k, rows, w):
    nominal = a_ref[base + e] + k * w
    return nominal, jnp.minimum(nominal, rows - w)


def _moe_hits(slot_row, nominal, actual, w):
    rid = lax.broadcasted_iota(jnp.int32, (w, slot_row.shape[-1]), 0) + actual
    return jnp.where(slot_row >= nominal, slot_row, -1) == rid


def _moe_gather_kernel(a_ref, nr_ref, slot_ref, aff_ref, f_ref, xs_ref, gate_ref, *, rows, w):
    b = pl.program_id(0)
    j = pl.program_id(1)
    n_e = slot_ref.shape[2]
    step = b * pl.num_programs(1) + j

    @pl.when(j == 0)
    def _():
        xs_ref[...] = jnp.zeros_like(xs_ref)
        gate_ref[...] = jnp.zeros_like(gate_ref)

    fj = f_ref[0]

    def one_round(k, carry):
        for e in range(n_e):
            nominal, actual = _moe_window(a_ref, step * n_e, e, k, rows, w)
            hit = _moe_hits(slot_ref[0, 0, e:e + 1, :], nominal, actual, w)
            off = pl.multiple_of(e * rows + actual, 16)
            xs_ref[0, pl.ds(off, w), :] += _dot(jnp.where(hit, 1.0, 0.0).astype(BF16), fj).astype(BF16)
            gate_ref[0, pl.ds(off, w), :] += jnp.sum(jnp.where(hit, aff_ref[0, 0, e:e + 1, :], 0.0),
                                                     axis=1, keepdims=True)
        return carry

    lax.fori_loop(0, nr_ref[step], one_round, 0)


def _moe_ffn_kernel(xs_ref, gate_ref, w1_ref, w3_ref, w2_ref, y_ref, ys_s, *, n_ft):
    ft = pl.program_id(2)
    nb, rows, d = xs_ref.shape
    xs = xs_ref[...].reshape(nb * rows, d)
    hid = (_silu(_dot(xs, w1_ref[0, 0].astype(BF16))) * _dot(xs, w3_ref[0, 0].astype(BF16))).astype(BF16)
    part = _dot(hid, w2_ref[0, 0].astype(BF16))

    @pl.when(ft == 0)
    def _():
        ys_s[...] = part

    @pl.when(ft > 0)
    def _():
        ys_s[...] += part

    @pl.when(ft == n_ft - 1)
    def _():
        y_ref[...] = (ys_s[...] * gate_ref[...].reshape(nb * rows, 1)).astype(BF16).reshape(nb, rows, d)


def _moe_combine_kernel(a_ref, nr_ref, slot_ref, y_ref, x_ref, mod_ref, *rest, rows, w, nb, lead, tail):
    o_ref = rest[-1] if tail != "s5" else rest[-2]
    b = pl.program_id(0)
    n_e = slot_ref.shape[2]
    step = b * nb + pl.program_id(1) + lead

    def one_round(k, acc):
        onehots, ys = [], []
        for e in range(n_e):
            nominal, actual = _moe_window(a_ref, step * n_e, e, k, rows, w)
            hit = _moe_hits(slot_ref[0, 0, e:e + 1, :], nominal, actual, w)
            onehots.append(jnp.where(hit, 1.0, 0.0).astype(BF16))
            ys.append(y_ref[0, pl.ds(pl.multiple_of(e * rows + actual, 16), w), :])
        return acc + _dot_tn(jnp.concatenate(onehots, axis=0), jnp.concatenate(ys, axis=0))

    acc = lax.fori_loop(0, nr_ref[step], one_round, jnp.zeros(o_ref.shape[1:], F32))
    xn = x_ref[0] + mod_ref[0, 0, 5:6, :] * acc
    if tail == "final":
        xn = xn * lax.rsqrt(jnp.mean(xn * xn, axis=-1, keepdims=True) + EPS) * rest[0][...]
    o_ref[0] = xn
    if tail == "s5":
        m = rest[0][0, 0]
        rest[3][0] = _norm_mod(xn, rest[1][...], m[1:2], m[0:1])


def _moe_windows(slot, tb, w):
    b, e, t = slot.shape
    nb = t // tb
    cnt = jnp.sum((slot >= 0).reshape(b, e, nb, tb), axis=-1, dtype=jnp.int32)
    start = jnp.cumsum(cnt, axis=-1) - cnt
    a0 = (start // 16) * 16
    rounds = jnp.maximum(jnp.max((start + cnt - a0 + w - 1) // w, axis=1), 1)
    return jnp.swapaxes(a0, 1, 2).reshape(-1).astype(jnp.int32), rounds.reshape(-1).astype(jnp.int32)


def _route_and_moe(layer, x, aff, f, modtab, w1, w3, w2, n_ctx, tb, tf, final_g=None, next_s5=None):
    b, t, d = f.shape
    n_e = w1.shape[1]
    ff = w1.shape[-1]
    nb = t // tb
    aff_t = jnp.swapaxes(aff[:, :, :N_EXPERTS], 1, 2)
    cap_ctx = (CAPACITY_FACTOR * n_ctx) // N_EXPERTS
    cap_lat = (CAPACITY_FACTOR * (t - n_ctx)) // N_EXPERTS
    rows = cap_ctx + cap_lat
    w = min(MOE_WINDOW, rows)
    slot = _select(aff_t, n_ctx, cap_ctx, cap_lat)
    a0, rounds = _moe_windows(slot, tb, w)
    by_block = lambda v: jnp.swapaxes(v.reshape(b, n_e, nb, tb), 1, 2)
    slot_b, aff_b = by_block(slot), by_block(aff_t)
    blk_spec = pl.BlockSpec((1, 1, n_e, tb), lambda bi, j, *_: (bi, j, 0, 0))
    whole = lambda width: pl.BlockSpec((1, n_e * rows, width), lambda bi, j, *_: (bi, 0, 0))
    xs, gate = pl.pallas_call(
        functools.partial(_moe_gather_kernel, rows=rows, w=w),
        grid_spec=pltpu.PrefetchScalarGridSpec(
            num_scalar_prefetch=2, grid=(b, nb),
            in_specs=[blk_spec, blk_spec, pl.BlockSpec((1, tb, d), lambda bi, j, *_: (bi, j, 0))],
            out_specs=[whole(d), whole(1)]),
        out_shape=[jax.ShapeDtypeStruct((b, n_e * rows, d), BF16), jax.ShapeDtypeStruct((b, n_e * rows, 1), F32)],
        compiler_params=_cparams(("parallel", "arbitrary")),
    )(a0, rounds, slot_b, aff_b, f)
    n_ft = ff // tf
    sb = MOE_FFN_SAMPLES
    y = pl.pallas_call(
        functools.partial(_moe_ffn_kernel, n_ft=n_ft),
        grid=(b // sb, n_e, n_ft),
        in_specs=[
            pl.BlockSpec((sb, rows, d), lambda bi, e, k: (bi, e, 0)),
            pl.BlockSpec((sb, rows, 1), lambda bi, e, k: (bi, e, 0)),
            pl.BlockSpec((1, 1, d, tf), lambda bi, e, k: (layer, e, 0, k)),
            pl.BlockSpec((1, 1, d, tf), lambda bi, e, k: (layer, e, 0, k)),
            pl.BlockSpec((1, 1, tf, d), lambda bi, e, k: (layer, e, k, 0)),
        ],
        out_specs=pl.BlockSpec((sb, rows, d), lambda bi, e, k: (bi, e, 0)),
        out_shape=jax.ShapeDtypeStruct((b, n_e * rows, d), BF16),
        scratch_shapes=[pltpu.VMEM((sb * rows, d), F32)],
        compiler_params=_cparams(("parallel", "parallel", "arbitrary")),
    )(xs, gate, w1, w3, w2)
    seg_blocks = n_ctx // tb
    lead = 0 if final_g is None else seg_blocks
    seg_of = lambda bi, j, *_: (bi, jnp.where(j + lead >= seg_blocks, 1, 0), 0, 0)
    gain_spec = pl.BlockSpec((1, d), lambda bi, j, *_: (0, 0))
    out_spec = pl.BlockSpec((1, tb, d), lambda bi, j, *_: (bi, j, 0))
    out_sds = jax.ShapeDtypeStruct((b, t - lead * tb, d), F32)
    tail, extra_args, extra_specs, out_specs, out_shape = "plain", (), [], out_spec, out_sds
    if final_g is not None:
        tail, extra_args, extra_specs = "final", (final_g,), [gain_spec]
    elif next_s5 is not None:
        tail, extra_args, extra_specs = "s5", tuple(next_s5), [pl.BlockSpec((1, 1, 6, d), seg_of), gain_spec]
        out_specs, out_shape = [out_spec, out_spec], [out_sds, out_sds]
    return pl.pallas_call(
        functools.partial(_moe_combine_kernel, rows=rows, w=w, nb=nb, lead=lead, tail=tail),
        grid_spec=pltpu.PrefetchScalarGridSpec(
            num_scalar_prefetch=2, grid=(b, nb - lead),
            in_specs=[pl.BlockSpec((1, 1, n_e, tb), lambda bi, j, *_: (bi, j + lead, 0, 0)), whole(d),
                      pl.BlockSpec((1, tb, d), lambda bi, j, *_: (bi, j + lead, 0)),
                      pl.BlockSpec((1, 1, 6, d), seg_of)]
            + extra_specs,
            out_specs=out_specs),
        out_shape=out_shape,
        compiler_params=_cparams(("parallel", "parallel")),
    )(a0, rounds, slot_b, y, x, modtab, *extra_args)


def _rope_tables(n_ctx, n_lat):
    n_rows = n_lat // GRID_W
    row = jnp.repeat(jnp.arange(n_rows, dtype=F32), GRID_W)
    col = jnp.tile(jnp.arange(GRID_W, dtype=F32), n_rows)
    inv = ROPE_THETA ** (-jnp.arange(0, ROPE_AXIS_DIM, 2, dtype=F32) / ROPE_AXIS_DIM)
    ang_r = row[:, None] * inv[None, :]
    ang_c = col[:, None] * inv[None, :]
    cos = jnp.concatenate([jnp.cos(ang_r), jnp.cos(ang_r), jnp.cos(ang_c), jnp.cos(ang_c)], axis=-1)
    sin = jnp.concatenate([-jnp.sin(ang_r), jnp.sin(ang_r), -jnp.sin(ang_c), jnp.sin(ang_c)], axis=-1)
    cos = jnp.concatenate([jnp.ones((n_ctx, HEAD_DIM), F32), cos], axis=0)
    sin = jnp.concatenate([jnp.zeros((n_ctx, HEAD_DIM), F32), sin], axis=0)
    reps = SEG_W // HEAD_DIM
    return jnp.tile(cos, (1, reps)), jnp.tile(sin, (1, reps))


def _extend_in_proj(w_in):
    base = 4 * RET_WIDTH + ATT_WIDTH
    ak = w_in[:, base:base + ATT_KV_WIDTH].reshape(-1, ATT_KV_HEADS, 1, HEAD_DIM)
    av = w_in[:, base + ATT_KV_WIDTH:].reshape(-1, ATT_KV_HEADS, 1, HEAD_DIM)
    rep = lambda w, n: jnp.broadcast_to(w, (w.shape[0], ATT_KV_HEADS, n, HEAD_DIM)).reshape(w.shape[0], -1)
    return jnp.concatenate([w_in[:, :base], rep(ak, ATT_GROUP), rep(av, 2)], axis=-1)


def kernel(x, c, ctx, c_ctx, mod_w, mod_b, norm_g, mix_in_w, mix_out_w, ret_log_rate, ret_gn_g, qk_norm_g,
           s5_a_re, s5_a_im, s5_log_dt, s5_b_re, s5_b_im, s5_c_re, s5_c_im, s5_d, s5_glu_w, s5_glu_b,
           moe_router_w, moe_w1, moe_w3, moe_w2, final_norm_g):
    batch, n_lat, d = x.shape
    n_ctx = ctx.shape[1]
    t = n_ctx + n_lat
    depth = mod_w.shape[0]
    assert batch == 8 and d % 128 == 0 and n_ctx % CHUNK == 0 and n_lat % CHUNK == 0
    tm = 256 if n_ctx % 256 == 0 else 128
    tb = tm
    tf = 1024

    xs = jnp.concatenate([ctx, x], axis=1)
    rows = 16
    cond = jnp.zeros((rows, d), F32).at[:batch].set(c).at[batch].set(c_ctx)
    mod_all = _modulation(cond, mod_w, mod_b)
    cos, sin = _rope_tables(n_ctx, n_lat)
    lane_head = jnp.arange(SEG_W) // HEAD_DIM
    bd = (lane_head[:, None] == lane_head[None, :]).astype(BF16)
    w1, w3, w2 = moe_w1, moe_w3, moe_w2

    def modtab_of(layer):
        m = mod_all[layer]
        lat = m[:batch].reshape(batch, 1, 6, d)
        cx = jnp.broadcast_to(m[batch].reshape(1, 1, 6, d), (batch, 1, 6, d))
        return jnp.concatenate([cx, lat], axis=1)

    u = None
    for layer in range(depth):
        modtab = modtab_of(layer)
        ng1 = norm_g[layer, 0].reshape(1, d)
        ng2 = norm_g[layer, 1].reshape(1, d)
        wr = jnp.zeros((d, ROUTER_LANES), F32).at[:, :N_EXPERTS].set(moe_router_w[layer])
        wrh = wr.astype(BF16)
        wrl = (wr - wrh.astype(F32)).astype(BF16)
        i = layer // 2
        if layer % 2 == 0:
            w_ext = _extend_in_proj(mix_in_w[i]).astype(BF16)
            reps = SEG_W // HEAD_DIM
            qkg = jnp.stack([jnp.tile(qk_norm_g[i, 0], reps), jnp.tile(qk_norm_g[i, 1], reps)])
            rq, rk, rv, rg, aq, ak4, av2 = _in_proj(xs, modtab, ng1, w_ext, cos, sin, qkg, bd, tm, n_ctx)
            rate = jnp.repeat(ret_log_rate[i], HEAD_DIM, axis=-1)
            rate_h = jnp.broadcast_to(ret_log_rate[i][:, :, None], (2, RET_HEADS, CHUNK))
            o_f, o_b = _retention(rq, rk, rv, rate, rate_h, bd, n_ctx)
            att = _attention(aq, ak4, av2, tm, tm, n_ctx)
            xs, f, aff = _out_proj(xs, o_f, o_b, rg, att, modtab, ret_gn_g[i].reshape(1, SEG_W),
                                   mix_out_w[i].astype(BF16), ng2, wrh, wrl, bd, tm, n_ctx)
        else:
            tabs = _s5_tables(s5_a_re[i], s5_a_im[i], s5_log_dt[i], s5_b_re[i], s5_b_im[i],
                              s5_c_re[i], s5_c_im[i], s5_d[i])
            z = _s5_scan(u, *tabs, n_ctx)
            xs, f, aff = _glu(xs, z, modtab, s5_glu_w[i].astype(BF16), s5_glu_b[i].reshape(1, 2 * d),
                              ng2, wrh, wrl, tm, n_ctx)
        final_g = final_norm_g.reshape(1, d) if layer == depth - 1 else None
        next_s5 = None
        if layer + 1 < depth and (layer + 1) % 2 == 1:
            next_s5 = (modtab_of(layer + 1), norm_g[layer + 1, 0].reshape(1, d))
        xs = _route_and_moe(layer, xs, aff, f, modtab, w1, w3, w2, n_ctx, tb, tf, final_g, next_s5)
        u = None
        if next_s5 is not None:
            xs, u = xs
    return xs
```

```python
import functools
import math

import jax
import jax.numpy as jnp
from jax import lax
from jax.experimental import pallas as pl
from jax.experimental.pallas import tpu as pltpu

F32 = jnp.float32
BF16 = jnp.bfloat16

GRID_W = 64
HEAD_DIM = 64
RET_HEADS = 8
ATT_HEADS = 8
ATT_KV_HEADS = 2
ATT_GROUP = ATT_HEADS // ATT_KV_HEADS
RET_WIDTH = RET_HEADS * HEAD_DIM
ATT_WIDTH = ATT_HEADS * HEAD_DIM
ATT_KV_WIDTH = ATT_KV_HEADS * HEAD_DIM
CHUNK = 128
ROPE_THETA = 10000.0
ROPE_AXIS_DIM = HEAD_DIM // 2
S5_GROUP = 16
S5_STATE = 64
S5_SUB = 8
S5_BLOCK_GROUPS = 8
S5_SCAN_UNROLL = 4
N_EXPERTS = 16
CAPACITY_FACTOR = 2
EPS = 1e-6
SEG_W = 512
LOG2_E = math.log2(math.e)
ROUTER_LANES = 128
MOE_WINDOW = 64
MOE_FFN_SAMPLES = 2

V7X_VMEM_BYTES = 64 * 1024 * 1024
VMEM_LIMIT = V7X_VMEM_BYTES - 8 * 1024 * 1024


def _cparams(sem):
    return pltpu.CompilerParams(dimension_semantics=sem, vmem_limit_bytes=VMEM_LIMIT)


def _dot(a, b):
    return jnp.dot(a, b, preferred_element_type=F32)


def _dot_nt(a, b):
    return lax.dot_general(a, b, (((1,), (1,)), ((), ())), preferred_element_type=F32)


def _dot_tn(a, b):
    return lax.dot_general(a, b, (((0,), (0,)), ((), ())), preferred_element_type=F32)


def _split(v):
    hi = v.astype(BF16)
    lo = (v - hi.astype(F32)).astype(BF16)
    return hi, lo


def _segsum(v, bd):
    return _dot(v.astype(BF16), bd)


def _sigmoid(v):
    return 1.0 / (1.0 + jnp.exp(-v))


def _silu(v):
    return v * _sigmoid(v)


def _gelu_tanh(v):
    c = math.sqrt(2.0 / math.pi)
    return 0.5 * v * (1.0 + jnp.tanh(c * (v + 0.044715 * (v * v * v))))


def _norm_mod(x, g, scale, shift):
    ms = jnp.mean(x * x, axis=-1, keepdims=True)
    return x * lax.rsqrt(ms + EPS) * g * (1.0 + scale) + shift


def _rope(x, cos, sin_signed):
    w = x.shape[-1]
    lane = lax.broadcasted_iota(jnp.int32, x.shape, 1)
    first = (lane % 32) < 16
    partner = jnp.where(first, pltpu.roll(x, w - 16, 1), pltpu.roll(x, 16, 1))
    return x * cos + partner * sin_signed


def _mod_kernel(c_ref, w_ref, b_ref, o_ref):
    s = _silu(c_ref[...]).astype(BF16)
    o_ref[0] = _dot(s, w_ref[0].astype(BF16)) + b_ref[0]


def _modulation(cond, mod_w, mod_b):
    depth, d, n6 = mod_w.shape
    rows = cond.shape[0]
    tn = n6 // 6
    return pl.pallas_call(
        _mod_kernel,
        grid=(depth, n6 // tn),
        in_specs=[
            pl.BlockSpec((rows, d), lambda l, j: (0, 0)),
            pl.BlockSpec((1, d, tn), lambda l, j: (l, 0, j)),
            pl.BlockSpec((1, 1, tn), lambda l, j: (l, 0, j)),
        ],
        out_specs=pl.BlockSpec((1, rows, tn), lambda l, j: (l, 0, j)),
        out_shape=jax.ShapeDtypeStruct((depth, rows, n6), F32),
        compiler_params=_cparams(("parallel", "parallel")),
    )(cond, mod_w, mod_b.reshape(depth, 1, n6))


def _in_kernel(x_ref, mod_ref, ng_ref, w_ref, cos_ref, sin_ref, qkg_ref, bd_ref,
               rq_ref, rk_ref, rv_ref, rg_ref, aq_ref, ak_ref, av_ref):
    x = x_ref[0]
    m = mod_ref[0, 0]
    h = _norm_mod(x, ng_ref[...], m[1:2], m[0:1]).astype(BF16)
    cos = cos_ref[...]
    sin = sin_ref[...]
    bd = bd_ref[...]

    def seg(i):
        return _dot(h, w_ref[:, i * SEG_W:(i + 1) * SEG_W])

    def headnorm(v, g):
        ms = _segsum(v * v, bd) * (1.0 / HEAD_DIM)
        return v * lax.rsqrt(ms + EPS) * g

    scale = HEAD_DIM ** -0.5
    rq_ref[0] = _rope(seg(0), cos, sin).astype(BF16)
    rk_ref[0] = _rope(seg(1) * scale, cos, sin).astype(BF16)
    rv_ref[0] = seg(2).astype(BF16)
    rg_ref[0] = seg(3)
    aq_ref[0] = (_rope(headnorm(seg(4), qkg_ref[0:1]), cos, sin) * (scale * LOG2_E)).astype(BF16)
    ak_ref[0] = _rope(headnorm(seg(5), qkg_ref[1:2]), cos, sin).astype(BF16)
    av_ref[0] = _dot(h, w_ref[:, 6 * SEG_W:]).astype(BF16)


def _row_spec(tm, d, lead=0):
    return pl.BlockSpec((1, tm, d), lambda b, i: (b, i + lead, 0))


def _mod_spec(d, n, seg_tiles):
    return pl.BlockSpec((1, 1, n, d), lambda b, i: (b, jnp.where(i >= seg_tiles, 1, 0), 0, 0))


def _const_spec(shape):
    nd = len(shape)
    return pl.BlockSpec(shape, lambda b, i: (0,) * nd)


def _in_proj(x, modtab, norm_g, w_ext, cos, sin, qkg, bd, tm, n_ctx):
    b, t, d = x.shape
    out_shape, out_specs = [], []
    for dt, width in ((BF16, SEG_W), (BF16, SEG_W), (BF16, SEG_W), (F32, SEG_W), (BF16, SEG_W), (BF16, SEG_W),
                      (BF16, ATT_KV_HEADS * 2 * HEAD_DIM)):
        out_shape.append(jax.ShapeDtypeStruct((b, t, width), dt))
        out_specs.append(_row_spec(tm, width))
    return pl.pallas_call(
        _in_kernel,
        grid=(b, t // tm),
        in_specs=[
            _row_spec(tm, d),
            _mod_spec(d, 6, n_ctx // tm),
            _const_spec((1, d)),
            _const_spec(w_ext.shape),
            pl.BlockSpec((tm, SEG_W), lambda b, i: (i, 0)),
            pl.BlockSpec((tm, SEG_W), lambda b, i: (i, 0)),
            _const_spec((2, SEG_W)),
            _const_spec((SEG_W, SEG_W)),
        ],
        out_specs=out_specs,
        out_shape=out_shape,
        compiler_params=_cparams(("parallel", "parallel")),
    )(x, modtab, norm_g, w_ext, cos, sin, qkg, bd)


def _ret_kernel(qf_ref, kf_ref, vf_ref, qb_ref, kb_ref, vb_ref, rate_ref, rate_h_ref, bd_ref,
                of_ref, ob_ref, sf_s, sb_s, dm_s, dq_s, dk_s, dc_s):
    s = pl.program_id(1)
    c = CHUNK

    @pl.when(s == 0)
    def _():
        sf_s[...] = jnp.zeros_like(sf_s)
        sb_s[...] = jnp.zeros_like(sb_s)
        pos = lax.broadcasted_iota(jnp.int32, (c, SEG_W), 0).astype(F32)
        lgf = -jnp.exp(rate_ref[0:1, :])
        lgb = -jnp.exp(rate_ref[1:2, :])
        dq_s[0] = jnp.exp((pos + 1.0) * lgf)
        dk_s[0] = jnp.exp((c - 1.0 - pos) * lgf)
        dc_s[0] = jnp.exp(float(c) * lgf)
        dq_s[1] = jnp.exp((c - pos) * lgb)
        dk_s[1] = jnp.exp(pos * lgb)
        dc_s[1] = jnp.exp(float(c) * lgb)
        ii = lax.broadcasted_iota(jnp.int32, (c, c), 0)
        jj = lax.broadcasted_iota(jnp.int32, (c, c), 1)
        for h in range(RET_HEADS):
            lf = -jnp.exp(rate_h_ref[0, h:h + 1, :])
            lb = -jnp.exp(rate_h_ref[1, h:h + 1, :])
            mf = ii >= jj
            mb = jj > ii
            dm_s[0, h] = jnp.where(mf, jnp.exp(jnp.where(mf, ii - jj, 0).astype(F32) * lf), 0.0)
            dm_s[1, h] = jnp.where(mb, jnp.exp(jnp.where(mb, jj - ii, 0).astype(F32) * lb), 0.0)

    bdm = bd_ref[...].astype(F32)
    head_of_lane = lax.broadcasted_iota(jnp.int32, (1, SEG_W // 2), 1) // HEAD_DIM

    def direction(d, q_ref, k_ref, v_ref, o_ref, st):
        q = q_ref[0]
        k = k_ref[0]
        v = v_ref[0]
        state = st[...]
        qd = (q.astype(F32) * dq_s[d]).astype(BF16)
        inter = _dot(qd, state.astype(BF16))
        for half in range(2):
            sl = slice(half * 256, half * 256 + 256)
            qh, kh, vh = q[:, sl], k[:, sl], v[:, sl]
            acc = inter[:, sl]
            for hh in range(4):
                msk = head_of_lane == hh
                sc = _dot_nt(jnp.where(msk, qh, jnp.zeros_like(qh)), kh) * dm_s[d, half * 4 + hh]
                acc = acc + _dot(sc.astype(BF16), jnp.where(msk, vh, jnp.zeros_like(vh)))
            o_ref[0, :, sl] = acc
        kd = (k.astype(F32) * dk_s[d]).astype(BF16)
        st[...] = state * dc_s[d] + bdm * _dot_tn(kd, v)

    direction(0, qf_ref, kf_ref, vf_ref, of_ref, sf_s)
    direction(1, qb_ref, kb_ref, vb_ref, ob_ref, sb_s)


def _retention(rq, rk, rv, rate, rate_h, bd, n_ctx):
    b, t, w = rq.shape
    nc = t // CHUNK
    lc = n_ctx // CHUNK

    def fwd(bi, s):
        return (bi, s, 0)

    def bwd(bi, s):
        return (bi, jnp.where(s < lc, lc - 1 - s, nc - 1 - (s - lc)), 0)

    blk = (1, CHUNK, w)
    return pl.pallas_call(
        _ret_kernel,
        grid=(b, nc),
        in_specs=[pl.BlockSpec(blk, fwd)] * 3 + [pl.BlockSpec(blk, bwd)] * 3 + [
            _const_spec((2, w)),
            _const_spec((2, RET_HEADS, CHUNK)),
            _const_spec((w, w)),
        ],
        out_specs=[pl.BlockSpec(blk, fwd), pl.BlockSpec(blk, bwd)],
        out_shape=[jax.ShapeDtypeStruct((b, t, w), F32)] * 2,
        scratch_shapes=[
            pltpu.VMEM((w, w), F32), pltpu.VMEM((w, w), F32),
            pltpu.VMEM((2, RET_HEADS, CHUNK, CHUNK), F32),
            pltpu.VMEM((2, CHUNK, w), F32), pltpu.VMEM((2, CHUNK, w), F32),
            pltpu.VMEM((2, 1, w), F32),
        ],
        compiler_params=_cparams(("parallel", "arbitrary")),
    )(rq, rk, rv, rq, rk, rv, rate, rate_h, bd)


ATT_ROW_BLOCK = 512


def _att_kernel(q_ref, k_ref, v_ref, o_ref, qs_s, m_s, acc_s, sa_s, sb_s, ma_s, mb_s, *, tq, tk, n_ctx, t):
    qi = pl.program_id(1)
    gw = ATT_GROUP * HEAD_DIM
    head_of_lane = lax.broadcasted_iota(jnp.int32, (1, gw), 1) // HEAD_DIM
    for h in range(ATT_HEADS):
        q = q_ref[0, :, (h // ATT_GROUP) * gw:(h // ATT_GROUP + 1) * gw]
        qs_s[h * tq:(h + 1) * tq, :] = jnp.where(head_of_lane == h % ATT_GROUP, q, jnp.zeros_like(q))
    m_s[...] = jnp.full(m_s.shape, -1e30, F32)
    acc_s[...] = jnp.zeros_like(acc_s)
    nk = jnp.where(qi * tq < n_ctx, n_ctx // tk, t // tk)
    vw = 2 * HEAD_DIM
    ones_lane = lax.broadcasted_iota(jnp.int32, (1, ATT_KV_HEADS * vw), 1) % vw >= HEAD_DIM
    rb = ATT_ROW_BLOCK

    blocks = [(slice(r * rb, (r + 1) * rb), (r * rb) // (ATT_GROUP * tq)) for r in range(ATT_HEADS * tq // rb)]

    def scores_into(j, dst, dst_max):
        kb = k_ref[0, pl.ds(pl.multiple_of(j * tk, tk), tk), :]
        for rows, g in blocks:
            sc = _dot_nt(qs_s[rows, :], kb[:, g * gw:(g + 1) * gw])
            dst[rows, :] = sc
            dst_max[rows, :] = jnp.broadcast_to(jnp.max(sc, axis=1, keepdims=True), (rb, vw))

    def consume(j, src, src_max):
        vb = v_ref[0, pl.ds(pl.multiple_of(j * tk, tk), tk), :]
        vext = jnp.where(ones_lane, jnp.ones_like(vb), vb)
        for rows, g in blocks:
            m_prev = m_s[rows, :]
            m_new = jnp.maximum(m_prev, src_max[rows, :])
            p = jnp.exp2(src[rows, :] - jnp.concatenate([m_new] * (tk // vw), axis=1)).astype(BF16)
            acc_s[rows, :] = jnp.exp2(m_prev - m_new) * acc_s[rows, :] + _dot(p, vext[:, g * vw:(g + 1) * vw])
            m_s[rows, :] = m_new

    scores_into(0, sa_s, ma_s)

    def body(i, carry):
        j = 2 * i
        scores_into(j + 1, sb_s, mb_s)
        consume(j, sa_s, ma_s)
        scores_into(jnp.minimum(j + 2, nk - 1), sa_s, ma_s)
        consume(j + 1, sb_s, mb_s)
        return carry

    lax.fori_loop(0, nk // 2, body, 0)

    @pl.when(nk % 2 == 1)
    def _():
        consume(nk - 1, sa_s, ma_s)

    low = lax.broadcasted_iota(jnp.int32, (tq, vw), 1) < HEAD_DIM
    for pair in range(ATT_HEADS // 2):
        a0 = acc_s[(2 * pair) * tq:(2 * pair + 1) * tq, :]
        a1 = acc_s[(2 * pair + 1) * tq:(2 * pair + 2) * tq, :]
        even = a0 / pltpu.roll(a0, HEAD_DIM, 1)
        odd = pltpu.roll(a1, HEAD_DIM, 1) / a1
        o_ref[0, :, pair * vw:(pair + 1) * vw] = jnp.where(low, even, odd).astype(BF16)


def _attention(aq, ak4, av2, tq, tk, n_ctx):
    b, t, w = aq.shape
    gw = ATT_GROUP * HEAD_DIM
    vw = 2 * HEAD_DIM
    rows = ATT_HEADS * tq
    return pl.pallas_call(
        functools.partial(_att_kernel, tq=tq, tk=tk, n_ctx=n_ctx, t=t),
        grid=(b, t // tq),
        in_specs=[
            pl.BlockSpec((1, tq, w), lambda bi, i: (bi, i, 0)),
            pl.BlockSpec((1, t, w), lambda bi, i: (bi, 0, 0)),
            pl.BlockSpec((1, t, ATT_KV_HEADS * vw), lambda bi, i: (bi, 0, 0)),
        ],
        out_specs=pl.BlockSpec((1, tq, w), lambda bi, i: (bi, i, 0)),
        out_shape=jax.ShapeDtypeStruct((b, t, w), BF16),
        scratch_shapes=[pltpu.VMEM((rows, gw), BF16), pltpu.VMEM((rows, vw), F32), pltpu.VMEM((rows, vw), F32),
                        pltpu.VMEM((rows, tk), F32), pltpu.VMEM((rows, tk), F32), pltpu.VMEM((rows, vw), F32),
                        pltpu.VMEM((rows, vw), F32)],
        compiler_params=_cparams(("parallel", "parallel")),
    )(aq, ak4, av2)


def _post_mixer(x, delta, m, ng2, wrh, wrl, x_ref, f_ref, aff_ref):
    xn = x + m[2:3] * delta
    x_ref[0] = xn
    f = _norm_mod(xn, ng2, m[4:5], m[3:4])
    fh, fl = _split(f)
    f_ref[0] = fh
    logits = _dot(fh, wrh) + _dot(fl, wrh) + _dot(fh, wrl)
    lane = lax.broadcasted_iota(jnp.int32, logits.shape, 1)
    logits = jnp.where(lane < N_EXPERTS, logits, -1e30)
    e = jnp.exp(logits - jnp.max(logits, axis=1, keepdims=True))
    aff_ref[0] = e / jnp.sum(e, axis=1, keepdims=True)


def _out_kernel(x_ref, of_ref, ob_ref, rg_ref, att_ref, mod_ref, gn_ref, w_ref, ng2_ref, wrh_ref, wrl_ref,
                bd_ref, xo_ref, f_ref, aff_ref):
    bd = bd_ref[...]
    o = of_ref[0] + ob_ref[0]
    mu = _segsum(o, bd) * (1.0 / HEAD_DIM)
    oc = o - mu
    var = _segsum(oc * oc, bd) * (1.0 / HEAD_DIM)
    r = _silu(rg_ref[0]) * (oc * lax.rsqrt(var + EPS) * gn_ref[...])
    y = _dot(r.astype(BF16), w_ref[0:RET_WIDTH, :]) + _dot(att_ref[0], w_ref[RET_WIDTH:, :])
    _post_mixer(x_ref[0], y, mod_ref[0, 0], ng2_ref[...], wrh_ref[...], wrl_ref[...], xo_ref, f_ref, aff_ref)


def _post_specs(b, t, d, tm):
    out_shape = [jax.ShapeDtypeStruct((b, t, d), F32), jax.ShapeDtypeStruct((b, t, d), BF16),
                 jax.ShapeDtypeStruct((b, t, ROUTER_LANES), F32)]
    out_specs = [_row_spec(tm, d), _row_spec(tm, d), _row_spec(tm, ROUTER_LANES)]
    return out_shape, out_specs


def _out_proj(x, of, ob, rg, att, modtab, gn_g, w_out, ng2, wrh, wrl, bd, tm, n_ctx):
    b, t, d = x.shape
    out_shape, out_specs = _post_specs(b, t, d, tm)
    return pl.pallas_call(
        _out_kernel,
        grid=(b, t // tm),
        in_specs=[
            _row_spec(tm, d), _row_spec(tm, SEG_W), _row_spec(tm, SEG_W), _row_spec(tm, SEG_W),
            _row_spec(tm, SEG_W), _mod_spec(d, 6, n_ctx // tm), _const_spec((1, SEG_W)),
            _const_spec(w_out.shape), _const_spec((1, d)), _const_spec(wrh.shape), _const_spec(wrl.shape),
            _const_spec((SEG_W, SEG_W)),
        ],
        out_specs=out_specs,
        out_shape=out_shape,
        compiler_params=_cparams(("parallel", "parallel")),
    )(x, of, ob, rg, att, modtab, gn_g, w_out, ng2, wrh, wrl, bd)


def _s5_kernel(u_ref, tm_ref, wz_ref, wo_ref, ar_ref, ai_ref, d_ref, z_ref,
               zr_s, zi_s, xr_s, xi_s, zs_s, *, n_sub, n_sub_ctx):
    s_n, j_sub, gb, half = n_sub, S5_SUB, S5_BLOCK_GROUPS, 2 * S5_STATE
    uf = jnp.concatenate([u_ref[0, pl.ds(m, s_n, stride=j_sub), :] for m in range(j_sub)], axis=1)
    lhs = uf.astype(BF16)
    for g in range(gb):
        zz = _dot(lhs, wz_ref[0, 0, g])
        zr_s[pl.ds(g, s_n, stride=gb), :] = zz[:, :half]
        zi_s[pl.ds(g, s_n, stride=gb), :] = zz[:, half:]
    fwd_lane = lax.broadcasted_iota(jnp.int32, (gb, half), 1) < S5_STATE
    ar = ar_ref[0, 0]
    ai = ai_ref[0, 0]

    def body(i, carry):
        xr, xi = carry
        sb = jnp.where(i < n_sub_ctx, n_sub_ctx - 1 - i, s_n - 1 - (i - n_sub_ctx))
        fwd = pl.ds(pl.multiple_of(i * gb, gb), gb)
        bwd = pl.ds(pl.multiple_of(sb * gb, gb), gb)
        xr_s[fwd, 0:S5_STATE] = xr[:, 0:S5_STATE]
        xi_s[fwd, 0:S5_STATE] = xi[:, 0:S5_STATE]
        xr_s[bwd, S5_STATE:half] = xr[:, S5_STATE:half]
        xi_s[bwd, S5_STATE:half] = xi[:, S5_STATE:half]
        zr = jnp.where(fwd_lane, zr_s[fwd, :], zr_s[bwd, :])
        zi = jnp.where(fwd_lane, zi_s[fwd, :], zi_s[bwd, :])
        return ar * xr - ai * xi + zr, ar * xi + ai * xr + zi

    zero = jnp.zeros((gb, half), F32)
    lax.fori_loop(0, s_n, body, (zero, zero), unroll=S5_SCAN_UNROLL)
    y = _dot(lhs, tm_ref[0, 0]) + d_ref[0, 0] * uf
    for g in range(gb):
        rows = pl.ds(g, s_n, stride=gb)
        xp = jnp.concatenate([xr_s[rows, :], xi_s[rows, :]], axis=1).astype(BF16)
        y = y + _dot(xp, wo_ref[0, 0, g])
    z = _gelu_tanh(y)
    w = z_ref.shape[-1]
    for n in range(j_sub):
        zs_s[pl.ds(n, s_n, stride=j_sub), :] = z[:, n * w:(n + 1) * w]
    z_ref[0] = zs_s[...].astype(BF16)


def _s5_scan(li, u, tmat, wz, wo, ar, ai, dt, n_ctx):
    b, t, d = u.shape
    cw = S5_BLOCK_GROUPS * S5_GROUP
    n_sub = t // S5_SUB
    lw = S5_SUB * cw
    sw = 4 * S5_STATE
    gb = S5_BLOCK_GROUPS
    once = pl.Buffered(1)
    return pl.pallas_call(
        functools.partial(_s5_kernel, n_sub=n_sub, n_sub_ctx=n_ctx // S5_SUB),
        grid=(d // cw, b),
        in_specs=[
            pl.BlockSpec((1, t, cw), lambda c, bi: (bi, 0, c)),
            pl.BlockSpec((1, 1, lw, lw), lambda c, bi: (li, c, 0, 0), pipeline_mode=once),
            pl.BlockSpec((1, 1, gb, lw, sw), lambda c, bi: (li, c, 0, 0, 0), pipeline_mode=once),
            pl.BlockSpec((1, 1, gb, sw, lw), lambda c, bi: (li, c, 0, 0, 0), pipeline_mode=once),
            pl.BlockSpec((1, 1, gb, 2 * S5_STATE), lambda c, bi: (li, c, 0, 0)),
            pl.BlockSpec((1, 1, gb, 2 * S5_STATE), lambda c, bi: (li, c, 0, 0)),
            pl.BlockSpec((1, 1, 1, lw), lambda c, bi: (li, c, 0, 0)),
        ],
        out_specs=pl.BlockSpec((1, t, cw), lambda c, bi: (bi, 0, c)),
        out_shape=jax.ShapeDtypeStruct((b, t, d), BF16),
        scratch_shapes=[pltpu.VMEM((gb * n_sub, 2 * S5_STATE), F32)] * 4 + [pltpu.VMEM((t, cw), F32)],
        compiler_params=_cparams(("parallel", "parallel")),
    )(u, tmat, wz, wo, ar, ai, dt)


def _s5_tables(a_re, a_im, log_dt, b_re, b_im, c_re, c_im, d_skip):
    hp = lax.Precision.HIGHEST
    j_sub = S5_SUB
    gb = S5_BLOCK_GROUPS
    lam_re = jnp.minimum(a_re, -1e-4)
    lam_im = a_im
    dt = jnp.exp(log_dt)[..., None]
    mag = jnp.exp(lam_re * dt)
    bar_re = mag * jnp.cos(lam_im * dt)
    bar_im = mag * jnp.sin(lam_im * dt)
    den = lam_re * lam_re + lam_im * lam_im
    nr, ni = bar_re - 1.0, bar_im
    k_re = (nr * lam_re + ni * lam_im) / den
    k_im = (ni * lam_re - nr * lam_im) / den
    bb_re = k_re[..., None] * b_re[None] - k_im[..., None] * b_im[None]
    bb_im = k_re[..., None] * b_im[None] + k_im[..., None] * b_re[None]
    jj = jnp.arange(j_sub + 1, dtype=F32)[:, None, None, None]
    pmag = jnp.exp(jj * (lam_re * dt)[None])
    pw_re = pmag * jnp.cos(jj * (lam_im * dt)[None])
    pw_im = pmag * jnp.sin(jj * (lam_im * dt)[None])
    pb_re = pw_re[..., None] * bb_re[None] - pw_im[..., None] * bb_im[None]
    pb_im = pw_re[..., None] * bb_im[None] + pw_im[..., None] * bb_re[None]
    cp_re = c_re[None] * pw_re[:, :, :, None, :] - c_im[None] * pw_im[:, :, :, None, :]
    cp_im = c_re[None] * pw_im[:, :, :, None, :] + c_im[None] * pw_re[:, :, :, None, :]
    kern = (jnp.einsum('dgcp,jdgpk->jdgkc', c_re, pb_re[:j_sub], precision=hp)
            - jnp.einsum('dgcp,jdgpk->jdgkc', c_im, pb_im[:j_sub], precision=hp))
    groups = a_re.shape[1]
    nblk = groups // gb
    k_w = S5_GROUP
    p_w = S5_STATE
    lw = j_sub * gb * k_w
    eye = jnp.eye(gb, dtype=F32)
    place = (jnp.arange(gb * k_w)[None, None, :] == (jnp.arange(gb)[:, None, None] * k_w
                                                      + jnp.arange(k_w)[None, :, None])).astype(F32)
    blk = lambda a: a.reshape(a.shape[0], nblk, gb, *a.shape[2:])
    pos = jnp.arange(j_sub)
    lag = pos[None, :] - pos[:, None]
    kf = jnp.where((lag >= 0)[:, :, None, None, None], kern[jnp.clip(lag, 0, j_sub - 1), 0], 0.0)
    kb = jnp.where((lag <= 0)[:, :, None, None, None], kern[jnp.clip(-lag, 0, j_sub - 1), 1], 0.0)
    kfull = (kf + kb).reshape(j_sub, j_sub, nblk, gb, k_w, k_w)
    tmat = jnp.einsum('mnBgkc,gcd->Bmgknd', kfull, place, precision=hp).reshape(nblk, lw, lw)
    rev = pos[::-1]
    zc = jnp.concatenate([pb_re[rev, 0], pb_re[pos, 1], pb_im[rev, 0], pb_im[pos, 1]], axis=2)
    wz = jnp.einsum('mBgqk,gh->Bgmhkq', blk(zc), eye).reshape(nblk, gb, lw, 4 * p_w)
    def orows(cp, d, order):
        return jnp.einsum('nBgcp,gcd->Bgpnd', blk(cp[order, d]), place, precision=hp).reshape(nblk, gb, p_w, lw)
    wo = jnp.concatenate([orows(cp_re, 0, pos + 1), orows(cp_re, 1, j_sub - pos),
                          -orows(cp_im, 0, pos + 1), -orows(cp_im, 1, j_sub - pos)], axis=2)
    ar = jnp.concatenate([pw_re[j_sub, 0], pw_re[j_sub, 1]], axis=-1).reshape(nblk, gb, 2 * p_w)
    ai = jnp.concatenate([pw_im[j_sub, 0], pw_im[j_sub, 1]], axis=-1).reshape(nblk, gb, 2 * p_w)
    dt_tile = jnp.tile(d_skip.reshape(nblk, 1, gb * k_w), (1, 1, j_sub))
    return tmat.astype(BF16), wz.astype(BF16), wo.astype(BF16), ar, ai, dt_tile


def _glu_kernel(x_ref, z_ref, mod_ref, w_ref, b_ref, ng2_ref, wrh_ref, wrl_ref, xo_ref, f_ref, aff_ref):
    d = x_ref.shape[-1]
    ag = _dot(z_ref[0], w_ref[...]) + b_ref[...]
    delta = ag[:, :d] * _sigmoid(ag[:, d:])
    _post_mixer(x_ref[0], delta, mod_ref[0, 0], ng2_ref[...], wrh_ref[...], wrl_ref[...], xo_ref, f_ref, aff_ref)


def _glu(x, z, modtab, glu_w, glu_b, ng2, wrh, wrl, tm, n_ctx):
    b, t, d = x.shape
    out_shape, out_specs = _post_specs(b, t, d, tm)
    return pl.pallas_call(
        _glu_kernel,
        grid=(b, t // tm),
        in_specs=[_row_spec(tm, d), _row_spec(tm, d), _mod_spec(d, 6, n_ctx // tm),
                  _const_spec(glu_w.shape), _const_spec((1, 2 * d)), _const_spec((1, d)),
                  _const_spec(wrh.shape), _const_spec(wrl.shape)],
        out_specs=out_specs,
        out_shape=out_shape,
        compiler_params=_cparams(("parallel", "parallel")),
    )(x, z, modtab, glu_w, glu_b, ng2, wrh, wrl)


def _cumsum_lanes(v):
    n = v.shape[-1]
    lane = lax.broadcasted_iota(jnp.int32, v.shape, 1)
    sh = 1
    while sh < n:
        v = v + jnp.where(lane >= sh, pltpu.roll(v, sh, 1), 0.0)
        sh *= 2
    return v


def _select_segment(aff, cap, base):
    bits = pltpu.bitcast(aff, jnp.int32)
    thr = jnp.zeros((aff.shape[0], 1), jnp.int32)
    for bit in range(30, -1, -1):
        cand = thr | (1 << bit)
        cnt = jnp.sum(jnp.where(bits >= cand, 1.0, 0.0), axis=1, keepdims=True)
        thr = jnp.where(cnt >= cap, cand, thr)
    gt = bits > thr
    eq = bits == thr
    need = cap - jnp.sum(jnp.where(gt, 1.0, 0.0), axis=1, keepdims=True)
    eqf = jnp.where(eq, 1.0, 0.0)
    eq_before = _cumsum_lanes(eqf) - eqf
    sel = jnp.where(gt, 1.0, jnp.where(eq, jnp.where(eq_before < need, 1.0, 0.0), 0.0))
    pos = _cumsum_lanes(sel) - sel
    return jnp.where(sel > 0.0, pos.astype(jnp.int32) + base, -1)


def _select_kernel(aff_ref, slot_ref, *, n_ctx, cap_ctx, cap_lat):
    slot_ref[0, :, 0:n_ctx] = _select_segment(aff_ref[0, :, 0:n_ctx], cap_ctx, 0)
    slot_ref[0, :, n_ctx:] = _select_segment(aff_ref[0, :, n_ctx:], cap_lat, cap_ctx)


def _select(aff_t, n_ctx, cap_ctx, cap_lat):
    b, e, t = aff_t.shape
    return pl.pallas_call(
        functools.partial(_select_kernel, n_ctx=n_ctx, cap_ctx=cap_ctx, cap_lat=cap_lat),
        grid=(b,),
        in_specs=[pl.BlockSpec((1, e, t), lambda i: (i, 0, 0))],
        out_specs=pl.BlockSpec((1, e, t), lambda i: (i, 0, 0)),
        out_shape=jax.ShapeDtypeStruct((b, e, t), jnp.int32),
        compiler_params=_cparams(("parallel",)),
    )(aff_t)


def _moe_window(a_ref, base, e, k, rows, w):
    nominal = a_ref[base + e] + k * w
    return nominal, jnp.minimum(nominal, rows - w)


def _moe_hits(slot_row, nominal, actual, w):
    rid = lax.broadcasted_iota(jnp.int32, (w, slot_row.shape[-1]), 0) + actual
    return jnp.where(slot_row >= nominal, slot_row, -1) == rid


def _moe_gather_kernel(a_ref, nr_ref, slot_ref, aff_ref, f_ref, xs_ref, gate_ref, *, rows, w):
    b = pl.program_id(0)
    j = pl.program_id(1)
    n_e = slot_ref.shape[2]
    step = b * pl.num_programs(1) + j

    @pl.when(j == 0)
    def _():
        xs_ref[...] = jnp.zeros_like(xs_ref)
        gate_ref[...] = jnp.zeros_like(gate_ref)

    fj = f_ref[0]

    def one_round(k, carry):
        for e in range(n_e):
            nominal, actual = _moe_window(a_ref, step * n_e, e, k, rows, w)
            hit = _moe_hits(slot_ref[0, 0, e:e + 1, :], nominal, actual, w)
            off = pl.multiple_of(e * rows + actual, 16)
            xs_ref[0, pl.ds(off, w), :] += _dot(jnp.where(hit, 1.0, 0.0).astype(BF16), fj).astype(BF16)
            gate_ref[0, pl.ds(off, w), :] += jnp.sum(jnp.where(hit, aff_ref[0, 0, e:e + 1, :], 0.0),
                                                     axis=1, keepdims=True)
        return carry

    lax.fori_loop(0, nr_ref[step], one_round, 0)


def _moe_ffn_kernel(xs_ref, gate_ref, w1_ref, w3_ref, w2_ref, y_ref, ys_s, *, n_ft):
    ft = pl.program_id(2)
    nb, rows, d = xs_ref.shape
    xs = xs_ref[...].reshape(nb * rows, d)
    hid = (_silu(_dot(xs, w1_ref[0, 0].astype(BF16))) * _dot(xs, w3_ref[0, 0].astype(BF16))).astype(BF16)
    part = _dot(hid, w2_ref[0, 0].astype(BF16))

    @pl.when(ft == 0)
    def _():
        ys_s[...] = part

    @pl.when(ft > 0)
    def _():
        ys_s[...] += part

    @pl.when(ft == n_ft - 1)
    def _():
        y_ref[...] = (ys_s[...] * gate_ref[...].reshape(nb * rows, 1)).astype(BF16).reshape(nb, rows, d)


def _moe_combine_kernel(a_ref, nr_ref, slot_ref, y_ref, x_ref, mod_ref, *rest, rows, w, nb, lead, tail):
    o_ref = rest[-1] if tail != "s5" else rest[-2]
    b = pl.program_id(0)
    n_e = slot_ref.shape[2]
    step = b * nb + pl.program_id(1) + lead

    def one_round(k, acc):
        onehots, ys = [], []
        for e in range(n_e):
            nominal, actual = _moe_window(a_ref, step * n_e, e, k, rows, w)
            hit = _moe_hits(slot_ref[0, 0, e:e + 1, :], nominal, actual, w)
            onehots.append(jnp.where(hit, 1.0, 0.0).astype(BF16))
            ys.append(y_ref[0, pl.ds(pl.multiple_of(e * rows + actual, 16), w), :])
        return acc + _dot_tn(jnp.concatenate(onehots, axis=0), jnp.concatenate(ys, axis=0))

    acc = lax.fori_loop(0, nr_ref[step], one_round, jnp.zeros(o_ref.shape[1:], F32))
    xn = x_ref[0] + mod_ref[0, 0, 5:6, :] * acc
    if tail == "final":
        xn = xn * lax.rsqrt(jnp.mean(xn * xn, axis=-1, keepdims=True) + EPS) * rest[0][...]
    o_ref[0] = xn
    if tail == "s5":
        m = rest[0][0, 0]
        rest[3][0] = _norm_mod(xn, rest[1][...], m[1:2], m[0:1])


def _moe_windows(slot, tb, w):
    b, e, t = slot.shape
    nb = t // tb
    cnt = jnp.sum((slot >= 0).reshape(b, e, nb, tb), axis=-1, dtype=jnp.int32)
    start = jnp.cumsum(cnt, axis=-1) - cnt
    a0 = (start // 16) * 16
    rounds = jnp.maximum(jnp.max((start + cnt - a0 + w - 1) // w, axis=1), 1)
    return jnp.swapaxes(a0, 1, 2).reshape(-1).astype(jnp.int32), rounds.reshape(-1).astype(jnp.int32)


def _route_and_moe(layer, x, aff, f, modtab, w1, w3, w2, n_ctx, tb, tf, final_g=None, next_s5=None):
    b, t, d = f.shape
    n_e = w1.shape[1]
    ff = w1.shape[-1]
    nb = t // tb
    aff_t = jnp.swapaxes(aff[:, :, :N_EXPERTS], 1, 2)
    cap_ctx = (CAPACITY_FACTOR * n_ctx) // N_EXPERTS
    cap_lat = (CAPACITY_FACTOR * (t - n_ctx)) // N_EXPERTS
    rows = cap_ctx + cap_lat
    w = min(MOE_WINDOW, rows)
    slot = _select(aff_t, n_ctx, cap_ctx, cap_lat)
    a0, rounds = _moe_windows(slot, tb, w)
    by_block = lambda v: jnp.swapaxes(v.reshape(b, n_e, nb, tb), 1, 2)
    slot_b, aff_b = by_block(slot), by_block(aff_t)
    blk_spec = pl.BlockSpec((1, 1, n_e, tb), lambda bi, j, *_: (bi, j, 0, 0))
    whole = lambda width: pl.BlockSpec((1, n_e * rows, width), lambda bi, j, *_: (bi, 0, 0))
    xs, gate = pl.pallas_call(
        functools.partial(_moe_gather_kernel, rows=rows, w=w),
        grid_spec=pltpu.PrefetchScalarGridSpec(
            num_scalar_prefetch=2, grid=(b, nb),
            in_specs=[blk_spec, blk_spec, pl.BlockSpec((1, tb, d), lambda bi, j, *_: (bi, j, 0))],
            out_specs=[whole(d), whole(1)]),
        out_shape=[jax.ShapeDtypeStruct((b, n_e * rows, d), BF16), jax.ShapeDtypeStruct((b, n_e * rows, 1), F32)],
        compiler_params=_cparams(("parallel", "arbitrary")),
    )(a0, rounds, slot_b, aff_b, f)
    n_ft = ff // tf
    sb = MOE_FFN_SAMPLES
    y = pl.pallas_call(
        functools.partial(_moe_ffn_kernel, n_ft=n_ft),
        grid=(b // sb, n_e, n_ft),
        in_specs=[
            pl.BlockSpec((sb, rows, d), lambda bi, e, k: (bi, e, 0)),
            pl.BlockSpec((sb, rows, 1), lambda bi, e, k: (bi, e, 0)),
            pl.BlockSpec((1, 1, d, tf), lambda bi, e, k: (layer, e, 0, k)),
            pl.BlockSpec((1, 1, d, tf), lambda bi, e, k: (layer, e, 0, k)),
            pl.BlockSpec((1, 1, tf, d), lambda bi, e, k: (layer, e, k, 0)),
        ],
        out_specs=pl.BlockSpec((sb, rows, d), lambda bi, e, k: (bi, e, 0)),
        out_shape=jax.ShapeDtypeStruct((b, n_e * rows, d), BF16),
        scratch_shapes=[pltpu.VMEM((sb * rows, d), F32)],
        compiler_params=_cparams(("parallel", "parallel", "arbitrary")),
    )(xs, gate, w1, w3, w2)
    seg_blocks = n_ctx // tb
    lead = 0 if final_g is None else seg_blocks
    seg_of = lambda bi, j, *_: (bi, jnp.where(j + lead >= seg_blocks, 1, 0), 0, 0)
    gain_spec = pl.BlockSpec((1, d), lambda bi, j, *_: (0, 0))
    out_spec = pl.BlockSpec((1, tb, d), lambda bi, j, *_: (bi, j, 0))
    out_sds = jax.ShapeDtypeStruct((b, t - lead * tb, d), F32)
    tail, extra_args, extra_specs, out_specs, out_shape = "plain", (), [], out_spec, out_sds
    if final_g is not None:
        tail, extra_args, extra_specs = "final", (final_g,), [gain_spec]
    elif next_s5 is not None:
        tail, extra_args, extra_specs = "s5", tuple(next_s5), [pl.BlockSpec((1, 1, 6, d), seg_of), gain_spec]
        out_specs, out_shape = [out_spec, out_spec], [out_sds, out_sds]
    return pl.pallas_call(
        functools.partial(_moe_combine_kernel, rows=rows, w=w, nb=nb, lead=lead, tail=tail),
        grid_spec=pltpu.PrefetchScalarGridSpec(
            num_scalar_prefetch=2, grid=(b, nb - lead),
            in_specs=[pl.BlockSpec((1, 1, n_e, tb), lambda bi, j, *_: (bi, j + lead, 0, 0)), whole(d),
                      pl.BlockSpec((1, tb, d), lambda bi, j, *_: (bi, j + lead, 0)),
                      pl.BlockSpec((1, 1, 6, d), seg_of)]
            + extra_specs,
            out_specs=out_specs),
        out_shape=out_shape,
        compiler_params=_cparams(("parallel", "parallel")),
    )(a0, rounds, slot_b, y, x, modtab, *extra_args)


def _rope_tables(n_ctx, n_lat):
    n_rows = n_lat // GRID_W
    row = jnp.repeat(jnp.arange(n_rows, dtype=F32), GRID_W)
    col = jnp.tile(jnp.arange(GRID_W, dtype=F32), n_rows)
    inv = ROPE_THETA ** (-jnp.arange(0, ROPE_AXIS_DIM, 2, dtype=F32) / ROPE_AXIS_DIM)
    ang_r = row[:, None] * inv[None, :]
    ang_c = col[:, None] * inv[None, :]
    cos = jnp.concatenate([jnp.cos(ang_r), jnp.cos(ang_r), jnp.cos(ang_c), jnp.cos(ang_c)], axis=-1)
    sin = jnp.concatenate([-jnp.sin(ang_r), jnp.sin(ang_r), -jnp.sin(ang_c), jnp.sin(ang_c)], axis=-1)
    cos = jnp.concatenate([jnp.ones((n_ctx, HEAD_DIM), F32), cos], axis=0)
    sin = jnp.concatenate([jnp.zeros((n_ctx, HEAD_DIM), F32), sin], axis=0)
    reps = SEG_W // HEAD_DIM
    return jnp.tile(cos, (1, reps)), jnp.tile(sin, (1, reps))


def _extend_in_proj(w_in):
    base = 4 * RET_WIDTH + ATT_WIDTH
    ak = w_in[:, base:base + ATT_KV_WIDTH].reshape(-1, ATT_KV_HEADS, 1, HEAD_DIM)
    av = w_in[:, base + ATT_KV_WIDTH:].reshape(-1, ATT_KV_HEADS, 1, HEAD_DIM)
    rep = lambda w, n: jnp.broadcast_to(w, (w.shape[0], ATT_KV_HEADS, n, HEAD_DIM)).reshape(w.shape[0], -1)
    return jnp.concatenate([w_in[:, :base], rep(ak, ATT_GROUP), rep(av, 2)], axis=-1)


def kernel(x, c, ctx, c_ctx, mod_w, mod_b, norm_g, mix_in_w, mix_out_w, ret_log_rate, ret_gn_g, qk_norm_g,
           s5_a_re, s5_a_im, s5_log_dt, s5_b_re, s5_b_im, s5_c_re, s5_c_im, s5_d, s5_glu_w, s5_glu_b,
           moe_router_w, moe_w1, moe_w3, moe_w2, final_norm_g):
    batch, n_lat, d = x.shape
    n_ctx = ctx.shape[1]
    t = n_ctx + n_lat
    depth = mod_w.shape[0]
    assert batch == 8 and d % 128 == 0 and n_ctx % CHUNK == 0 and n_lat % CHUNK == 0
    tm = 256 if n_ctx % 256 == 0 else 128
    tb = tm
    tf = 1024

    xs = jnp.concatenate([ctx, x], axis=1)
    rows = 16
    cond = jnp.zeros((rows, d), F32).at[:batch].set(c).at[batch].set(c_ctx)
    mod_all = _modulation(cond, mod_w, mod_b)
    cos, sin = _rope_tables(n_ctx, n_lat)
    lane_head = jnp.arange(SEG_W) // HEAD_DIM
    bd = (lane_head[:, None] == lane_head[None, :]).astype(BF16)
    w1, w3, w2 = moe_w1, moe_w3, moe_w2
    s5_tabs = jax.vmap(_s5_tables)(s5_a_re, s5_a_im, s5_log_dt, s5_b_re, s5_b_im, s5_c_re, s5_c_im, s5_d)

    def modtab_of(layer):
        m = mod_all[layer]
        lat = m[:batch].reshape(batch, 1, 6, d)
        cx = jnp.broadcast_to(m[batch].reshape(1, 1, 6, d), (batch, 1, 6, d))
        return jnp.concatenate([cx, lat], axis=1)

    u = None
    for layer in range(depth):
        modtab = modtab_of(layer)
        ng1 = norm_g[layer, 0].reshape(1, d)
        ng2 = norm_g[layer, 1].reshape(1, d)
        wr = jnp.zeros((d, ROUTER_LANES), F32).at[:, :N_EXPERTS].set(moe_router_w[layer])
        wrh = wr.astype(BF16)
        wrl = (wr - wrh.astype(F32)).astype(BF16)
        i = layer // 2
        if layer % 2 == 0:
            w_ext = _extend_in_proj(mix_in_w[i]).astype(BF16)
            reps = SEG_W // HEAD_DIM
            qkg = jnp.stack([jnp.tile(qk_norm_g[i, 0], reps), jnp.tile(qk_norm_g[i, 1], reps)])
            rq, rk, rv, rg, aq, ak4, av2 = _in_proj(xs, modtab, ng1, w_ext, cos, sin, qkg, bd, tm, n_ctx)
            rate = jnp.repeat(ret_log_rate[i], HEAD_DIM, axis=-1)
            rate_h = jnp.broadcast_to(ret_log_rate[i][:, :, None], (2, RET_HEADS, CHUNK))
            o_f, o_b = _retention(rq, rk, rv, rate, rate_h, bd, n_ctx)
            att = _attention(aq, ak4, av2, tm, tm, n_ctx)
            xs, f, aff = _out_proj(xs, o_f, o_b, rg, att, modtab, ret_gn_g[i].reshape(1, SEG_W),
                                   mix_out_w[i].astype(BF16), ng2, wrh, wrl, bd, tm, n_ctx)
        else:
            z = _s5_scan(i, u, *s5_tabs, n_ctx)
            xs, f, aff = _glu(xs, z, modtab, s5_glu_w[i].astype(BF16), s5_glu_b[i].reshape(1, 2 * d),
                              ng2, wrh, wrl, tm, n_ctx)
        final_g = final_norm_g.reshape(1, d) if layer == depth - 1 else None
        next_s5 = None
        if layer + 1 < depth and (layer + 1) % 2 == 1:
            next_s5 = (modtab_of(layer + 1), norm_g[layer + 1, 0].reshape(1, d))
        xs = _route_and_moe(layer, xs, aff, f, modtab, w1, w3, w2, n_ctx, tb, tf, final_g, next_s5)
        u = None
        if next_s5 is not None:
            xs, u = xs
    return xs
```

```python
import functools
import math

import jax
import jax.numpy as jnp
from jax import lax
from jax.experimental import pallas as pl
from jax.experimental.pallas import tpu as pltpu

F32 = jnp.float32
BF16 = jnp.bfloat16

GRID_W = 64
HEAD_DIM = 64
RET_HEADS = 8
ATT_HEADS = 8
ATT_KV_HEADS = 2
ATT_GROUP = ATT_HEADS // ATT_KV_HEADS
RET_WIDTH = RET_HEADS * HEAD_DIM
ATT_WIDTH = ATT_HEADS * HEAD_DIM
ATT_KV_WIDTH = ATT_KV_HEADS * HEAD_DIM
CHUNK = 128
ROPE_THETA = 10000.0
ROPE_AXIS_DIM = HEAD_DIM // 2
S5_GROUP = 16
S5_STATE = 64
S5_SUB = 8
S5_BLOCK_GROUPS = 8
S5_SCAN_UNROLL = 4
N_EXPERTS = 16
CAPACITY_FACTOR = 2
EPS = 1e-6
SEG_W = 512
LOG2_E = math.log2(math.e)
ROUTER_LANES = 128
MOE_WINDOW = 64
MOE_FFN_SAMPLES = 2

V7X_VMEM_BYTES = 64 * 1024 * 1024
VMEM_LIMIT = V7X_VMEM_BYTES - 8 * 1024 * 1024


def _cparams(sem):
    return pltpu.CompilerParams(dimension_semantics=sem, vmem_limit_bytes=VMEM_LIMIT)


def _dot(a, b):
    return jnp.dot(a, b, preferred_element_type=F32)


def _dot_nt(a, b):
    return lax.dot_general(a, b, (((1,), (1,)), ((), ())), preferred_element_type=F32)


def _dot_tn(a, b):
    return lax.dot_general(a, b, (((0,), (0,)), ((), ())), preferred_element_type=F32)


def _split(v):
    hi = v.astype(BF16)
    lo = (v - hi.astype(F32)).astype(BF16)
    return hi, lo


def _segsum(v, bd):
    return _dot(v.astype(BF16), bd)


def _sigmoid(v):
    return 1.0 / (1.0 + jnp.exp(-v))


def _silu(v):
    return v * _sigmoid(v)


def _gelu_tanh(v):
    c = math.sqrt(2.0 / math.pi)
    return 0.5 * v * (1.0 + jnp.tanh(c * (v + 0.044715 * (v * v * v))))


def _norm_mod(x, g, scale, shift):
    ms = jnp.mean(x * x, axis=-1, keepdims=True)
    return x * lax.rsqrt(ms + EPS) * g * (1.0 + scale) + shift


def _rope(x, cos, sin_signed):
    w = x.shape[-1]
    lane = lax.broadcasted_iota(jnp.int32, x.shape, 1)
    first = (lane % 32) < 16
    partner = jnp.where(first, pltpu.roll(x, w - 16, 1), pltpu.roll(x, 16, 1))
    return x * cos + partner * sin_signed


def _mod_kernel(c_ref, w_ref, b_ref, o_ref):
    s = _silu(c_ref[...]).astype(BF16)
    o_ref[0] = _dot(s, w_ref[0].astype(BF16)) + b_ref[0]


def _modulation(cond, mod_w, mod_b):
    depth, d, n6 = mod_w.shape
    rows = cond.shape[0]
    tn = n6 // 6
    return pl.pallas_call(
        _mod_kernel,
        grid=(depth, n6 // tn),
        in_specs=[
            pl.BlockSpec((rows, d), lambda l, j: (0, 0)),
            pl.BlockSpec((1, d, tn), lambda l, j: (l, 0, j)),
            pl.BlockSpec((1, 1, tn), lambda l, j: (l, 0, j)),
        ],
        out_specs=pl.BlockSpec((1, rows, tn), lambda l, j: (l, 0, j)),
        out_shape=jax.ShapeDtypeStruct((depth, rows, n6), F32),
        compiler_params=_cparams(("parallel", "parallel")),
    )(cond, mod_w, mod_b.reshape(depth, 1, n6))


def _in_kernel(x_ref, mod_ref, ng_ref, w_ref, cos_ref, sin_ref, qkg_ref, bd_ref,
               rq_ref, rk_ref, rv_ref, rg_ref, aq_ref, ak_ref, av_ref):
    x = x_ref[0]
    m = mod_ref[0, 0]
    h = _norm_mod(x, ng_ref[...], m[1:2], m[0:1]).astype(BF16)
    cos = cos_ref[...]
    sin = sin_ref[...]
    bd = bd_ref[...]

    def seg(i):
        return _dot(h, w_ref[:, i * SEG_W:(i + 1) * SEG_W])

    def headnorm(v, g):
        ms = _segsum(v * v, bd) * (1.0 / HEAD_DIM)
        return v * lax.rsqrt(ms + EPS) * g

    scale = HEAD_DIM ** -0.5
    rq_ref[0] = _rope(seg(0), cos, sin).astype(BF16)
    rk_ref[0] = _rope(seg(1) * scale, cos, sin).astype(BF16)
    rv_ref[0] = seg(2).astype(BF16)
    rg_ref[0] = seg(3)
    aq_ref[0] = (_rope(headnorm(seg(4), qkg_ref[0:1]), cos, sin) * (scale * LOG2_E)).astype(BF16)
    ak_ref[0] = _rope(headnorm(seg(5), qkg_ref[1:2]), cos, sin).astype(BF16)
    av_ref[0] = _dot(h, w_ref[:, 6 * SEG_W:]).astype(BF16)


def _row_spec(tm, d, lead=0):
    return pl.BlockSpec((1, tm, d), lambda b, i: (b, i + lead, 0))


def _mod_spec(d, n, seg_tiles):
    return pl.BlockSpec((1, 1, n, d), lambda b, i: (b, jnp.where(i >= seg_tiles, 1, 0), 0, 0))


def _const_spec(shape):
    nd = len(shape)
    return pl.BlockSpec(shape, lambda b, i: (0,) * nd)


def _in_proj(x, modtab, norm_g, w_ext, cos, sin, qkg, bd, tm, n_ctx):
    b, t, d = x.shape
    out_shape, out_specs = [], []
    for dt, width in ((BF16, SEG_W), (BF16, SEG_W), (BF16, SEG_W), (F32, SEG_W), (BF16, SEG_W), (BF16, SEG_W),
                      (BF16, ATT_KV_HEADS * 2 * HEAD_DIM)):
        out_shape.append(jax.ShapeDtypeStruct((b, t, width), dt))
        out_specs.append(_row_spec(tm, width))
    return pl.pallas_call(
        _in_kernel,
        grid=(b, t // tm),
        in_specs=[
            _row_spec(tm, d),
            _mod_spec(d, 6, n_ctx // tm),
            _const_spec((1, d)),
            _const_spec(w_ext.shape),
            pl.BlockSpec((tm, SEG_W), lambda b, i: (i, 0)),
            pl.BlockSpec((tm, SEG_W), lambda b, i: (i, 0)),
            _const_spec((2, SEG_W)),
            _const_spec((SEG_W, SEG_W)),
        ],
        out_specs=out_specs,
        out_shape=out_shape,
        compiler_params=_cparams(("parallel", "parallel")),
    )(x, modtab, norm_g, w_ext, cos, sin, qkg, bd)


def _ret_kernel(qf_ref, kf_ref, vf_ref, qb_ref, kb_ref, vb_ref, rate_ref, rate_h_ref, bd_ref,
                of_ref, ob_ref, sf_s, sb_s, dm_s, dq_s, dk_s, dc_s):
    s = pl.program_id(1)
    c = CHUNK

    @pl.when(s == 0)
    def _():
        sf_s[...] = jnp.zeros_like(sf_s)
        sb_s[...] = jnp.zeros_like(sb_s)
        pos = lax.broadcasted_iota(jnp.int32, (c, SEG_W), 0).astype(F32)
        lgf = -jnp.exp(rate_ref[0:1, :])
        lgb = -jnp.exp(rate_ref[1:2, :])
        dq_s[0] = jnp.exp((pos + 1.0) * lgf)
        dk_s[0] = jnp.exp((c - 1.0 - pos) * lgf)
        dc_s[0] = jnp.exp(float(c) * lgf)
        dq_s[1] = jnp.exp((c - pos) * lgb)
        dk_s[1] = jnp.exp(pos * lgb)
        dc_s[1] = jnp.exp(float(c) * lgb)
        ii = lax.broadcasted_iota(jnp.int32, (c, c), 0)
        jj = lax.broadcasted_iota(jnp.int32, (c, c), 1)
        for h in range(RET_HEADS):
            lf = -jnp.exp(rate_h_ref[0, h:h + 1, :])
            lb = -jnp.exp(rate_h_ref[1, h:h + 1, :])
            mf = ii >= jj
            mb = jj > ii
            dm_s[0, h] = jnp.where(mf, jnp.exp(jnp.where(mf, ii - jj, 0).astype(F32) * lf), 0.0)
            dm_s[1, h] = jnp.where(mb, jnp.exp(jnp.where(mb, jj - ii, 0).astype(F32) * lb), 0.0)

    bdm = bd_ref[...].astype(F32)
    head_of_lane = lax.broadcasted_iota(jnp.int32, (1, SEG_W // 2), 1) // HEAD_DIM

    def direction(d, q_ref, k_ref, v_ref, o_ref, st):
        q = q_ref[0]
        k = k_ref[0]
        v = v_ref[0]
        state = st[...]
        qd = (q.astype(F32) * dq_s[d]).astype(BF16)
        inter = _dot(qd, state.astype(BF16))
        for half in range(2):
            sl = slice(half * 256, half * 256 + 256)
            qh, kh, vh = q[:, sl], k[:, sl], v[:, sl]
            acc = inter[:, sl]
            for hh in range(4):
                msk = head_of_lane == hh
                sc = _dot_nt(jnp.where(msk, qh, jnp.zeros_like(qh)), kh) * dm_s[d, half * 4 + hh]
                acc = acc + _dot(sc.astype(BF16), jnp.where(msk, vh, jnp.zeros_like(vh)))
            o_ref[0, :, sl] = acc
        kd = (k.astype(F32) * dk_s[d]).astype(BF16)
        st[...] = state * dc_s[d] + bdm * _dot_tn(kd, v)

    direction(0, qf_ref, kf_ref, vf_ref, of_ref, sf_s)
    direction(1, qb_ref, kb_ref, vb_ref, ob_ref, sb_s)


def _retention(rq, rk, rv, rate, rate_h, bd, n_ctx):
    b, t, w = rq.shape
    nc = t // CHUNK
    lc = n_ctx // CHUNK

    def fwd(bi, s):
        return (bi, s, 0)

    def bwd(bi, s):
        return (bi, jnp.where(s < lc, lc - 1 - s, nc - 1 - (s - lc)), 0)

    blk = (1, CHUNK, w)
    return pl.pallas_call(
        _ret_kernel,
        grid=(b, nc),
        in_specs=[pl.BlockSpec(blk, fwd)] * 3 + [pl.BlockSpec(blk, bwd)] * 3 + [
            _const_spec((2, w)),
            _const_spec((2, RET_HEADS, CHUNK)),
            _const_spec((w, w)),
        ],
        out_specs=[pl.BlockSpec(blk, fwd), pl.BlockSpec(blk, bwd)],
        out_shape=[jax.ShapeDtypeStruct((b, t, w), F32)] * 2,
        scratch_shapes=[
            pltpu.VMEM((w, w), F32), pltpu.VMEM((w, w), F32),
            pltpu.VMEM((2, RET_HEADS, CHUNK, CHUNK), F32),
            pltpu.VMEM((2, CHUNK, w), F32), pltpu.VMEM((2, CHUNK, w), F32),
            pltpu.VMEM((2, 1, w), F32),
        ],
        compiler_params=_cparams(("parallel", "arbitrary")),
    )(rq, rk, rv, rq, rk, rv, rate, rate_h, bd)


ATT_ROW_BLOCK = 512


def _att_kernel(q_ref, k_ref, v_ref, o_ref, qs_s, m_s, acc_s, sa_s, sb_s, ma_s, mb_s, *, tq, tk, n_ctx, t):
    qi = pl.program_id(1)
    gw = ATT_GROUP * HEAD_DIM
    head_of_lane = lax.broadcasted_iota(jnp.int32, (1, gw), 1) // HEAD_DIM
    for h in range(ATT_HEADS):
        q = q_ref[0, :, (h // ATT_GROUP) * gw:(h // ATT_GROUP + 1) * gw]
        qs_s[h * tq:(h + 1) * tq, :] = jnp.where(head_of_lane == h % ATT_GROUP, q, jnp.zeros_like(q))
    m_s[...] = jnp.full(m_s.shape, -1e30, F32)
    acc_s[...] = jnp.zeros_like(acc_s)
    nk = jnp.where(qi * tq < n_ctx, n_ctx // tk, t // tk)
    vw = 2 * HEAD_DIM
    ones_lane = lax.broadcasted_iota(jnp.int32, (1, ATT_KV_HEADS * vw), 1) % vw >= HEAD_DIM
    rb = ATT_ROW_BLOCK

    blocks = [(slice(r * rb, (r + 1) * rb), (r * rb) // (ATT_GROUP * tq)) for r in range(ATT_HEADS * tq // rb)]

    def scores_into(j, dst, dst_max):
        kb = k_ref[0, pl.ds(pl.multiple_of(j * tk, tk), tk), :]
        for rows, g in blocks:
            sc = _dot_nt(qs_s[rows, :], kb[:, g * gw:(g + 1) * gw])
            dst[rows, :] = sc
            dst_max[rows, :] = jnp.broadcast_to(jnp.max(sc, axis=1, keepdims=True), (rb, vw))

    def consume(j, src, src_max):
        vb = v_ref[0, pl.ds(pl.multiple_of(j * tk, tk), tk), :]
        vext = jnp.where(ones_lane, jnp.ones_like(vb), vb)
        for rows, g in blocks:
            m_prev = m_s[rows, :]
            m_new = jnp.maximum(m_prev, src_max[rows, :])
            p = jnp.exp2(src[rows, :] - jnp.concatenate([m_new] * (tk // vw), axis=1)).astype(BF16)
            acc_s[rows, :] = jnp.exp2(m_prev - m_new) * acc_s[rows, :] + _dot(p, vext[:, g * vw:(g + 1) * vw])
            m_s[rows, :] = m_new

    scores_into(0, sa_s, ma_s)

    def body(i, carry):
        j = 2 * i
        scores_into(j + 1, sb_s, mb_s)
        consume(j, sa_s, ma_s)
        scores_into(jnp.minimum(j + 2, nk - 1), sa_s, ma_s)
        consume(j + 1, sb_s, mb_s)
        return carry

    lax.fori_loop(0, nk // 2, body, 0)

    @pl.when(nk % 2 == 1)
    def _():
        consume(nk - 1, sa_s, ma_s)

    low = lax.broadcasted_iota(jnp.int32, (tq, vw), 1) < HEAD_DIM
    for pair in range(ATT_HEADS // 2):
        a0 = acc_s[(2 * pair) * tq:(2 * pair + 1) * tq, :]
        a1 = acc_s[(2 * pair + 1) * tq:(2 * pair + 2) * tq, :]
        even = a0 / pltpu.roll(a0, HEAD_DIM, 1)
        odd = pltpu.roll(a1, HEAD_DIM, 1) / a1
        o_ref[0, :, pair * vw:(pair + 1) * vw] = jnp.where(low, even, odd).astype(BF16)


def _attention(aq, ak4, av2, tq, tk, n_ctx):
    b, t, w = aq.shape
    gw = ATT_GROUP * HEAD_DIM
    vw = 2 * HEAD_DIM
    rows = ATT_HEADS * tq
    return pl.pallas_call(
        functools.partial(_att_kernel, tq=tq, tk=tk, n_ctx=n_ctx, t=t),
        grid=(b, t // tq),
        in_specs=[
            pl.BlockSpec((1, tq, w), lambda bi, i: (bi, i, 0)),
            pl.BlockSpec((1, t, w), lambda bi, i: (bi, 0, 0)),
            pl.BlockSpec((1, t, ATT_KV_HEADS * vw), lambda bi, i: (bi, 0, 0)),
        ],
        out_specs=pl.BlockSpec((1, tq, w), lambda bi, i: (bi, i, 0)),
        out_shape=jax.ShapeDtypeStruct((b, t, w), BF16),
        scratch_shapes=[pltpu.VMEM((rows, gw), BF16), pltpu.VMEM((rows, vw), F32), pltpu.VMEM((rows, vw), F32),
                        pltpu.VMEM((rows, tk), F32), pltpu.VMEM((rows, tk), F32), pltpu.VMEM((rows, vw), F32),
                        pltpu.VMEM((rows, vw), F32)],
        compiler_params=_cparams(("parallel", "parallel")),
    )(aq, ak4, av2)


def _post_mixer(x, delta, m, ng2, wrh, wrl, x_ref, f_ref, aff_ref):
    xn = x + m[2:3] * delta
    x_ref[0] = xn
    f = _norm_mod(xn, ng2, m[4:5], m[3:4])
    fh, fl = _split(f)
    f_ref[0] = fh
    logits = _dot(fh, wrh) + _dot(fl, wrh) + _dot(fh, wrl)
    lane = lax.broadcasted_iota(jnp.int32, logits.shape, 1)
    logits = jnp.where(lane < N_EXPERTS, logits, -1e30)
    e = jnp.exp(logits - jnp.max(logits, axis=1, keepdims=True))
    aff_ref[0] = e / jnp.sum(e, axis=1, keepdims=True)


def _out_kernel(x_ref, of_ref, ob_ref, rg_ref, att_ref, mod_ref, gn_ref, w_ref, ng2_ref, wrh_ref, wrl_ref,
                bd_ref, xo_ref, f_ref, aff_ref):
    bd = bd_ref[...]
    o = of_ref[0] + ob_ref[0]
    mu = _segsum(o, bd) * (1.0 / HEAD_DIM)
    oc = o - mu
    var = _segsum(oc * oc, bd) * (1.0 / HEAD_DIM)
    r = _silu(rg_ref[0]) * (oc * lax.rsqrt(var + EPS) * gn_ref[...])
    y = _dot(r.astype(BF16), w_ref[0:RET_WIDTH, :]) + _dot(att_ref[0], w_ref[RET_WIDTH:, :])
    _post_mixer(x_ref[0], y, mod_ref[0, 0], ng2_ref[...], wrh_ref[...], wrl_ref[...], xo_ref, f_ref, aff_ref)


def _post_specs(b, t, d, tm):
    out_shape = [jax.ShapeDtypeStruct((b, t, d), F32), jax.ShapeDtypeStruct((b, t, d), BF16),
                 jax.ShapeDtypeStruct((b, t, ROUTER_LANES), F32)]
    out_specs = [_row_spec(tm, d), _row_spec(tm, d), _row_spec(tm, ROUTER_LANES)]
    return out_shape, out_specs


def _out_proj(x, of, ob, rg, att, modtab, gn_g, w_out, ng2, wrh, wrl, bd, tm, n_ctx):
    b, t, d = x.shape
    out_shape, out_specs = _post_specs(b, t, d, tm)
    return pl.pallas_call(
        _out_kernel,
        grid=(b, t // tm),
        in_specs=[
            _row_spec(tm, d), _row_spec(tm, SEG_W), _row_spec(tm, SEG_W), _row_spec(tm, SEG_W),
            _row_spec(tm, SEG_W), _mod_spec(d, 6, n_ctx // tm), _const_spec((1, SEG_W)),
            _const_spec(w_out.shape), _const_spec((1, d)), _const_spec(wrh.shape), _const_spec(wrl.shape),
            _const_spec((SEG_W, SEG_W)),
        ],
        out_specs=out_specs,
        out_shape=out_shape,
        compiler_params=_cparams(("parallel", "parallel")),
    )(x, of, ob, rg, att, modtab, gn_g, w_out, ng2, wrh, wrl, bd)


def _s5_kernel(u_ref, perm_ref, tm_ref, wz_ref, wo_ref, ar_ref, ai_ref, d_ref, z_ref,
               zr_s, zi_s, xr_s, xi_s, zs_s, *, n_sub, n_sub_ctx):
    s_n, j_sub, gb, half = n_sub, S5_SUB, S5_BLOCK_GROUPS, 2 * S5_STATE
    cw = z_ref.shape[-1]
    uf = jnp.concatenate([u_ref[0, pl.ds(m, s_n, stride=j_sub), :] for m in range(j_sub)], axis=1)
    perm = perm_ref[...]
    ug = _dot(uf.astype(BF16), perm).astype(BF16)
    for g in range(gb):
        zz = _dot(ug[:, g * cw:(g + 1) * cw], wz_ref[0, 0, g])
        zr_s[pl.ds(g, s_n, stride=gb), :] = zz[:, :half]
        zi_s[pl.ds(g, s_n, stride=gb), :] = zz[:, half:]
    fwd_lane = lax.broadcasted_iota(jnp.int32, (gb, half), 1) < S5_STATE
    ar = ar_ref[0, 0]
    ai = ai_ref[0, 0]

    def body(i, carry):
        xr, xi = carry
        sb = jnp.where(i < n_sub_ctx, n_sub_ctx - 1 - i, s_n - 1 - (i - n_sub_ctx))
        fwd = pl.ds(pl.multiple_of(i * gb, gb), gb)
        bwd = pl.ds(pl.multiple_of(sb * gb, gb), gb)
        xr_s[fwd, 0:S5_STATE] = xr[:, 0:S5_STATE]
        xi_s[fwd, 0:S5_STATE] = xi[:, 0:S5_STATE]
        xr_s[bwd, S5_STATE:half] = xr[:, S5_STATE:half]
        xi_s[bwd, S5_STATE:half] = xi[:, S5_STATE:half]
        zr = jnp.where(fwd_lane, zr_s[fwd, :], zr_s[bwd, :])
        zi = jnp.where(fwd_lane, zi_s[fwd, :], zi_s[bwd, :])
        return ar * xr - ai * xi + zr, ar * xi + ai * xr + zi

    zero = jnp.zeros((gb, half), F32)
    lax.fori_loop(0, s_n, body, (zero, zero), unroll=S5_SCAN_UNROLL)
    parts = []
    for g in range(gb):
        rows = pl.ds(g, s_n, stride=gb)
        xp = jnp.concatenate([xr_s[rows, :], xi_s[rows, :]], axis=1).astype(BF16)
        parts.append(_dot(ug[:, g * cw:(g + 1) * cw], tm_ref[0, 0, g]) + _dot(xp, wo_ref[0, 0, g]))
    y = _dot_nt(jnp.concatenate(parts, axis=1).astype(BF16), perm) + d_ref[0, 0] * uf
    z = _gelu_tanh(y)
    for n in range(j_sub):
        zs_s[pl.ds(n, s_n, stride=j_sub), :] = z[:, n * cw:(n + 1) * cw]
    z_ref[0] = zs_s[...].astype(BF16)


def _s5_scan(li, u, perm, tmat, wz, wo, ar, ai, dt, n_ctx):
    b, t, d = u.shape
    cw = S5_BLOCK_GROUPS * S5_GROUP
    n_sub = t // S5_SUB
    lw = S5_SUB * cw
    sw = 4 * S5_STATE
    gb = S5_BLOCK_GROUPS
    return pl.pallas_call(
        functools.partial(_s5_kernel, n_sub=n_sub, n_sub_ctx=n_ctx // S5_SUB),
        grid=(d // cw, b),
        in_specs=[
            pl.BlockSpec((1, t, cw), lambda c, bi: (bi, 0, c)),
            pl.BlockSpec((lw, lw), lambda c, bi: (0, 0)),
            pl.BlockSpec((1, 1, gb, cw, cw), lambda c, bi: (li, c, 0, 0, 0)),
            pl.BlockSpec((1, 1, gb, cw, sw), lambda c, bi: (li, c, 0, 0, 0)),
            pl.BlockSpec((1, 1, gb, sw, cw), lambda c, bi: (li, c, 0, 0, 0)),
            pl.BlockSpec((1, 1, gb, 2 * S5_STATE), lambda c, bi: (li, c, 0, 0)),
            pl.BlockSpec((1, 1, gb, 2 * S5_STATE), lambda c, bi: (li, c, 0, 0)),
            pl.BlockSpec((1, 1, 1, lw), lambda c, bi: (li, c, 0, 0)),
        ],
        out_specs=pl.BlockSpec((1, t, cw), lambda c, bi: (bi, 0, c)),
        out_shape=jax.ShapeDtypeStruct((b, t, d), BF16),
        scratch_shapes=[pltpu.VMEM((gb * n_sub, 2 * S5_STATE), F32)] * 4 + [pltpu.VMEM((t, cw), F32)],
        compiler_params=_cparams(("parallel", "parallel")),
    )(u, perm, tmat, wz, wo, ar, ai, dt)


def _s5_lane_regroup():
    j_sub, gb, k_w = S5_SUB, S5_BLOCK_GROUPS, S5_GROUP
    src = jnp.arange(j_sub * gb * k_w)
    m, g, k = src // (gb * k_w), (src // k_w) % gb, src % k_w
    dst = g * (j_sub * k_w) + m * k_w + k
    return (dst[:, None] == jnp.arange(j_sub * gb * k_w)[None, :]).astype(BF16)


def _s5_tables(a_re, a_im, log_dt, b_re, b_im, c_re, c_im, d_skip):
    hp = lax.Precision.HIGHEST
    j_sub = S5_SUB
    gb = S5_BLOCK_GROUPS
    lam_re = jnp.minimum(a_re, -1e-4)
    lam_im = a_im
    dt = jnp.exp(log_dt)[..., None]
    mag = jnp.exp(lam_re * dt)
    bar_re = mag * jnp.cos(lam_im * dt)
    bar_im = mag * jnp.sin(lam_im * dt)
    den = lam_re * lam_re + lam_im * lam_im
    nr, ni = bar_re - 1.0, bar_im
    k_re = (nr * lam_re + ni * lam_im) / den
    k_im = (ni * lam_re - nr * lam_im) / den
    bb_re = k_re[..., None] * b_re[None] - k_im[..., None] * b_im[None]
    bb_im = k_re[..., None] * b_im[None] + k_im[..., None] * b_re[None]
    jj = jnp.arange(j_sub + 1, dtype=F32)[:, None, None, None]
    pmag = jnp.exp(jj * (lam_re * dt)[None])
    pw_re = pmag * jnp.cos(jj * (lam_im * dt)[None])
    pw_im = pmag * jnp.sin(jj * (lam_im * dt)[None])
    pb_re = pw_re[..., None] * bb_re[None] - pw_im[..., None] * bb_im[None]
    pb_im = pw_re[..., None] * bb_im[None] + pw_im[..., None] * bb_re[None]
    cp_re = c_re[None] * pw_re[:, :, :, None, :] - c_im[None] * pw_im[:, :, :, None, :]
    cp_im = c_re[None] * pw_im[:, :, :, None, :] + c_im[None] * pw_re[:, :, :, None, :]
    kern = (jnp.einsum('dgcp,jdgpk->jdgkc', c_re, pb_re[:j_sub], precision=hp)
            - jnp.einsum('dgcp,jdgpk->jdgkc', c_im, pb_im[:j_sub], precision=hp))
    groups = a_re.shape[1]
    nblk = groups // gb
    k_w = S5_GROUP
    p_w = S5_STATE
    gw = j_sub * k_w
    pos = jnp.arange(j_sub)
    lag = pos[None, :] - pos[:, None]
    kf = jnp.where((lag >= 0)[:, :, None, None, None], kern[jnp.clip(lag, 0, j_sub - 1), 0], 0.0)
    kb = jnp.where((lag <= 0)[:, :, None, None, None], kern[jnp.clip(-lag, 0, j_sub - 1), 1], 0.0)
    tmat = (kf + kb).transpose(2, 0, 3, 1, 4).reshape(nblk, gb, gw, gw)
    rev = pos[::-1]
    zc = jnp.concatenate([pb_re[rev, 0], pb_re[pos, 1], pb_im[rev, 0], pb_im[pos, 1]], axis=2)
    wz = zc.transpose(1, 0, 3, 2).reshape(nblk, gb, gw, 4 * p_w)
    orows = lambda cp, d, order: cp[order, d].transpose(1, 3, 0, 2).reshape(nblk, gb, p_w, gw)
    wo = jnp.concatenate([orows(cp_re, 0, pos + 1), orows(cp_re, 1, j_sub - pos),
                          -orows(cp_im, 0, pos + 1), -orows(cp_im, 1, j_sub - pos)], axis=2)
    ar = jnp.concatenate([pw_re[j_sub, 0], pw_re[j_sub, 1]], axis=-1).reshape(nblk, gb, 2 * p_w)
    ai = jnp.concatenate([pw_im[j_sub, 0], pw_im[j_sub, 1]], axis=-1).reshape(nblk, gb, 2 * p_w)
    dt_tile = jnp.tile(d_skip.reshape(nblk, 1, gb * k_w), (1, 1, j_sub))
    return tmat.astype(BF16), wz.astype(BF16), wo.astype(BF16), ar, ai, dt_tile


def _glu_kernel(x_ref, z_ref, mod_ref, w_ref, b_ref, ng2_ref, wrh_ref, wrl_ref, xo_ref, f_ref, aff_ref):
    d = x_ref.shape[-1]
    ag = _dot(z_ref[0], w_ref[...]) + b_ref[...]
    delta = ag[:, :d] * _sigmoid(ag[:, d:])
    _post_mixer(x_ref[0], delta, mod_ref[0, 0], ng2_ref[...], wrh_ref[...], wrl_ref[...], xo_ref, f_ref, aff_ref)


def _glu(x, z, modtab, glu_w, glu_b, ng2, wrh, wrl, tm, n_ctx):
    b, t, d = x.shape
    out_shape, out_specs = _post_specs(b, t, d, tm)
    return pl.pallas_call(
        _glu_kernel,
        grid=(b, t // tm),
        in_specs=[_row_spec(tm, d), _row_spec(tm, d), _mod_spec(d, 6, n_ctx // tm),
                  _const_spec(glu_w.shape), _const_spec((1, 2 * d)), _const_spec((1, d)),
                  _const_spec(wrh.shape), _const_spec(wrl.shape)],
        out_specs=out_specs,
        out_shape=out_shape,
        compiler_params=_cparams(("parallel", "parallel")),
    )(x, z, modtab, glu_w, glu_b, ng2, wrh, wrl)


def _cumsum_lanes(v):
    n = v.shape[-1]
    lane = lax.broadcasted_iota(jnp.int32, v.shape, 1)
    sh = 1
    while sh < n:
        v = v + jnp.where(lane >= sh, pltpu.roll(v, sh, 1), 0.0)
        sh *= 2
    return v


def _select_segment(aff, cap, base):
    bits = pltpu.bitcast(aff, jnp.int32)
    thr = jnp.zeros((aff.shape[0], 1), jnp.int32)
    for bit in range(30, -1, -1):
        cand = thr | (1 << bit)
        cnt = jnp.sum(jnp.where(bits >= cand, 1.0, 0.0), axis=1, keepdims=True)
        thr = jnp.where(cnt >= cap, cand, thr)
    gt = bits > thr
    eq = bits == thr
    need = cap - jnp.sum(jnp.where(gt, 1.0, 0.0), axis=1, keepdims=True)
    eqf = jnp.where(eq, 1.0, 0.0)
    eq_before = _cumsum_lanes(eqf) - eqf
    sel = jnp.where(gt, 1.0, jnp.where(eq, jnp.where(eq_before < need, 1.0, 0.0), 0.0))
    pos = _cumsum_lanes(sel) - sel
    return jnp.where(sel > 0.0, pos.astype(jnp.int32) + base, -1)


def _select_kernel(aff_ref, slot_ref, *, n_ctx, cap_ctx, cap_lat):
    slot_ref[0, :, 0:n_ctx] = _select_segment(aff_ref[0, :, 0:n_ctx], cap_ctx, 0)
    slot_ref[0, :, n_ctx:] = _select_segment(aff_ref[0, :, n_ctx:], cap_lat, cap_ctx)


def _select(aff_t, n_ctx, cap_ctx, cap_lat):
    b, e, t = aff_t.shape
    return pl.pallas_call(
        functools.partial(_select_kernel, n_ctx=n_ctx, cap_ctx=cap_ctx, cap_lat=cap_lat),
        grid=(b,),
        in_specs=[pl.BlockSpec((1, e, t), lambda i: (i, 0, 0))],
        out_specs=pl.BlockSpec((1, e, t), lambda i: (i, 0, 0)),
        out_shape=jax.ShapeDtypeStruct((b, e, t), jnp.int32),
        compiler_params=_cparams(("parallel",)),
    )(aff_t)


def _moe_window(a_ref, base, e, k, rows, w):
    nominal = a_ref[base + e] + k * w
    return nominal, jnp.minimum(nominal, rows - w)


def _moe_hits(slot_row, nominal, actual, w):
    rid = lax.broadcasted_iota(jnp.int32, (w, slot_row.shape[-1]), 0) + actual
    return jnp.where(slot_row >= nominal, slot_row, -1) == rid


def _moe_gather_kernel(a_ref, nr_ref, slot_ref, aff_ref, f_ref, xs_ref, gate_ref, *, rows, w):
    b = pl.program_id(0)
    j = pl.program_id(1)
    n_e = slot_ref.shape[2]
    step = b * pl.num_programs(1) + j

    @pl.when(j == 0)
    def _():
        xs_ref[...] = jnp.zeros_like(xs_ref)
        gate_ref[...] = jnp.zeros_like(gate_ref)

    fj = f_ref[0]

    def one_round(k, carry):
        for e in range(n_e):
            nominal, actual = _moe_window(a_ref, step * n_e, e, k, rows, w)
            hit = _moe_hits(slot_ref[0, 0, e:e + 1, :], nominal, actual, w)
            off = pl.multiple_of(e * rows + actual, 16)
            xs_ref[0, pl.ds(off, w), :] += _dot(jnp.where(hit, 1.0, 0.0).astype(BF16), fj).astype(BF16)
            gate_ref[0, pl.ds(off, w), :] += jnp.sum(jnp.where(hit, aff_ref[0, 0, e:e + 1, :], 0.0),
                                                     axis=1, keepdims=True)
        return carry

    lax.fori_loop(0, nr_ref[step], one_round, 0)


def _moe_ffn_kernel(xs_ref, gate_ref, w1_ref, w3_ref, w2_ref, y_ref, ys_s, *, n_ft):
    ft = pl.program_id(2)
    nb, rows, d = xs_ref.shape
    xs = xs_ref[...].reshape(nb * rows, d)
    hid = (_silu(_dot(xs, w1_ref[0, 0].astype(BF16))) * _dot(xs, w3_ref[0, 0].astype(BF16))).astype(BF16)
    part = _dot(hid, w2_ref[0, 0].astype(BF16))

    @pl.when(ft == 0)
    def _():
        ys_s[...] = part

    @pl.when(ft > 0)
    def _():
        ys_s[...] += part

    @pl.when(ft == n_ft - 1)
    def _():
        y_ref[...] = (ys_s[...] * gate_ref[...].reshape(nb * rows, 1)).astype(BF16).reshape(nb, rows, d)


def _moe_combine_kernel(a_ref, nr_ref, slot_ref, y_ref, x_ref, mod_ref, *rest, rows, w, nb, lead, tail):
    o_ref = rest[-1] if tail != "s5" else rest[-2]
    b = pl.program_id(0)
    n_e = slot_ref.shape[2]
    step = b * nb + pl.program_id(1) + lead

    def one_round(k, acc):
        onehots, ys = [], []
        for e in range(n_e):
            nominal, actual = _moe_window(a_ref, step * n_e, e, k, rows, w)
            hit = _moe_hits(slot_ref[0, 0, e:e + 1, :], nominal, actual, w)
            onehots.append(jnp.where(hit, 1.0, 0.0).astype(BF16))
            ys.append(y_ref[0, pl.ds(pl.multiple_of(e * rows + actual, 16), w), :])
        return acc + _dot_tn(jnp.concatenate(onehots, axis=0), jnp.concatenate(ys, axis=0))

    acc = lax.fori_loop(0, nr_ref[step], one_round, jnp.zeros(o_ref.shape[1:], F32))
    xn = x_ref[0] + mod_ref[0, 0, 5:6, :] * acc
    if tail == "final":
        xn = xn * lax.rsqrt(jnp.mean(xn * xn, axis=-1, keepdims=True) + EPS) * rest[0][...]
    o_ref[0] = xn
    if tail == "s5":
        m = rest[0][0, 0]
        rest[3][0] = _norm_mod(xn, rest[1][...], m[1:2], m[0:1])


def _moe_windows(slot, tb, w):
    b, e, t = slot.shape
    nb = t // tb
    cnt = jnp.sum((slot >= 0).reshape(b, e, nb, tb), axis=-1, dtype=jnp.int32)
    start = jnp.cumsum(cnt, axis=-1) - cnt
    a0 = (start // 16) * 16
    rounds = jnp.maximum(jnp.max((start + cnt - a0 + w - 1) // w, axis=1), 1)
    return jnp.swapaxes(a0, 1, 2).reshape(-1).astype(jnp.int32), rounds.reshape(-1).astype(jnp.int32)


def _route_and_moe(layer, x, aff, f, modtab, w1, w3, w2, n_ctx, tb, tf, final_g=None, next_s5=None):
    b, t, d = f.shape
    n_e = w1.shape[1]
    ff = w1.shape[-1]
    nb = t // tb
    aff_t = jnp.swapaxes(aff[:, :, :N_EXPERTS], 1, 2)
    cap_ctx = (CAPACITY_FACTOR * n_ctx) // N_EXPERTS
    cap_lat = (CAPACITY_FACTOR * (t - n_ctx)) // N_EXPERTS
    rows = cap_ctx + cap_lat
    w = min(MOE_WINDOW, rows)
    slot = _select(aff_t, n_ctx, cap_ctx, cap_lat)
    a0, rounds = _moe_windows(slot, tb, w)
    by_block = lambda v: jnp.swapaxes(v.reshape(b, n_e, nb, tb), 1, 2)
    slot_b, aff_b = by_block(slot), by_block(aff_t)
    blk_spec = pl.BlockSpec((1, 1, n_e, tb), lambda bi, j, *_: (bi, j, 0, 0))
    whole = lambda width: pl.BlockSpec((1, n_e * rows, width), lambda bi, j, *_: (bi, 0, 0))
    xs, gate = pl.pallas_call(
        functools.partial(_moe_gather_kernel, rows=rows, w=w),
        grid_spec=pltpu.PrefetchScalarGridSpec(
            num_scalar_prefetch=2, grid=(b, nb),
            in_specs=[blk_spec, blk_spec, pl.BlockSpec((1, tb, d), lambda bi, j, *_: (bi, j, 0))],
            out_specs=[whole(d), whole(1)]),
        out_shape=[jax.ShapeDtypeStruct((b, n_e * rows, d), BF16), jax.ShapeDtypeStruct((b, n_e * rows, 1), F32)],
        compiler_params=_cparams(("parallel", "arbitrary")),
    )(a0, rounds, slot_b, aff_b, f)
    n_ft = ff // tf
    sb = MOE_FFN_SAMPLES
    y = pl.pallas_call(
        functools.partial(_moe_ffn_kernel, n_ft=n_ft),
        grid=(b // sb, n_e, n_ft),
        in_specs=[
            pl.BlockSpec((sb, rows, d), lambda bi, e, k: (bi, e, 0)),
            pl.BlockSpec((sb, rows, 1), lambda bi, e, k: (bi, e, 0)),
            pl.BlockSpec((1, 1, d, tf), lambda bi, e, k: (layer, e, 0, k)),
            pl.BlockSpec((1, 1, d, tf), lambda bi, e, k: (layer, e, 0, k)),
            pl.BlockSpec((1, 1, tf, d), lambda bi, e, k: (layer, e, k, 0)),
        ],
        out_specs=pl.BlockSpec((sb, rows, d), lambda bi, e, k: (bi, e, 0)),
        out_shape=jax.ShapeDtypeStruct((b, n_e * rows, d), BF16),
        scratch_shapes=[pltpu.VMEM((sb * rows, d), F32)],
        compiler_params=_cparams(("parallel", "parallel", "arbitrary")),
    )(xs, gate, w1, w3, w2)
    seg_blocks = n_ctx // tb
    lead = 0 if final_g is None else seg_blocks
    seg_of = lambda bi, j, *_: (bi, jnp.where(j + lead >= seg_blocks, 1, 0), 0, 0)
    gain_spec = pl.BlockSpec((1, d), lambda bi, j, *_: (0, 0))
    out_spec = pl.BlockSpec((1, tb, d), lambda bi, j, *_: (bi, j, 0))
    out_sds = jax.ShapeDtypeStruct((b, t - lead * tb, d), F32)
    tail, extra_args, extra_specs, out_specs, out_shape = "plain", (), [], out_spec, out_sds
    if final_g is not None:
        tail, extra_args, extra_specs = "final", (final_g,), [gain_spec]
    elif next_s5 is not None:
        tail, extra_args, extra_specs = "s5", tuple(next_s5), [pl.BlockSpec((1, 1, 6, d), seg_of), gain_spec]
        out_specs, out_shape = [out_spec, out_spec], [out_sds, out_sds]
    return pl.pallas_call(
        functools.partial(_moe_combine_kernel, rows=rows, w=w, nb=nb, lead=lead, tail=tail),
        grid_spec=pltpu.PrefetchScalarGridSpec(
            num_scalar_prefetch=2, grid=(b, nb - lead),
            in_specs=[pl.BlockSpec((1, 1, n_e, tb), lambda bi, j, *_: (bi, j + lead, 0, 0)), whole(d),
                      pl.BlockSpec((1, tb, d), lambda bi, j, *_: (bi, j + lead, 0)),
                      pl.BlockSpec((1, 1, 6, d), seg_of)]
            + extra_specs,
            out_specs=out_specs),
        out_shape=out_shape,
        compiler_params=_cparams(("parallel", "parallel")),
    )(a0, rounds, slot_b, y, x, modtab, *extra_args)


def _rope_tables(n_ctx, n_lat):
    n_rows = n_lat // GRID_W
    row = jnp.repeat(jnp.arange(n_rows, dtype=F32), GRID_W)
    col = jnp.tile(jnp.arange(GRID_W, dtype=F32), n_rows)
    inv = ROPE_THETA ** (-jnp.arange(0, ROPE_AXIS_DIM, 2, dtype=F32) / ROPE_AXIS_DIM)
    ang_r = row[:, None] * inv[None, :]
    ang_c = col[:, None] * inv[None, :]
    cos = jnp.concatenate([jnp.cos(ang_r), jnp.cos(ang_r), jnp.cos(ang_c), jnp.cos(ang_c)], axis=-1)
    sin = jnp.concatenate([-jnp.sin(ang_r), jnp.sin(ang_r), -jnp.sin(ang_c), jnp.sin(ang_c)], axis=-1)
    cos = jnp.concatenate([jnp.ones((n_ctx, HEAD_DIM), F32), cos], axis=0)
    sin = jnp.concatenate([jnp.zeros((n_ctx, HEAD_DIM), F32), sin], axis=0)
    reps = SEG_W // HEAD_DIM
    return jnp.tile(cos, (1, reps)), jnp.tile(sin, (1, reps))


def _extend_in_proj(w_in):
    base = 4 * RET_WIDTH + ATT_WIDTH
    ak = w_in[:, base:base + ATT_KV_WIDTH].reshape(-1, ATT_KV_HEADS, 1, HEAD_DIM)
    av = w_in[:, base + ATT_KV_WIDTH:].reshape(-1, ATT_KV_HEADS, 1, HEAD_DIM)
    rep = lambda w, n: jnp.broadcast_to(w, (w.shape[0], ATT_KV_HEADS, n, HEAD_DIM)).reshape(w.shape[0], -1)
    return jnp.concatenate([w_in[:, :base], rep(ak, ATT_GROUP), rep(av, 2)], axis=-1)


def kernel(x, c, ctx, c_ctx, mod_w, mod_b, norm_g, mix_in_w, mix_out_w, ret_log_rate, ret_gn_g, qk_norm_g,
           s5_a_re, s5_a_im, s5_log_dt, s5_b_re, s5_b_im, s5_c_re, s5_c_im, s5_d, s5_glu_w, s5_glu_b,
           moe_router_w, moe_w1, moe_w3, moe_w2, final_norm_g):
    batch, n_lat, d = x.shape
    n_ctx = ctx.shape[1]
    t = n_ctx + n_lat
    depth = mod_w.shape[0]
    assert batch == 8 and d % 128 == 0 and n_ctx % CHUNK == 0 and n_lat % CHUNK == 0
    tm = 256 if n_ctx % 256 == 0 else 128
    tb = tm
    tf = 1024

    xs = jnp.concatenate([ctx, x], axis=1)
    rows = 16
    cond = jnp.zeros((rows, d), F32).at[:batch].set(c).at[batch].set(c_ctx)
    mod_all = _modulation(cond, mod_w, mod_b)
    cos, sin = _rope_tables(n_ctx, n_lat)
    lane_head = jnp.arange(SEG_W) // HEAD_DIM
    bd = (lane_head[:, None] == lane_head[None, :]).astype(BF16)
    w1, w3, w2 = moe_w1, moe_w3, moe_w2
    s5_perm = _s5_lane_regroup()
    s5_tabs = jax.vmap(_s5_tables)(s5_a_re, s5_a_im, s5_log_dt, s5_b_re, s5_b_im, s5_c_re, s5_c_im, s5_d)

    def modtab_of(layer):
        m = mod_all[layer]
        lat = m[:batch].reshape(batch, 1, 6, d)
        cx = jnp.broadcast_to(m[batch].reshape(1, 1, 6, d), (batch, 1, 6, d))
        return jnp.concatenate([cx, lat], axis=1)

    u = None
    for layer in range(depth):
        modtab = modtab_of(layer)
        ng1 = norm_g[layer, 0].reshape(1, d)
        ng2 = norm_g[layer, 1].reshape(1, d)
        wr = jnp.zeros((d, ROUTER_LANES), F32).at[:, :N_EXPERTS].set(moe_router_w[layer])
        wrh = wr.astype(BF16)
        wrl = (wr - wrh.astype(F32)).astype(BF16)
        i = layer // 2
        if layer % 2 == 0:
            w_ext = _extend_in_proj(mix_in_w[i]).astype(BF16)
            reps = SEG_W // HEAD_DIM
            qkg = jnp.stack([jnp.tile(qk_norm_g[i, 0], reps), jnp.tile(qk_norm_g[i, 1], reps)])
            rq, rk, rv, rg, aq, ak4, av2 = _in_proj(xs, modtab, ng1, w_ext, cos, sin, qkg, bd, tm, n_ctx)
            rate = jnp.repeat(ret_log_rate[i], HEAD_DIM, axis=-1)
            rate_h = jnp.broadcast_to(ret_log_rate[i][:, :, None], (2, RET_HEADS, CHUNK))
            o_f, o_b = _retention(rq, rk, rv, rate, rate_h, bd, n_ctx)
            att = _attention(aq, ak4, av2, tm, tm, n_ctx)
            xs, f, aff = _out_proj(xs, o_f, o_b, rg, att, modtab, ret_gn_g[i].reshape(1, SEG_W),
                                   mix_out_w[i].astype(BF16), ng2, wrh, wrl, bd, tm, n_ctx)
        else:
            z = _s5_scan(i, u, s5_perm, *s5_tabs, n_ctx)
            xs, f, aff = _glu(xs, z, modtab, s5_glu_w[i].astype(BF16), s5_glu_b[i].reshape(1, 2 * d),
                              ng2, wrh, wrl, tm, n_ctx)
        final_g = final_norm_g.reshape(1, d) if layer == depth - 1 else None
        next_s5 = None
        if layer + 1 < depth and (layer + 1) % 2 == 1:
            next_s5 = (modtab_of(layer + 1), norm_g[layer + 1, 0].reshape(1, d))
        xs = _route_and_moe(layer, xs, aff, f, modtab, w1, w3, w2, n_ctx, tb, tf, final_g, next_s5)
        u = None
        if next_s5 is not None:
            xs, u = xs
    return xs
```

```python
import functools
import math

import jax
import jax.numpy as jnp
from jax import lax
from jax.experimental import pallas as pl
from jax.experimental.pallas import tpu as pltpu

F32 = jnp.float32
BF16 = jnp.bfloat16

GRID_W = 64
HEAD_DIM = 64
RET_HEADS = 8
ATT_HEADS = 8
ATT_KV_HEADS = 2
ATT_GROUP = ATT_HEADS // ATT_KV_HEADS
RET_WIDTH = RET_HEADS * HEAD_DIM
ATT_WIDTH = ATT_HEADS * HEAD_DIM
ATT_KV_WIDTH = ATT_KV_HEADS * HEAD_DIM
CHUNK = 128
ROPE_THETA = 10000.0
ROPE_AXIS_DIM = HEAD_DIM // 2
S5_GROUP = 16
S5_STATE = 64
S5_SUB = 8
S5_BLOCK_GROUPS = 8
S5_SCAN_UNROLL = 4
N_EXPERTS = 16
CAPACITY_FACTOR = 2
EPS = 1e-6
SEG_W = 512
LOG2_E = math.log2(math.e)
ROUTER_LANES = 128
MOE_WINDOW = 64
MOE_FFN_SAMPLES = 2

V7X_VMEM_BYTES = 64 * 1024 * 1024
VMEM_LIMIT = V7X_VMEM_BYTES - 8 * 1024 * 1024


def _cparams(sem):
    return pltpu.CompilerParams(dimension_semantics=sem, vmem_limit_bytes=VMEM_LIMIT)


def _dot(a, b):
    return jnp.dot(a, b, preferred_element_type=F32)


def _dot_nt(a, b):
    return lax.dot_general(a, b, (((1,), (1,)), ((), ())), preferred_element_type=F32)


def _dot_tn(a, b):
    return lax.dot_general(a, b, (((0,), (0,)), ((), ())), preferred_element_type=F32)


def _split(v):
    hi = v.astype(BF16)
    lo = (v - hi.astype(F32)).astype(BF16)
    return hi, lo


def _segsum(v, bd):
    return _dot(v.astype(BF16), bd)


def _sigmoid(v):
    return 1.0 / (1.0 + jnp.exp(-v))


def _silu(v):
    return v * _sigmoid(v)


def _gelu_tanh(v):
    c = math.sqrt(2.0 / math.pi)
    return 0.5 * v * (1.0 + jnp.tanh(c * (v + 0.044715 * (v * v * v))))


def _norm_mod(x, g, scale, shift):
    ms = jnp.mean(x * x, axis=-1, keepdims=True)
    return x * lax.rsqrt(ms + EPS) * g * (1.0 + scale) + shift


def _rope(x, cos, sin_signed):
    w = x.shape[-1]
    lane = lax.broadcasted_iota(jnp.int32, x.shape, 1)
    first = (lane % 32) < 16
    partner = jnp.where(first, pltpu.roll(x, w - 16, 1), pltpu.roll(x, 16, 1))
    return x * cos + partner * sin_signed


def _mod_kernel(c_ref, w_ref, b_ref, o_ref):
    s = _silu(c_ref[...]).astype(BF16)
    o_ref[0] = _dot(s, w_ref[0].astype(BF16)) + b_ref[0]


def _modulation(cond, mod_w, mod_b):
    depth, d, n6 = mod_w.shape
    rows = cond.shape[0]
    tn = n6 // 6
    return pl.pallas_call(
        _mod_kernel,
        grid=(depth, n6 // tn),
        in_specs=[
            pl.BlockSpec((rows, d), lambda l, j: (0, 0)),
            pl.BlockSpec((1, d, tn), lambda l, j: (l, 0, j)),
            pl.BlockSpec((1, 1, tn), lambda l, j: (l, 0, j)),
        ],
        out_specs=pl.BlockSpec((1, rows, tn), lambda l, j: (l, 0, j)),
        out_shape=jax.ShapeDtypeStruct((depth, rows, n6), F32),
        compiler_params=_cparams(("parallel", "parallel")),
    )(cond, mod_w, mod_b.reshape(depth, 1, n6))


def _in_kernel(x_ref, mod_ref, ng_ref, w_ref, cos_ref, sin_ref, qkg_ref, bd_ref,
               rq_ref, rk_ref, rv_ref, rg_ref, aq_ref, ak_ref, av_ref):
    x = x_ref[0]
    m = mod_ref[0, 0]
    h = _norm_mod(x, ng_ref[...], m[1:2], m[0:1]).astype(BF16)
    cos = cos_ref[...]
    sin = sin_ref[...]
    bd = bd_ref[...]

    def seg(i):
        return _dot(h, w_ref[:, i * SEG_W:(i + 1) * SEG_W])

    def headnorm(v, g):
        ms = _segsum(v * v, bd) * (1.0 / HEAD_DIM)
        return v * lax.rsqrt(ms + EPS) * g

    scale = HEAD_DIM ** -0.5
    rq_ref[0] = _rope(seg(0), cos, sin).astype(BF16)
    rk_ref[0] = _rope(seg(1) * scale, cos, sin).astype(BF16)
    rv_ref[0] = seg(2).astype(BF16)
    rg_ref[0] = seg(3)
    aq_ref[0] = (_rope(headnorm(seg(4), qkg_ref[0:1]), cos, sin) * (scale * LOG2_E)).astype(BF16)
    ak_ref[0] = _rope(headnorm(seg(5), qkg_ref[1:2]), cos, sin).astype(BF16)
    av_ref[0] = _dot(h, w_ref[:, 6 * SEG_W:]).astype(BF16)


def _row_spec(tm, d, lead=0):
    return pl.BlockSpec((1, tm, d), lambda b, i: (b, i + lead, 0))


def _mod_spec(d, n, seg_tiles):
    return pl.BlockSpec((1, 1, n, d), lambda b, i: (b, jnp.where(i >= seg_tiles, 1, 0), 0, 0))


def _const_spec(shape):
    nd = len(shape)
    return pl.BlockSpec(shape, lambda b, i: (0,) * nd)


def _in_proj(x, modtab, norm_g, w_ext, cos, sin, qkg, bd, tm, n_ctx):
    b, t, d = x.shape
    out_shape, out_specs = [], []
    for dt, width in ((BF16, SEG_W), (BF16, SEG_W), (BF16, SEG_W), (F32, SEG_W), (BF16, SEG_W), (BF16, SEG_W),
                      (BF16, ATT_KV_HEADS * 2 * HEAD_DIM)):
        out_shape.append(jax.ShapeDtypeStruct((b, t, width), dt))
        out_specs.append(_row_spec(tm, width))
    return pl.pallas_call(
        _in_kernel,
        grid=(b, t // tm),
        in_specs=[
            _row_spec(tm, d),
            _mod_spec(d, 6, n_ctx // tm),
            _const_spec((1, d)),
            _const_spec(w_ext.shape),
            pl.BlockSpec((tm, SEG_W), lambda b, i: (i, 0)),
            pl.BlockSpec((tm, SEG_W), lambda b, i: (i, 0)),
            _const_spec((2, SEG_W)),
            _const_spec((SEG_W, SEG_W)),
        ],
        out_specs=out_specs,
        out_shape=out_shape,
        compiler_params=_cparams(("parallel", "parallel")),
    )(x, modtab, norm_g, w_ext, cos, sin, qkg, bd)


def _ret_kernel(qf_ref, kf_ref, vf_ref, qb_ref, kb_ref, vb_ref, rate_ref, rate_h_ref, bd_ref,
                of_ref, ob_ref, sf_s, sb_s, dm_s, dq_s, dk_s, dc_s):
    s = pl.program_id(1)
    c = CHUNK

    @pl.when(s == 0)
    def _():
        sf_s[...] = jnp.zeros_like(sf_s)
        sb_s[...] = jnp.zeros_like(sb_s)
        pos = lax.broadcasted_iota(jnp.int32, (c, SEG_W), 0).astype(F32)
        lgf = -jnp.exp(rate_ref[0:1, :])
        lgb = -jnp.exp(rate_ref[1:2, :])
        dq_s[0] = jnp.exp((pos + 1.0) * lgf)
        dk_s[0] = jnp.exp((c - 1.0 - pos) * lgf)
        dc_s[0] = jnp.exp(float(c) * lgf)
        dq_s[1] = jnp.exp((c - pos) * lgb)
        dk_s[1] = jnp.exp(pos * lgb)
        dc_s[1] = jnp.exp(float(c) * lgb)
        ii = lax.broadcasted_iota(jnp.int32, (c, c), 0)
        jj = lax.broadcasted_iota(jnp.int32, (c, c), 1)
        for h in range(RET_HEADS):
            lf = -jnp.exp(rate_h_ref[0, h:h + 1, :])
            lb = -jnp.exp(rate_h_ref[1, h:h + 1, :])
            mf = ii >= jj
            mb = jj > ii
            dm_s[0, h] = jnp.where(mf, jnp.exp(jnp.where(mf, ii - jj, 0).astype(F32) * lf), 0.0)
            dm_s[1, h] = jnp.where(mb, jnp.exp(jnp.where(mb, jj - ii, 0).astype(F32) * lb), 0.0)

    bdm = bd_ref[...].astype(F32)
    head_of_lane = lax.broadcasted_iota(jnp.int32, (1, SEG_W // 2), 1) // HEAD_DIM

    def direction(d, q_ref, k_ref, v_ref, o_ref, st):
        q = q_ref[0]
        k = k_ref[0]
        v = v_ref[0]
        state = st[...]
        qd = (q.astype(F32) * dq_s[d]).astype(BF16)
        inter = _dot(qd, state.astype(BF16))
        for half in range(2):
            sl = slice(half * 256, half * 256 + 256)
            qh, kh, vh = q[:, sl], k[:, sl], v[:, sl]
            acc = inter[:, sl]
            for hh in range(4):
                msk = head_of_lane == hh
                sc = _dot_nt(jnp.where(msk, qh, jnp.zeros_like(qh)), kh) * dm_s[d, half * 4 + hh]
                acc = acc + _dot(sc.astype(BF16), jnp.where(msk, vh, jnp.zeros_like(vh)))
            o_ref[0, :, sl] = acc
        kd = (k.astype(F32) * dk_s[d]).astype(BF16)
        st[...] = state * dc_s[d] + bdm * _dot_tn(kd, v)

    direction(0, qf_ref, kf_ref, vf_ref, of_ref, sf_s)
    direction(1, qb_ref, kb_ref, vb_ref, ob_ref, sb_s)


def _retention(rq, rk, rv, rate, rate_h, bd, n_ctx):
    b, t, w = rq.shape
    nc = t // CHUNK
    lc = n_ctx // CHUNK

    def fwd(bi, s):
        return (bi, s, 0)

    def bwd(bi, s):
        return (bi, jnp.where(s < lc, lc - 1 - s, nc - 1 - (s - lc)), 0)

    blk = (1, CHUNK, w)
    return pl.pallas_call(
        _ret_kernel,
        grid=(b, nc),
        in_specs=[pl.BlockSpec(blk, fwd)] * 3 + [pl.BlockSpec(blk, bwd)] * 3 + [
            _const_spec((2, w)),
            _const_spec((2, RET_HEADS, CHUNK)),
            _const_spec((w, w)),
        ],
        out_specs=[pl.BlockSpec(blk, fwd), pl.BlockSpec(blk, bwd)],
        out_shape=[jax.ShapeDtypeStruct((b, t, w), F32)] * 2,
        scratch_shapes=[
            pltpu.VMEM((w, w), F32), pltpu.VMEM((w, w), F32),
            pltpu.VMEM((2, RET_HEADS, CHUNK, CHUNK), F32),
            pltpu.VMEM((2, CHUNK, w), F32), pltpu.VMEM((2, CHUNK, w), F32),
            pltpu.VMEM((2, 1, w), F32),
        ],
        compiler_params=_cparams(("parallel", "arbitrary")),
    )(rq, rk, rv, rq, rk, rv, rate, rate_h, bd)


ATT_ROW_BLOCK = 512


def _att_kernel(q_ref, k_ref, v_ref, o_ref, qs_s, m_s, acc_s, sa_s, sb_s, ma_s, mb_s, *, tq, tk, n_ctx, t):
    qi = pl.program_id(1)
    gw = ATT_GROUP * HEAD_DIM
    head_of_lane = lax.broadcasted_iota(jnp.int32, (1, gw), 1) // HEAD_DIM
    for h in range(ATT_HEADS):
        q = q_ref[0, :, (h // ATT_GROUP) * gw:(h // ATT_GROUP + 1) * gw]
        qs_s[h * tq:(h + 1) * tq, :] = jnp.where(head_of_lane == h % ATT_GROUP, q, jnp.zeros_like(q))
    m_s[...] = jnp.full(m_s.shape, -1e30, F32)
    acc_s[...] = jnp.zeros_like(acc_s)
    nk = jnp.where(qi * tq < n_ctx, n_ctx // tk, t // tk)
    vw = 2 * HEAD_DIM
    ones_lane = lax.broadcasted_iota(jnp.int32, (1, ATT_KV_HEADS * vw), 1) % vw >= HEAD_DIM
    rb = ATT_ROW_BLOCK

    blocks = [(slice(r * rb, (r + 1) * rb), (r * rb) // (ATT_GROUP * tq)) for r in range(ATT_HEADS * tq // rb)]

    def scores_into(j, dst, dst_max):
        kb = k_ref[0, pl.ds(pl.multiple_of(j * tk, tk), tk), :]
        for rows, g in blocks:
            sc = _dot_nt(qs_s[rows, :], kb[:, g * gw:(g + 1) * gw])
            dst[rows, :] = sc
            dst_max[rows, :] = jnp.broadcast_to(jnp.max(sc, axis=1, keepdims=True), (rb, vw))

    def consume(j, src, src_max):
        vb = v_ref[0, pl.ds(pl.multiple_of(j * tk, tk), tk), :]
        vext = jnp.where(ones_lane, jnp.ones_like(vb), vb)
        for rows, g in blocks:
            m_prev = m_s[rows, :]
            m_new = jnp.maximum(m_prev, src_max[rows, :])
            p = jnp.exp2(src[rows, :] - jnp.concatenate([m_new] * (tk // vw), axis=1)).astype(BF16)
            acc_s[rows, :] = jnp.exp2(m_prev - m_new) * acc_s[rows, :] + _dot(p, vext[:, g * vw:(g + 1) * vw])
            m_s[rows, :] = m_new

    scores_into(0, sa_s, ma_s)
    pairs = max(p for p in (4, 2, 1) if p == 1 or ((n_ctx // tk) % (2 * p) == 1 and (t // tk) % (2 * p) == 1))

    def body(i, carry):
        for u in range(pairs):
            j = 2 * (pairs * i + u)
            scores_into(j + 1, sb_s, mb_s)
            consume(j, sa_s, ma_s)
            scores_into(jnp.minimum(j + 2, nk - 1), sa_s, ma_s)
            consume(j + 1, sb_s, mb_s)
        return carry

    lax.fori_loop(0, nk // (2 * pairs), body, 0)

    @pl.when(nk % 2 == 1)
    def _():
        consume(nk - 1, sa_s, ma_s)

    low = lax.broadcasted_iota(jnp.int32, (tq, vw), 1) < HEAD_DIM
    for pair in range(ATT_HEADS // 2):
        a0 = acc_s[(2 * pair) * tq:(2 * pair + 1) * tq, :]
        a1 = acc_s[(2 * pair + 1) * tq:(2 * pair + 2) * tq, :]
        even = a0 / pltpu.roll(a0, HEAD_DIM, 1)
        odd = pltpu.roll(a1, HEAD_DIM, 1) / a1
        o_ref[0, :, pair * vw:(pair + 1) * vw] = jnp.where(low, even, odd).astype(BF16)


def _attention(aq, ak4, av2, tq, tk, n_ctx):
    b, t, w = aq.shape
    gw = ATT_GROUP * HEAD_DIM
    vw = 2 * HEAD_DIM
    rows = ATT_HEADS * tq
    return pl.pallas_call(
        functools.partial(_att_kernel, tq=tq, tk=tk, n_ctx=n_ctx, t=t),
        grid=(b, t // tq),
        in_specs=[
            pl.BlockSpec((1, tq, w), lambda bi, i: (bi, i, 0)),
            pl.BlockSpec((1, t, w), lambda bi, i: (bi, 0, 0)),
            pl.BlockSpec((1, t, ATT_KV_HEADS * vw), lambda bi, i: (bi, 0, 0)),
        ],
        out_specs=pl.BlockSpec((1, tq, w), lambda bi, i: (bi, i, 0)),
        out_shape=jax.ShapeDtypeStruct((b, t, w), BF16),
        scratch_shapes=[pltpu.VMEM((rows, gw), BF16), pltpu.VMEM((rows, vw), F32), pltpu.VMEM((rows, vw), F32),
                        pltpu.VMEM((rows, tk), F32), pltpu.VMEM((rows, tk), F32), pltpu.VMEM((rows, vw), F32),
                        pltpu.VMEM((rows, vw), F32)],
        compiler_params=_cparams(("parallel", "parallel")),
    )(aq, ak4, av2)


def _post_mixer(x, delta, m, ng2, wrh, wrl, x_ref, f_ref, aff_ref):
    xn = x + m[2:3] * delta
    x_ref[0] = xn
    f = _norm_mod(xn, ng2, m[4:5], m[3:4])
    fh, fl = _split(f)
    f_ref[0] = fh
    logits = _dot(fh, wrh) + _dot(fl, wrh) + _dot(fh, wrl)
    lane = lax.broadcasted_iota(jnp.int32, logits.shape, 1)
    logits = jnp.where(lane < N_EXPERTS, logits, -1e30)
    e = jnp.exp(logits - jnp.max(logits, axis=1, keepdims=True))
    aff_ref[0] = e / jnp.sum(e, axis=1, keepdims=True)


def _out_kernel(x_ref, of_ref, ob_ref, rg_ref, att_ref, mod_ref, gn_ref, w_ref, ng2_ref, wrh_ref, wrl_ref,
                bd_ref, xo_ref, f_ref, aff_ref):
    bd = bd_ref[...]
    o = of_ref[0] + ob_ref[0]
    mu = _segsum(o, bd) * (1.0 / HEAD_DIM)
    oc = o - mu
    var = _segsum(oc * oc, bd) * (1.0 / HEAD_DIM)
    r = _silu(rg_ref[0]) * (oc * lax.rsqrt(var + EPS) * gn_ref[...])
    y = _dot(r.astype(BF16), w_ref[0:RET_WIDTH, :]) + _dot(att_ref[0], w_ref[RET_WIDTH:, :])
    _post_mixer(x_ref[0], y, mod_ref[0, 0], ng2_ref[...], wrh_ref[...], wrl_ref[...], xo_ref, f_ref, aff_ref)


def _post_specs(b, t, d, tm):
    out_shape = [jax.ShapeDtypeStruct((b, t, d), F32), jax.ShapeDtypeStruct((b, t, d), BF16),
                 jax.ShapeDtypeStruct((b, t, ROUTER_LANES), F32)]
    out_specs = [_row_spec(tm, d), _row_spec(tm, d), _row_spec(tm, ROUTER_LANES)]
    return out_shape, out_specs


def _out_proj(x, of, ob, rg, att, modtab, gn_g, w_out, ng2, wrh, wrl, bd, tm, n_ctx):
    b, t, d = x.shape
    out_shape, out_specs = _post_specs(b, t, d, tm)
    return pl.pallas_call(
        _out_kernel,
        grid=(b, t // tm),
        in_specs=[
            _row_spec(tm, d), _row_spec(tm, SEG_W), _row_spec(tm, SEG_W), _row_spec(tm, SEG_W),
            _row_spec(tm, SEG_W), _mod_spec(d, 6, n_ctx // tm), _const_spec((1, SEG_W)),
            _const_spec(w_out.shape), _const_spec((1, d)), _const_spec(wrh.shape), _const_spec(wrl.shape),
            _const_spec((SEG_W, SEG_W)),
        ],
        out_specs=out_specs,
        out_shape=out_shape,
        compiler_params=_cparams(("parallel", "parallel")),
    )(x, of, ob, rg, att, modtab, gn_g, w_out, ng2, wrh, wrl, bd)


def _s5_kernel(u_ref, perm_ref, tm_ref, wz_ref, wo_ref, ar_ref, ai_ref, d_ref, z_ref,
               zr_s, zi_s, xr_s, xi_s, zs_s, *, n_sub, n_sub_ctx):
    s_n, j_sub, gb, half = n_sub, S5_SUB, S5_BLOCK_GROUPS, 2 * S5_STATE
    cw = z_ref.shape[-1]
    uf = jnp.concatenate([u_ref[0, pl.ds(m, s_n, stride=j_sub), :] for m in range(j_sub)], axis=1)
    perm = perm_ref[...]
    ug = _dot(uf.astype(BF16), perm).astype(BF16)
    for g in range(gb):
        zz = _dot(ug[:, g * cw:(g + 1) * cw], wz_ref[0, 0, g])
        zr_s[pl.ds(g, s_n, stride=gb), :] = zz[:, :half]
        zi_s[pl.ds(g, s_n, stride=gb), :] = zz[:, half:]
    fwd_lane = lax.broadcasted_iota(jnp.int32, (gb, half), 1) < S5_STATE
    ar = ar_ref[0, 0]
    ai = ai_ref[0, 0]

    def body(i, carry):
        xr, xi = carry
        sb = jnp.where(i < n_sub_ctx, n_sub_ctx - 1 - i, s_n - 1 - (i - n_sub_ctx))
        fwd = pl.ds(pl.multiple_of(i * gb, gb), gb)
        bwd = pl.ds(pl.multiple_of(sb * gb, gb), gb)
        xr_s[fwd, 0:S5_STATE] = xr[:, 0:S5_STATE]
        xi_s[fwd, 0:S5_STATE] = xi[:, 0:S5_STATE]
        xr_s[bwd, S5_STATE:half] = xr[:, S5_STATE:half]
        xi_s[bwd, S5_STATE:half] = xi[:, S5_STATE:half]
        zr = jnp.where(fwd_lane, zr_s[fwd, :], zr_s[bwd, :])
        zi = jnp.where(fwd_lane, zi_s[fwd, :], zi_s[bwd, :])
        return ar * xr - ai * xi + zr, ar * xi + ai * xr + zi

    zero = jnp.zeros((gb, half), F32)
    lax.fori_loop(0, s_n, body, (zero, zero), unroll=S5_SCAN_UNROLL)
    parts = []
    for g in range(gb):
        rows = pl.ds(g, s_n, stride=gb)
        xp = jnp.concatenate([xr_s[rows, :], xi_s[rows, :]], axis=1).astype(BF16)
        parts.append(_dot(ug[:, g * cw:(g + 1) * cw], tm_ref[0, 0, g]) + _dot(xp, wo_ref[0, 0, g]))
    y = _dot_nt(jnp.concatenate(parts, axis=1).astype(BF16), perm) + d_ref[0, 0] * uf
    z = _gelu_tanh(y)
    for n in range(j_sub):
        zs_s[pl.ds(n, s_n, stride=j_sub), :] = z[:, n * cw:(n + 1) * cw]
    z_ref[0] = zs_s[...].astype(BF16)


def _s5_scan(li, u, perm, tmat, wz, wo, ar, ai, dt, n_ctx):
    b, t, d = u.shape
    cw = S5_BLOCK_GROUPS * S5_GROUP
    n_sub = t // S5_SUB
    lw = S5_SUB * cw
    sw = 4 * S5_STATE
    gb = S5_BLOCK_GROUPS
    return pl.pallas_call(
        functools.partial(_s5_kernel, n_sub=n_sub, n_sub_ctx=n_ctx // S5_SUB),
        grid=(d // cw, b),
        in_specs=[
            pl.BlockSpec((1, t, cw), lambda c, bi: (bi, 0, c)),
            pl.BlockSpec((lw, lw), lambda c, bi: (0, 0)),
            pl.BlockSpec((1, 1, gb, cw, cw), lambda c, bi: (li, c, 0, 0, 0)),
            pl.BlockSpec((1, 1, gb, cw, sw), lambda c, bi: (li, c, 0, 0, 0)),
            pl.BlockSpec((1, 1, gb, sw, cw), lambda c, bi: (li, c, 0, 0, 0)),
            pl.BlockSpec((1, 1, gb, 2 * S5_STATE), lambda c, bi: (li, c, 0, 0)),
            pl.BlockSpec((1, 1, gb, 2 * S5_STATE), lambda c, bi: (li, c, 0, 0)),
            pl.BlockSpec((1, 1, 1, lw), lambda c, bi: (li, c, 0, 0)),
        ],
        out_specs=pl.BlockSpec((1, t, cw), lambda c, bi: (bi, 0, c)),
        out_shape=jax.ShapeDtypeStruct((b, t, d), BF16),
        scratch_shapes=[pltpu.VMEM((gb * n_sub, 2 * S5_STATE), F32)] * 4 + [pltpu.VMEM((t, cw), F32)],
        compiler_params=_cparams(("parallel", "parallel")),
    )(u, perm, tmat, wz, wo, ar, ai, dt)


def _s5_lane_regroup():
    j_sub, gb, k_w = S5_SUB, S5_BLOCK_GROUPS, S5_GROUP
    src = jnp.arange(j_sub * gb * k_w)
    m, g, k = src // (gb * k_w), (src // k_w) % gb, src % k_w
    dst = g * (j_sub * k_w) + m * k_w + k
    return (dst[:, None] == jnp.arange(j_sub * gb * k_w)[None, :]).astype(BF16)


def _s5_tables(a_re, a_im, log_dt, b_re, b_im, c_re, c_im, d_skip):
    hp = lax.Precision.HIGHEST
    j_sub = S5_SUB
    gb = S5_BLOCK_GROUPS
    lam_re = jnp.minimum(a_re, -1e-4)
    lam_im = a_im
    dt = jnp.exp(log_dt)[..., None]
    mag = jnp.exp(lam_re * dt)
    bar_re = mag * jnp.cos(lam_im * dt)
    bar_im = mag * jnp.sin(lam_im * dt)
    den = lam_re * lam_re + lam_im * lam_im
    nr, ni = bar_re - 1.0, bar_im
    k_re = (nr * lam_re + ni * lam_im) / den
    k_im = (ni * lam_re - nr * lam_im) / den
    bb_re = k_re[..., None] * b_re[None] - k_im[..., None] * b_im[None]
    bb_im = k_re[..., None] * b_im[None] + k_im[..., None] * b_re[None]
    jj = jnp.arange(j_sub + 1, dtype=F32)[:, None, None, None]
    pmag = jnp.exp(jj * (lam_re * dt)[None])
    pw_re = pmag * jnp.cos(jj * (lam_im * dt)[None])
    pw_im = pmag * jnp.sin(jj * (lam_im * dt)[None])
    pb_re = pw_re[..., None] * bb_re[None] - pw_im[..., None] * bb_im[None]
    pb_im = pw_re[..., None] * bb_im[None] + pw_im[..., None] * bb_re[None]
    cp_re = c_re[None] * pw_re[:, :, :, None, :] - c_im[None] * pw_im[:, :, :, None, :]
    cp_im = c_re[None] * pw_im[:, :, :, None, :] + c_im[None] * pw_re[:, :, :, None, :]
    kern = (jnp.einsum('dgcp,jdgpk->jdgkc', c_re, pb_re[:j_sub], precision=hp)
            - jnp.einsum('dgcp,jdgpk->jdgkc', c_im, pb_im[:j_sub], precision=hp))
    groups = a_re.shape[1]
    nblk = groups // gb
    k_w = S5_GROUP
    p_w = S5_STATE
    gw = j_sub * k_w
    pos = jnp.arange(j_sub)
    lag = pos[None, :] - pos[:, None]
    kf = jnp.where((lag >= 0)[:, :, None, None, None], kern[jnp.clip(lag, 0, j_sub - 1), 0], 0.0)
    kb = jnp.where((lag <= 0)[:, :, None, None, None], kern[jnp.clip(-lag, 0, j_sub - 1), 1], 0.0)
    tmat = (kf + kb).transpose(2, 0, 3, 1, 4).reshape(nblk, gb, gw, gw)
    rev = pos[::-1]
    zc = jnp.concatenate([pb_re[rev, 0], pb_re[pos, 1], pb_im[rev, 0], pb_im[pos, 1]], axis=2)
    wz = zc.transpose(1, 0, 3, 2).reshape(nblk, gb, gw, 4 * p_w)
    orows = lambda cp, d, order: cp[order, d].transpose(1, 3, 0, 2).reshape(nblk, gb, p_w, gw)
    wo = jnp.concatenate([orows(cp_re, 0, pos + 1), orows(cp_re, 1, j_sub - pos),
                          -orows(cp_im, 0, pos + 1), -orows(cp_im, 1, j_sub - pos)], axis=2)
    ar = jnp.concatenate([pw_re[j_sub, 0], pw_re[j_sub, 1]], axis=-1).reshape(nblk, gb, 2 * p_w)
    ai = jnp.concatenate([pw_im[j_sub, 0], pw_im[j_sub, 1]], axis=-1).reshape(nblk, gb, 2 * p_w)
    dt_tile = jnp.tile(d_skip.reshape(nblk, 1, gb * k_w), (1, 1, j_sub))
    return tmat.astype(BF16), wz.astype(BF16), wo.astype(BF16), ar, ai, dt_tile


def _glu_kernel(x_ref, z_ref, mod_ref, w_ref, b_ref, ng2_ref, wrh_ref, wrl_ref, xo_ref, f_ref, aff_ref):
    d = x_ref.shape[-1]
    ag = _dot(z_ref[0], w_ref[...]) + b_ref[...]
    delta = ag[:, :d] * _sigmoid(ag[:, d:])
    _post_mixer(x_ref[0], delta, mod_ref[0, 0], ng2_ref[...], wrh_ref[...], wrl_ref[...], xo_ref, f_ref, aff_ref)


def _glu(x, z, modtab, glu_w, glu_b, ng2, wrh, wrl, tm, n_ctx):
    b, t, d = x.shape
    out_shape, out_specs = _post_specs(b, t, d, tm)
    return pl.pallas_call(
        _glu_kernel,
        grid=(b, t // tm),
        in_specs=[_row_spec(tm, d), _row_spec(tm, d), _mod_spec(d, 6, n_ctx // tm),
                  _const_spec(glu_w.shape), _const_spec((1, 2 * d)), _const_spec((1, d)),
                  _const_spec(wrh.shape), _const_spec(wrl.shape)],
        out_specs=out_specs,
        out_shape=out_shape,
        compiler_params=_cparams(("parallel", "parallel")),
    )(x, z, modtab, glu_w, glu_b, ng2, wrh, wrl)


def _cumsum_lanes(v):
    n = v.shape[-1]
    lane = lax.broadcasted_iota(jnp.int32, v.shape, 1)
    sh = 1
    while sh < n:
        v = v + jnp.where(lane >= sh, pltpu.roll(v, sh, 1), 0.0)
        sh *= 2
    return v


def _select_segment(aff, cap, base):
    bits = pltpu.bitcast(aff, jnp.int32)
    thr = jnp.zeros((aff.shape[0], 1), jnp.int32)
    for bit in range(30, -1, -1):
        cand = thr | (1 << bit)
        cnt = jnp.sum(jnp.where(bits >= cand, 1.0, 0.0), axis=1, keepdims=True)
        thr = jnp.where(cnt >= cap, cand, thr)
    gt = bits > thr
    eq = bits == thr
    need = cap - jnp.sum(jnp.where(gt, 1.0, 0.0), axis=1, keepdims=True)
    eqf = jnp.where(eq, 1.0, 0.0)
    eq_before = _cumsum_lanes(eqf) - eqf
    sel = jnp.where(gt, 1.0, jnp.where(eq, jnp.where(eq_before < need, 1.0, 0.0), 0.0))
    pos = _cumsum_lanes(sel) - sel
    return jnp.where(sel > 0.0, pos.astype(jnp.int32) + base, -1)


def _select_kernel(aff_ref, slot_ref, *, n_ctx, cap_ctx, cap_lat):
    slot_ref[0, :, 0:n_ctx] = _select_segment(aff_ref[0, :, 0:n_ctx], cap_ctx, 0)
    slot_ref[0, :, n_ctx:] = _select_segment(aff_ref[0, :, n_ctx:], cap_lat, cap_ctx)


def _select(aff_t, n_ctx, cap_ctx, cap_lat):
    b, e, t = aff_t.shape
    return pl.pallas_call(
        functools.partial(_select_kernel, n_ctx=n_ctx, cap_ctx=cap_ctx, cap_lat=cap_lat),
        grid=(b,),
        in_specs=[pl.BlockSpec((1, e, t), lambda i: (i, 0, 0))],
        out_specs=pl.BlockSpec((1, e, t), lambda i: (i, 0, 0)),
        out_shape=jax.ShapeDtypeStruct((b, e, t), jnp.int32),
        compiler_params=_cparams(("parallel",)),
    )(aff_t)


def _moe_window(a_ref, base, e, k, rows, w):
    nominal = a_ref[base + e] + k * w
    return nominal, jnp.minimum(nominal, rows - w)


def _moe_hits(slot_row, nominal, actual, w):
    rid = lax.broadcasted_iota(jnp.int32, (w, slot_row.shape[-1]), 0) + actual
    return jnp.where(slot_row >= nominal, slot_row, -1) == rid


def _moe_gather_kernel(a_ref, nr_ref, slot_ref, aff_ref, f_ref, xs_ref, gate_ref, *, rows, w):
    b = pl.program_id(0)
    j = pl.program_id(1)
    n_e = slot_ref.shape[2]
    step = b * pl.num_programs(1) + j

    @pl.when(j == 0)
    def _():
        xs_ref[...] = jnp.zeros_like(xs_ref)
        gate_ref[...] = jnp.zeros_like(gate_ref)

    fj = f_ref[0]

    def one_round(k, carry):
        for e in range(n_e):
            nominal, actual = _moe_window(a_ref, step * n_e, e, k, rows, w)
            hit = _moe_hits(slot_ref[0, 0, e:e + 1, :], nominal, actual, w)
            off = pl.multiple_of(e * rows + actual, 16)
            xs_ref[0, pl.ds(off, w), :] += _dot(jnp.where(hit, 1.0, 0.0).astype(BF16), fj).astype(BF16)
            gate_ref[0, pl.ds(off, w), :] += jnp.sum(jnp.where(hit, aff_ref[0, 0, e:e + 1, :], 0.0),
                                                     axis=1, keepdims=True)
        return carry

    lax.fori_loop(0, nr_ref[step], one_round, 0)


def _moe_ffn_kernel(xs_ref, gate_ref, w1_ref, w3_ref, w2_ref, y_ref, ys_s, *, n_ft):
    ft = pl.program_id(2)
    nb, rows, d = xs_ref.shape
    xs = xs_ref[...].reshape(nb * rows, d)
    hid = (_silu(_dot(xs, w1_ref[0, 0].astype(BF16))) * _dot(xs, w3_ref[0, 0].astype(BF16))).astype(BF16)
    part = _dot(hid, w2_ref[0, 0].astype(BF16))

    @pl.when(ft == 0)
    def _():
        ys_s[...] = part

    @pl.when(ft > 0)
    def _():
        ys_s[...] += part

    @pl.when(ft == n_ft - 1)
    def _():
        y_ref[...] = (ys_s[...] * gate_ref[...].reshape(nb * rows, 1)).astype(BF16).reshape(nb, rows, d)


def _moe_combine_kernel(a_ref, nr_ref, slot_ref, y_ref, x_ref, mod_ref, *rest, rows, w, nb, lead, tail):
    o_ref = rest[-1] if tail != "s5" else rest[-2]
    b = pl.program_id(0)
    n_e = slot_ref.shape[2]
    step = b * nb + pl.program_id(1) + lead

    def one_round(k, acc):
        onehots, ys = [], []
        for e in range(n_e):
            nominal, actual = _moe_window(a_ref, step * n_e, e, k, rows, w)
            hit = _moe_hits(slot_ref[0, 0, e:e + 1, :], nominal, actual, w)
            onehots.append(jnp.where(hit, 1.0, 0.0).astype(BF16))
            ys.append(y_ref[0, pl.ds(pl.multiple_of(e * rows + actual, 16), w), :])
        return acc + _dot_tn(jnp.concatenate(onehots, axis=0), jnp.concatenate(ys, axis=0))

    acc = lax.fori_loop(0, nr_ref[step], one_round, jnp.zeros(o_ref.shape[1:], F32))
    xn = x_ref[0] + mod_ref[0, 0, 5:6, :] * acc
    if tail == "final":
        xn = xn * lax.rsqrt(jnp.mean(xn * xn, axis=-1, keepdims=True) + EPS) * rest[0][...]
    o_ref[0] = xn
    if tail == "s5":
        m = rest[0][0, 0]
        rest[3][0] = _norm_mod(xn, rest[1][...], m[1:2], m[0:1])


def _moe_windows(slot, tb, w):
    b, e, t = slot.shape
    nb = t // tb
    cnt = jnp.sum((slot >= 0).reshape(b, e, nb, tb), axis=-1, dtype=jnp.int32)
    start = jnp.cumsum(cnt, axis=-1) - cnt
    a0 = (start // 16) * 16
    rounds = jnp.maximum(jnp.max((start + cnt - a0 + w - 1) // w, axis=1), 1)
    return jnp.swapaxes(a0, 1, 2).reshape(-1).astype(jnp.int32), rounds.reshape(-1).astype(jnp.int32)


def _route_and_moe(layer, x, aff, f, modtab, w1, w3, w2, n_ctx, tb, tf, final_g=None, next_s5=None):
    b, t, d = f.shape
    n_e = w1.shape[1]
    ff = w1.shape[-1]
    nb = t // tb
    aff_t = jnp.swapaxes(aff[:, :, :N_EXPERTS], 1, 2)
    cap_ctx = (CAPACITY_FACTOR * n_ctx) // N_EXPERTS
    cap_lat = (CAPACITY_FACTOR * (t - n_ctx)) // N_EXPERTS
    rows = cap_ctx + cap_lat
    w = min(MOE_WINDOW, rows)
    slot = _select(aff_t, n_ctx, cap_ctx, cap_lat)
    a0, rounds = _moe_windows(slot, tb, w)
    by_block = lambda v: jnp.swapaxes(v.reshape(b, n_e, nb, tb), 1, 2)
    slot_b, aff_b = by_block(slot), by_block(aff_t)
    blk_spec = pl.BlockSpec((1, 1, n_e, tb), lambda bi, j, *_: (bi, j, 0, 0))
    whole = lambda width: pl.BlockSpec((1, n_e * rows, width), lambda bi, j, *_: (bi, 0, 0))
    xs, gate = pl.pallas_call(
        functools.partial(_moe_gather_kernel, rows=rows, w=w),
        grid_spec=pltpu.PrefetchScalarGridSpec(
            num_scalar_prefetch=2, grid=(b, nb),
            in_specs=[blk_spec, blk_spec, pl.BlockSpec((1, tb, d), lambda bi, j, *_: (bi, j, 0))],
            out_specs=[whole(d), whole(1)]),
        out_shape=[jax.ShapeDtypeStruct((b, n_e * rows, d), BF16), jax.ShapeDtypeStruct((b, n_e * rows, 1), F32)],
        compiler_params=_cparams(("parallel", "arbitrary")),
    )(a0, rounds, slot_b, aff_b, f)
    n_ft = ff // tf
    sb = MOE_FFN_SAMPLES
    y = pl.pallas_call(
        functools.partial(_moe_ffn_kernel, n_ft=n_ft),
        grid=(b // sb, n_e, n_ft),
        in_specs=[
            pl.BlockSpec((sb, rows, d), lambda bi, e, k: (bi, e, 0)),
            pl.BlockSpec((sb, rows, 1), lambda bi, e, k: (bi, e, 0)),
            pl.BlockSpec((1, 1, d, tf), lambda bi, e, k: (layer, e, 0, k)),
            pl.BlockSpec((1, 1, d, tf), lambda bi, e, k: (layer, e, 0, k)),
            pl.BlockSpec((1, 1, tf, d), lambda bi, e, k: (layer, e, k, 0)),
        ],
        out_specs=pl.BlockSpec((sb, rows, d), lambda bi, e, k: (bi, e, 0)),
        out_shape=jax.ShapeDtypeStruct((b, n_e * rows, d), BF16),
        scratch_shapes=[pltpu.VMEM((sb * rows, d), F32)],
        compiler_params=_cparams(("parallel", "parallel", "arbitrary")),
    )(xs, gate, w1, w3, w2)
    seg_blocks = n_ctx // tb
    lead = 0 if final_g is None else seg_blocks
    seg_of = lambda bi, j, *_: (bi, jnp.where(j + lead >= seg_blocks, 1, 0), 0, 0)
    gain_spec = pl.BlockSpec((1, d), lambda bi, j, *_: (0, 0))
    out_spec = pl.BlockSpec((1, tb, d), lambda bi, j, *_: (bi, j, 0))
    out_sds = jax.ShapeDtypeStruct((b, t - lead * tb, d), F32)
    tail, extra_args, extra_specs, out_specs, out_shape = "plain", (), [], out_spec, out_sds
    if final_g is not None:
        tail, extra_args, extra_specs = "final", (final_g,), [gain_spec]
    elif next_s5 is not None:
        tail, extra_args, extra_specs = "s5", tuple(next_s5), [pl.BlockSpec((1, 1, 6, d), seg_of), gain_spec]
        out_specs, out_shape = [out_spec, out_spec], [out_sds, out_sds]
    return pl.pallas_call(
        functools.partial(_moe_combine_kernel, rows=rows, w=w, nb=nb, lead=lead, tail=tail),
        grid_spec=pltpu.PrefetchScalarGridSpec(
            num_scalar_prefetch=2, grid=(b, nb - lead),
            in_specs=[pl.BlockSpec((1, 1, n_e, tb), lambda bi, j, *_: (bi, j + lead, 0, 0)), whole(d),
                      pl.BlockSpec((1, tb, d), lambda bi, j, *_: (bi, j + lead, 0)),
                      pl.BlockSpec((1, 1, 6, d), seg_of)]
            + extra_specs,
            out_specs=out_specs),
        out_shape=out_shape,
        compiler_params=_cparams(("parallel", "parallel")),
    )(a0, rounds, slot_b, y, x, modtab, *extra_args)


def _rope_tables(n_ctx, n_lat):
    n_rows = n_lat // GRID_W
    row = jnp.repeat(jnp.arange(n_rows, dtype=F32), GRID_W)
    col = jnp.tile(jnp.arange(GRID_W, dtype=F32), n_rows)
    inv = ROPE_THETA ** (-jnp.arange(0, ROPE_AXIS_DIM, 2, dtype=F32) / ROPE_AXIS_DIM)
    ang_r = row[:, None] * inv[None, :]
    ang_c = col[:, None] * inv[None, :]
    cos = jnp.concatenate([jnp.cos(ang_r), jnp.cos(ang_r), jnp.cos(ang_c), jnp.cos(ang_c)], axis=-1)
    sin = jnp.concatenate([-jnp.sin(ang_r), jnp.sin(ang_r), -jnp.sin(ang_c), jnp.sin(ang_c)], axis=-1)
    cos = jnp.concatenate([jnp.ones((n_ctx, HEAD_DIM), F32), cos], axis=0)
    sin = jnp.concatenate([jnp.zeros((n_ctx, HEAD_DIM), F32), sin], axis=0)
    reps = SEG_W // HEAD_DIM
    return jnp.tile(cos, (1, reps)), jnp.tile(sin, (1, reps))


def _extend_in_proj(w_in):
    base = 4 * RET_WIDTH + ATT_WIDTH
    ak = w_in[:, base:base + ATT_KV_WIDTH].reshape(-1, ATT_KV_HEADS, 1, HEAD_DIM)
    av = w_in[:, base + ATT_KV_WIDTH:].reshape(-1, ATT_KV_HEADS, 1, HEAD_DIM)
    rep = lambda w, n: jnp.broadcast_to(w, (w.shape[0], ATT_KV_HEADS, n, HEAD_DIM)).reshape(w.shape[0], -1)
    return jnp.concatenate([w_in[:, :base], rep(ak, ATT_GROUP), rep(av, 2)], axis=-1)


def kernel(x, c, ctx, c_ctx, mod_w, mod_b, norm_g, mix_in_w, mix_out_w, ret_log_rate, ret_gn_g, qk_norm_g,
           s5_a_re, s5_a_im, s5_log_dt, s5_b_re, s5_b_im, s5_c_re, s5_c_im, s5_d, s5_glu_w, s5_glu_b,
           moe_router_w, moe_w1, moe_w3, moe_w2, final_norm_g):
    batch, n_lat, d = x.shape
    n_ctx = ctx.shape[1]
    t = n_ctx + n_lat
    depth = mod_w.shape[0]
    assert batch == 8 and d % 128 == 0 and n_ctx % CHUNK == 0 and n_lat % CHUNK == 0
    tm = 256 if n_ctx % 256 == 0 else 128
    tb = tm
    tf = 1024

    xs = jnp.concatenate([ctx, x], axis=1)
    rows = 16
    cond = jnp.zeros((rows, d), F32).at[:batch].set(c).at[batch].set(c_ctx)
    mod_all = _modulation(cond, mod_w, mod_b)
    cos, sin = _rope_tables(n_ctx, n_lat)
    lane_head = jnp.arange(SEG_W) // HEAD_DIM
    bd = (lane_head[:, None] == lane_head[None, :]).astype(BF16)
    w1, w3, w2 = moe_w1, moe_w3, moe_w2
    s5_perm = _s5_lane_regroup()
    s5_tabs = jax.vmap(_s5_tables)(s5_a_re, s5_a_im, s5_log_dt, s5_b_re, s5_b_im, s5_c_re, s5_c_im, s5_d)

    def modtab_of(layer):
        m = mod_all[layer]
        lat = m[:batch].reshape(batch, 1, 6, d)
        cx = jnp.broadcast_to(m[batch].reshape(1, 1, 6, d), (batch, 1, 6, d))
        return jnp.concatenate([cx, lat], axis=1)

    u = None
    for layer in range(depth):
        modtab = modtab_of(layer)
        ng1 = norm_g[layer, 0].reshape(1, d)
        ng2 = norm_g[layer, 1].reshape(1, d)
        wr = jnp.zeros((d, ROUTER_LANES), F32).at[:, :N_EXPERTS].set(moe_router_w[layer])
        wrh = wr.astype(BF16)
        wrl = (wr - wrh.astype(F32)).astype(BF16)
        i = layer // 2
        if layer % 2 == 0:
            w_ext = _extend_in_proj(mix_in_w[i]).astype(BF16)
            reps = SEG_W // HEAD_DIM
            qkg = jnp.stack([jnp.tile(qk_norm_g[i, 0], reps), jnp.tile(qk_norm_g[i, 1], reps)])
            rq, rk, rv, rg, aq, ak4, av2 = _in_proj(xs, modtab, ng1, w_ext, cos, sin, qkg, bd, tm, n_ctx)
            rate = jnp.repeat(ret_log_rate[i], HEAD_DIM, axis=-1)
            rate_h = jnp.broadcast_to(ret_log_rate[i][:, :, None], (2, RET_HEADS, CHUNK))
            o_f, o_b = _retention(rq, rk, rv, rate, rate_h, bd, n_ctx)
            att = _attention(aq, ak4, av2, tm, tm, n_ctx)
            xs, f, aff = _out_proj(xs, o_f, o_b, rg, att, modtab, ret_gn_g[i].reshape(1, SEG_W),
                                   mix_out_w[i].astype(BF16), ng2, wrh, wrl, bd, tm, n_ctx)
        else:
            z = _s5_scan(i, u, s5_perm, *s5_tabs, n_ctx)
            xs, f, aff = _glu(xs, z, modtab, s5_glu_w[i].astype(BF16), s5_glu_b[i].reshape(1, 2 * d),
                              ng2, wrh, wrl, tm, n_ctx)
        final_g = final_norm_g.reshape(1, d) if layer == depth - 1 else None
        next_s5 = None
        if layer + 1 < depth and (layer + 1) % 2 == 1:
            next_s5 = (modtab_of(layer + 1), norm_g[layer + 1, 0].reshape(1, d))
        xs = _route_and_moe(layer, xs, aff, f, modtab, w1, w3, w2, n_ctx, tb, tf, final_g, next_s5)
        u = None
        if next_s5 is not None:
            xs, u = xs
    return xs
```

```python
import functools
import math

import jax
import jax.numpy as jnp
from jax import lax
from jax.experimental import pallas as pl
from jax.experimental.pallas import tpu as pltpu

F32 = jnp.float32
BF16 = jnp.bfloat16

GRID_W = 64
HEAD_DIM = 64
RET_HEADS = 8
ATT_HEADS = 8
ATT_KV_HEADS = 2
ATT_GROUP = ATT_HEADS // ATT_KV_HEADS
RET_WIDTH = RET_HEADS * HEAD_DIM
ATT_WIDTH = ATT_HEADS * HEAD_DIM
ATT_KV_WIDTH = ATT_KV_HEADS * HEAD_DIM
CHUNK = 128
ROPE_THETA = 10000.0
ROPE_AXIS_DIM = HEAD_DIM // 2
S5_GROUP = 16
S5_STATE = 64
S5_SUB = 8
S5_BLOCK_GROUPS = 8
S5_SCAN_UNROLL = 4
N_EXPERTS = 16
CAPACITY_FACTOR = 2
EPS = 1e-6
SEG_W = 512
LOG2_E = math.log2(math.e)
ROUTER_LANES = 128
MOE_WINDOW = 64
MOE_FFN_SAMPLES = 2

V7X_VMEM_BYTES = 64 * 1024 * 1024
VMEM_LIMIT = V7X_VMEM_BYTES - 8 * 1024 * 1024


def _cparams(sem):
    return pltpu.CompilerParams(dimension_semantics=sem, vmem_limit_bytes=VMEM_LIMIT)


def _dot(a, b):
    return jnp.dot(a, b, preferred_element_type=F32)


def _dot_nt(a, b):
    return lax.dot_general(a, b, (((1,), (1,)), ((), ())), preferred_element_type=F32)


def _dot_tn(a, b):
    return lax.dot_general(a, b, (((0,), (0,)), ((), ())), preferred_element_type=F32)


def _split(v):
    hi = v.astype(BF16)
    lo = (v - hi.astype(F32)).astype(BF16)
    return hi, lo


def _segsum(v, bd):
    return _dot(v.astype(BF16), bd)


def _sigmoid(v):
    return 1.0 / (1.0 + jnp.exp(-v))


def _silu(v):
    return v * _sigmoid(v)


def _gelu_tanh(v):
    c = math.sqrt(2.0 / math.pi)
    return 0.5 * v * (1.0 + jnp.tanh(c * (v + 0.044715 * (v * v * v))))


def _norm_mod(x, g, scale, shift):
    ms = jnp.mean(x * x, axis=-1, keepdims=True)
    return x * lax.rsqrt(ms + EPS) * g * (1.0 + scale) + shift


def _rope(x, cos, sin_signed):
    w = x.shape[-1]
    lane = lax.broadcasted_iota(jnp.int32, x.shape, 1)
    first = (lane % 32) < 16
    partner = jnp.where(first, pltpu.roll(x, w - 16, 1), pltpu.roll(x, 16, 1))
    return x * cos + partner * sin_signed


def _mod_kernel(c_ref, w_ref, b_ref, o_ref):
    s = _silu(c_ref[...]).astype(BF16)
    o_ref[0] = _dot(s, w_ref[0].astype(BF16)) + b_ref[0]


def _modulation(cond, mod_w, mod_b):
    depth, d, n6 = mod_w.shape
    rows = cond.shape[0]
    tn = n6 // 6
    return pl.pallas_call(
        _mod_kernel,
        grid=(depth, n6 // tn),
        in_specs=[
            pl.BlockSpec((rows, d), lambda l, j: (0, 0)),
            pl.BlockSpec((1, d, tn), lambda l, j: (l, 0, j)),
            pl.BlockSpec((1, 1, tn), lambda l, j: (l, 0, j)),
        ],
        out_specs=pl.BlockSpec((1, rows, tn), lambda l, j: (l, 0, j)),
        out_shape=jax.ShapeDtypeStruct((depth, rows, n6), F32),
        compiler_params=_cparams(("parallel", "parallel")),
    )(cond, mod_w, mod_b.reshape(depth, 1, n6))


def _in_kernel(x_ref, mod_ref, ng_ref, w_ref, cos_ref, sin_ref, qkg_ref, bd_ref,
               rq_ref, rk_ref, rv_ref, rg_ref, aq_ref, ak_ref, av_ref):
    x = x_ref[0]
    m = mod_ref[0, 0]
    h = _norm_mod(x, ng_ref[...], m[1:2], m[0:1]).astype(BF16)
    cos = cos_ref[...]
    sin = sin_ref[...]
    bd = bd_ref[...]

    def seg(i):
        return _dot(h, w_ref[:, i * SEG_W:(i + 1) * SEG_W])

    def headnorm(v, g):
        ms = _segsum(v * v, bd) * (1.0 / HEAD_DIM)
        return v * lax.rsqrt(ms + EPS) * g

    scale = HEAD_DIM ** -0.5
    rq_ref[0] = _rope(seg(0), cos, sin).astype(BF16)
    rk_ref[0] = _rope(seg(1) * scale, cos, sin).astype(BF16)
    rv_ref[0] = seg(2).astype(BF16)
    rg_ref[0] = seg(3)
    aq_ref[0] = (_rope(headnorm(seg(4), qkg_ref[0:1]), cos, sin) * (scale * LOG2_E)).astype(BF16)
    ak_ref[0] = _rope(headnorm(seg(5), qkg_ref[1:2]), cos, sin).astype(BF16)
    av_ref[0] = _dot(h, w_ref[:, 6 * SEG_W:]).astype(BF16)


def _row_spec(tm, d, lead=0):
    return pl.BlockSpec((1, tm, d), lambda b, i: (b, i + lead, 0))


def _mod_spec(d, n, seg_tiles):
    return pl.BlockSpec((1, 1, n, d), lambda b, i: (b, jnp.where(i >= seg_tiles, 1, 0), 0, 0))


def _const_spec(shape):
    nd = len(shape)
    return pl.BlockSpec(shape, lambda b, i: (0,) * nd)


def _in_proj(x, modtab, norm_g, w_ext, cos, sin, qkg, bd, tm, n_ctx):
    b, t, d = x.shape
    out_shape, out_specs = [], []
    for dt, width in ((BF16, SEG_W), (BF16, SEG_W), (BF16, SEG_W), (F32, SEG_W), (BF16, SEG_W), (BF16, SEG_W),
                      (BF16, ATT_KV_HEADS * 2 * HEAD_DIM)):
        out_shape.append(jax.ShapeDtypeStruct((b, t, width), dt))
        out_specs.append(_row_spec(tm, width))
    return pl.pallas_call(
        _in_kernel,
        grid=(b, t // tm),
        in_specs=[
            _row_spec(tm, d),
            _mod_spec(d, 6, n_ctx // tm),
            _const_spec((1, d)),
            _const_spec(w_ext.shape),
            pl.BlockSpec((tm, SEG_W), lambda b, i: (i, 0)),
            pl.BlockSpec((tm, SEG_W), lambda b, i: (i, 0)),
            _const_spec((2, SEG_W)),
            _const_spec((SEG_W, SEG_W)),
        ],
        out_specs=out_specs,
        out_shape=out_shape,
        compiler_params=_cparams(("parallel", "parallel")),
    )(x, modtab, norm_g, w_ext, cos, sin, qkg, bd)


def _ret_kernel(qf_ref, kf_ref, vf_ref, qb_ref, kb_ref, vb_ref, rate_ref, rate_h_ref, bd_ref,
                of_ref, ob_ref, sf_s, sb_s, dm_s, dq_s, dk_s, dc_s):
    s = pl.program_id(1)
    c = CHUNK

    @pl.when(s == 0)
    def _():
        sf_s[...] = jnp.zeros_like(sf_s)
        sb_s[...] = jnp.zeros_like(sb_s)
        pos = lax.broadcasted_iota(jnp.int32, (c, SEG_W), 0).astype(F32)
        lgf = -jnp.exp(rate_ref[0:1, :])
        lgb = -jnp.exp(rate_ref[1:2, :])
        dq_s[0] = jnp.exp((pos + 1.0) * lgf)
        dk_s[0] = jnp.exp((c - 1.0 - pos) * lgf)
        dc_s[0] = jnp.exp(float(c) * lgf)
        dq_s[1] = jnp.exp((c - pos) * lgb)
        dk_s[1] = jnp.exp(pos * lgb)
        dc_s[1] = jnp.exp(float(c) * lgb)
        ii = lax.broadcasted_iota(jnp.int32, (c, c), 0)
        jj = lax.broadcasted_iota(jnp.int32, (c, c), 1)
        for h in range(RET_HEADS):
            lf = -jnp.exp(rate_h_ref[0, h:h + 1, :])
            lb = -jnp.exp(rate_h_ref[1, h:h + 1, :])
            mf = ii >= jj
            mb = jj > ii
            dm_s[0, h] = jnp.where(mf, jnp.exp(jnp.where(mf, ii - jj, 0).astype(F32) * lf), 0.0)
            dm_s[1, h] = jnp.where(mb, jnp.exp(jnp.where(mb, jj - ii, 0).astype(F32) * lb), 0.0)

    bdm = bd_ref[...].astype(F32)
    head_of_lane = lax.broadcasted_iota(jnp.int32, (1, SEG_W // 2), 1) // HEAD_DIM

    def direction(d, q_ref, k_ref, v_ref, o_ref, st, sub):
        rows = slice(sub * c, (sub + 1) * c)
        q = q_ref[0, rows, :]
        k = k_ref[0, rows, :]
        v = v_ref[0, rows, :]
        state = st[...]
        qd = (q.astype(F32) * dq_s[d]).astype(BF16)
        inter = _dot(qd, state.astype(BF16))
        for half in range(2):
            sl = slice(half * 256, half * 256 + 256)
            qh, kh, vh = q[:, sl], k[:, sl], v[:, sl]
            acc = inter[:, sl]
            for hh in range(4):
                msk = head_of_lane == hh
                sc = _dot_nt(jnp.where(msk, qh, jnp.zeros_like(qh)), kh) * dm_s[d, half * 4 + hh]
                acc = acc + _dot(sc.astype(BF16), jnp.where(msk, vh, jnp.zeros_like(vh)))
            o_ref[0, rows, sl] = acc
        kd = (k.astype(F32) * dk_s[d]).astype(BF16)
        st[...] = state * dc_s[d] + bdm * _dot_tn(kd, v)

    n_sub = qf_ref.shape[1] // c
    for sub in range(n_sub):
        direction(0, qf_ref, kf_ref, vf_ref, of_ref, sf_s, sub)
        direction(1, qb_ref, kb_ref, vb_ref, ob_ref, sb_s, n_sub - 1 - sub)


def _retention(rq, rk, rv, rate, rate_h, bd, n_ctx):
    b, t, w = rq.shape
    cps = 2 if (t // CHUNK) % 2 == 0 and (n_ctx // CHUNK) % 2 == 0 else 1
    nc = t // (cps * CHUNK)
    lc = n_ctx // (cps * CHUNK)

    def fwd(bi, s):
        return (bi, s, 0)

    def bwd(bi, s):
        return (bi, jnp.where(s < lc, lc - 1 - s, nc - 1 - (s - lc)), 0)

    blk = (1, cps * CHUNK, w)
    return pl.pallas_call(
        _ret_kernel,
        grid=(b, nc),
        in_specs=[pl.BlockSpec(blk, fwd)] * 3 + [pl.BlockSpec(blk, bwd)] * 3 + [
            _const_spec((2, w)),
            _const_spec((2, RET_HEADS, CHUNK)),
            _const_spec((w, w)),
        ],
        out_specs=[pl.BlockSpec(blk, fwd), pl.BlockSpec(blk, bwd)],
        out_shape=[jax.ShapeDtypeStruct((b, t, w), F32)] * 2,
        scratch_shapes=[
            pltpu.VMEM((w, w), F32), pltpu.VMEM((w, w), F32),
            pltpu.VMEM((2, RET_HEADS, CHUNK, CHUNK), F32),
            pltpu.VMEM((2, CHUNK, w), F32), pltpu.VMEM((2, CHUNK, w), F32),
            pltpu.VMEM((2, 1, w), F32),
        ],
        compiler_params=_cparams(("parallel", "arbitrary")),
    )(rq, rk, rv, rq, rk, rv, rate, rate_h, bd)


ATT_ROW_BLOCK = 512


def _att_kernel(q_ref, k_ref, v_ref, o_ref, qs_s, m_s, acc_s, sa_s, sb_s, ma_s, mb_s, *, tq, tk, n_ctx, t):
    qi = pl.program_id(1)
    gw = ATT_GROUP * HEAD_DIM
    head_of_lane = lax.broadcasted_iota(jnp.int32, (1, gw), 1) // HEAD_DIM
    for h in range(ATT_HEADS):
        q = q_ref[0, :, (h // ATT_GROUP) * gw:(h // ATT_GROUP + 1) * gw]
        qs_s[h * tq:(h + 1) * tq, :] = jnp.where(head_of_lane == h % ATT_GROUP, q, jnp.zeros_like(q))
    m_s[...] = jnp.full(m_s.shape, -1e30, F32)
    acc_s[...] = jnp.zeros_like(acc_s)
    nk = jnp.where(qi * tq < n_ctx, n_ctx // tk, t // tk)
    vw = 2 * HEAD_DIM
    ones_lane = lax.broadcasted_iota(jnp.int32, (1, ATT_KV_HEADS * vw), 1) % vw >= HEAD_DIM
    rb = ATT_ROW_BLOCK

    blocks = [(slice(r * rb, (r + 1) * rb), (r * rb) // (ATT_GROUP * tq)) for r in range(ATT_HEADS * tq // rb)]

    def scores_into(j, dst, dst_max):
        kb = k_ref[0, pl.ds(pl.multiple_of(j * tk, tk), tk), :]
        for rows, g in blocks:
            sc = _dot_nt(qs_s[rows, :], kb[:, g * gw:(g + 1) * gw])
            dst[rows, :] = sc
            dst_max[rows, :] = jnp.broadcast_to(jnp.max(sc, axis=1, keepdims=True), (rb, vw))

    def consume(j, src, src_max):
        vb = v_ref[0, pl.ds(pl.multiple_of(j * tk, tk), tk), :]
        vext = jnp.where(ones_lane, jnp.ones_like(vb), vb)
        for rows, g in blocks:
            m_prev = m_s[rows, :]
            m_new = jnp.maximum(m_prev, src_max[rows, :])
            p = jnp.exp2(src[rows, :] - jnp.concatenate([m_new] * (tk // vw), axis=1)).astype(BF16)
            acc_s[rows, :] = jnp.exp2(m_prev - m_new) * acc_s[rows, :] + _dot(p, vext[:, g * vw:(g + 1) * vw])
            m_s[rows, :] = m_new

    scores_into(0, sa_s, ma_s)
    pairs = max(p for p in (4, 2, 1) if p == 1 or ((n_ctx // tk) % (2 * p) == 1 and (t // tk) % (2 * p) == 1))

    def body(i, carry):
        for u in range(pairs):
            j = 2 * (pairs * i + u)
            scores_into(j + 1, sb_s, mb_s)
            consume(j, sa_s, ma_s)
            scores_into(jnp.minimum(j + 2, nk - 1), sa_s, ma_s)
            consume(j + 1, sb_s, mb_s)
        return carry

    lax.fori_loop(0, nk // (2 * pairs), body, 0)

    @pl.when(nk % 2 == 1)
    def _():
        consume(nk - 1, sa_s, ma_s)

    low = lax.broadcasted_iota(jnp.int32, (tq, vw), 1) < HEAD_DIM
    for pair in range(ATT_HEADS // 2):
        a0 = acc_s[(2 * pair) * tq:(2 * pair + 1) * tq, :]
        a1 = acc_s[(2 * pair + 1) * tq:(2 * pair + 2) * tq, :]
        even = a0 / pltpu.roll(a0, HEAD_DIM, 1)
        odd = pltpu.roll(a1, HEAD_DIM, 1) / a1
        o_ref[0, :, pair * vw:(pair + 1) * vw] = jnp.where(low, even, odd).astype(BF16)


def _attention(aq, ak4, av2, tq, tk, n_ctx):
    b, t, w = aq.shape
    gw = ATT_GROUP * HEAD_DIM
    vw = 2 * HEAD_DIM
    rows = ATT_HEADS * tq
    return pl.pallas_call(
        functools.partial(_att_kernel, tq=tq, tk=tk, n_ctx=n_ctx, t=t),
        grid=(b, t // tq),
        in_specs=[
            pl.BlockSpec((1, tq, w), lambda bi, i: (bi, i, 0)),
            pl.BlockSpec((1, t, w), lambda bi, i: (bi, 0, 0)),
            pl.BlockSpec((1, t, ATT_KV_HEADS * vw), lambda bi, i: (bi, 0, 0)),
        ],
        out_specs=pl.BlockSpec((1, tq, w), lambda bi, i: (bi, i, 0)),
        out_shape=jax.ShapeDtypeStruct((b, t, w), BF16),
        scratch_shapes=[pltpu.VMEM((rows, gw), BF16), pltpu.VMEM((rows, vw), F32), pltpu.VMEM((rows, vw), F32),
                        pltpu.VMEM((rows, tk), F32), pltpu.VMEM((rows, tk), F32), pltpu.VMEM((rows, vw), F32),
                        pltpu.VMEM((rows, vw), F32)],
        compiler_params=_cparams(("parallel", "parallel")),
    )(aq, ak4, av2)


def _post_mixer(x, delta, m, ng2, wrh, wrl, x_ref, f_ref, aff_ref):
    xn = x + m[2:3] * delta
    x_ref[0] = xn
    f = _norm_mod(xn, ng2, m[4:5], m[3:4])
    fh, fl = _split(f)
    f_ref[0] = fh
    logits = _dot(fh, wrh) + _dot(fl, wrh) + _dot(fh, wrl)
    lane = lax.broadcasted_iota(jnp.int32, logits.shape, 1)
    logits = jnp.where(lane < N_EXPERTS, logits, -1e30)
    e = jnp.exp(logits - jnp.max(logits, axis=1, keepdims=True))
    aff_ref[0] = e / jnp.sum(e, axis=1, keepdims=True)


def _out_kernel(x_ref, of_ref, ob_ref, rg_ref, att_ref, mod_ref, gn_ref, w_ref, ng2_ref, wrh_ref, wrl_ref,
                bd_ref, xo_ref, f_ref, aff_ref):
    bd = bd_ref[...]
    o = of_ref[0] + ob_ref[0]
    mu = _segsum(o, bd) * (1.0 / HEAD_DIM)
    oc = o - mu
    var = _segsum(oc * oc, bd) * (1.0 / HEAD_DIM)
    r = _silu(rg_ref[0]) * (oc * lax.rsqrt(var + EPS) * gn_ref[...])
    y = _dot(r.astype(BF16), w_ref[0:RET_WIDTH, :]) + _dot(att_ref[0], w_ref[RET_WIDTH:, :])
    _post_mixer(x_ref[0], y, mod_ref[0, 0], ng2_ref[...], wrh_ref[...], wrl_ref[...], xo_ref, f_ref, aff_ref)


def _post_specs(b, t, d, tm):
    out_shape = [jax.ShapeDtypeStruct((b, t, d), F32), jax.ShapeDtypeStruct((b, t, d), BF16),
                 jax.ShapeDtypeStruct((b, t, ROUTER_LANES), F32)]
    out_specs = [_row_spec(tm, d), _row_spec(tm, d), _row_spec(tm, ROUTER_LANES)]
    return out_shape, out_specs


def _out_proj(x, of, ob, rg, att, modtab, gn_g, w_out, ng2, wrh, wrl, bd, tm, n_ctx):
    b, t, d = x.shape
    out_shape, out_specs = _post_specs(b, t, d, tm)
    return pl.pallas_call(
        _out_kernel,
        grid=(b, t // tm),
        in_specs=[
            _row_spec(tm, d), _row_spec(tm, SEG_W), _row_spec(tm, SEG_W), _row_spec(tm, SEG_W),
            _row_spec(tm, SEG_W), _mod_spec(d, 6, n_ctx // tm), _const_spec((1, SEG_W)),
            _const_spec(w_out.shape), _const_spec((1, d)), _const_spec(wrh.shape), _const_spec(wrl.shape),
            _const_spec((SEG_W, SEG_W)),
        ],
        out_specs=out_specs,
        out_shape=out_shape,
        compiler_params=_cparams(("parallel", "parallel")),
    )(x, of, ob, rg, att, modtab, gn_g, w_out, ng2, wrh, wrl, bd)


def _s5_kernel(u_ref, perm_ref, tm_ref, wz_ref, wo_ref, ar_ref, ai_ref, d_ref, z_ref,
               zr_s, zi_s, xr_s, xi_s, zs_s, *, n_sub, n_sub_ctx):
    s_n, j_sub, gb, half = n_sub, S5_SUB, S5_BLOCK_GROUPS, 2 * S5_STATE
    cw = z_ref.shape[-1]
    uf = jnp.concatenate([u_ref[0, pl.ds(m, s_n, stride=j_sub), :] for m in range(j_sub)], axis=1)
    perm = perm_ref[...]
    ug = _dot(uf.astype(BF16), perm).astype(BF16)
    for g in range(gb):
        zz = _dot(ug[:, g * cw:(g + 1) * cw], wz_ref[0, 0, g])
        zr_s[pl.ds(g, s_n, stride=gb), :] = zz[:, :half]
        zi_s[pl.ds(g, s_n, stride=gb), :] = zz[:, half:]
    fwd_lane = lax.broadcasted_iota(jnp.int32, (gb, half), 1) < S5_STATE
    ar = ar_ref[0, 0]
    ai = ai_ref[0, 0]

    def body(i, carry):
        xr, xi = carry
        sb = jnp.where(i < n_sub_ctx, n_sub_ctx - 1 - i, s_n - 1 - (i - n_sub_ctx))
        fwd = pl.ds(pl.multiple_of(i * gb, gb), gb)
        bwd = pl.ds(pl.multiple_of(sb * gb, gb), gb)
        xr_s[fwd, 0:S5_STATE] = xr[:, 0:S5_STATE]
        xi_s[fwd, 0:S5_STATE] = xi[:, 0:S5_STATE]
        xr_s[bwd, S5_STATE:half] = xr[:, S5_STATE:half]
        xi_s[bwd, S5_STATE:half] = xi[:, S5_STATE:half]
        zr = jnp.where(fwd_lane, zr_s[fwd, :], zr_s[bwd, :])
        zi = jnp.where(fwd_lane, zi_s[fwd, :], zi_s[bwd, :])
        return ar * xr - ai * xi + zr, ar * xi + ai * xr + zi

    zero = jnp.zeros((gb, half), F32)
    lax.fori_loop(0, s_n, body, (zero, zero), unroll=S5_SCAN_UNROLL)
    parts = []
    for g in range(gb):
        rows = pl.ds(g, s_n, stride=gb)
        xp = jnp.concatenate([xr_s[rows, :], xi_s[rows, :]], axis=1).astype(BF16)
        parts.append(_dot(ug[:, g * cw:(g + 1) * cw], tm_ref[0, 0, g]) + _dot(xp, wo_ref[0, 0, g]))
    y = _dot_nt(jnp.concatenate(parts, axis=1).astype(BF16), perm) + d_ref[0, 0] * uf
    z = _gelu_tanh(y)
    for n in range(j_sub):
        zs_s[pl.ds(n, s_n, stride=j_sub), :] = z[:, n * cw:(n + 1) * cw]
    z_ref[0] = zs_s[...].astype(BF16)


def _s5_scan(li, u, perm, tmat, wz, wo, ar, ai, dt, n_ctx):
    b, t, d = u.shape
    cw = S5_BLOCK_GROUPS * S5_GROUP
    n_sub = t // S5_SUB
    lw = S5_SUB * cw
    sw = 4 * S5_STATE
    gb = S5_BLOCK_GROUPS
    return pl.pallas_call(
        functools.partial(_s5_kernel, n_sub=n_sub, n_sub_ctx=n_ctx // S5_SUB),
        grid=(d // cw, b),
        in_specs=[
            pl.BlockSpec((1, t, cw), lambda c, bi: (bi, 0, c)),
            pl.BlockSpec((lw, lw), lambda c, bi: (0, 0)),
            pl.BlockSpec((1, 1, gb, cw, cw), lambda c, bi: (li, c, 0, 0, 0)),
            pl.BlockSpec((1, 1, gb, cw, sw), lambda c, bi: (li, c, 0, 0, 0)),
            pl.BlockSpec((1, 1, gb, sw, cw), lambda c, bi: (li, c, 0, 0, 0)),
            pl.BlockSpec((1, 1, gb, 2 * S5_STATE), lambda c, bi: (li, c, 0, 0)),
            pl.BlockSpec((1, 1, gb, 2 * S5_STATE), lambda c, bi: (li, c, 0, 0)),
            pl.BlockSpec((1, 1, 1, lw), lambda c, bi: (li, c, 0, 0)),
        ],
        out_specs=pl.BlockSpec((1, t, cw), lambda c, bi: (bi, 0, c)),
        out_shape=jax.ShapeDtypeStruct((b, t, d), BF16),
        scratch_shapes=[pltpu.VMEM((gb * n_sub, 2 * S5_STATE), F32)] * 4 + [pltpu.VMEM((t, cw), F32)],
        compiler_params=_cparams(("parallel", "parallel")),
    )(u, perm, tmat, wz, wo, ar, ai, dt)


def _s5_lane_regroup():
    j_sub, gb, k_w = S5_SUB, S5_BLOCK_GROUPS, S5_GROUP
    src = jnp.arange(j_sub * gb * k_w)
    m, g, k = src // (gb * k_w), (src // k_w) % gb, src % k_w
    dst = g * (j_sub * k_w) + m * k_w + k
    return (dst[:, None] == jnp.arange(j_sub * gb * k_w)[None, :]).astype(BF16)


def _s5_tables(a_re, a_im, log_dt, b_re, b_im, c_re, c_im, d_skip):
    hp = lax.Precision.HIGHEST
    j_sub = S5_SUB
    gb = S5_BLOCK_GROUPS
    lam_re = jnp.minimum(a_re, -1e-4)
    lam_im = a_im
    dt = jnp.exp(log_dt)[..., None]
    mag = jnp.exp(lam_re * dt)
    bar_re = mag * jnp.cos(lam_im * dt)
    bar_im = mag * jnp.sin(lam_im * dt)
    den = lam_re * lam_re + lam_im * lam_im
    nr, ni = bar_re - 1.0, bar_im
    k_re = (nr * lam_re + ni * lam_im) / den
    k_im = (ni * lam_re - nr * lam_im) / den
    bb_re = k_re[..., None] * b_re[None] - k_im[..., None] * b_im[None]
    bb_im = k_re[..., None] * b_im[None] + k_im[..., None] * b_re[None]
    jj = jnp.arange(j_sub + 1, dtype=F32)[:, None, None, None]
    pmag = jnp.exp(jj * (lam_re * dt)[None])
    pw_re = pmag * jnp.cos(jj * (lam_im * dt)[None])
    pw_im = pmag * jnp.sin(jj * (lam_im * dt)[None])
    pb_re = pw_re[..., None] * bb_re[None] - pw_im[..., None] * bb_im[None]
    pb_im = pw_re[..., None] * bb_im[None] + pw_im[..., None] * bb_re[None]
    cp_re = c_re[None] * pw_re[:, :, :, None, :] - c_im[None] * pw_im[:, :, :, None, :]
    cp_im = c_re[None] * pw_im[:, :, :, None, :] + c_im[None] * pw_re[:, :, :, None, :]
    kern = (jnp.einsum('dgcp,jdgpk->jdgkc', c_re, pb_re[:j_sub], precision=hp)
            - jnp.einsum('dgcp,jdgpk->jdgkc', c_im, pb_im[:j_sub], precision=hp))
    groups = a_re.shape[1]
    nblk = groups // gb
    k_w = S5_GROUP
    p_w = S5_STATE
    gw = j_sub * k_w
    pos = jnp.arange(j_sub)
    lag = pos[None, :] - pos[:, None]
    kf = jnp.where((lag >= 0)[:, :, None, None, None], kern[jnp.clip(lag, 0, j_sub - 1), 0], 0.0)
    kb = jnp.where((lag <= 0)[:, :, None, None, None], kern[jnp.clip(-lag, 0, j_sub - 1), 1], 0.0)
    tmat = (kf + kb).transpose(2, 0, 3, 1, 4).reshape(nblk, gb, gw, gw)
    rev = pos[::-1]
    zc = jnp.concatenate([pb_re[rev, 0], pb_re[pos, 1], pb_im[rev, 0], pb_im[pos, 1]], axis=2)
    wz = zc.transpose(1, 0, 3, 2).reshape(nblk, gb, gw, 4 * p_w)
    orows = lambda cp, d, order: cp[order, d].transpose(1, 3, 0, 2).reshape(nblk, gb, p_w, gw)
    wo = jnp.concatenate([orows(cp_re, 0, pos + 1), orows(cp_re, 1, j_sub - pos),
                          -orows(cp_im, 0, pos + 1), -orows(cp_im, 1, j_sub - pos)], axis=2)
    ar = jnp.concatenate([pw_re[j_sub, 0], pw_re[j_sub, 1]], axis=-1).reshape(nblk, gb, 2 * p_w)
    ai = jnp.concatenate([pw_im[j_sub, 0], pw_im[j_sub, 1]], axis=-1).reshape(nblk, gb, 2 * p_w)
    dt_tile = jnp.tile(d_skip.reshape(nblk, 1, gb * k_w), (1, 1, j_sub))
    return tmat.astype(BF16), wz.astype(BF16), wo.astype(BF16), ar, ai, dt_tile


def _glu_kernel(x_ref, z_ref, mod_ref, w_ref, b_ref, ng2_ref, wrh_ref, wrl_ref, xo_ref, f_ref, aff_ref):
    d = x_ref.shape[-1]
    ag = _dot(z_ref[0], w_ref[...]) + b_ref[...]
    delta = ag[:, :d] * _sigmoid(ag[:, d:])
    _post_mixer(x_ref[0], delta, mod_ref[0, 0], ng2_ref[...], wrh_ref[...], wrl_ref[...], xo_ref, f_ref, aff_ref)


def _glu(x, z, modtab, glu_w, glu_b, ng2, wrh, wrl, tm, n_ctx):
    b, t, d = x.shape
    out_shape, out_specs = _post_specs(b, t, d, tm)
    return pl.pallas_call(
        _glu_kernel,
        grid=(b, t // tm),
        in_specs=[_row_spec(tm, d), _row_spec(tm, d), _mod_spec(d, 6, n_ctx // tm),
                  _const_spec(glu_w.shape), _const_spec((1, 2 * d)), _const_spec((1, d)),
                  _const_spec(wrh.shape), _const_spec(wrl.shape)],
        out_specs=out_specs,
        out_shape=out_shape,
        compiler_params=_cparams(("parallel", "parallel")),
    )(x, z, modtab, glu_w, glu_b, ng2, wrh, wrl)


def _cumsum_lanes(v):
    n = v.shape[-1]
    lane = lax.broadcasted_iota(jnp.int32, v.shape, 1)
    sh = 1
    while sh < n:
        v = v + jnp.where(lane >= sh, pltpu.roll(v, sh, 1), 0.0)
        sh *= 2
    return v


def _select_segment(aff, cap, base):
    bits = pltpu.bitcast(aff, jnp.int32)
    thr = jnp.zeros((aff.shape[0], 1), jnp.int32)
    for bit in range(30, -1, -1):
        cand = thr | (1 << bit)
        cnt = jnp.sum(jnp.where(bits >= cand, 1.0, 0.0), axis=1, keepdims=True)
        thr = jnp.where(cnt >= cap, cand, thr)
    gt = bits > thr
    eq = bits == thr
    need = cap - jnp.sum(jnp.where(gt, 1.0, 0.0), axis=1, keepdims=True)
    eqf = jnp.where(eq, 1.0, 0.0)
    eq_before = _cumsum_lanes(eqf) - eqf
    sel = jnp.where(gt, 1.0, jnp.where(eq, jnp.where(eq_before < need, 1.0, 0.0), 0.0))
    pos = _cumsum_lanes(sel) - sel
    return jnp.where(sel > 0.0, pos.astype(jnp.int32) + base, -1)


def _select_kernel(aff_ref, slot_ref, *, n_ctx, cap_ctx, cap_lat):
    slot_ref[0, :, 0:n_ctx] = _select_segment(aff_ref[0, :, 0:n_ctx], cap_ctx, 0)
    slot_ref[0, :, n_ctx:] = _select_segment(aff_ref[0, :, n_ctx:], cap_lat, cap_ctx)


def _select(aff_t, n_ctx, cap_ctx, cap_lat):
    b, e, t = aff_t.shape
    return pl.pallas_call(
        functools.partial(_select_kernel, n_ctx=n_ctx, cap_ctx=cap_ctx, cap_lat=cap_lat),
        grid=(b,),
        in_specs=[pl.BlockSpec((1, e, t), lambda i: (i, 0, 0))],
        out_specs=pl.BlockSpec((1, e, t), lambda i: (i, 0, 0)),
        out_shape=jax.ShapeDtypeStruct((b, e, t), jnp.int32),
        compiler_params=_cparams(("parallel",)),
    )(aff_t)


def _moe_window(a_ref, base, e, k, rows, w):
    nominal = a_ref[base + e] + k * w
    return nominal, jnp.minimum(nominal, rows - w)


def _moe_hits(slot_row, nominal, actual, w):
    rid = lax.broadcasted_iota(jnp.int32, (w, slot_row.shape[-1]), 0) + actual
    return jnp.where(slot_row >= nominal, slot_row, -1) == rid


def _moe_gather_kernel(a_ref, nr_ref, slot_ref, aff_ref, f_ref, xs_ref, gate_ref, *, rows, w):
    b = pl.program_id(0)
    j = pl.program_id(1)
    n_e = slot_ref.shape[2]
    step = b * pl.num_programs(1) + j

    @pl.when(j == 0)
    def _():
        xs_ref[...] = jnp.zeros_like(xs_ref)
        gate_ref[...] = jnp.zeros_like(gate_ref)

    fj = f_ref[0]

    def one_round(k, carry):
        for e in range(n_e):
            nominal, actual = _moe_window(a_ref, step * n_e, e, k, rows, w)
            hit = _moe_hits(slot_ref[0, 0, e:e + 1, :], nominal, actual, w)
            off = pl.multiple_of(e * rows + actual, 16)
            xs_ref[0, pl.ds(off, w), :] += _dot(jnp.where(hit, 1.0, 0.0).astype(BF16), fj).astype(BF16)
            gate_ref[0, pl.ds(off, w), :] += jnp.sum(jnp.where(hit, aff_ref[0, 0, e:e + 1, :], 0.0),
                                                     axis=1, keepdims=True)
        return carry

    lax.fori_loop(0, nr_ref[step], one_round, 0)


def _moe_ffn_kernel(xs_ref, gate_ref, w1_ref, w3_ref, w2_ref, y_ref, ys_s, *, n_ft):
    ft = pl.program_id(2)
    nb, rows, d = xs_ref.shape
    xs = xs_ref[...].reshape(nb * rows, d)
    hid = (_silu(_dot(xs, w1_ref[0, 0].astype(BF16))) * _dot(xs, w3_ref[0, 0].astype(BF16))).astype(BF16)
    part = _dot(hid, w2_ref[0, 0].astype(BF16))

    @pl.when(ft == 0)
    def _():
        ys_s[...] = part

    @pl.when(ft > 0)
    def _():
        ys_s[...] += part

    @pl.when(ft == n_ft - 1)
    def _():
        y_ref[...] = (ys_s[...] * gate_ref[...].reshape(nb * rows, 1)).astype(BF16).reshape(nb, rows, d)


def _moe_combine_kernel(a_ref, nr_ref, slot_ref, y_ref, x_ref, mod_ref, *rest, rows, w, nb, lead, tail):
    o_ref = rest[-1] if tail != "s5" else rest[-2]
    b = pl.program_id(0)
    n_e = slot_ref.shape[2]
    step = b * nb + pl.program_id(1) + lead

    def one_round(k, acc):
        onehots, ys = [], []
        for e in range(n_e):
            nominal, actual = _moe_window(a_ref, step * n_e, e, k, rows, w)
            hit = _moe_hits(slot_ref[0, 0, e:e + 1, :], nominal, actual, w)
            onehots.append(jnp.where(hit, 1.0, 0.0).astype(BF16))
            ys.append(y_ref[0, pl.ds(pl.multiple_of(e * rows + actual, 16), w), :])
        return acc + _dot_tn(jnp.concatenate(onehots, axis=0), jnp.concatenate(ys, axis=0))

    acc = lax.fori_loop(0, nr_ref[step], one_round, jnp.zeros(o_ref.shape[1:], F32))
    xn = x_ref[0] + mod_ref[0, 0, 5:6, :] * acc
    if tail == "final":
        xn = xn * lax.rsqrt(jnp.mean(xn * xn, axis=-1, keepdims=True) + EPS) * rest[0][...]
    o_ref[0] = xn
    if tail == "s5":
        m = rest[0][0, 0]
        rest[3][0] = _norm_mod(xn, rest[1][...], m[1:2], m[0:1])


def _moe_windows(slot, tb, w):
    b, e, t = slot.shape
    nb = t // tb
    cnt = jnp.sum((slot >= 0).reshape(b, e, nb, tb), axis=-1, dtype=jnp.int32)
    start = jnp.cumsum(cnt, axis=-1) - cnt
    a0 = (start // 16) * 16
    rounds = jnp.maximum(jnp.max((start + cnt - a0 + w - 1) // w, axis=1), 1)
    return jnp.swapaxes(a0, 1, 2).reshape(-1).astype(jnp.int32), rounds.reshape(-1).astype(jnp.int32)


def _route_and_moe(layer, x, aff, f, modtab, w1, w3, w2, n_ctx, tb, tf, final_g=None, next_s5=None):
    b, t, d = f.shape
    n_e = w1.shape[1]
    ff = w1.shape[-1]
    nb = t // tb
    aff_t = jnp.swapaxes(aff[:, :, :N_EXPERTS], 1, 2)
    cap_ctx = (CAPACITY_FACTOR * n_ctx) // N_EXPERTS
    cap_lat = (CAPACITY_FACTOR * (t - n_ctx)) // N_EXPERTS
    rows = cap_ctx + cap_lat
    w = min(MOE_WINDOW, rows)
    slot = _select(aff_t, n_ctx, cap_ctx, cap_lat)
    a0, rounds = _moe_windows(slot, tb, w)
    by_block = lambda v: jnp.swapaxes(v.reshape(b, n_e, nb, tb), 1, 2)
    slot_b, aff_b = by_block(slot), by_block(aff_t)
    blk_spec = pl.BlockSpec((1, 1, n_e, tb), lambda bi, j, *_: (bi, j, 0, 0))
    whole = lambda width: pl.BlockSpec((1, n_e * rows, width), lambda bi, j, *_: (bi, 0, 0))
    xs, gate = pl.pallas_call(
        functools.partial(_moe_gather_kernel, rows=rows, w=w),
        grid_spec=pltpu.PrefetchScalarGridSpec(
            num_scalar_prefetch=2, grid=(b, nb),
            in_specs=[blk_spec, blk_spec, pl.BlockSpec((1, tb, d), lambda bi, j, *_: (bi, j, 0))],
            out_specs=[whole(d), whole(1)]),
        out_shape=[jax.ShapeDtypeStruct((b, n_e * rows, d), BF16), jax.ShapeDtypeStruct((b, n_e * rows, 1), F32)],
        compiler_params=_cparams(("parallel", "arbitrary")),
    )(a0, rounds, slot_b, aff_b, f)
    n_ft = ff // tf
    sb = MOE_FFN_SAMPLES
    y = pl.pallas_call(
        functools.partial(_moe_ffn_kernel, n_ft=n_ft),
        grid=(b // sb, n_e, n_ft),
        in_specs=[
            pl.BlockSpec((sb, rows, d), lambda bi, e, k: (bi, e, 0)),
            pl.BlockSpec((sb, rows, 1), lambda bi, e, k: (bi, e, 0)),
            pl.BlockSpec((1, 1, d, tf), lambda bi, e, k: (layer, e, 0, k)),
            pl.BlockSpec((1, 1, d, tf), lambda bi, e, k: (layer, e, 0, k)),
            pl.BlockSpec((1, 1, tf, d), lambda bi, e, k: (layer, e, k, 0)),
        ],
        out_specs=pl.BlockSpec((sb, rows, d), lambda bi, e, k: (bi, e, 0)),
        out_shape=jax.ShapeDtypeStruct((b, n_e * rows, d), BF16),
        scratch_shapes=[pltpu.VMEM((sb * rows, d), F32)],
        compiler_params=_cparams(("parallel", "parallel", "arbitrary")),
    )(xs, gate, w1, w3, w2)
    seg_blocks = n_ctx // tb
    lead = 0 if final_g is None else seg_blocks
    seg_of = lambda bi, j, *_: (bi, jnp.where(j + lead >= seg_blocks, 1, 0), 0, 0)
    gain_spec = pl.BlockSpec((1, d), lambda bi, j, *_: (0, 0))
    out_spec = pl.BlockSpec((1, tb, d), lambda bi, j, *_: (bi, j, 0))
    out_sds = jax.ShapeDtypeStruct((b, t - lead * tb, d), F32)
    tail, extra_args, extra_specs, out_specs, out_shape = "plain", (), [], out_spec, out_sds
    if final_g is not None:
        tail, extra_args, extra_specs = "final", (final_g,), [gain_spec]
    elif next_s5 is not None:
        tail, extra_args, extra_specs = "s5", tuple(next_s5), [pl.BlockSpec((1, 1, 6, d), seg_of), gain_spec]
        out_specs, out_shape = [out_spec, out_spec], [out_sds, out_sds]
    return pl.pallas_call(
        functools.partial(_moe_combine_kernel, rows=rows, w=w, nb=nb, lead=lead, tail=tail),
        grid_spec=pltpu.PrefetchScalarGridSpec(
            num_scalar_prefetch=2, grid=(b, nb - lead),
            in_specs=[pl.BlockSpec((1, 1, n_e, tb), lambda bi, j, *_: (bi, j + lead, 0, 0)), whole(d),
                      pl.BlockSpec((1, tb, d), lambda bi, j, *_: (bi, j + lead, 0)),
                      pl.BlockSpec((1, 1, 6, d), seg_of)]
            + extra_specs,
            out_specs=out_specs),
        out_shape=out_shape,
        compiler_params=_cparams(("parallel", "parallel")),
    )(a0, rounds, slot_b, y, x, modtab, *extra_args)


def _rope_tables(n_ctx, n_lat):
    n_rows = n_lat // GRID_W
    row = jnp.repeat(jnp.arange(n_rows, dtype=F32), GRID_W)
    col = jnp.tile(jnp.arange(GRID_W, dtype=F32), n_rows)
    inv = ROPE_THETA ** (-jnp.arange(0, ROPE_AXIS_DIM, 2, dtype=F32) / ROPE_AXIS_DIM)
    ang_r = row[:, None] * inv[None, :]
    ang_c = col[:, None] * inv[None, :]
    cos = jnp.concatenate([jnp.cos(ang_r), jnp.cos(ang_r), jnp.cos(ang_c), jnp.cos(ang_c)], axis=-1)
    sin = jnp.concatenate([-jnp.sin(ang_r), jnp.sin(ang_r), -jnp.sin(ang_c), jnp.sin(ang_c)], axis=-1)
    cos = jnp.concatenate([jnp.ones((n_ctx, HEAD_DIM), F32), cos], axis=0)
    sin = jnp.concatenate([jnp.zeros((n_ctx, HEAD_DIM), F32), sin], axis=0)
    reps = SEG_W // HEAD_DIM
    return jnp.tile(cos, (1, reps)), jnp.tile(sin, (1, reps))


def _extend_in_proj(w_in):
    base = 4 * RET_WIDTH + ATT_WIDTH
    ak = w_in[:, base:base + ATT_KV_WIDTH].reshape(-1, ATT_KV_HEADS, 1, HEAD_DIM)
    av = w_in[:, base + ATT_KV_WIDTH:].reshape(-1, ATT_KV_HEADS, 1, HEAD_DIM)
    rep = lambda w, n: jnp.broadcast_to(w, (w.shape[0], ATT_KV_HEADS, n, HEAD_DIM)).reshape(w.shape[0], -1)
    return jnp.concatenate([w_in[:, :base], rep(ak, ATT_GROUP), rep(av, 2)], axis=-1)


def kernel(x, c, ctx, c_ctx, mod_w, mod_b, norm_g, mix_in_w, mix_out_w, ret_log_rate, ret_gn_g, qk_norm_g,
           s5_a_re, s5_a_im, s5_log_dt, s5_b_re, s5_b_im, s5_c_re, s5_c_im, s5_d, s5_glu_w, s5_glu_b,
           moe_router_w, moe_w1, moe_w3, moe_w2, final_norm_g):
    batch, n_lat, d = x.shape
    n_ctx = ctx.shape[1]
    t = n_ctx + n_lat
    depth = mod_w.shape[0]
    assert batch == 8 and d % 128 == 0 and n_ctx % CHUNK == 0 and n_lat % CHUNK == 0
    tm = 256 if n_ctx % 256 == 0 else 128
    tb = tm
    tf = 1024

    xs = jnp.concatenate([ctx, x], axis=1)
    rows = 16
    cond = jnp.zeros((rows, d), F32).at[:batch].set(c).at[batch].set(c_ctx)
    mod_all = _modulation(cond, mod_w, mod_b)
    cos, sin = _rope_tables(n_ctx, n_lat)
    lane_head = jnp.arange(SEG_W) // HEAD_DIM
    bd = (lane_head[:, None] == lane_head[None, :]).astype(BF16)
    w1, w3, w2 = moe_w1, moe_w3, moe_w2
    s5_perm = _s5_lane_regroup()
    s5_tabs = jax.vmap(_s5_tables)(s5_a_re, s5_a_im, s5_log_dt, s5_b_re, s5_b_im, s5_c_re, s5_c_im, s5_d)

    def modtab_of(layer):
        m = mod_all[layer]
        lat = m[:batch].reshape(batch, 1, 6, d)
        cx = jnp.broadcast_to(m[batch].reshape(1, 1, 6, d), (batch, 1, 6, d))
        return jnp.concatenate([cx, lat], axis=1)

    u = None
    for layer in range(depth):
        modtab = modtab_of(layer)
        ng1 = norm_g[layer, 0].reshape(1, d)
        ng2 = norm_g[layer, 1].reshape(1, d)
        wr = jnp.zeros((d, ROUTER_LANES), F32).at[:, :N_EXPERTS].set(moe_router_w[layer])
        wrh = wr.astype(BF16)
        wrl = (wr - wrh.astype(F32)).astype(BF16)
        i = layer // 2
        if layer % 2 == 0:
            w_ext = _extend_in_proj(mix_in_w[i]).astype(BF16)
            reps = SEG_W // HEAD_DIM
            qkg = jnp.stack([jnp.tile(qk_norm_g[i, 0], reps), jnp.tile(qk_norm_g[i, 1], reps)])
            rq, rk, rv, rg, aq, ak4, av2 = _in_proj(xs, modtab, ng1, w_ext, cos, sin, qkg, bd, tm, n_ctx)
            rate = jnp.repeat(ret_log_rate[i], HEAD_DIM, axis=-1)
            rate_h = jnp.broadcast_to(ret_log_rate[i][:, :, None], (2, RET_HEADS, CHUNK))
            o_f, o_b = _retention(rq, rk, rv, rate, rate_h, bd, n_ctx)
            att = _attention(aq, ak4, av2, tm, tm, n_ctx)
            xs, f, aff = _out_proj(xs, o_f, o_b, rg, att, modtab, ret_gn_g[i].reshape(1, SEG_W),
                                   mix_out_w[i].astype(BF16), ng2, wrh, wrl, bd, tm, n_ctx)
        else:
            z = _s5_scan(i, u, s5_perm, *s5_tabs, n_ctx)
            xs, f, aff = _glu(xs, z, modtab, s5_glu_w[i].astype(BF16), s5_glu_b[i].reshape(1, 2 * d),
                              ng2, wrh, wrl, tm, n_ctx)
        final_g = final_norm_g.reshape(1, d) if layer == depth - 1 else None
        next_s5 = None
        if layer + 1 < depth and (layer + 1) % 2 == 1:
            next_s5 = (modtab_of(layer + 1), norm_g[layer + 1, 0].reshape(1, d))
        xs = _route_and_moe(layer, xs, aff, f, modtab, w1, w3, w2, n_ctx, tb, tf, final_g, next_s5)
        u = None
        if next_s5 is not None:
            xs, u = xs
    return xs
```

```python
import functools
import math

import jax
import jax.numpy as jnp
from jax import lax
from jax.experimental import pallas as pl
from jax.experimental.pallas import tpu as pltpu

F32 = jnp.float32
BF16 = jnp.bfloat16

GRID_W = 64
HEAD_DIM = 64
RET_HEADS = 8
ATT_HEADS = 8
ATT_KV_HEADS = 2
ATT_GROUP = ATT_HEADS // ATT_KV_HEADS
RET_WIDTH = RET_HEADS * HEAD_DIM
ATT_WIDTH = ATT_HEADS * HEAD_DIM
ATT_KV_WIDTH = ATT_KV_HEADS * HEAD_DIM
CHUNK = 128
ROPE_THETA = 10000.0
ROPE_AXIS_DIM = HEAD_DIM // 2
S5_GROUP = 16
S5_STATE = 64
S5_SUB = 8
S5_BLOCK_GROUPS = 8
S5_SCAN_UNROLL = 4
N_EXPERTS = 16
CAPACITY_FACTOR = 2
EPS = 1e-6
SEG_W = 512
LOG2_E = math.log2(math.e)
ROUTER_LANES = 128
MOE_WINDOW = 64
MOE_FFN_SAMPLES = 2

V7X_VMEM_BYTES = 64 * 1024 * 1024
VMEM_LIMIT = V7X_VMEM_BYTES - 8 * 1024 * 1024


def _cparams(sem):
    return pltpu.CompilerParams(dimension_semantics=sem, vmem_limit_bytes=VMEM_LIMIT)


def _dot(a, b):
    return jnp.dot(a, b, preferred_element_type=F32)


def _dot_nt(a, b):
    return lax.dot_general(a, b, (((1,), (1,)), ((), ())), preferred_element_type=F32)


def _dot_tn(a, b):
    return lax.dot_general(a, b, (((0,), (0,)), ((), ())), preferred_element_type=F32)


def _split(v):
    hi = v.astype(BF16)
    lo = (v - hi.astype(F32)).astype(BF16)
    return hi, lo


def _segsum(v, bd):
    return _dot(v.astype(BF16), bd)


def _sigmoid(v):
    return 1.0 / (1.0 + jnp.exp(-v))


def _silu(v):
    return v * _sigmoid(v)


def _gelu_tanh(v):
    c = math.sqrt(2.0 / math.pi)
    return 0.5 * v * (1.0 + jnp.tanh(c * (v + 0.044715 * (v * v * v))))


def _norm_mod(x, g, scale, shift):
    ms = jnp.mean(x * x, axis=-1, keepdims=True)
    return x * lax.rsqrt(ms + EPS) * g * (1.0 + scale) + shift


def _rope(x, cos, sin_signed):
    w = x.shape[-1]
    lane = lax.broadcasted_iota(jnp.int32, x.shape, 1)
    first = (lane % 32) < 16
    partner = jnp.where(first, pltpu.roll(x, w - 16, 1), pltpu.roll(x, 16, 1))
    return x * cos + partner * sin_signed


def _mod_kernel(c_ref, w_ref, b_ref, o_ref):
    s = _silu(c_ref[...]).astype(BF16)
    o_ref[0] = _dot(s, w_ref[0].astype(BF16)) + b_ref[0]


def _modulation(cond, mod_w, mod_b):
    depth, d, n6 = mod_w.shape
    rows = cond.shape[0]
    tn = n6 // 6
    return pl.pallas_call(
        _mod_kernel,
        grid=(depth, n6 // tn),
        in_specs=[
            pl.BlockSpec((rows, d), lambda l, j: (0, 0)),
            pl.BlockSpec((1, d, tn), lambda l, j: (l, 0, j)),
            pl.BlockSpec((1, 1, tn), lambda l, j: (l, 0, j)),
        ],
        out_specs=pl.BlockSpec((1, rows, tn), lambda l, j: (l, 0, j)),
        out_shape=jax.ShapeDtypeStruct((depth, rows, n6), F32),
        compiler_params=_cparams(("parallel", "parallel")),
    )(cond, mod_w, mod_b.reshape(depth, 1, n6))


def _in_kernel(x_ref, mod_ref, ng_ref, w_ref, cos_ref, sin_ref, qkg_ref, bd_ref,
               rq_ref, rk_ref, rv_ref, rg_ref, aq_ref, ak_ref, av_ref):
    x = x_ref[0]
    m = mod_ref[0, 0]
    h = _norm_mod(x, ng_ref[...], m[1:2], m[0:1]).astype(BF16)
    cos = cos_ref[...]
    sin = sin_ref[...]
    bd = bd_ref[...]

    def seg(i):
        return _dot(h, w_ref[:, i * SEG_W:(i + 1) * SEG_W])

    def headnorm(v, g):
        ms = _segsum(v * v, bd) * (1.0 / HEAD_DIM)
        return v * lax.rsqrt(ms + EPS) * g

    scale = HEAD_DIM ** -0.5
    rq_ref[0] = _rope(seg(0), cos, sin).astype(BF16)
    rk_ref[0] = _rope(seg(1) * scale, cos, sin).astype(BF16)
    rv_ref[0] = seg(2).astype(BF16)
    rg_ref[0] = seg(3)
    aq_ref[0] = (_rope(headnorm(seg(4), qkg_ref[0:1]), cos, sin) * (scale * LOG2_E)).astype(BF16)
    ak_ref[0] = _rope(headnorm(seg(5), qkg_ref[1:2]), cos, sin).astype(BF16)
    av_ref[0] = _dot(h, w_ref[:, 6 * SEG_W:]).astype(BF16)


def _row_spec(tm, d, lead=0):
    return pl.BlockSpec((1, tm, d), lambda b, i: (b, i + lead, 0))


def _mod_spec(d, n, seg_tiles):
    return pl.BlockSpec((1, 1, n, d), lambda b, i: (b, jnp.where(i >= seg_tiles, 1, 0), 0, 0))


def _const_spec(shape):
    nd = len(shape)
    return pl.BlockSpec(shape, lambda b, i: (0,) * nd)


def _in_proj(x, modtab, norm_g, w_ext, cos, sin, qkg, bd, tm, n_ctx):
    b, t, d = x.shape
    out_shape, out_specs = [], []
    for dt, width in ((BF16, SEG_W), (BF16, SEG_W), (BF16, SEG_W), (F32, SEG_W), (BF16, SEG_W), (BF16, SEG_W),
                      (BF16, ATT_KV_HEADS * 2 * HEAD_DIM)):
        out_shape.append(jax.ShapeDtypeStruct((b, t, width), dt))
        out_specs.append(_row_spec(tm, width))
    return pl.pallas_call(
        _in_kernel,
        grid=(b, t // tm),
        in_specs=[
            _row_spec(tm, d),
            _mod_spec(d, 6, n_ctx // tm),
            _const_spec((1, d)),
            _const_spec(w_ext.shape),
            pl.BlockSpec((tm, SEG_W), lambda b, i: (i, 0)),
            pl.BlockSpec((tm, SEG_W), lambda b, i: (i, 0)),
            _const_spec((2, SEG_W)),
            _const_spec((SEG_W, SEG_W)),
        ],
        out_specs=out_specs,
        out_shape=out_shape,
        compiler_params=_cparams(("parallel", "parallel")),
    )(x, modtab, norm_g, w_ext, cos, sin, qkg, bd)


def _ret_kernel(qf_ref, kf_ref, vf_ref, qb_ref, kb_ref, vb_ref, rate_ref, rate_h_ref, bd_ref,
                of_ref, ob_ref, sf_s, sb_s, dm_s, dq_s, dk_s, dc_s):
    s = pl.program_id(1)
    c = CHUNK

    @pl.when(s == 0)
    def _():
        sf_s[...] = jnp.zeros_like(sf_s)
        sb_s[...] = jnp.zeros_like(sb_s)
        pos = lax.broadcasted_iota(jnp.int32, (c, SEG_W), 0).astype(F32)
        lgf = -jnp.exp(rate_ref[0:1, :])
        lgb = -jnp.exp(rate_ref[1:2, :])
        dq_s[0] = jnp.exp((pos + 1.0) * lgf)
        dk_s[0] = jnp.exp((c - 1.0 - pos) * lgf)
        dc_s[0] = jnp.exp(float(c) * lgf)
        dq_s[1] = jnp.exp((c - pos) * lgb)
        dk_s[1] = jnp.exp(pos * lgb)
        dc_s[1] = jnp.exp(float(c) * lgb)
        ii = lax.broadcasted_iota(jnp.int32, (c, c), 0)
        jj = lax.broadcasted_iota(jnp.int32, (c, c), 1)
        for h in range(RET_HEADS):
            lf = -jnp.exp(rate_h_ref[0, h:h + 1, :])
            lb = -jnp.exp(rate_h_ref[1, h:h + 1, :])
            mf = ii >= jj
            mb = jj > ii
            dm_s[0, h] = jnp.where(mf, jnp.exp(jnp.where(mf, ii - jj, 0).astype(F32) * lf), 0.0)
            dm_s[1, h] = jnp.where(mb, jnp.exp(jnp.where(mb, jj - ii, 0).astype(F32) * lb), 0.0)

    bdm = bd_ref[...].astype(F32)
    head_of_lane = lax.broadcasted_iota(jnp.int32, (1, SEG_W // 2), 1) // HEAD_DIM

    def direction(d, q_ref, k_ref, v_ref, o_ref, st, sub):
        rows = slice(sub * c, (sub + 1) * c)
        q = q_ref[0, rows, :]
        k = k_ref[0, rows, :]
        v = v_ref[0, rows, :]
        state = st[...]
        qd = (q.astype(F32) * dq_s[d]).astype(BF16)
        inter = _dot(qd, state.astype(BF16))
        for half in range(2):
            sl = slice(half * 256, half * 256 + 256)
            qh, kh, vh = q[:, sl], k[:, sl], v[:, sl]
            acc = inter[:, sl]
            for hh in range(4):
                msk = head_of_lane == hh
                sc = _dot_nt(jnp.where(msk, qh, jnp.zeros_like(qh)), kh) * dm_s[d, half * 4 + hh]
                acc = acc + _dot(sc.astype(BF16), jnp.where(msk, vh, jnp.zeros_like(vh)))
            o_ref[0, rows, sl] = acc
        kd = (k.astype(F32) * dk_s[d]).astype(BF16)
        st[...] = state * dc_s[d] + bdm * _dot_tn(kd, v)

    n_sub = qf_ref.shape[1] // c
    for sub in range(n_sub):
        direction(0, qf_ref, kf_ref, vf_ref, of_ref, sf_s, sub)
        direction(1, qb_ref, kb_ref, vb_ref, ob_ref, sb_s, n_sub - 1 - sub)


def _retention(rq, rk, rv, rate, rate_h, bd, n_ctx):
    b, t, w = rq.shape
    cps = 2 if (t // CHUNK) % 2 == 0 and (n_ctx // CHUNK) % 2 == 0 else 1
    nc = t // (cps * CHUNK)
    lc = n_ctx // (cps * CHUNK)

    def fwd(bi, s):
        return (bi, s, 0)

    def bwd(bi, s):
        return (bi, jnp.where(s < lc, lc - 1 - s, nc - 1 - (s - lc)), 0)

    blk = (1, cps * CHUNK, w)
    return pl.pallas_call(
        _ret_kernel,
        grid=(b, nc),
        in_specs=[pl.BlockSpec(blk, fwd)] * 3 + [pl.BlockSpec(blk, bwd)] * 3 + [
            _const_spec((2, w)),
            _const_spec((2, RET_HEADS, CHUNK)),
            _const_spec((w, w)),
        ],
        out_specs=[pl.BlockSpec(blk, fwd), pl.BlockSpec(blk, bwd)],
        out_shape=[jax.ShapeDtypeStruct((b, t, w), F32)] * 2,
        scratch_shapes=[
            pltpu.VMEM((w, w), F32), pltpu.VMEM((w, w), F32),
            pltpu.VMEM((2, RET_HEADS, CHUNK, CHUNK), F32),
            pltpu.VMEM((2, CHUNK, w), F32), pltpu.VMEM((2, CHUNK, w), F32),
            pltpu.VMEM((2, 1, w), F32),
        ],
        compiler_params=_cparams(("parallel", "arbitrary")),
    )(rq, rk, rv, rq, rk, rv, rate, rate_h, bd)


ATT_ROW_BLOCK = 512


def _att_kernel(q_ref, k_ref, v_ref, o_ref, qs_s, m_s, acc_s, sa_s, sb_s, ma_s, mb_s, *, tq, tk, n_ctx, t):
    qi = pl.program_id(1)
    gw = ATT_GROUP * HEAD_DIM
    head_of_lane = lax.broadcasted_iota(jnp.int32, (1, gw), 1) // HEAD_DIM
    for h in range(ATT_HEADS):
        q = q_ref[0, :, (h // ATT_GROUP) * gw:(h // ATT_GROUP + 1) * gw]
        qs_s[h * tq:(h + 1) * tq, :] = jnp.where(head_of_lane == h % ATT_GROUP, q, jnp.zeros_like(q))
    m_s[...] = jnp.full(m_s.shape, -1e30, F32)
    acc_s[...] = jnp.zeros_like(acc_s)
    nk = jnp.where(qi * tq < n_ctx, n_ctx // tk, t // tk)
    vw = 2 * HEAD_DIM
    ones_lane = lax.broadcasted_iota(jnp.int32, (1, ATT_KV_HEADS * vw), 1) % vw >= HEAD_DIM
    rb = ATT_ROW_BLOCK

    blocks = [(slice(r * rb, (r + 1) * rb), (r * rb) // (ATT_GROUP * tq)) for r in range(ATT_HEADS * tq // rb)]

    def scores_into(j, dst, dst_max):
        kb = k_ref[0, pl.ds(pl.multiple_of(j * tk, tk), tk), :]
        for rows, g in blocks:
            sc = _dot_nt(qs_s[rows, :], kb[:, g * gw:(g + 1) * gw])
            dst[rows, :] = sc
            dst_max[rows, :] = jnp.broadcast_to(jnp.max(sc, axis=1, keepdims=True), (rb, vw))

    def consume(j, src, src_max):
        vb = v_ref[0, pl.ds(pl.multiple_of(j * tk, tk), tk), :]
        vext = jnp.where(ones_lane, jnp.ones_like(vb), vb)
        for rows, g in blocks:
            m_prev = m_s[rows, :]
            m_new = jnp.maximum(m_prev, src_max[rows, :])
            p = jnp.exp2(src[rows, :] - jnp.concatenate([m_new] * (tk // vw), axis=1)).astype(BF16)
            acc_s[rows, :] = jnp.exp2(m_prev - m_new) * acc_s[rows, :] + _dot(p, vext[:, g * vw:(g + 1) * vw])
            m_s[rows, :] = m_new

    scores_into(0, sa_s, ma_s)
    pairs = max(p for p in (4, 2, 1) if p == 1 or ((n_ctx // tk) % (2 * p) == 1 and (t // tk) % (2 * p) == 1))

    def body(i, carry):
        for u in range(pairs):
            j = 2 * (pairs * i + u)
            scores_into(j + 1, sb_s, mb_s)
            consume(j, sa_s, ma_s)
            scores_into(jnp.minimum(j + 2, nk - 1), sa_s, ma_s)
            consume(j + 1, sb_s, mb_s)
        return carry

    lax.fori_loop(0, nk // (2 * pairs), body, 0)

    @pl.when(nk % 2 == 1)
    def _():
        consume(nk - 1, sa_s, ma_s)

    low = lax.broadcasted_iota(jnp.int32, (tq, vw), 1) < HEAD_DIM
    for pair in range(ATT_HEADS // 2):
        a0 = acc_s[(2 * pair) * tq:(2 * pair + 1) * tq, :]
        a1 = acc_s[(2 * pair + 1) * tq:(2 * pair + 2) * tq, :]
        even = a0 / pltpu.roll(a0, HEAD_DIM, 1)
        odd = pltpu.roll(a1, HEAD_DIM, 1) / a1
        o_ref[0, :, pair * vw:(pair + 1) * vw] = jnp.where(low, even, odd).astype(BF16)


def _attention(aq, ak4, av2, tq, tk, n_ctx):
    b, t, w = aq.shape
    gw = ATT_GROUP * HEAD_DIM
    vw = 2 * HEAD_DIM
    rows = ATT_HEADS * tq
    return pl.pallas_call(
        functools.partial(_att_kernel, tq=tq, tk=tk, n_ctx=n_ctx, t=t),
        grid=(b, t // tq),
        in_specs=[
            pl.BlockSpec((1, tq, w), lambda bi, i: (bi, i, 0)),
            pl.BlockSpec((1, t, w), lambda bi, i: (bi, 0, 0)),
            pl.BlockSpec((1, t, ATT_KV_HEADS * vw), lambda bi, i: (bi, 0, 0)),
        ],
        out_specs=pl.BlockSpec((1, tq, w), lambda bi, i: (bi, i, 0)),
        out_shape=jax.ShapeDtypeStruct((b, t, w), BF16),
        scratch_shapes=[pltpu.VMEM((rows, gw), BF16), pltpu.VMEM((rows, vw), F32), pltpu.VMEM((rows, vw), F32),
                        pltpu.VMEM((rows, tk), F32), pltpu.VMEM((rows, tk), F32), pltpu.VMEM((rows, vw), F32),
                        pltpu.VMEM((rows, vw), F32)],
        compiler_params=_cparams(("parallel", "parallel")),
    )(aq, ak4, av2)


def _post_mixer(x, delta, m, ng2, wrh, wrl, x_ref, f_ref, aff_ref):
    xn = x + m[2:3] * delta
    x_ref[0] = xn
    f = _norm_mod(xn, ng2, m[4:5], m[3:4])
    fh, fl = _split(f)
    f_ref[0] = fh
    logits = _dot(fh, wrh) + _dot(fl, wrh) + _dot(fh, wrl)
    lane = lax.broadcasted_iota(jnp.int32, logits.shape, 1)
    logits = jnp.where(lane < N_EXPERTS, logits, -1e30)
    e = jnp.exp(logits - jnp.max(logits, axis=1, keepdims=True))
    aff_ref[0] = e / jnp.sum(e, axis=1, keepdims=True)


def _out_kernel(x_ref, of_ref, ob_ref, rg_ref, att_ref, mod_ref, gn_ref, w_ref, ng2_ref, wrh_ref, wrl_ref,
                bd_ref, xo_ref, f_ref, aff_ref):
    bd = bd_ref[...]
    o = of_ref[0] + ob_ref[0]
    mu = _segsum(o, bd) * (1.0 / HEAD_DIM)
    oc = o - mu
    var = _segsum(oc * oc, bd) * (1.0 / HEAD_DIM)
    r = _silu(rg_ref[0]) * (oc * lax.rsqrt(var + EPS) * gn_ref[...])
    y = _dot(r.astype(BF16), w_ref[0:RET_WIDTH, :]) + _dot(att_ref[0], w_ref[RET_WIDTH:, :])
    _post_mixer(x_ref[0], y, mod_ref[0, 0], ng2_ref[...], wrh_ref[...], wrl_ref[...], xo_ref, f_ref, aff_ref)


def _post_specs(b, t, d, tm):
    out_shape = [jax.ShapeDtypeStruct((b, t, d), F32), jax.ShapeDtypeStruct((b, t, d), BF16),
                 jax.ShapeDtypeStruct((b, t, ROUTER_LANES), F32)]
    out_specs = [_row_spec(tm, d), _row_spec(tm, d), _row_spec(tm, ROUTER_LANES)]
    return out_shape, out_specs


def _out_proj(x, of, ob, rg, att, modtab, gn_g, w_out, ng2, wrh, wrl, bd, tm, n_ctx):
    b, t, d = x.shape
    out_shape, out_specs = _post_specs(b, t, d, tm)
    return pl.pallas_call(
        _out_kernel,
        grid=(b, t // tm),
        in_specs=[
            _row_spec(tm, d), _row_spec(tm, SEG_W), _row_spec(tm, SEG_W), _row_spec(tm, SEG_W),
            _row_spec(tm, SEG_W), _mod_spec(d, 6, n_ctx // tm), _const_spec((1, SEG_W)),
            _const_spec(w_out.shape), _const_spec((1, d)), _const_spec(wrh.shape), _const_spec(wrl.shape),
            _const_spec((SEG_W, SEG_W)),
        ],
        out_specs=out_specs,
        out_shape=out_shape,
        compiler_params=_cparams(("parallel", "parallel")),
    )(x, of, ob, rg, att, modtab, gn_g, w_out, ng2, wrh, wrl, bd)


def _s5_kernel(u_ref, perm_ref, tm_ref, wz_ref, wo_ref, ar_ref, ai_ref, d_ref, z_ref,
               zr_s, zi_s, xr_s, xi_s, zs_s, *, n_sub, n_sub_ctx):
    s_n, j_sub, gb, half = n_sub, S5_SUB, S5_BLOCK_GROUPS, 2 * S5_STATE
    cw = z_ref.shape[-1]
    uf = jnp.concatenate([u_ref[0, pl.ds(m, s_n, stride=j_sub), :] for m in range(j_sub)], axis=1)
    perm = perm_ref[...]
    ug = _dot(uf.astype(BF16), perm).astype(BF16)
    for g in range(gb):
        zz = _dot(ug[:, g * cw:(g + 1) * cw], wz_ref[0, 0, g])
        zr_s[pl.ds(g, s_n, stride=gb), :] = zz[:, :half]
        zi_s[pl.ds(g, s_n, stride=gb), :] = zz[:, half:]
    fwd_lane = lax.broadcasted_iota(jnp.int32, (gb, half), 1) < S5_STATE
    ar = ar_ref[0, 0]
    ai = ai_ref[0, 0]

    def body(i, carry, mirror):
        xr, xi = carry
        sb = mirror - i
        fwd = pl.ds(pl.multiple_of(i * gb, gb), gb)
        bwd = pl.ds(pl.multiple_of(sb * gb, gb), gb)
        xr_s[fwd, 0:S5_STATE] = xr[:, 0:S5_STATE]
        xi_s[fwd, 0:S5_STATE] = xi[:, 0:S5_STATE]
        xr_s[bwd, S5_STATE:half] = xr[:, S5_STATE:half]
        xi_s[bwd, S5_STATE:half] = xi[:, S5_STATE:half]
        zr = jnp.where(fwd_lane, zr_s[fwd, :], zr_s[bwd, :])
        zi = jnp.where(fwd_lane, zi_s[fwd, :], zi_s[bwd, :])
        return ar * xr - ai * xi + zr, ar * xi + ai * xr + zi

    zero = jnp.zeros((gb, half), F32)
    carry = lax.fori_loop(0, n_sub_ctx, functools.partial(body, mirror=n_sub_ctx - 1), (zero, zero),
                          unroll=S5_SCAN_UNROLL)
    lax.fori_loop(n_sub_ctx, s_n, functools.partial(body, mirror=s_n - 1 + n_sub_ctx), carry,
                  unroll=S5_SCAN_UNROLL)
    parts = []
    for g in range(gb):
        rows = pl.ds(g, s_n, stride=gb)
        xp = jnp.concatenate([xr_s[rows, :], xi_s[rows, :]], axis=1).astype(BF16)
        parts.append(_dot(ug[:, g * cw:(g + 1) * cw], tm_ref[0, 0, g]) + _dot(xp, wo_ref[0, 0, g]))
    y = _dot_nt(jnp.concatenate(parts, axis=1).astype(BF16), perm) + d_ref[0, 0] * uf
    z = _gelu_tanh(y)
    for n in range(j_sub):
        zs_s[pl.ds(n, s_n, stride=j_sub), :] = z[:, n * cw:(n + 1) * cw]
    z_ref[0] = zs_s[...].astype(BF16)


def _s5_scan(li, u, perm, tmat, wz, wo, ar, ai, dt, n_ctx):
    b, t, d = u.shape
    cw = S5_BLOCK_GROUPS * S5_GROUP
    n_sub = t // S5_SUB
    lw = S5_SUB * cw
    sw = 4 * S5_STATE
    gb = S5_BLOCK_GROUPS
    return pl.pallas_call(
        functools.partial(_s5_kernel, n_sub=n_sub, n_sub_ctx=n_ctx // S5_SUB),
        grid=(d // cw, b),
        in_specs=[
            pl.BlockSpec((1, t, cw), lambda c, bi: (bi, 0, c)),
            pl.BlockSpec((lw, lw), lambda c, bi: (0, 0)),
            pl.BlockSpec((1, 1, gb, cw, cw), lambda c, bi: (li, c, 0, 0, 0)),
            pl.BlockSpec((1, 1, gb, cw, sw), lambda c, bi: (li, c, 0, 0, 0)),
            pl.BlockSpec((1, 1, gb, sw, cw), lambda c, bi: (li, c, 0, 0, 0)),
            pl.BlockSpec((1, 1, gb, 2 * S5_STATE), lambda c, bi: (li, c, 0, 0)),
            pl.BlockSpec((1, 1, gb, 2 * S5_STATE), lambda c, bi: (li, c, 0, 0)),
            pl.BlockSpec((1, 1, 1, lw), lambda c, bi: (li, c, 0, 0)),
        ],
        out_specs=pl.BlockSpec((1, t, cw), lambda c, bi: (bi, 0, c)),
        out_shape=jax.ShapeDtypeStruct((b, t, d), BF16),
        scratch_shapes=[pltpu.VMEM((gb * n_sub, 2 * S5_STATE), F32)] * 4 + [pltpu.VMEM((t, cw), F32)],
        compiler_params=_cparams(("parallel", "parallel")),
    )(u, perm, tmat, wz, wo, ar, ai, dt)


def _s5_lane_regroup():
    j_sub, gb, k_w = S5_SUB, S5_BLOCK_GROUPS, S5_GROUP
    src = jnp.arange(j_sub * gb * k_w)
    m, g, k = src // (gb * k_w), (src // k_w) % gb, src % k_w
    dst = g * (j_sub * k_w) + m * k_w + k
    return (dst[:, None] == jnp.arange(j_sub * gb * k_w)[None, :]).astype(BF16)


def _s5_tables(a_re, a_im, log_dt, b_re, b_im, c_re, c_im, d_skip):
    hp = lax.Precision.HIGHEST
    j_sub = S5_SUB
    gb = S5_BLOCK_GROUPS
    lam_re = jnp.minimum(a_re, -1e-4)
    lam_im = a_im
    dt = jnp.exp(log_dt)[..., None]
    mag = jnp.exp(lam_re * dt)
    bar_re = mag * jnp.cos(lam_im * dt)
    bar_im = mag * jnp.sin(lam_im * dt)
    den = lam_re * lam_re + lam_im * lam_im
    nr, ni = bar_re - 1.0, bar_im
    k_re = (nr * lam_re + ni * lam_im) / den
    k_im = (ni * lam_re - nr * lam_im) / den
    bb_re = k_re[..., None] * b_re[None] - k_im[..., None] * b_im[None]
    bb_im = k_re[..., None] * b_im[None] + k_im[..., None] * b_re[None]
    jj = jnp.arange(j_sub + 1, dtype=F32)[:, None, None, None]
    pmag = jnp.exp(jj * (lam_re * dt)[None])
    pw_re = pmag * jnp.cos(jj * (lam_im * dt)[None])
    pw_im = pmag * jnp.sin(jj * (lam_im * dt)[None])
    pb_re = pw_re[..., None] * bb_re[None] - pw_im[..., None] * bb_im[None]
    pb_im = pw_re[..., None] * bb_im[None] + pw_im[..., None] * bb_re[None]
    cp_re = c_re[None] * pw_re[:, :, :, None, :] - c_im[None] * pw_im[:, :, :, None, :]
    cp_im = c_re[None] * pw_im[:, :, :, None, :] + c_im[None] * pw_re[:, :, :, None, :]
    kern = (jnp.einsum('dgcp,jdgpk->jdgkc', c_re, pb_re[:j_sub], precision=hp)
            - jnp.einsum('dgcp,jdgpk->jdgkc', c_im, pb_im[:j_sub], precision=hp))
    groups = a_re.shape[1]
    nblk = groups // gb
    k_w = S5_GROUP
    p_w = S5_STATE
    gw = j_sub * k_w
    pos = jnp.arange(j_sub)
    lag = pos[None, :] - pos[:, None]
    kf = jnp.where((lag >= 0)[:, :, None, None, None], kern[jnp.clip(lag, 0, j_sub - 1), 0], 0.0)
    kb = jnp.where((lag <= 0)[:, :, None, None, None], kern[jnp.clip(-lag, 0, j_sub - 1), 1], 0.0)
    tmat = (kf + kb).transpose(2, 0, 3, 1, 4).reshape(nblk, gb, gw, gw)
    rev = pos[::-1]
    zc = jnp.concatenate([pb_re[rev, 0], pb_re[pos, 1], pb_im[rev, 0], pb_im[pos, 1]], axis=2)
    wz = zc.transpose(1, 0, 3, 2).reshape(nblk, gb, gw, 4 * p_w)
    orows = lambda cp, d, order: cp[order, d].transpose(1, 3, 0, 2).reshape(nblk, gb, p_w, gw)
    wo = jnp.concatenate([orows(cp_re, 0, pos + 1), orows(cp_re, 1, j_sub - pos),
                          -orows(cp_im, 0, pos + 1), -orows(cp_im, 1, j_sub - pos)], axis=2)
    ar = jnp.concatenate([pw_re[j_sub, 0], pw_re[j_sub, 1]], axis=-1).reshape(nblk, gb, 2 * p_w)
    ai = jnp.concatenate([pw_im[j_sub, 0], pw_im[j_sub, 1]], axis=-1).reshape(nblk, gb, 2 * p_w)
    dt_tile = jnp.tile(d_skip.reshape(nblk, 1, gb * k_w), (1, 1, j_sub))
    return tmat.astype(BF16), wz.astype(BF16), wo.astype(BF16), ar, ai, dt_tile


def _glu_kernel(x_ref, z_ref, mod_ref, w_ref, b_ref, ng2_ref, wrh_ref, wrl_ref, xo_ref, f_ref, aff_ref):
    d = x_ref.shape[-1]
    ag = _dot(z_ref[0], w_ref[...]) + b_ref[...]
    delta = ag[:, :d] * _sigmoid(ag[:, d:])
    _post_mixer(x_ref[0], delta, mod_ref[0, 0], ng2_ref[...], wrh_ref[...], wrl_ref[...], xo_ref, f_ref, aff_ref)


def _glu(x, z, modtab, glu_w, glu_b, ng2, wrh, wrl, tm, n_ctx):
    b, t, d = x.shape
    out_shape, out_specs = _post_specs(b, t, d, tm)
    return pl.pallas_call(
        _glu_kernel,
        grid=(b, t // tm),
        in_specs=[_row_spec(tm, d), _row_spec(tm, d), _mod_spec(d, 6, n_ctx // tm),
                  _const_spec(glu_w.shape), _const_spec((1, 2 * d)), _const_spec((1, d)),
                  _const_spec(wrh.shape), _const_spec(wrl.shape)],
        out_specs=out_specs,
        out_shape=out_shape,
        compiler_params=_cparams(("parallel", "parallel")),
    )(x, z, modtab, glu_w, glu_b, ng2, wrh, wrl)


def _cumsum_lanes(v):
    n = v.shape[-1]
    lane = lax.broadcasted_iota(jnp.int32, v.shape, 1)
    sh = 1
    while sh < n:
        v = v + jnp.where(lane >= sh, pltpu.roll(v, sh, 1), 0.0)
        sh *= 2
    return v


def _select_segment(aff, cap, base):
    bits = pltpu.bitcast(aff, jnp.int32)
    thr = jnp.zeros((aff.shape[0], 1), jnp.int32)
    for bit in range(30, -1, -1):
        cand = thr | (1 << bit)
        cnt = jnp.sum(jnp.where(bits >= cand, 1.0, 0.0), axis=1, keepdims=True)
        thr = jnp.where(cnt >= cap, cand, thr)
    gt = bits > thr
    eq = bits == thr
    need = cap - jnp.sum(jnp.where(gt, 1.0, 0.0), axis=1, keepdims=True)
    eqf = jnp.where(eq, 1.0, 0.0)
    eq_before = _cumsum_lanes(eqf) - eqf
    sel = jnp.where(gt, 1.0, jnp.where(eq, jnp.where(eq_before < need, 1.0, 0.0), 0.0))
    pos = _cumsum_lanes(sel) - sel
    return jnp.where(sel > 0.0, pos.astype(jnp.int32) + base, -1)


def _select_kernel(aff_ref, slot_ref, *, n_ctx, cap_ctx, cap_lat):
    slot_ref[0, :, 0:n_ctx] = _select_segment(aff_ref[0, :, 0:n_ctx], cap_ctx, 0)
    slot_ref[0, :, n_ctx:] = _select_segment(aff_ref[0, :, n_ctx:], cap_lat, cap_ctx)


def _select(aff_t, n_ctx, cap_ctx, cap_lat):
    b, e, t = aff_t.shape
    return pl.pallas_call(
        functools.partial(_select_kernel, n_ctx=n_ctx, cap_ctx=cap_ctx, cap_lat=cap_lat),
        grid=(b,),
        in_specs=[pl.BlockSpec((1, e, t), lambda i: (i, 0, 0))],
        out_specs=pl.BlockSpec((1, e, t), lambda i: (i, 0, 0)),
        out_shape=jax.ShapeDtypeStruct((b, e, t), jnp.int32),
        compiler_params=_cparams(("parallel",)),
    )(aff_t)


def _moe_window(a_ref, base, e, k, rows, w):
    nominal = a_ref[base + e] + k * w
    return nominal, jnp.minimum(nominal, rows - w)


def _moe_hits(slot_row, nominal, actual, w):
    rid = lax.broadcasted_iota(jnp.int32, (w, slot_row.shape[-1]), 0) + actual
    return jnp.where(slot_row >= nominal, slot_row, -1) == rid


def _moe_gather_kernel(a_ref, nr_ref, slot_ref, aff_ref, f_ref, xs_ref, gate_ref, *, rows, w):
    b = pl.program_id(0)
    j = pl.program_id(1)
    n_e = slot_ref.shape[2]
    step = b * pl.num_programs(1) + j

    @pl.when(j == 0)
    def _():
        xs_ref[...] = jnp.zeros_like(xs_ref)
        gate_ref[...] = jnp.zeros_like(gate_ref)

    fj = f_ref[0]

    def one_round(k, carry):
        for e in range(n_e):
            nominal, actual = _moe_window(a_ref, step * n_e, e, k, rows, w)
            hit = _moe_hits(slot_ref[0, 0, e:e + 1, :], nominal, actual, w)
            off = pl.multiple_of(e * rows + actual, 16)
            xs_ref[0, pl.ds(off, w), :] += _dot(jnp.where(hit, 1.0, 0.0).astype(BF16), fj).astype(BF16)
            gate_ref[0, pl.ds(off, w), :] += jnp.sum(jnp.where(hit, aff_ref[0, 0, e:e + 1, :], 0.0),
                                                     axis=1, keepdims=True)
        return carry

    lax.fori_loop(0, nr_ref[step], one_round, 0)


def _moe_ffn_kernel(xs_ref, gate_ref, w1_ref, w3_ref, w2_ref, y_ref, ys_s, *, n_ft):
    ft = pl.program_id(2)
    nb, rows, d = xs_ref.shape
    xs = xs_ref[...].reshape(nb * rows, d)
    hid = (_silu(_dot(xs, w1_ref[0, 0].astype(BF16))) * _dot(xs, w3_ref[0, 0].astype(BF16))).astype(BF16)
    part = _dot(hid, w2_ref[0, 0].astype(BF16))

    @pl.when(ft == 0)
    def _():
        ys_s[...] = part

    @pl.when(ft > 0)
    def _():
        ys_s[...] += part

    @pl.when(ft == n_ft - 1)
    def _():
        y_ref[...] = (ys_s[...] * gate_ref[...].reshape(nb * rows, 1)).astype(BF16).reshape(nb, rows, d)


def _moe_combine_kernel(a_ref, nr_ref, slot_ref, y_ref, x_ref, mod_ref, *rest, rows, w, nb, lead, tail):
    o_ref = rest[-1] if tail != "s5" else rest[-2]
    b = pl.program_id(0)
    n_e = slot_ref.shape[2]
    step = b * nb + pl.program_id(1) + lead

    def one_round(k, acc):
        onehots, ys = [], []
        for e in range(n_e):
            nominal, actual = _moe_window(a_ref, step * n_e, e, k, rows, w)
            hit = _moe_hits(slot_ref[0, 0, e:e + 1, :], nominal, actual, w)
            onehots.append(jnp.where(hit, 1.0, 0.0).astype(BF16))
            ys.append(y_ref[0, pl.ds(pl.multiple_of(e * rows + actual, 16), w), :])
        return acc + _dot_tn(jnp.concatenate(onehots, axis=0), jnp.concatenate(ys, axis=0))

    acc = lax.fori_loop(0, nr_ref[step], one_round, jnp.zeros(o_ref.shape[1:], F32))
    xn = x_ref[0] + mod_ref[0, 0, 5:6, :] * acc
    if tail == "final":
        xn = xn * lax.rsqrt(jnp.mean(xn * xn, axis=-1, keepdims=True) + EPS) * rest[0][...]
    o_ref[0] = xn
    if tail == "s5":
        m = rest[0][0, 0]
        rest[3][0] = _norm_mod(xn, rest[1][...], m[1:2], m[0:1])


def _moe_windows(slot, tb, w):
    b, e, t = slot.shape
    nb = t // tb
    cnt = jnp.sum((slot >= 0).reshape(b, e, nb, tb), axis=-1, dtype=jnp.int32)
    start = jnp.cumsum(cnt, axis=-1) - cnt
    a0 = (start // 16) * 16
    rounds = jnp.maximum(jnp.max((start + cnt - a0 + w - 1) // w, axis=1), 1)
    return jnp.swapaxes(a0, 1, 2).reshape(-1).astype(jnp.int32), rounds.reshape(-1).astype(jnp.int32)


def _route_and_moe(layer, x, aff, f, modtab, w1, w3, w2, n_ctx, tb, tf, final_g=None, next_s5=None):
    b, t, d = f.shape
    n_e = w1.shape[1]
    ff = w1.shape[-1]
    nb = t // tb
    aff_t = jnp.swapaxes(aff[:, :, :N_EXPERTS], 1, 2)
    cap_ctx = (CAPACITY_FACTOR * n_ctx) // N_EXPERTS
    cap_lat = (CAPACITY_FACTOR * (t - n_ctx)) // N_EXPERTS
    rows = cap_ctx + cap_lat
    w = min(MOE_WINDOW, rows)
    slot = _select(aff_t, n_ctx, cap_ctx, cap_lat)
    a0, rounds = _moe_windows(slot, tb, w)
    by_block = lambda v: jnp.swapaxes(v.reshape(b, n_e, nb, tb), 1, 2)
    slot_b, aff_b = by_block(slot), by_block(aff_t)
    blk_spec = pl.BlockSpec((1, 1, n_e, tb), lambda bi, j, *_: (bi, j, 0, 0))
    whole = lambda width: pl.BlockSpec((1, n_e * rows, width), lambda bi, j, *_: (bi, 0, 0))
    xs, gate = pl.pallas_call(
        functools.partial(_moe_gather_kernel, rows=rows, w=w),
        grid_spec=pltpu.PrefetchScalarGridSpec(
            num_scalar_prefetch=2, grid=(b, nb),
            in_specs=[blk_spec, blk_spec, pl.BlockSpec((1, tb, d), lambda bi, j, *_: (bi, j, 0))],
            out_specs=[whole(d), whole(1)]),
        out_shape=[jax.ShapeDtypeStruct((b, n_e * rows, d), BF16), jax.ShapeDtypeStruct((b, n_e * rows, 1), F32)],
        compiler_params=_cparams(("parallel", "arbitrary")),
    )(a0, rounds, slot_b, aff_b, f)
    n_ft = ff // tf
    sb = MOE_FFN_SAMPLES
    y = pl.pallas_call(
        functools.partial(_moe_ffn_kernel, n_ft=n_ft),
        grid=(b // sb, n_e, n_ft),
        in_specs=[
            pl.BlockSpec((sb, rows, d), lambda bi, e, k: (bi, e, 0)),
            pl.BlockSpec((sb, rows, 1), lambda bi, e, k: (bi, e, 0)),
            pl.BlockSpec((1, 1, d, tf), lambda bi, e, k: (layer, e, 0, k)),
            pl.BlockSpec((1, 1, d, tf), lambda bi, e, k: (layer, e, 0, k)),
            pl.BlockSpec((1, 1, tf, d), lambda bi, e, k: (layer, e, k, 0)),
        ],
        out_specs=pl.BlockSpec((sb, rows, d), lambda bi, e, k: (bi, e, 0)),
        out_shape=jax.ShapeDtypeStruct((b, n_e * rows, d), BF16),
        scratch_shapes=[pltpu.VMEM((sb * rows, d), F32)],
        compiler_params=_cparams(("parallel", "parallel", "arbitrary")),
    )(xs, gate, w1, w3, w2)
    seg_blocks = n_ctx // tb
    lead = 0 if final_g is None else seg_blocks
    seg_of = lambda bi, j, *_: (bi, jnp.where(j + lead >= seg_blocks, 1, 0), 0, 0)
    gain_spec = pl.BlockSpec((1, d), lambda bi, j, *_: (0, 0))
    out_spec = pl.BlockSpec((1, tb, d), lambda bi, j, *_: (bi, j, 0))
    out_sds = jax.ShapeDtypeStruct((b, t - lead * tb, d), F32)
    tail, extra_args, extra_specs, out_specs, out_shape = "plain", (), [], out_spec, out_sds
    if final_g is not None:
        tail, extra_args, extra_specs = "final", (final_g,), [gain_spec]
    elif next_s5 is not None:
        tail, extra_args, extra_specs = "s5", tuple(next_s5), [pl.BlockSpec((1, 1, 6, d), seg_of), gain_spec]
        out_specs, out_shape = [out_spec, out_spec], [out_sds, out_sds]
    return pl.pallas_call(
        functools.partial(_moe_combine_kernel, rows=rows, w=w, nb=nb, lead=lead, tail=tail),
        grid_spec=pltpu.PrefetchScalarGridSpec(
            num_scalar_prefetch=2, grid=(b, nb - lead),
            in_specs=[pl.BlockSpec((1, 1, n_e, tb), lambda bi, j, *_: (bi, j + lead, 0, 0)), whole(d),
                      pl.BlockSpec((1, tb, d), lambda bi, j, *_: (bi, j + lead, 0)),
                      pl.BlockSpec((1, 1, 6, d), seg_of)]
            + extra_specs,
            out_specs=out_specs),
        out_shape=out_shape,
        compiler_params=_cparams(("parallel", "parallel")),
    )(a0, rounds, slot_b, y, x, modtab, *extra_args)


def _rope_tables(n_ctx, n_lat):
    n_rows = n_lat // GRID_W
    row = jnp.repeat(jnp.arange(n_rows, dtype=F32), GRID_W)
    col = jnp.tile(jnp.arange(GRID_W, dtype=F32), n_rows)
    inv = ROPE_THETA ** (-jnp.arange(0, ROPE_AXIS_DIM, 2, dtype=F32) / ROPE_AXIS_DIM)
    ang_r = row[:, None] * inv[None, :]
    ang_c = col[:, None] * inv[None, :]
    cos = jnp.concatenate([jnp.cos(ang_r), jnp.cos(ang_r), jnp.cos(ang_c), jnp.cos(ang_c)], axis=-1)
    sin = jnp.concatenate([-jnp.sin(ang_r), jnp.sin(ang_r), -jnp.sin(ang_c), jnp.sin(ang_c)], axis=-1)
    cos = jnp.concatenate([jnp.ones((n_ctx, HEAD_DIM), F32), cos], axis=0)
    sin = jnp.concatenate([jnp.zeros((n_ctx, HEAD_DIM), F32), sin], axis=0)
    reps = SEG_W // HEAD_DIM
    return jnp.tile(cos, (1, reps)), jnp.tile(sin, (1, reps))


def _extend_in_proj(w_in):
    base = 4 * RET_WIDTH + ATT_WIDTH
    ak = w_in[:, base:base + ATT_KV_WIDTH].reshape(-1, ATT_KV_HEADS, 1, HEAD_DIM)
    av = w_in[:, base + ATT_KV_WIDTH:].reshape(-1, ATT_KV_HEADS, 1, HEAD_DIM)
    rep = lambda w, n: jnp.broadcast_to(w, (w.shape[0], ATT_KV_HEADS, n, HEAD_DIM)).reshape(w.shape[0], -1)
    return jnp.concatenate([w_in[:, :base], rep(ak, ATT_GROUP), rep(av, 2)], axis=-1)


def kernel(x, c, ctx, c_ctx, mod_w, mod_b, norm_g, mix_in_w, mix_out_w, ret_log_rate, ret_gn_g, qk_norm_g,
           s5_a_re, s5_a_im, s5_log_dt, s5_b_re, s5_b_im, s5_c_re, s5_c_im, s5_d, s5_glu_w, s5_glu_b,
           moe_router_w, moe_w1, moe_w3, moe_w2, final_norm_g):
    batch, n_lat, d = x.shape
    n_ctx = ctx.shape[1]
    t = n_ctx + n_lat
    depth = mod_w.shape[0]
    assert batch == 8 and d % 128 == 0 and n_ctx % CHUNK == 0 and n_lat % CHUNK == 0
    tm = 256 if n_ctx % 256 == 0 else 128
    tb = tm
    tf = 1024

    xs = jnp.concatenate([ctx, x], axis=1)
    rows = 16
    cond = jnp.zeros((rows, d), F32).at[:batch].set(c).at[batch].set(c_ctx)
    mod_all = _modulation(cond, mod_w, mod_b)
    cos, sin = _rope_tables(n_ctx, n_lat)
    lane_head = jnp.arange(SEG_W) // HEAD_DIM
    bd = (lane_head[:, None] == lane_head[None, :]).astype(BF16)
    w1, w3, w2 = moe_w1, moe_w3, moe_w2
    s5_perm = _s5_lane_regroup()
    s5_tabs = jax.vmap(_s5_tables)(s5_a_re, s5_a_im, s5_log_dt, s5_b_re, s5_b_im, s5_c_re, s5_c_im, s5_d)

    def modtab_of(layer):
        m = mod_all[layer]
        lat = m[:batch].reshape(batch, 1, 6, d)
        cx = jnp.broadcast_to(m[batch].reshape(1, 1, 6, d), (batch, 1, 6, d))
        return jnp.concatenate([cx, lat], axis=1)

    u = None
    for layer in range(depth):
        modtab = modtab_of(layer)
        ng1 = norm_g[layer, 0].reshape(1, d)
        ng2 = norm_g[layer, 1].reshape(1, d)
        wr = jnp.zeros((d, ROUTER_LANES), F32).at[:, :N_EXPERTS].set(moe_router_w[layer])
        wrh = wr.astype(BF16)
        wrl = (wr - wrh.astype(F32)).astype(BF16)
        i = layer // 2
        if layer % 2 == 0:
            w_ext = _extend_in_proj(mix_in_w[i]).astype(BF16)
            reps = SEG_W // HEAD_DIM
            qkg = jnp.stack([jnp.tile(qk_norm_g[i, 0], reps), jnp.tile(qk_norm_g[i, 1], reps)])
            rq, rk, rv, rg, aq, ak4, av2 = _in_proj(xs, modtab, ng1, w_ext, cos, sin, qkg, bd, tm, n_ctx)
            rate = jnp.repeat(ret_log_rate[i], HEAD_DIM, axis=-1)
            rate_h = jnp.broadcast_to(ret_log_rate[i][:, :, None], (2, RET_HEADS, CHUNK))
            o_f, o_b = _retention(rq, rk, rv, rate, rate_h, bd, n_ctx)
            att = _attention(aq, ak4, av2, tm, tm, n_ctx)
            xs, f, aff = _out_proj(xs, o_f, o_b, rg, att, modtab, ret_gn_g[i].reshape(1, SEG_W),
                                   mix_out_w[i].astype(BF16), ng2, wrh, wrl, bd, tm, n_ctx)
        else:
            z = _s5_scan(i, u, s5_perm, *s5_tabs, n_ctx)
            xs, f, aff = _glu(xs, z, modtab, s5_glu_w[i].astype(BF16), s5_glu_b[i].reshape(1, 2 * d),
                              ng2, wrh, wrl, tm, n_ctx)
        final_g = final_norm_g.reshape(1, d) if layer == depth - 1 else None
        next_s5 = None
        if layer + 1 < depth and (layer + 1) % 2 == 1:
            next_s5 = (modtab_of(layer + 1), norm_g[layer + 1, 0].reshape(1, d))
        xs = _route_and_moe(layer, xs, aff, f, modtab, w1, w3, w2, n_ctx, tb, tf, final_g, next_s5)
        u = None
        if next_s5 is not None:
            xs, u = xs
    return xs
```
